```python
import math
import jax, jax.numpy as jnp
from jax import lax
import numpy as np

D_MODEL = 2048
BATCH = 1
SEQ = 8192
DEPTH = 1
DEC_BATCH = 128
DEC_SEQ = 1
PAST_LEN = 2048
PAGE_SIZE = 128

N_META = 16
EPS = 1e-6
ATT_WIDTH = D_MODEL // 2
DV = 128
DK = DV // 2
N_HEADS_A = ATT_WIDTH // DV
N_KV_HEADS = N_HEADS_A // 2
KV_REP = N_HEADS_A // N_KV_HEADS
ROT_DIM = DK // 4
ROPE_THETA = 500000.0
Q_BLOCK = 128
NEG_INF = -1e30
D_INNER = D_MODEL // 2
SSM_HEAD_DIM = 64
N_HEADS_S = D_INNER // SSM_HEAD_DIM
N_BC_GROUPS = 2
HEADS_PER_GROUP = N_HEADS_S // N_BC_GROUPS
D_STATE = 128
CONV_W = 4
XBC_DIM = D_INNER + 2 * N_BC_GROUPS * D_STATE
CHUNK = 128
MIX_WIDTH = ATT_WIDTH + D_INNER
Q_DIM = N_HEADS_A * 2 * DK
K_DIM = N_KV_HEADS * 2 * DK
V_DIM = N_KV_HEADS * DV
OFF_K = Q_DIM
OFF_V = OFF_K + K_DIM
OFF_Z = OFF_V + V_DIM
OFF_XBC = OFF_Z + D_INNER
OFF_DT = OFF_XBC + XBC_DIM
IN_DIM = OFF_DT + N_HEADS_S
N_EGROUPS = 4
EXPERTS_PER_GROUP = 8
N_EXPERTS = N_EGROUPS * EXPERTS_PER_GROUP
TOP_K = 2
D_FF_EXPERT = D_MODEL // 2
MOE_BLOCK = 128

kernel_name = "hymba_diffattn_ssd_hmoe_step"


def _rmsnorm(x, w):
    xf = x.astype(jnp.float32)
    y = xf * lax.rsqrt(jnp.mean(xf * xf, axis=-1, keepdims=True) + EPS)
    return (y * w.astype(jnp.float32)).astype(x.dtype)


def _rope(x, pos):
    inv = ROPE_THETA ** (-jnp.arange(0, ROT_DIM, 2, dtype=jnp.float32) / ROT_DIM)
    ang = pos.astype(jnp.float32)[:, None] * inv[None, :]
    cos = jnp.cos(ang)[None, :, None, None, :].astype(x.dtype)
    sin = jnp.sin(ang)[None, :, None, None, :].astype(x.dtype)
    half = ROT_DIM // 2
    x1, x2 = x[..., :half], x[..., half:ROT_DIM]
    return jnp.concatenate([x1 * cos - x2 * sin, x2 * cos + x1 * sin, x[..., ROT_DIM:]], axis=-1)


def _project(u, w_in, pos):
    Bsz, T, _ = u.shape
    p = u @ w_in
    q = _rope(p[..., :OFF_K].reshape(Bsz, T, N_HEADS_A, 2, DK), pos)
    k = _rope(p[..., OFF_K:OFF_V].reshape(Bsz, T, N_KV_HEADS, 2, DK), pos)
    v = p[..., OFF_V:OFF_Z].reshape(Bsz, T, N_KV_HEADS, DV)
    z = p[..., OFF_Z:OFF_XBC]
    xbc = p[..., OFF_XBC:OFF_DT]
    dt_raw = p[..., OFF_DT:IN_DIM]
    return q, k, v, z, xbc, dt_raw


def _diff_lambda(lq1, lk1, lq2, lk2, lam_init):
    f = jnp.float32
    return (jnp.exp(jnp.sum(lq1.astype(f) * lk1.astype(f)))
            - jnp.exp(jnp.sum(lq2.astype(f) * lk2.astype(f))) + lam_init)


def _diff_attention(q, k, v, mask, lam, subln_w, lam_init):
    Bsz, Tq = q.shape[:2]
    qg = q.reshape(Bsz, Tq, N_KV_HEADS, KV_REP, 2, DK)
    s = jnp.einsum('bqhrid,bkhid->bhriqk', qg, k).astype(jnp.float32) * (DK ** -0.5)
    s = jnp.where(mask[None, None, None, None], s, NEG_INF)
    p = jax.nn.softmax(s, axis=-1)
    a = p[:, :, :, 0] - lam * p[:, :, :, 1]
    o = jnp.einsum('bhrqk,bkhd->bqhrd', a.astype(v.dtype), v).reshape(Bsz, Tq, N_HEADS_A, DV)
    o = _rmsnorm(o, subln_w) * (1.0 - lam_init)
    return o.reshape(Bsz, Tq, ATT_WIDTH).astype(q.dtype)


def _prompt_attention(q, k, v, lam, subln_w, lam_init):
    Bsz, T = q.shape[:2]
    nb = (T + Q_BLOCK - 1) // Q_BLOCK
    tp = nb * Q_BLOCK
    qp = jnp.pad(q, [(0, 0), (0, tp - T), (0, 0), (0, 0), (0, 0)])
    qb = jnp.moveaxis(qp.reshape(Bsz, nb, Q_BLOCK, N_HEADS_A, 2, DK), 1, 0)
    starts = jnp.arange(nb, dtype=jnp.int32) * Q_BLOCK
    kpos = jnp.arange(T, dtype=jnp.int32)

    def block(args):
        qblk, start = args
        qpos = start + jnp.arange(Q_BLOCK, dtype=jnp.int32)
        mask = kpos[None, :] <= qpos[:, None]
        return _diff_attention(qblk, k, v, mask, lam, subln_w, lam_init)

    o = lax.map(block, (qb, starts))
    return jnp.moveaxis(o, 0, 1).reshape(Bsz, tp, ATT_WIDTH)[:, :T]


def _causal_conv(xbc, conv_state, w, b):
    T = xbc.shape[1]
    xp = jnp.concatenate([conv_state.astype(xbc.dtype), xbc], axis=1)
    out = b
    for j in range(CONV_W):
        out = out + xp[:, j:j + T] * w[j]
    return jax.nn.silu(out), xp[:, T:]


def _ssm_split(xc, dt_raw, dt_bias):
    Bsz, T, _ = xc.shape
    nb_ = N_BC_GROUPS * D_STATE
    xs = xc[..., :D_INNER].reshape(Bsz, T, N_BC_GROUPS, HEADS_PER_GROUP, SSM_HEAD_DIM)
    Bm = xc[..., D_INNER:D_INNER + nb_].reshape(Bsz, T, N_BC_GROUPS, D_STATE)
    Cm = xc[..., D_INNER + nb_:].reshape(Bsz, T, N_BC_GROUPS, D_STATE)
    dt = jax.nn.softplus(dt_raw.astype(jnp.float32) + dt_bias.astype(jnp.float32))
    return xs, dt.reshape(Bsz, T, N_BC_GROUPS, HEADS_PER_GROUP), Bm, Cm


def _ssd_chunked(xs, dt, A, Bm, Cm):
    f = jnp.float32
    Bsz, T = xs.shape[:2]
    padf = (-N_META) % CHUNK
    tt = ((padf + T + CHUNK - 1) // CHUNK) * CHUNK
    back = tt - padf - T
    padT = lambda a: jnp.pad(a.astype(f), [(0, 0), (padf, back)] + [(0, 0)] * (a.ndim - 2))
    nc = tt // CHUNK
    x = padT(xs).reshape(Bsz, nc, CHUNK, N_BC_GROUPS, HEADS_PER_GROUP, SSM_HEAD_DIM)
    d = padT(dt).reshape(Bsz, nc, CHUNK, N_BC_GROUPS, HEADS_PER_GROUP)
    Bc = padT(Bm).reshape(Bsz, nc, CHUNK, N_BC_GROUPS, D_STATE)
    Cc = padT(Cm).reshape(Bsz, nc, CHUNK, N_BC_GROUPS, D_STATE)
    acs = jnp.cumsum(jnp.moveaxis(d * A, 2, -1), axis=-1)
    seg = acs[..., :, None] - acs[..., None, :]
    causal = jnp.tril(jnp.ones((CHUNK, CHUNK), dtype=bool))
    Lmat = jnp.where(causal, jnp.exp(jnp.where(causal, seg, 0.0)), 0.0)
    CB = jnp.einsum('bclgn,bcsgn->bcgls', Cc, Bc)
    y_diag = jnp.einsum('bcgls,bcgrls,bcsgr,bcsgrp->bclgrp', CB, Lmat, d, x)
    decay_states = jnp.exp(acs[..., -1:] - acs)
    states = jnp.einsum('bcsgn,bcgrs,bcsgr,bcsgrp->bcgrpn', Bc, decay_states, d, x)
    chunk_decay = jnp.exp(acs[..., -1])

    def step(h, inp):
        st, dec = inp
        return dec[..., None, None] * h + st, h

    h0 = jnp.zeros((Bsz, N_BC_GROUPS, HEADS_PER_GROUP, SSM_HEAD_DIM, D_STATE), f)
    h_fin, h_prev = lax.scan(step, h0, (jnp.moveaxis(states, 1, 0), jnp.moveaxis(chunk_decay, 1, 0)))
    h_prev = jnp.moveaxis(h_prev, 0, 1)
    y_off = jnp.einsum('bclgn,bcgrpn,bcgrl->bclgrp', Cc, h_prev, jnp.exp(acs))
    y = (y_diag + y_off).reshape(Bsz, tt, N_BC_GROUPS, HEADS_PER_GROUP, SSM_HEAD_DIM)
    return y[:, padf:padf + T], h_fin


def _ssd_recurrent(xs, dt, A, Bm, Cm, h0):
    f = jnp.float32

    def step(h, inp):
        x_t, dt_t, b_t, c_t = inp
        h = jnp.exp(dt_t * A)[..., None, None] * h + jnp.einsum('bgr,bgrp,bgn->bgrpn', dt_t, x_t, b_t)
        return h, jnp.einsum('bgn,bgrpn->bgrp', c_t, h)

    tm = lambda a: jnp.moveaxis(a.astype(f), 1, 0)
    h_fin, y = lax.scan(step, h0.astype(f), (tm(xs), tm(dt), tm(Bm), tm(Cm)))
    return jnp.moveaxis(y, 0, 1), h_fin


def _ssm_out(y, xs, z, d_skip, norm_w):
    f = jnp.float32
    Bsz, T = z.shape[:2]
    y = y + d_skip.astype(f).reshape(N_BC_GROUPS, HEADS_PER_GROUP)[:, :, None] * xs.astype(f)
    y = y.reshape(Bsz, T, D_INNER) * jax.nn.silu(z.astype(f))
    yg = y.reshape(Bsz, T, N_BC_GROUPS, D_INNER // N_BC_GROUPS)
    yg = yg * lax.rsqrt(jnp.mean(yg * yg, axis=-1, keepdims=True) + EPS)
    return (yg.reshape(Bsz, T, D_INNER) * norm_w.astype(f)).astype(z.dtype)


def _hier_moe(h, w_rg, w_re, w_gate, w_up, w_down):
    Bsz, T, D = h.shape
    xf = h.reshape(-1, D)
    N = xf.shape[0]
    gl = (xf @ w_rg).astype(jnp.float32)
    pg = jax.nn.softmax(gl, axis=-1)
    g = jnp.argmax(gl, axis=-1).astype(jnp.int32)
    el = (xf @ w_re).astype(jnp.float32).reshape(N, N_EGROUPS, EXPERTS_PER_GROUP)
    el_g = jnp.take_along_axis(el, g[:, None, None], axis=1)[:, 0]
    top_v, top_i = lax.top_k(jax.nn.softmax(el_g, axis=-1), TOP_K)
    wts = top_v / jnp.sum(top_v, axis=-1, keepdims=True) * jnp.take_along_axis(pg, g[:, None], axis=1)
    eid = g[:, None] * EXPERTS_PER_GROUP + top_i.astype(jnp.int32)
    A_ = N * TOP_K
    flat_e = eid.reshape(-1)
    flat_tok = jnp.repeat(jnp.arange(N, dtype=jnp.int32), TOP_K)
    order = jnp.argsort(flat_e)
    se, stok, sw = flat_e[order], flat_tok[order], wts.reshape(-1)[order]
    counts = jnp.bincount(flat_e, length=N_EXPERTS).astype(jnp.int32)
    padded = (counts + MOE_BLOCK - 1) // MOE_BLOCK * MOE_BLOCK
    start = jnp.cumsum(counts) - counts
    ends = jnp.cumsum(padded)
    pstart = ends - padded
    dest = pstart[se] + jnp.arange(A_, dtype=jnp.int32) - start[se]
    nblk = (A_ + N_EXPERTS * (MOE_BLOCK - 1) + MOE_BLOCK - 1) // MOE_BLOCK
    rows = nblk * MOE_BLOCK
    buf_tok = jnp.full((rows,), N, dtype=jnp.int32).at[dest].set(stok)
    block_e = jnp.minimum(jnp.searchsorted(ends, jnp.arange(nblk, dtype=jnp.int32) * MOE_BLOCK, side='right'),
                          N_EXPERTS - 1)
    xpad = jnp.concatenate([xf, jnp.zeros((1, D), xf.dtype)], axis=0)
    xb = xpad[buf_tok].reshape(nblk, MOE_BLOCK, D)

    def expert_block(args):
        xblk, e = args
        return (jax.nn.silu(xblk @ w_gate[e]) * (xblk @ w_up[e])) @ w_down[e]

    yb = lax.map(expert_block, (xb, block_e)).reshape(rows, D)
    contrib = yb[dest] * sw[:, None].astype(yb.dtype)
    out = jnp.zeros((N, D), h.dtype).at[stok].add(contrib.astype(h.dtype))
    return out.reshape(Bsz, T, D)


def setup_inputs(seed: int = 0) -> dict:
    key = jax.random.key(seed)
    ks = jax.random.split(key, 32)
    f = jnp.float32
    nrm = lambda k, shape, scale: jax.random.normal(k, shape, f) * scale
    n_pages = PAST_LEN // PAGE_SIZE
    n_phys = (DEC_BATCH * n_pages * 5) // 4
    page_table = jax.random.permutation(ks[0], n_phys)[:DEC_BATCH * n_pages].reshape(DEC_BATCH, n_pages).astype(jnp.int32)
    dt0 = jnp.exp(jax.random.uniform(ks[1], (DEPTH, N_HEADS_S), f, math.log(1e-3), math.log(1e-1)))
    dt_bias = dt0 + jnp.log(-jnp.expm1(-dt0))
    a_log = jnp.log(jax.random.uniform(ks[2], (DEPTH, N_HEADS_S), f, 1.0, 16.0))
    return {
        "x_prompt": nrm(ks[3], (BATCH, SEQ, D_MODEL), 1.0),
        "x_sample": nrm(ks[4], (DEC_BATCH, DEC_SEQ, D_MODEL), 1.0),
        "cache_k": nrm(ks[5], (DEPTH, n_phys, PAGE_SIZE, N_KV_HEADS, 2 * DK), 1.0),
        "cache_v": nrm(ks[6], (DEPTH, n_phys, PAGE_SIZE, N_KV_HEADS, DV), 1.0),
        "state_conv": nrm(ks[7], (DEPTH, DEC_BATCH, CONV_W - 1, XBC_DIM), 1.0),
        "state_ssm": nrm(ks[8], (DEPTH, DEC_BATCH, N_HEADS_S, SSM_HEAD_DIM, D_STATE), 0.5),
        "page_table": page_table,
        "meta_tokens": nrm(ks[9], (N_META, D_MODEL), 1.0),
        "norm_mix_w": 1.0 + nrm(ks[10], (DEPTH, D_MODEL), 0.02),
        "w_in": nrm(ks[11], (DEPTH, D_MODEL, IN_DIM), D_MODEL ** -0.5),
        "lambda_q1": nrm(ks[12], (DEPTH, DK), 0.1),
        "lambda_k1": nrm(ks[13], (DEPTH, DK), 0.1),
        "lambda_q2": nrm(ks[14], (DEPTH, DK), 0.1),
        "lambda_k2": nrm(ks[15], (DEPTH, DK), 0.1),
        "subln_w": 1.0 + nrm(ks[16], (DEPTH, DV), 0.02),
        "conv_w": nrm(ks[17], (DEPTH, CONV_W, XBC_DIM), CONV_W ** -0.5),
        "conv_b": nrm(ks[18], (DEPTH, XBC_DIM), 0.02),
        "dt_bias": dt_bias,
        "a_log": a_log,
        "d_skip": 1.0 + nrm(ks[19], (DEPTH, N_HEADS_S), 0.02),
        "ssm_norm_w": 1.0 + nrm(ks[20], (DEPTH, D_INNER), 0.02),
        "w_out": nrm(ks[21], (DEPTH, MIX_WIDTH, D_MODEL), MIX_WIDTH ** -0.5),
        "norm_ffn_w": 1.0 + nrm(ks[22], (DEPTH, D_MODEL), 0.02),
        "w_router_group": nrm(ks[23], (DEPTH, D_MODEL, N_EGROUPS), D_MODEL ** -0.5),
        "w_router_expert": nrm(ks[24], (DEPTH, D_MODEL, N_EXPERTS), D_MODEL ** -0.5),
        "w_gate": nrm(ks[25], (DEPTH, N_EXPERTS, D_MODEL, D_FF_EXPERT), D_MODEL ** -0.5),
        "w_up": nrm(ks[26], (DEPTH, N_EXPERTS, D_MODEL, D_FF_EXPERT), D_MODEL ** -0.5),
        "w_down": nrm(ks[27], (DEPTH, N_EXPERTS, D_FF_EXPERT, D_MODEL), D_FF_EXPERT ** -0.5),
        "final_norm_w": 1.0 + nrm(ks[28], (D_MODEL,), 0.02),
    }


def reference(x_prompt, x_sample, cache_k, cache_v, state_conv, state_ssm, page_table,
              meta_tokens, norm_mix_w, w_in, lambda_q1, lambda_k1, lambda_q2, lambda_k2,
              subln_w, conv_w, conv_b, dt_bias, a_log, d_skip, ssm_norm_w, w_out,
              norm_ffn_w, w_router_group, w_router_expert, w_gate, w_up, w_down, final_norm_w):
    Bp = x_prompt.shape[0]
    hp = jnp.concatenate([jnp.broadcast_to(meta_tokens[None].astype(x_prompt.dtype), (Bp, N_META, D_MODEL)),
                          x_prompt], axis=1)
    T = hp.shape[1]
    pos_p = jnp.arange(T, dtype=jnp.int32)
    hs = x_sample
    Bs, S = x_sample.shape[:2]
    past = page_table.shape[1] * cache_k.shape[2]
    pos_s = past + jnp.arange(S, dtype=jnp.int32)
    mask_s = jnp.arange(past + S, dtype=jnp.int32)[None, :] <= pos_s[:, None]

    kp_l, vp_l, cp_l, sp_l, ks_l, vs_l, cs_l, ss_l = [], [], [], [], [], [], [], []
    for l in range(DEPTH):
        lam_init = 0.8 - 0.6 * math.exp(-0.3 * l)
        lam = _diff_lambda(lambda_q1[l], lambda_k1[l], lambda_q2[l], lambda_k2[l], lam_init)
        A = -jnp.exp(a_log[l].astype(jnp.float32)).reshape(N_BC_GROUPS, HEADS_PER_GROUP)

        u = _rmsnorm(hp, norm_mix_w[l])
        q, k, v, z, xbc, dt_raw = _project(u, w_in[l], pos_p)
        att = _prompt_attention(q, k, v, lam, subln_w[l], lam_init)
        xc, conv_new = _causal_conv(xbc, jnp.zeros((Bp, CONV_W - 1, XBC_DIM), xbc.dtype), conv_w[l], conv_b[l])
        xs, dt, Bm, Cm = _ssm_split(xc, dt_raw, dt_bias[l])
        y, h_fin = _ssd_chunked(xs, dt, A, Bm, Cm)
        ssm = _ssm_out(y, xs, z, d_skip[l], ssm_norm_w[l])
        hp = hp + jnp.concatenate([att, ssm], axis=-1) @ w_out[l]
        hp = hp + _hier_moe(_rmsnorm(hp, norm_ffn_w[l]), w_router_group[l], w_router_expert[l],
                            w_gate[l], w_up[l], w_down[l])
        kp_l.append(k.reshape(Bp, T, N_KV_HEADS, 2 * DK))
        vp_l.append(v)
        cp_l.append(conv_new)
        sp_l.append(h_fin.reshape(Bp, N_HEADS_S, SSM_HEAD_DIM, D_STATE).astype(state_ssm.dtype))

        u = _rmsnorm(hs, norm_mix_w[l])
        q, k, v, z, xbc, dt_raw = _project(u, w_in[l], pos_s)
        k_past = cache_k[l][page_table].reshape(Bs, past, N_KV_HEADS, 2, DK).astype(k.dtype)
        v_past = cache_v[l][page_table].reshape(Bs, past, N_KV_HEADS, DV).astype(v.dtype)
        att = _diff_attention(q, jnp.concatenate([k_past, k], axis=1), jnp.concatenate([v_past, v], axis=1),
                              mask_s, lam, subln_w[l], lam_init)
        xc, conv_new = _causal_conv(xbc, state_conv[l], conv_w[l], conv_b[l])
        xs, dt, Bm, Cm = _ssm_split(xc, dt_raw, dt_bias[l])
        h0 = state_ssm[l].reshape(Bs, N_BC_GROUPS, HEADS_PER_GROUP, SSM_HEAD_DIM, D_STATE)
        y, h_fin = _ssd_recurrent(xs, dt, A, Bm, Cm, h0)
        ssm = _ssm_out(y, xs, z, d_skip[l], ssm_norm_w[l])
        hs = hs + jnp.concatenate([att, ssm], axis=-1) @ w_out[l]
        hs = hs + _hier_moe(_rmsnorm(hs, norm_ffn_w[l]), w_router_group[l], w_router_expert[l],
                            w_gate[l], w_up[l], w_down[l])
        ks_l.append(k.reshape(Bs, S, N_KV_HEADS, 2 * DK))
        vs_l.append(v)
        cs_l.append(conv_new)
        ss_l.append(h_fin.reshape(Bs, N_HEADS_S, SSM_HEAD_DIM, D_STATE).astype(state_ssm.dtype))

    y_prompt = _rmsnorm(hp, final_norm_w)[:, N_META:]
    y_sample = _rmsnorm(hs, final_norm_w)
    return (y_prompt, y_sample,
            jnp.stack(kp_l), jnp.stack(vp_l), jnp.stack(cp_l), jnp.stack(sp_l),
            jnp.stack(ks_l), jnp.stack(vs_l), jnp.stack(cs_l), jnp.stack(ss_l))
```

```python
import functools
import math

import jax
import jax.numpy as jnp
from jax import lax
from jax.experimental import pallas as pl
from jax.experimental.pallas import tpu as pltpu

F32 = jnp.float32
BF16 = jnp.bfloat16
I32 = jnp.int32

D_MODEL = 2048
N_META = 16
EPS = 1e-6
DV = 128
DK = 64
N_HEADS_A = 8
N_KV_HEADS = 4
ROT_DIM = 16
ROPE_THETA = 500000.0
NEG_INF = -1e30
D_INNER = 1024
SSM_HEAD_DIM = 64
N_HEADS_S = 16
N_BC_GROUPS = 2
D_STATE = 128
CONV_W = 4
XBC_DIM = D_INNER + 2 * N_BC_GROUPS * D_STATE
CHUNK = 128
Q_DIM = N_HEADS_A * 2 * DK
K_DIM = N_KV_HEADS * 2 * DK
V_DIM = N_KV_HEADS * DV
OFF_K = Q_DIM
OFF_V = OFF_K + K_DIM
OFF_Z = OFF_V + V_DIM
OFF_XBC = OFF_Z + D_INNER
OFF_DT = OFF_XBC + XBC_DIM
IN_DIM = OFF_DT + N_HEADS_S
N_EGROUPS = 4
EXPERTS_PER_GROUP = 8
N_EXPERTS = N_EGROUPS * EXPERTS_PER_GROUP
TOP_K = 2
D_FF = 1024
PAGE = 128

LANES = 128
VMEM_LIMIT = 56 * 1024 * 1024

FRONT = CHUNK
PAD_ROWS = FRONT - N_META
TM = 256
TQ = 256
TK = 256
TB = 8
TMOE = 512
TF = 256
TG = 256


def _cparams(sem):
    return pltpu.CompilerParams(dimension_semantics=sem, vmem_limit_bytes=VMEM_LIMIT)


def _resident(shape):
    nd = len(shape)
    return pl.BlockSpec(shape, lambda *_: (0,) * nd, pipeline_mode=pl.Buffered(1))


def _split2(x):
    hi = x.astype(BF16)
    lo = (x - hi.astype(F32)).astype(BF16)
    return hi, lo


def _split3(x):
    hi = x.astype(BF16)
    r = x - hi.astype(F32)
    mid = r.astype(BF16)
    lo = (r - mid.astype(F32)).astype(BF16)
    return hi, mid, lo


def _silu(x):
    return x * (1.0 / (1.0 + jnp.exp(-x)))


def _softplus(x):
    return jnp.maximum(x, 0.0) + jnp.log(1.0 + jnp.exp(-jnp.abs(x)))


def _inproj_kernel(x_ref, nw_ref, wq_ref, wk_ref, wv_ref, wz_ref, wx_ref, wdt_ref,
                   c_ref, s1_ref, s2_ref,
                   q_ref, k_ref, v_ref, kb_ref, vb_ref, z_ref, xbc_ref, dt_ref):
    x = x_ref[...]
    ms = jnp.mean(x * x, axis=-1, keepdims=True)
    u = (x * lax.rsqrt(ms + EPS) * nw_ref[...]).astype(BF16)
    c, s1, s2 = c_ref[...], s1_ref[...], s2_ref[...]

    def rope(p):
        outs = []
        for h in range(p.shape[1] // LANES):
            xh = p[:, h * LANES:(h + 1) * LANES]
            outs.append(xh * c + pltpu.roll(xh, LANES - ROT_DIM // 2, 1) * s1
                        + pltpu.roll(xh, ROT_DIM // 2, 1) * s2)
        return jnp.concatenate(outs, axis=1)

    q = rope(jnp.dot(u, wq_ref[...], preferred_element_type=F32))
    q_ref[...] = (q * (DK ** -0.5)).astype(BF16)
    k = rope(jnp.dot(u, wk_ref[...], preferred_element_type=F32))
    k_ref[...] = k
    kb_ref[...] = k.astype(BF16)
    v = jnp.dot(u, wv_ref[...], preferred_element_type=F32)
    v_ref[...] = v
    vb_ref[...] = v.astype(BF16)
    z_ref[...] = jnp.dot(u, wz_ref[...], preferred_element_type=F32)
    xbc_ref[...] = jnp.dot(u, wx_ref[...], preferred_element_type=F32)
    dt_ref[...] = jnp.dot(u, wdt_ref[...], preferred_element_type=F32)


def _inproj(xall, norm_w, wq, wk, wv, wz, wx, wdt, ctab, s1tab, s2tab):
    R = xall.shape[0]
    row = lambda n: pl.BlockSpec((TM, n), lambda i: (i, 0))
    return pl.pallas_call(
        _inproj_kernel,
        grid=(R // TM,),
        in_specs=[row(D_MODEL), _resident((1, D_MODEL)),
                  _resident(wq.shape), _resident(wk.shape), _resident(wv.shape),
                  _resident(wz.shape), _resident(wx.shape), _resident(wdt.shape),
                  row(LANES), row(LANES), row(LANES)],
        out_specs=[row(Q_DIM), row(K_DIM), row(V_DIM), row(K_DIM), row(V_DIM),
                   row(D_INNER), row(XBC_DIM), row(LANES)],
        out_shape=[jax.ShapeDtypeStruct((R, Q_DIM), BF16),
                   jax.ShapeDtypeStruct((R, K_DIM), F32),
                   jax.ShapeDtypeStruct((R, V_DIM), F32),
                   jax.ShapeDtypeStruct((R, K_DIM), BF16),
                   jax.ShapeDtypeStruct((R, V_DIM), BF16),
                   jax.ShapeDtypeStruct((R, D_INNER), F32),
                   jax.ShapeDtypeStruct((R, XBC_DIM), F32),
                   jax.ShapeDtypeStruct((R, LANES), F32)],
        compiler_params=_cparams(("arbitrary",)),
        name="inproj",
    )(xall, norm_w, wq, wk, wv, wz, wx, wdt, ctab, s1tab, s2tab)


def _diff_lambda(lq1, lk1, lq2, lk2, lam_init):
    a = jnp.sum(lq1[...] * lk1[...], axis=-1, keepdims=True)
    b = jnp.sum(lq2[...] * lk2[...], axis=-1, keepdims=True)
    return jnp.exp(a) - jnp.exp(b) + lam_init


def _subln(o, sub_w, lam_init):
    ms = jnp.mean(o * o, axis=-1, keepdims=True)
    return (o * lax.rsqrt(ms + EPS) * sub_w) * (1.0 - lam_init)


def _attn_kernel(lq1, lk1, lq2, lk2, sub_ref, q_ref, k_ref, v_ref, o_ref,
                 qs_scr, m_scr, l_scr, acc_scr, *, lam_init):
    qi = pl.program_id(1)
    lam = _diff_lambda(lq1, lk1, lq2, lk2, lam_init)

    q = q_ref[...]
    lane = lax.broadcasted_iota(I32, (TQ, LANES), 1)
    first = lane < DK
    zero = jnp.zeros((TQ, LANES), BF16)
    for hh in range(2):
        qh = q[:, hh * LANES:(hh + 1) * LANES]
        qs_scr[(2 * hh) * TQ:(2 * hh + 1) * TQ, :] = jnp.where(first, qh, zero)
        qs_scr[(2 * hh + 1) * TQ:(2 * hh + 2) * TQ, :] = jnp.where(first, zero, qh)
    m_scr[...] = jnp.full(m_scr.shape, NEG_INF, F32)
    l_scr[...] = jnp.zeros(l_scr.shape, F32)
    acc_scr[...] = jnp.zeros(acc_scr.shape, F32)

    def chunk(kc, masked):
        start = pl.multiple_of(kc * TK, TK)
        kblk = k_ref[pl.ds(start, TK), :]
        vblk = v_ref[pl.ds(start, TK), :]
        s = lax.dot_general(qs_scr[...], kblk, (((1,), (1,)), ((), ())),
                            preferred_element_type=F32)
        if masked:
            qpos = qi * TQ + (lax.broadcasted_iota(I32, (4 * TQ, TK), 0) & (TQ - 1))
            kpos = kc * TK + lax.broadcasted_iota(I32, (4 * TQ, TK), 1)
            s = jnp.where((kpos <= qpos) & (kpos >= PAD_ROWS), s, NEG_INF)
        m_prev = m_scr[...]
        m_new = jnp.maximum(m_prev, jnp.max(s, axis=-1, keepdims=True))
        alpha = jnp.exp(m_prev - m_new)
        p = jnp.exp(s - m_new[:, :1])
        l_scr[...] = alpha * l_scr[...] + jnp.sum(p, axis=-1, keepdims=True)
        acc_scr[...] = alpha * acc_scr[...] + jnp.dot(p.astype(BF16), vblk,
                                                      preferred_element_type=F32)
        m_scr[...] = m_new

    @pl.when(qi > 0)
    def _():
        chunk(0, True)

    def body(kc, carry):
        chunk(kc, False)
        return carry

    lax.fori_loop(1, qi, body, 0)
    chunk(qi, True)

    o = acc_scr[...] / l_scr[...]
    sub_w = sub_ref[...]
    for hh in range(2):
        o1 = o[(2 * hh) * TQ:(2 * hh + 1) * TQ, :]
        o2 = o[(2 * hh + 1) * TQ:(2 * hh + 2) * TQ, :]
        o_ref[:, hh * LANES:(hh + 1) * LANES] = _subln(o1 - lam * o2, sub_w, lam_init).astype(BF16)


def _prompt_attention(q, kb, vb, lq1, lk1, lq2, lk2, sub_w, lam_init):
    R = q.shape[0]
    assert TQ == TK and R % TQ == 0
    small = lambda n: pl.BlockSpec((1, n), lambda g, i: (0, 0))
    return pl.pallas_call(
        functools.partial(_attn_kernel, lam_init=lam_init),
        grid=(N_KV_HEADS, R // TQ),
        in_specs=[small(DK), small(DK), small(DK), small(DK), small(DV),
                  pl.BlockSpec((TQ, 2 * LANES), lambda g, i: (i, g)),
                  pl.BlockSpec((R, LANES), lambda g, i: (0, g)),
                  pl.BlockSpec((R, LANES), lambda g, i: (0, g))],
        out_specs=pl.BlockSpec((TQ, 2 * LANES), lambda g, i: (i, g)),
        out_shape=jax.ShapeDtypeStruct((R, N_HEADS_A * DV), BF16),
        scratch_shapes=[pltpu.VMEM((4 * TQ, LANES), BF16),
                        pltpu.VMEM((4 * TQ, LANES), F32),
                        pltpu.VMEM((4 * TQ, LANES), F32),
                        pltpu.VMEM((4 * TQ, LANES), F32)],
        compiler_params=_cparams(("arbitrary", "arbitrary")),
        name="prompt_attn",
    )(lq1, lk1, lq2, lk2, sub_w, q, kb, vb)


def _expand_heads(x, e_ref):
    hi, lo = _split2(x)
    e = e_ref[...]
    return (jnp.dot(hi, e, preferred_element_type=F32)
            + jnp.dot(lo, e, preferred_element_type=F32))


def _gated_out(y, xs, z, dskip_full, normw):
    y = (y + dskip_full * xs) * _silu(z)
    half = D_INNER // N_BC_GROUPS
    outs = []
    for g in range(N_BC_GROUPS):
        yg = y[:, g * half:(g + 1) * half]
        ms = jnp.mean(yg * yg, axis=-1, keepdims=True)
        outs.append(yg * lax.rsqrt(ms + EPS))
    return jnp.concatenate(outs, axis=1) * normw


def _ssd_kernel(xbc_ref, z_ref, dt_ref, cw_ref, cb_ref, dtb_ref, a_ref, dskip_ref, normw_ref,
                e_ref, et_ref, tri_ref, ssm_ref, state_ref, ext_scr, h_scr):
    c = pl.program_id(0)
    L = CHUNK

    @pl.when(c == 0)
    def _():
        ext_scr[0:8, :] = jnp.zeros((8, XBC_DIM), F32)
        h_scr[...] = jnp.zeros(h_scr.shape, F32)

    x_new = xbc_ref[...]
    ext_scr[8:8 + L, :] = x_new
    conv = cb_ref[...] + cw_ref[CONV_W - 1:CONV_W, :] * x_new
    for j in range(CONV_W - 1):
        sh = CONV_W - 1 - j
        conv = conv + cw_ref[j:j + 1, :] * ext_scr[8 - sh:8 - sh + L, :]
    ext_scr[0:8, :] = x_new[L - 8:L, :]
    xc = _silu(conv)
    xs = xc[:, :D_INNER]

    rowi = lax.broadcasted_iota(I32, (L, LANES), 0)
    coli = lax.broadcasted_iota(I32, (L, LANES), 1)
    valid = (c * L + rowi) >= PAD_ROWS
    dt = jnp.where(valid, _softplus(dt_ref[...] + dtb_ref[...]), 0.0)
    da = dt * a_ref[...]
    tri = tri_ref[...]
    acs = sum(jnp.dot(tri, part, preferred_element_type=F32) for part in _split3(da))
    acs_t = acs.T
    dt_t = dt.T
    causal = coli <= rowi
    last = acs[L - 1:L, :]

    w_state = _expand_heads(dt * jnp.exp(last - acs), e_ref)
    dec_out = _expand_heads(jnp.exp(acs), e_ref)
    dec_rows = jnp.sum(et_ref[...] * jnp.exp(last), axis=-1, keepdims=True)
    xw = xs * w_state
    xs_b = xs.astype(BF16)

    half = D_INNER // N_BC_GROUPS
    hpg = N_HEADS_S // N_BC_GROUPS
    y_parts = []
    for g in range(N_BC_GROUPS):
        bm = xc[:, D_INNER + g * D_STATE:D_INNER + (g + 1) * D_STATE].astype(BF16)
        cm = xc[:, D_INNER + (N_BC_GROUPS + g) * D_STATE:
                D_INNER + (N_BC_GROUPS + g + 1) * D_STATE].astype(BF16)
        cb = lax.dot_general(cm, bm, (((1,), (1,)), ((), ())), preferred_element_type=F32)
        h_prev = h_scr[g * half:(g + 1) * half, :]
        y_off = lax.dot_general(cm, h_prev.astype(BF16), (((1,), (1,)), ((), ())),
                                preferred_element_type=F32)
        y_off = y_off * dec_out[:, g * half:(g + 1) * half]
        yd = []
        for r in range(0, hpg, 2):
            pair = []
            for hh in (r, r + 1):
                h = g * hpg + hh
                seg = jnp.broadcast_to(acs[:, h:h + 1], (L, L)) - acs_t[h:h + 1, :]
                lmat = jnp.where(causal, jnp.exp(jnp.where(causal, seg, 0.0)), 0.0)
                pair.append((cb * lmat * dt_t[h:h + 1, :]).astype(BF16))
            h0 = g * hpg + r
            xp = xs_b[:, h0 * SSM_HEAD_DIM:(h0 + 2) * SSM_HEAD_DIM]
            ya = jnp.dot(pair[0], xp, preferred_element_type=F32)
            yb = jnp.dot(pair[1], xp, preferred_element_type=F32)
            yd.append(jnp.where(coli < SSM_HEAD_DIM, ya, yb))
        y_parts.append(jnp.concatenate(yd, axis=1) + y_off)
        xw_t = xw[:, g * half:(g + 1) * half].T.astype(BF16)
        st = jnp.dot(xw_t, bm, preferred_element_type=F32)
        h_scr[g * half:(g + 1) * half, :] = dec_rows[g * half:(g + 1) * half, :] * h_prev + st
    y = jnp.concatenate(y_parts, axis=1)

    ssm_ref[...] = _gated_out(y, xs, z_ref[...], dskip_ref[...], normw_ref[...]).astype(BF16)

    @pl.when(c == pl.num_programs(0) - 1)
    def _():
        state_ref[...] = h_scr[...]


def _prompt_ssd(xbc, z, dt_raw, conv_w, conv_b, dtb, a_row, dskip_full, normw, e_mat, e_t, tri, n_t):
    row = lambda n: pl.BlockSpec((CHUNK, n), lambda c: (c, 0))
    const = lambda shp: pl.BlockSpec(shp, lambda c: (0, 0))
    return pl.pallas_call(
        _ssd_kernel,
        grid=(n_t // CHUNK,),
        in_specs=[row(XBC_DIM), row(D_INNER), row(LANES),
                  const((CONV_W, XBC_DIM)), const((1, XBC_DIM)), const((1, LANES)), const((1, LANES)),
                  const((1, D_INNER)), const((1, D_INNER)), const((LANES, D_INNER)),
                  const((D_INNER, LANES)), const((CHUNK, CHUNK))],
        out_specs=[row(D_INNER), pl.BlockSpec((D_INNER, D_STATE), lambda c: (0, 0))],
        out_shape=[jax.ShapeDtypeStruct((n_t, D_INNER), BF16),
                   jax.ShapeDtypeStruct((D_INNER, D_STATE), F32)],
        scratch_shapes=[pltpu.VMEM((8 + CHUNK, XBC_DIM), F32),
                        pltpu.VMEM((D_INNER, D_STATE), F32)],
        compiler_params=_cparams(("arbitrary",)),
        name="prompt_ssd",
    )(xbc, z, dt_raw, conv_w, conv_b, dtb, a_row, dskip_full, normw, e_mat, e_t, tri)


def _decode_attn_kernel(pt_ref, lq1, lk1, lq2, lk2, sub_ref, q_ref, kn_ref, vn_ref, *rest,
                        n_pages, lam_init):
    k_pages = rest[:n_pages]
    v_pages = rest[n_pages:2 * n_pages]
    o_ref = rest[2 * n_pages]
    del pt_ref
    lam = _diff_lambda(lq1, lk1, lq2, lk2, lam_init)
    nq = 2 * N_HEADS_A
    q8 = q_ref[...]
    lane = lax.broadcasted_iota(I32, (N_HEADS_A, LANES), 1)
    zero = jnp.zeros((N_HEADS_A, LANES), BF16)
    q16 = jnp.concatenate([jnp.where(lane < DK, q8, zero), jnp.where(lane < DK, zero, q8)], axis=0)

    pw = PAGE * N_KV_HEADS
    row_head = (lax.broadcasted_iota(I32, (nq, pw), 0) & (N_HEADS_A - 1)) >> 1
    col_head = lax.broadcasted_iota(I32, (nq, pw), 1) & (N_KV_HEADS - 1)
    own = row_head == col_head
    nt = (((1,), (1,)), ((), ()))
    s_pages = []
    for j in range(n_pages):
        kp = k_pages[j][...].astype(BF16)
        s = lax.dot_general(q16, kp, nt, preferred_element_type=F32)
        s_pages.append(jnp.where(own, s, NEG_INF))
    kn = kn_ref[...].astype(BF16)
    row_head_n = (lax.broadcasted_iota(I32, (nq, LANES), 0) & (N_HEADS_A - 1)) >> 1
    col_n = lax.broadcasted_iota(I32, (nq, LANES), 1)
    s_new = jnp.where(row_head_n == col_n, lax.dot_general(q16, kn, nt, preferred_element_type=F32), NEG_INF)

    m = jnp.max(s_new, axis=-1, keepdims=True)
    for s in s_pages:
        m = jnp.maximum(m, jnp.max(s, axis=-1, keepdims=True))
    p_new = jnp.exp(s_new - m)
    l = jnp.sum(p_new, axis=-1, keepdims=True)
    acc = jnp.dot(p_new.astype(BF16), vn_ref[...].astype(BF16), preferred_element_type=F32)
    for j in range(n_pages):
        p = jnp.exp(s_pages[j] - m)
        l = l + jnp.sum(p, axis=-1, keepdims=True)
        acc = acc + jnp.dot(p.astype(BF16), v_pages[j][...].astype(BF16), preferred_element_type=F32)
    o = acc / l
    res = o[:N_HEADS_A, :] - lam * o[N_HEADS_A:, :]
    o_ref[...] = _subln(res, sub_ref[...], lam_init).astype(BF16)


def _decode_attention(page_table, q_s, k_new8, v_new8, cache_k2, cache_v2, lq1, lk1, lq2, lk2, sub_w, lam_init):
    B, n_pages = page_table.shape
    pw = PAGE * N_KV_HEADS
    small = lambda n: pl.BlockSpec((1, n), lambda b, pt: (0, 0))
    per_b = pl.BlockSpec((None, 8, LANES), lambda b, pt: (b, 0, 0))
    new_tok = pl.BlockSpec((None, LANES, LANES), lambda b, pt: (b, 0, 0))
    page_spec = lambda j: pl.BlockSpec((pw, LANES), lambda b, pt: (pt[b, j], 0))
    grid_spec = pltpu.PrefetchScalarGridSpec(
        num_scalar_prefetch=1,
        grid=(B,),
        in_specs=[small(DK), small(DK), small(DK), small(DK), small(DV), per_b, new_tok, new_tok]
                 + [page_spec(j) for j in range(n_pages)] + [page_spec(j) for j in range(n_pages)],
        out_specs=per_b,
    )
    return pl.pallas_call(
        functools.partial(_decode_attn_kernel, n_pages=n_pages, lam_init=lam_init),
        grid_spec=grid_spec,
        out_shape=jax.ShapeDtypeStruct((B, 8, LANES), BF16),
        compiler_params=_cparams(("arbitrary",)),
        name="decode_attn",
    )(page_table, lq1, lk1, lq2, lk2, sub_w, q_s, k_new8, v_new8,
      *([cache_k2] * n_pages), *([cache_v2] * n_pages))


def _dec_pre_kernel(xbc_ref, s0_ref, s1_ref, s2_ref, dt_ref, cw_ref, cb_ref, dtb_ref, a_ref, e_ref,
                    xc_ref, xdt_ref, dect_ref):
    conv = (cb_ref[...] + cw_ref[0:1, :] * s0_ref[...] + cw_ref[1:2, :] * s1_ref[...]
            + cw_ref[2:3, :] * s2_ref[...] + cw_ref[3:4, :] * xbc_ref[...])
    xc = _silu(conv)
    xc_ref[...] = xc
    dt = _softplus(dt_ref[...] + dtb_ref[...])
    dec = jnp.exp(dt * a_ref[...])
    hi, mid, lo = _split3(dt)
    e = e_ref[...]
    dt_full = (jnp.dot(hi, e, preferred_element_type=F32) + jnp.dot(mid, e, preferred_element_type=F32)
               + jnp.dot(lo, e, preferred_element_type=F32))
    hi, mid, lo = _split3(dec)
    dec_full = (jnp.dot(hi, e, preferred_element_type=F32) + jnp.dot(mid, e, preferred_element_type=F32)
                + jnp.dot(lo, e, preferred_element_type=F32))
    xdt_ref[...] = (dt_full * xc[:, :D_INNER]).T
    dect_ref[...] = dec_full.T


def _dec_state_kernel(h_ref, xdt_ref, dect_ref, bm_ref, cm_ref, hout_ref, y_ref):
    i = pl.program_id(0)
    B = xdt_ref.shape[1]
    lane = lax.broadcasted_iota(I32, (D_INNER, B), 1)
    sub = lax.broadcasted_iota(I32, (TB, D_INNER // N_BC_GROUPS), 0)
    ones = jnp.ones((B, D_STATE), BF16)
    half = D_INNER // N_BC_GROUPS
    nt = (((1,), (1,)), ((), ()))
    y_acc = [jnp.zeros((TB, half), F32) for _ in range(N_BC_GROUPS)]
    for j in range(TB):
        b = i * TB + j
        sel = lane == b
        xcol = jnp.where(sel, xdt_ref[...], 0.0)
        dcol = jnp.where(sel, dect_ref[...], 0.0)
        dec = sum(jnp.dot(part, ones, preferred_element_type=F32) for part in _split3(dcol))
        xhi, xlo = _split2(xcol)
        hnew = dec * h_ref[j]
        for g in range(N_BC_GROUPS):
            bmat = bm_ref[:, g * D_STATE:(g + 1) * D_STATE].astype(BF16)
            upd = (jnp.dot(xhi[g * half:(g + 1) * half], bmat, preferred_element_type=F32)
                   + jnp.dot(xlo[g * half:(g + 1) * half], bmat, preferred_element_type=F32))
            hg = hnew[g * half:(g + 1) * half] + upd
            hout_ref[j, g * half:(g + 1) * half, :] = hg
            start = pl.multiple_of(i * TB, TB)
            c8 = cm_ref[pl.ds(start, TB), g * D_STATE:(g + 1) * D_STATE].astype(BF16)
            yj = lax.dot_general(c8, hg.astype(BF16), nt, preferred_element_type=F32)
            y_acc[g] = y_acc[g] + jnp.where(sub == j, yj, 0.0)
    y_ref[...] = jnp.concatenate(y_acc, axis=1)


def _dec_post_kernel(y_ref, xc_ref, z_ref, dskip_ref, normw_ref, o_ref):
    o_ref[...] = _gated_out(y_ref[...], xc_ref[:, :D_INNER], z_ref[...], dskip_ref[...],
                            normw_ref[...]).astype(BF16)


def _decode_ssm(xbc_s, sc0, sc1, sc2, dt_s, z_s, state, conv_w, conv_b, dtb, a_row, dskip_full, normw, e_mat):
    B = xbc_s.shape[0]
    xc, xdt_t, dec_t = pl.pallas_call(
        _dec_pre_kernel,
        out_shape=[jax.ShapeDtypeStruct((B, XBC_DIM), F32),
                   jax.ShapeDtypeStruct((D_INNER, B), F32),
                   jax.ShapeDtypeStruct((D_INNER, B), F32)],
        compiler_params=pltpu.CompilerParams(vmem_limit_bytes=VMEM_LIMIT),
        name="decode_ssm_pre",
    )(xbc_s, sc0, sc1, sc2, dt_s, conv_w, conv_b, dtb, a_row, e_mat)
    bm = xc[:, D_INNER:D_INNER + N_BC_GROUPS * D_STATE]
    cm = xc[:, D_INNER + N_BC_GROUPS * D_STATE:]
    const = lambda shp: pl.BlockSpec(shp, lambda i: (0,) * len(shp))
    state_spec = pl.BlockSpec((TB, D_INNER, D_STATE), lambda i: (i, 0, 0))
    h_new, y = pl.pallas_call(
        _dec_state_kernel,
        grid=(B // TB,),
        in_specs=[state_spec, const((D_INNER, B)), const((D_INNER, B)),
                  const((B, N_BC_GROUPS * D_STATE)), const((B, N_BC_GROUPS * D_STATE))],
        out_specs=[state_spec, pl.BlockSpec((TB, D_INNER), lambda i: (i, 0))],
        out_shape=[jax.ShapeDtypeStruct((B, D_INNER, D_STATE), F32),
                   jax.ShapeDtypeStruct((B, D_INNER), F32)],
        compiler_params=_cparams(("arbitrary",)),
        name="decode_ssm_state",
    )(state, xdt_t, dec_t, bm, cm)
    ssm = pl.pallas_call(
        _dec_post_kernel,
        out_shape=jax.ShapeDtypeStruct((B, D_INNER), BF16),
        compiler_params=pltpu.CompilerParams(vmem_limit_bytes=VMEM_LIMIT),
        name="decode_ssm_post",
    )(y, xc, z_s, dskip_full, normw)
    return ssm, h_new


def _outproj_kernel(x_ref, att_ref, ssm_ref, wa_ref, ws_ref, nw_ref, wrh_ref, wrl_ref,
                    h_ref, u_ref, lg_ref):
    acc = jnp.dot(att_ref[...], wa_ref[...], preferred_element_type=F32)
    acc = acc + jnp.dot(ssm_ref[...], ws_ref[...], preferred_element_type=F32)
    h = x_ref[...] + acc
    h_ref[...] = h
    ms = jnp.mean(h * h, axis=-1, keepdims=True)
    u = h * lax.rsqrt(ms + EPS) * nw_ref[...]
    u_ref[...] = u
    uh, ul = _split2(u)
    wh, wl = wrh_ref[...], wrl_ref[...]
    lg_ref[...] = (jnp.dot(uh, wh, preferred_element_type=F32) + jnp.dot(uh, wl, preferred_element_type=F32)
                   + jnp.dot(ul, wh, preferred_element_type=F32))


def _outproj(xall, att, ssm, wa, ws, norm_w, wr_hi, wr_lo):
    R = xall.shape[0]
    row = lambda n: pl.BlockSpec((TM, n), lambda i: (i, 0))
    return pl.pallas_call(
        _outproj_kernel,
        grid=(R // TM,),
        in_specs=[row(D_MODEL), row(N_HEADS_A * DV), row(D_INNER),
                  _resident(wa.shape), _resident(ws.shape), _resident((1, D_MODEL)),
                  _resident(wr_hi.shape), _resident(wr_lo.shape)],
        out_specs=[row(D_MODEL), row(D_MODEL), row(LANES)],
        out_shape=[jax.ShapeDtypeStruct((R, D_MODEL), F32),
                   jax.ShapeDtypeStruct((R, D_MODEL), F32),
                   jax.ShapeDtypeStruct((R, LANES), F32)],
        compiler_params=_cparams(("arbitrary",)),
        name="outproj",
    )(xall, att, ssm, wa, ws, norm_w, wr_hi, wr_lo)


def _gather_kernel(idx_ref, src_ref, out_ref, sem):
    def row_copy(src_row, dst_row):
        return pltpu.make_async_copy(src_ref.at[pl.ds(src_row, 1)], out_ref.at[pl.ds(dst_row, 1)], sem)

    def start(r, carry):
        row_copy(idx_ref[0, r], r).start()
        return carry

    lax.fori_loop(0, TG, start, 0)

    def wait(r, carry):
        row_copy(0, r).wait()
        return carry

    lax.fori_loop(0, TG, wait, 0)


def _gather_rows(src, idx):
    n = idx.shape[0]
    assert n % TG == 0
    width = src.shape[1]
    return pl.pallas_call(
        _gather_kernel,
        grid=(n // TG,),
        in_specs=[pl.BlockSpec((None, 1, TG), lambda i: (i, 0, 0), memory_space=pltpu.SMEM),
                  pl.BlockSpec(memory_space=pl.ANY)],
        out_specs=pl.BlockSpec((TG, width), lambda i: (i, 0)),
        out_shape=jax.ShapeDtypeStruct((n, width), src.dtype),
        scratch_shapes=[pltpu.SemaphoreType.DMA(())],
        compiler_params=_cparams(("arbitrary",)),
        name="gather_rows",
    )(idx.reshape(n // TG, 1, TG), src)


def _moe_kernel(be_ref, nv_ref, x_ref, wg_ref, wu_ref, wd_ref, o_ref):
    del be_ref
    i = pl.program_id(0)
    f = pl.program_id(1)

    @pl.when(i < nv_ref[0])
    def _():
        x = x_ref[...].astype(BF16)
        g = jnp.dot(x, wg_ref[...].astype(BF16), preferred_element_type=F32)
        u = jnp.dot(x, wu_ref[...].astype(BF16), preferred_element_type=F32)
        hmid = (_silu(g) * u).astype(BF16)
        y = jnp.dot(hmid, wd_ref[...].astype(BF16), preferred_element_type=F32)

        @pl.when(f == 0)
        def _():
            o_ref[...] = y

        @pl.when(f > 0)
        def _():
            o_ref[...] += y

    @pl.when((i >= nv_ref[0]) & (f == 0))
    def _():
        o_ref[...] = jnp.zeros(o_ref.shape, F32)


def _moe(block_e, n_valid, xs, w_gate, w_up, w_down):
    rows = xs.shape[0]
    nblk = rows // TMOE
    nf = D_FF // TF

    def blk(i, nv):
        return jnp.minimum(i, nv[0] - 1)

    def fidx(i, f, nv):
        return jnp.where(i < nv[0], f, nf - 1)

    grid_spec = pltpu.PrefetchScalarGridSpec(
        num_scalar_prefetch=2,
        grid=(nblk, nf),
        in_specs=[pl.BlockSpec((TMOE, D_MODEL), lambda i, f, be, nv: (blk(i, nv), 0)),
                  pl.BlockSpec((None, D_MODEL, TF), lambda i, f, be, nv: (be[blk(i, nv)], 0, fidx(i, f, nv))),
                  pl.BlockSpec((None, D_MODEL, TF), lambda i, f, be, nv: (be[blk(i, nv)], 0, fidx(i, f, nv))),
                  pl.BlockSpec((None, TF, D_MODEL), lambda i, f, be, nv: (be[blk(i, nv)], fidx(i, f, nv), 0))],
        out_specs=pl.BlockSpec((TMOE, D_MODEL), lambda i, f, be, nv: (i, 0)),
    )
    return pl.pallas_call(
        _moe_kernel,
        grid_spec=grid_spec,
        out_shape=jax.ShapeDtypeStruct((rows, D_MODEL), F32),
        compiler_params=_cparams(("arbitrary", "arbitrary")),
        name="moe_experts",
    )(block_e, n_valid, xs, w_gate, w_up, w_down)


def _final_kernel(h_ref, y0_ref, y1_ref, w_ref, nw_ref, o_ref):
    w = w_ref[...]
    moe = y0_ref[...] * w[:, 0:1] + y1_ref[...] * w[:, 1:2]
    h = h_ref[...] + moe
    ms = jnp.mean(h * h, axis=-1, keepdims=True)
    o_ref[...] = h * lax.rsqrt(ms + EPS) * nw_ref[...]


def _final(h1, y0, y1, wts, norm_w):
    R = h1.shape[0]
    row = lambda n: pl.BlockSpec((TM, n), lambda i: (i, 0))
    return pl.pallas_call(
        _final_kernel,
        grid=(R // TM,),
        in_specs=[row(D_MODEL), row(D_MODEL), row(D_MODEL), row(TOP_K),
                  pl.BlockSpec((1, D_MODEL), lambda i: (0, 0))],
        out_specs=row(D_MODEL),
        out_shape=jax.ShapeDtypeStruct((R, D_MODEL), F32),
        compiler_params=_cparams(("arbitrary",)),
        name="final_norm",
    )(h1, y0, y1, wts, norm_w)


def _route(logits, row_valid):
    R = logits.shape[0]
    gl = logits[:, :N_EGROUPS]
    pg = jax.nn.softmax(gl, axis=-1)
    g = jnp.argmax(gl, axis=-1).astype(I32)
    el = logits[:, N_EGROUPS:N_EGROUPS + N_EXPERTS].reshape(R, N_EGROUPS, EXPERTS_PER_GROUP)
    el_g = jnp.take_along_axis(el, g[:, None, None], axis=1)[:, 0]
    top_v, top_i = lax.top_k(jax.nn.softmax(el_g, axis=-1), TOP_K)
    wts = top_v / jnp.sum(top_v, axis=-1, keepdims=True) * jnp.take_along_axis(pg, g[:, None], axis=1)
    eid = g[:, None] * EXPERTS_PER_GROUP + top_i.astype(I32)
    eid = jnp.where(row_valid[:, None], eid, N_EXPERTS)
    wts = jnp.where(row_valid[:, None], wts, 0.0)

    flat_e = eid.reshape(-1)
    onehot = (flat_e[:, None] == jnp.arange(N_EXPERTS, dtype=I32)[None, :]).astype(I32)
    rank = jnp.sum((jnp.cumsum(onehot, axis=0) - onehot) * onehot, axis=1)
    counts = jnp.sum(onehot, axis=0)
    padded = (counts + TMOE - 1) // TMOE * TMOE
    ends = jnp.cumsum(padded)
    pstart = ends - padded
    is_real = flat_e < N_EXPERTS
    dest = jnp.where(is_real, pstart[jnp.minimum(flat_e, N_EXPERTS - 1)] + rank, 0)

    a_max = TOP_K * R
    nblk = (a_max + N_EXPERTS * (TMOE - 1) + TMOE - 1) // TMOE
    rows = nblk * TMOE
    rows = (rows + TG - 1) // TG * TG
    nblk = rows // TMOE
    flat_tok = jnp.repeat(jnp.arange(R, dtype=I32), TOP_K)
    buf_tok = jnp.zeros((rows,), I32).at[jnp.where(is_real, dest, rows)].set(flat_tok, mode="drop")
    n_valid = (ends[-1] // TMOE).astype(I32)
    block_e = jnp.minimum(jnp.searchsorted(ends, jnp.arange(nblk, dtype=I32) * TMOE, side="right"),
                          N_EXPERTS - 1).astype(I32)
    return wts, dest.reshape(R, TOP_K), buf_tok, block_e, n_valid.reshape(1)


def kernel(x_prompt, x_sample, cache_k, cache_v, state_conv, state_ssm, page_table, meta_tokens, norm_mix_w,
           w_in, lambda_q1, lambda_k1, lambda_q2, lambda_k2, subln_w, conv_w, conv_b, dt_bias, a_log, d_skip,
           ssm_norm_w, w_out, norm_ffn_w, w_router_group, w_router_expert, w_gate, w_up, w_down, final_norm_w):
    depth = w_in.shape[0]
    assert depth == 1 and x_prompt.shape[0] == 1 and x_sample.shape[1] == 1
    seq = x_prompt.shape[1]
    B = x_sample.shape[0]
    n_pages = page_table.shape[1]
    n_phys = cache_k.shape[1]
    past = n_pages * cache_k.shape[2]
    assert cache_k.shape[2] == PAGE and seq % CHUNK == 0 and B == LANES and (FRONT + seq + B) % TM == 0
    n_t = FRONT + seq
    R = n_t + B
    l = 0
    lam_init = 0.8 - 0.6 * math.exp(-0.3 * l)

    xall = jnp.concatenate([jnp.zeros((PAD_ROWS, D_MODEL), F32), meta_tokens.astype(F32),
                            x_prompt[0], x_sample[:, 0]], axis=0)
    w = w_in[l]
    wq = w[:, :OFF_K].astype(BF16)
    wk = w[:, OFF_K:OFF_V].astype(BF16)
    wv = w[:, OFF_V:OFF_Z].astype(BF16)
    wz = w[:, OFF_Z:OFF_XBC].astype(BF16)
    wx = w[:, OFF_XBC:OFF_DT].astype(BF16)
    wdt = jnp.pad(w[:, OFF_DT:], ((0, 0), (0, LANES - N_HEADS_S))).astype(BF16)

    pos = jnp.concatenate([jnp.maximum(jnp.arange(n_t, dtype=I32) - PAD_ROWS, 0),
                           jnp.full((B,), past, I32)])
    inv = ROPE_THETA ** (-jnp.arange(0, ROT_DIM, 2, dtype=F32) / ROT_DIM)
    ang = pos.astype(F32)[:, None] * inv[None, :]
    cos, sin = jnp.cos(ang), jnp.sin(ang)
    half = ROT_DIM // 2
    ones = jnp.ones((R, DK - ROT_DIM), F32)
    zeros = jnp.zeros((R, DK - ROT_DIM), F32)
    zh = jnp.zeros((R, half), F32)
    ctab = jnp.tile(jnp.concatenate([cos, cos, ones], axis=1), (1, 2))
    s1tab = jnp.tile(jnp.concatenate([-sin, zh, zeros], axis=1), (1, 2))
    s2tab = jnp.tile(jnp.concatenate([zh, sin, zeros], axis=1), (1, 2))

    row1 = lambda v: v.reshape(1, -1).astype(F32)
    pad_heads = lambda v: jnp.pad(v.astype(F32), (0, LANES - N_HEADS_S)).reshape(1, LANES)
    a_row = pad_heads(-jnp.exp(a_log[l].astype(F32)))
    dtb = pad_heads(dt_bias[l])
    dskip_full = jnp.repeat(d_skip[l].astype(F32), SSM_HEAD_DIM).reshape(1, D_INNER)
    normw = row1(ssm_norm_w[l])
    head_of_col = jnp.arange(D_INNER, dtype=I32) // SSM_HEAD_DIM
    e_mat = (jnp.arange(LANES, dtype=I32)[:, None] == head_of_col[None, :]).astype(BF16)
    tri = (jnp.arange(CHUNK)[:, None] >= jnp.arange(CHUNK)[None, :]).astype(BF16)
    lq1, lk1, lq2, lk2 = (row1(v[l]) for v in (lambda_q1, lambda_k1, lambda_q2, lambda_k2))
    sub_w = row1(subln_w[l])

    q, k, v, kb, vb, z, xbc, dt_raw = _inproj(xall, row1(norm_mix_w[l]), wq, wk, wv, wz, wx, wdt,
                                              ctab, s1tab, s2tab)

    att_all = _prompt_attention(q, kb, vb, lq1, lk1, lq2, lk2, sub_w, lam_init)
    ssm_p, state_p = _prompt_ssd(xbc, z, dt_raw, conv_w[l].astype(F32), row1(conv_b[l]), dtb, a_row,
                                 dskip_full, normw, e_mat, e_mat.T.astype(F32), tri, n_t)

    k_s, v_s = k[n_t:], v[n_t:]
    pad8 = lambda a: jnp.pad(a.reshape(B, N_KV_HEADS, LANES), ((0, 0), (0, LANES - N_KV_HEADS), (0, 0)))
    att_s = _decode_attention(page_table, q[n_t:].reshape(B, N_HEADS_A, LANES), pad8(k_s), pad8(v_s),
                              cache_k[l].reshape(n_phys * PAGE * N_KV_HEADS, LANES),
                              cache_v[l].reshape(n_phys * PAGE * N_KV_HEADS, LANES),
                              lq1, lk1, lq2, lk2, sub_w, lam_init)
    sc = state_conv[l]
    xbc_s = xbc[n_t:]
    ssm_s, state_s = _decode_ssm(xbc_s, sc[:, 0], sc[:, 1], sc[:, 2], dt_raw[n_t:], z[n_t:],
                                 state_ssm[l].reshape(B, D_INNER, D_STATE), conv_w[l].astype(F32),
                                 row1(conv_b[l]), dtb, a_row, dskip_full, normw, e_mat)

    att = lax.dynamic_update_slice(att_all, att_s.reshape(B, N_HEADS_A * DV), (n_t, 0))
    ssm = jnp.concatenate([ssm_p, ssm_s], axis=0)

    wo = w_out[l].astype(BF16)
    wr = jnp.pad(jnp.concatenate([w_router_group[l], w_router_expert[l]], axis=1).astype(F32),
                 ((0, 0), (0, LANES - N_EGROUPS - N_EXPERTS)))
    wr_hi = wr.astype(BF16)
    wr_lo = (wr - wr_hi.astype(F32)).astype(BF16)
    h1, u2, logits = _outproj(xall, att, ssm, wo[:N_HEADS_A * DV], wo[N_HEADS_A * DV:],
                              row1(norm_ffn_w[l]), wr_hi, wr_lo)

    row_valid = jnp.arange(R, dtype=I32) >= PAD_ROWS
    wts, dest, buf_tok, block_e, n_valid = _route(logits, row_valid)
    xs = _gather_rows(u2, buf_tok)
    yb = _moe(block_e, n_valid, xs, w_gate[l], w_up[l], w_down[l])
    y0 = _gather_rows(yb, _pad_to(dest[:, 0], TG))[:R]
    y1 = _gather_rows(yb, _pad_to(dest[:, 1], TG))[:R]
    y = _final(h1, y0, y1, wts, row1(final_norm_w))

    t_real = N_META + seq
    y_prompt = y[FRONT:n_t].reshape(1, seq, D_MODEL)
    y_sample = y[n_t:].reshape(B, 1, D_MODEL)
    k_prompt = k[PAD_ROWS:n_t].reshape(1, 1, t_real, N_KV_HEADS, 2 * DK)
    v_prompt = v[PAD_ROWS:n_t].reshape(1, 1, t_real, N_KV_HEADS, DV)
    conv_prompt = xbc[n_t - (CONV_W - 1):n_t].reshape(1, 1, CONV_W - 1, XBC_DIM)
    ssm_prompt = state_p.reshape(1, 1, N_HEADS_S, SSM_HEAD_DIM, D_STATE)
    k_sample = k_s.reshape(1, B, 1, N_KV_HEADS, 2 * DK)
    v_sample = v_s.reshape(1, B, 1, N_KV_HEADS, DV)
    conv_sample = jnp.stack([sc[:, 1], sc[:, 2], xbc_s], axis=1)[None]
    ssm_sample = state_s.reshape(1, B, N_HEADS_S, SSM_HEAD_DIM, D_STATE)
    return (y_prompt, y_sample, k_prompt, v_prompt, conv_prompt, ssm_prompt,
            k_sample, v_sample, conv_sample, ssm_sample)


def _pad_to(idx, mult):
    n = idx.shape[0]
    return jnp.pad(idx, (0, (-n) % mult))
```

```python
import functools
import math

import jax
import jax.numpy as jnp
from jax import lax
from jax.experimental import pallas as pl
from jax.experimental.pallas import tpu as pltpu

F32 = jnp.float32
BF16 = jnp.bfloat16
I32 = jnp.int32

D_MODEL = 2048
N_META = 16
EPS = 1e-6
DV = 128
DK = 64
N_HEADS_A = 8
N_KV_HEADS = 4
ROT_DIM = 16
ROPE_THETA = 500000.0
NEG_INF = -1e30
LOG2E = 1.4426950408889634
D_INNER = 1024
SSM_HEAD_DIM = 64
N_HEADS_S = 16
N_BC_GROUPS = 2
D_STATE = 128
CONV_W = 4
XBC_DIM = D_INNER + 2 * N_BC_GROUPS * D_STATE
CHUNK = 128
Q_DIM = N_HEADS_A * 2 * DK
K_DIM = N_KV_HEADS * 2 * DK
V_DIM = N_KV_HEADS * DV
OFF_K = Q_DIM
OFF_V = OFF_K + K_DIM
OFF_Z = OFF_V + V_DIM
OFF_XBC = OFF_Z + D_INNER
OFF_DT = OFF_XBC + XBC_DIM
IN_DIM = OFF_DT + N_HEADS_S
N_EGROUPS = 4
EXPERTS_PER_GROUP = 8
N_EXPERTS = N_EGROUPS * EXPERTS_PER_GROUP
TOP_K = 2
D_FF = 1024
PAGE = 128

LANES = 128
VMEM_LIMIT = 56 * 1024 * 1024

FRONT = CHUNK
PAD_ROWS = FRONT - N_META
TM = 256
TQ = 256
TK = 512
TB = 8
TMOE = 512
TF = 256
TG = 256


def _cparams(sem):
    return pltpu.CompilerParams(dimension_semantics=sem, vmem_limit_bytes=VMEM_LIMIT)


def _resident(shape):
    nd = len(shape)
    return pl.BlockSpec(shape, lambda *_: (0,) * nd, pipeline_mode=pl.Buffered(1))


def _split2(x):
    hi = x.astype(BF16)
    lo = (x - hi.astype(F32)).astype(BF16)
    return hi, lo


def _split3(x):
    hi = x.astype(BF16)
    r = x - hi.astype(F32)
    mid = r.astype(BF16)
    lo = (r - mid.astype(F32)).astype(BF16)
    return hi, mid, lo


def _silu(x):
    return x * (1.0 / (1.0 + jnp.exp(-x)))


def _softplus(x):
    return jnp.maximum(x, 0.0) + jnp.log(1.0 + jnp.exp(-jnp.abs(x)))


def _inproj_kernel(x_ref, nw_ref, wq_ref, wk_ref, wv_ref, wz_ref, wx_ref, wdt_ref,
                   c_ref, s1_ref, s2_ref,
                   q_ref, k_ref, v_ref, kb_ref, vb_ref, z_ref, xbc_ref, dt_ref):
    x = x_ref[...]
    ms = jnp.mean(x * x, axis=-1, keepdims=True)
    u = (x * lax.rsqrt(ms + EPS) * nw_ref[...]).astype(BF16)
    c, s1, s2 = c_ref[...], s1_ref[...], s2_ref[...]

    def rope(p):
        outs = []
        for h in range(p.shape[1] // LANES):
            xh = p[:, h * LANES:(h + 1) * LANES]
            outs.append(xh * c + pltpu.roll(xh, LANES - ROT_DIM // 2, 1) * s1
                        + pltpu.roll(xh, ROT_DIM // 2, 1) * s2)
        return jnp.concatenate(outs, axis=1)

    q = rope(jnp.dot(u, wq_ref[...], preferred_element_type=F32))
    q_ref[...] = (q * (DK ** -0.5 * LOG2E)).astype(BF16)
    k = rope(jnp.dot(u, wk_ref[...], preferred_element_type=F32))
    k_ref[...] = k
    kb_ref[...] = k.astype(BF16)
    v = jnp.dot(u, wv_ref[...], preferred_element_type=F32)
    v_ref[...] = v
    vb = v.astype(BF16)
    ones = jnp.ones((vb.shape[0], DV), BF16)
    vb_ref[...] = jnp.concatenate(
        [piece for g in range(N_KV_HEADS) for piece in (vb[:, g * DV:(g + 1) * DV], ones)], axis=1)
    z_ref[...] = jnp.dot(u, wz_ref[...], preferred_element_type=F32)
    xbc_ref[...] = jnp.dot(u, wx_ref[...], preferred_element_type=F32)
    dt_ref[...] = jnp.dot(u, wdt_ref[...], preferred_element_type=F32)


def _inproj(xall, norm_w, wq, wk, wv, wz, wx, wdt, ctab, s1tab, s2tab):
    R = xall.shape[0]
    row = lambda n: pl.BlockSpec((TM, n), lambda i: (i, 0))
    return pl.pallas_call(
        _inproj_kernel,
        grid=(R // TM,),
        in_specs=[row(D_MODEL), _resident((1, D_MODEL)),
                  _resident(wq.shape), _resident(wk.shape), _resident(wv.shape),
                  _resident(wz.shape), _resident(wx.shape), _resident(wdt.shape),
                  row(LANES), row(LANES), row(LANES)],
        out_specs=[row(Q_DIM), row(K_DIM), row(V_DIM), row(K_DIM), row(2 * V_DIM),
                   row(D_INNER), row(XBC_DIM), row(LANES)],
        out_shape=[jax.ShapeDtypeStruct((R, Q_DIM), BF16),
                   jax.ShapeDtypeStruct((R, K_DIM), F32),
                   jax.ShapeDtypeStruct((R, V_DIM), F32),
                   jax.ShapeDtypeStruct((R, K_DIM), BF16),
                   jax.ShapeDtypeStruct((R, 2 * V_DIM), BF16),
                   jax.ShapeDtypeStruct((R, D_INNER), F32),
                   jax.ShapeDtypeStruct((R, XBC_DIM), F32),
                   jax.ShapeDtypeStruct((R, LANES), F32)],
        compiler_params=_cparams(("arbitrary",)),
        name="inproj",
    )(xall, norm_w, wq, wk, wv, wz, wx, wdt, ctab, s1tab, s2tab)


def _diff_lambda(lq1, lk1, lq2, lk2, lam_init):
    a = jnp.sum(lq1[...] * lk1[...], axis=-1, keepdims=True)
    b = jnp.sum(lq2[...] * lk2[...], axis=-1, keepdims=True)
    return jnp.exp(a) - jnp.exp(b) + lam_init


def _subln(o, sub_w, lam_init):
    ms = jnp.mean(o * o, axis=-1, keepdims=True)
    return (o * lax.rsqrt(ms + EPS) * sub_w) * (1.0 - lam_init)


def _attn_kernel(lq1, lk1, lq2, lk2, sub_ref, q_ref, k_ref, ve_ref, o_ref,
                 qs_scr, m_scr, acc_scr, *, lam_init, n_rows):
    qi = pl.program_id(1)
    lam = _diff_lambda(lq1, lk1, lq2, lk2, lam_init)

    q = q_ref[...]
    lane = lax.broadcasted_iota(I32, (TQ, LANES), 1)
    first = lane < DK
    zero = jnp.zeros((TQ, LANES), BF16)
    for hh in range(2):
        qh = q[:, hh * LANES:(hh + 1) * LANES]
        qs_scr[(2 * hh) * TQ:(2 * hh + 1) * TQ, :] = jnp.where(first, qh, zero)
        qs_scr[(2 * hh + 1) * TQ:(2 * hh + 2) * TQ, :] = jnp.where(first, zero, qh)
    m_scr[...] = jnp.full(m_scr.shape, NEG_INF, F32)
    acc_scr[...] = jnp.zeros(acc_scr.shape, F32)

    nt = (((1,), (1,)), ((), ()))

    def update(s, start, rows):
        m_prev = m_scr[rows, :]
        m_new = jnp.maximum(m_prev, jnp.max(s, axis=-1, keepdims=True))
        alpha = jnp.exp2(m_prev - m_new)
        p = jnp.concatenate([jnp.exp2(s[:, t * LANES:(t + 1) * LANES] - m_new)
                             for t in range(TK // LANES)], axis=1).astype(BF16)
        pv = jnp.dot(p, ve_ref[pl.ds(start, TK), :], preferred_element_type=F32)
        acc_scr[rows, :LANES] = alpha * acc_scr[rows, :LANES] + pv[:, :LANES]
        acc_scr[rows, LANES:] = alpha * acc_scr[rows, LANES:] + pv[:, LANES:]
        m_scr[rows, :] = m_new

    halves = [slice(0, 2 * TQ), slice(2 * TQ, 4 * TQ)]

    q_lo = qi * TQ
    n_full = jnp.maximum(q_lo - PAD_ROWS, 0) // TK
    n_all = (q_lo + TQ - PAD_ROWS + TK - 1) // TK

    def full(c, carry):
        start = pl.multiple_of(PAD_ROWS + c * TK, 16)
        for rows in halves:
            s = lax.dot_general(qs_scr[rows, :], k_ref[pl.ds(start, TK), :], nt, preferred_element_type=F32)
            update(s, start, rows)
        return carry

    lax.fori_loop(0, n_full, full, 0)

    def tail(c, carry):
        nominal = PAD_ROWS + c * TK
        start = pl.multiple_of(jnp.minimum(nominal, n_rows - TK), 16)
        col = lax.broadcasted_iota(I32, (2 * TQ, TK), 1)
        qpos = q_lo + (lax.broadcasted_iota(I32, (2 * TQ, TK), 0) & (TQ - 1))
        keep = (col <= qpos - start) & (col >= nominal - start)
        for rows in halves:
            s = lax.dot_general(qs_scr[rows, :], k_ref[pl.ds(start, TK), :], nt, preferred_element_type=F32)
            update(jnp.where(keep, s, NEG_INF), start, rows)
        return carry

    lax.fori_loop(n_full, n_all, tail, 0)

    acc = acc_scr[...]
    o = acc[:, :LANES] / acc[:, LANES:]
    sub_w = sub_ref[...]
    for hh in range(2):
        o1 = o[(2 * hh) * TQ:(2 * hh + 1) * TQ, :]
        o2 = o[(2 * hh + 1) * TQ:(2 * hh + 2) * TQ, :]
        o_ref[:, hh * LANES:(hh + 1) * LANES] = _subln(o1 - lam * o2, sub_w, lam_init).astype(BF16)


def _prompt_attention(q, kb, vbe, lq1, lk1, lq2, lk2, sub_w, lam_init):
    R = q.shape[0]
    assert R % TQ == 0 and (R - TK) % 16 == 0 and R >= TK
    small = lambda n: pl.BlockSpec((1, n), lambda g, i: (0, 0))
    return pl.pallas_call(
        functools.partial(_attn_kernel, lam_init=lam_init, n_rows=R),
        grid=(N_KV_HEADS, R // TQ),
        in_specs=[small(DK), small(DK), small(DK), small(DK), small(DV),
                  pl.BlockSpec((TQ, 2 * LANES), lambda g, i: (i, g)),
                  pl.BlockSpec((R, LANES), lambda g, i: (0, g)),
                  pl.BlockSpec((R, 2 * LANES), lambda g, i: (0, g))],
        out_specs=pl.BlockSpec((TQ, 2 * LANES), lambda g, i: (i, g)),
        out_shape=jax.ShapeDtypeStruct((R, N_HEADS_A * DV), BF16),
        scratch_shapes=[pltpu.VMEM((4 * TQ, LANES), BF16),
                        pltpu.VMEM((4 * TQ, LANES), F32),
                        pltpu.VMEM((4 * TQ, 2 * LANES), F32)],
        compiler_params=_cparams(("arbitrary", "arbitrary")),
        name="prompt_attn",
    )(lq1, lk1, lq2, lk2, sub_w, q, kb, vbe)


def _expand_heads(x, e_ref):
    hi, lo = _split2(x)
    e = e_ref[...]
    return (jnp.dot(hi, e, preferred_element_type=F32)
            + jnp.dot(lo, e, preferred_element_type=F32))


def _gated_out(y, xs, z, dskip_full, normw):
    y = (y + dskip_full * xs) * _silu(z)
    half = D_INNER // N_BC_GROUPS
    outs = []
    for g in range(N_BC_GROUPS):
        yg = y[:, g * half:(g + 1) * half]
        ms = jnp.mean(yg * yg, axis=-1, keepdims=True)
        outs.append(yg * lax.rsqrt(ms + EPS))
    return jnp.concatenate(outs, axis=1) * normw


def _ssd_kernel(xbc_ref, z_ref, dt_ref, cw_ref, cb_ref, dtb_ref, a_ref, dskip_ref, normw_ref,
                e_ref, et_ref, tri_ref, ssm_ref, state_ref, ext_scr, h_scr):
    c = pl.program_id(0)
    L = CHUNK

    @pl.when(c == 0)
    def _():
        ext_scr[0:8, :] = jnp.zeros((8, XBC_DIM), F32)
        h_scr[...] = jnp.zeros(h_scr.shape, F32)

    x_new = xbc_ref[...]
    ext_scr[8:8 + L, :] = x_new
    conv = cb_ref[...] + cw_ref[CONV_W - 1:CONV_W, :] * x_new
    for j in range(CONV_W - 1):
        sh = CONV_W - 1 - j
        conv = conv + cw_ref[j:j + 1, :] * ext_scr[8 - sh:8 - sh + L, :]
    ext_scr[0:8, :] = x_new[L - 8:L, :]
    xc = _silu(conv)
    xs = xc[:, :D_INNER]

    rowi = lax.broadcasted_iota(I32, (L, LANES), 0)
    coli = lax.broadcasted_iota(I32, (L, LANES), 1)
    valid = (c * L + rowi) >= PAD_ROWS
    dt = jnp.where(valid, _softplus(dt_ref[...] + dtb_ref[...]), 0.0)
    da = dt * a_ref[...]
    tri = tri_ref[...]
    acs = sum(jnp.dot(tri, part, preferred_element_type=F32) for part in _split3(da))
    acs_t = acs.T
    dt_t = dt.T
    causal = coli <= rowi
    last = acs[L - 1:L, :]

    w_state = _expand_heads(dt * jnp.exp(last - acs), e_ref)
    dec_out = _expand_heads(jnp.exp(acs), e_ref)
    dec_rows = jnp.sum(et_ref[...] * jnp.exp(last), axis=-1, keepdims=True)
    xw = xs * w_state
    xs_b = xs.astype(BF16)

    half = D_INNER // N_BC_GROUPS
    hpg = N_HEADS_S // N_BC_GROUPS
    y_parts = []
    for g in range(N_BC_GROUPS):
        bm = xc[:, D_INNER + g * D_STATE:D_INNER + (g + 1) * D_STATE].astype(BF16)
        cm = xc[:, D_INNER + (N_BC_GROUPS + g) * D_STATE:
                D_INNER + (N_BC_GROUPS + g + 1) * D_STATE].astype(BF16)
        cb = lax.dot_general(cm, bm, (((1,), (1,)), ((), ())), preferred_element_type=F32)
        h_prev = h_scr[g * half:(g + 1) * half, :]
        y_off = lax.dot_general(cm, h_prev.astype(BF16), (((1,), (1,)), ((), ())),
                                preferred_element_type=F32)
        y_off = y_off * dec_out[:, g * half:(g + 1) * half]
        yd = []
        for r in range(0, hpg, 2):
            pair = []
            for hh in (r, r + 1):
                h = g * hpg + hh
                seg = jnp.broadcast_to(acs[:, h:h + 1], (L, L)) - acs_t[h:h + 1, :]
                lmat = jnp.where(causal, jnp.exp(jnp.where(causal, seg, 0.0)), 0.0)
                pair.append((cb * lmat * dt_t[h:h + 1, :]).astype(BF16))
            h0 = g * hpg + r
            xp = xs_b[:, h0 * SSM_HEAD_DIM:(h0 + 2) * SSM_HEAD_DIM]
            ya = jnp.dot(pair[0], xp, preferred_element_type=F32)
            yb = jnp.dot(pair[1], xp, preferred_element_type=F32)
            yd.append(jnp.where(coli < SSM_HEAD_DIM, ya, yb))
        y_parts.append(jnp.concatenate(yd, axis=1) + y_off)
        xw_t = xw[:, g * half:(g + 1) * half].T.astype(BF16)
        st = jnp.dot(xw_t, bm, preferred_element_type=F32)
        h_scr[g * half:(g + 1) * half, :] = dec_rows[g * half:(g + 1) * half, :] * h_prev + st
    y = jnp.concatenate(y_parts, axis=1)

    ssm_ref[...] = _gated_out(y, xs, z_ref[...], dskip_ref[...], normw_ref[...]).astype(BF16)

    @pl.when(c == pl.num_programs(0) - 1)
    def _():
        state_ref[...] = h_scr[...]


def _prompt_ssd(xbc, z, dt_raw, conv_w, conv_b, dtb, a_row, dskip_full, normw, e_mat, e_t, tri, n_t):
    row = lambda n: pl.BlockSpec((CHUNK, n), lambda c: (c, 0))
    const = lambda shp: pl.BlockSpec(shp, lambda c: (0, 0))
    return pl.pallas_call(
        _ssd_kernel,
        grid=(n_t // CHUNK,),
        in_specs=[row(XBC_DIM), row(D_INNER), row(LANES),
                  const((CONV_W, XBC_DIM)), const((1, XBC_DIM)), const((1, LANES)), const((1, LANES)),
                  const((1, D_INNER)), const((1, D_INNER)), const((LANES, D_INNER)),
                  const((D_INNER, LANES)), const((CHUNK, CHUNK))],
        out_specs=[row(D_INNER), pl.BlockSpec((D_INNER, D_STATE), lambda c: (0, 0))],
        out_shape=[jax.ShapeDtypeStruct((n_t, D_INNER), BF16),
                   jax.ShapeDtypeStruct((D_INNER, D_STATE), F32)],
        scratch_shapes=[pltpu.VMEM((8 + CHUNK, XBC_DIM), F32),
                        pltpu.VMEM((D_INNER, D_STATE), F32)],
        compiler_params=_cparams(("arbitrary",)),
        name="prompt_ssd",
    )(xbc, z, dt_raw, conv_w, conv_b, dtb, a_row, dskip_full, normw, e_mat, e_t, tri)


def _decode_attn_kernel(pt_ref, lq1, lk1, lq2, lk2, sub_ref, q_ref, kn_ref, vn_ref, *rest,
                        n_pages, lam_init):
    k_pages = rest[:n_pages]
    v_pages = rest[n_pages:2 * n_pages]
    o_ref = rest[2 * n_pages]
    del pt_ref
    lam = _diff_lambda(lq1, lk1, lq2, lk2, lam_init)
    nq = 2 * N_HEADS_A
    q8 = q_ref[...]
    lane = lax.broadcasted_iota(I32, (N_HEADS_A, LANES), 1)
    zero = jnp.zeros((N_HEADS_A, LANES), BF16)
    q16 = jnp.concatenate([jnp.where(lane < DK, q8, zero), jnp.where(lane < DK, zero, q8)], axis=0)

    pw = PAGE * N_KV_HEADS
    row_head = (lax.broadcasted_iota(I32, (nq, pw), 0) & (N_HEADS_A - 1)) >> 1
    col_head = lax.broadcasted_iota(I32, (nq, pw), 1) & (N_KV_HEADS - 1)
    own = row_head == col_head
    nt = (((1,), (1,)), ((), ()))
    s_pages = []
    for j in range(n_pages):
        kp = k_pages[j][...].astype(BF16)
        s = lax.dot_general(q16, kp, nt, preferred_element_type=F32)
        s_pages.append(jnp.where(own, s, NEG_INF))
    kn = kn_ref[...].astype(BF16)
    row_head_n = (lax.broadcasted_iota(I32, (nq, LANES), 0) & (N_HEADS_A - 1)) >> 1
    col_n = lax.broadcasted_iota(I32, (nq, LANES), 1)
    s_new = jnp.where(row_head_n == col_n, lax.dot_general(q16, kn, nt, preferred_element_type=F32), NEG_INF)

    m = jnp.max(s_new, axis=-1, keepdims=True)
    for s in s_pages:
        m = jnp.maximum(m, jnp.max(s, axis=-1, keepdims=True))
    p_new = jnp.exp2(s_new - m)
    l = jnp.sum(p_new, axis=-1, keepdims=True)
    acc = jnp.dot(p_new.astype(BF16), vn_ref[...].astype(BF16), preferred_element_type=F32)
    for j in range(n_pages):
        p = jnp.exp2(s_pages[j] - m)
        l = l + jnp.sum(p, axis=-1, keepdims=True)
        acc = acc + jnp.dot(p.astype(BF16), v_pages[j][...].astype(BF16), preferred_element_type=F32)
    o = acc / l
    res = o[:N_HEADS_A, :] - lam * o[N_HEADS_A:, :]
    o_ref[...] = _subln(res, sub_ref[...], lam_init).astype(BF16)


def _decode_attention(page_table, q_s, k_new8, v_new8, cache_k2, cache_v2, lq1, lk1, lq2, lk2, sub_w, lam_init):
    B, n_pages = page_table.shape
    pw = PAGE * N_KV_HEADS
    small = lambda n: pl.BlockSpec((1, n), lambda b, pt: (0, 0))
    per_b = pl.BlockSpec((None, 8, LANES), lambda b, pt: (b, 0, 0))
    new_tok = pl.BlockSpec((None, LANES, LANES), lambda b, pt: (b, 0, 0))
    page_spec = lambda j: pl.BlockSpec((pw, LANES), lambda b, pt: (pt[b, j], 0))
    grid_spec = pltpu.PrefetchScalarGridSpec(
        num_scalar_prefetch=1,
        grid=(B,),
        in_specs=[small(DK), small(DK), small(DK), small(DK), small(DV), per_b, new_tok, new_tok]
                 + [page_spec(j) for j in range(n_pages)] + [page_spec(j) for j in range(n_pages)],
        out_specs=per_b,
    )
    return pl.pallas_call(
        functools.partial(_decode_attn_kernel, n_pages=n_pages, lam_init=lam_init),
        grid_spec=grid_spec,
        out_shape=jax.ShapeDtypeStruct((B, 8, LANES), BF16),
        compiler_params=_cparams(("arbitrary",)),
        name="decode_attn",
    )(page_table, lq1, lk1, lq2, lk2, sub_w, q_s, k_new8, v_new8,
      *([cache_k2] * n_pages), *([cache_v2] * n_pages))


def _dec_pre_kernel(xbc_ref, s0_ref, s1_ref, s2_ref, dt_ref, cw_ref, cb_ref, dtb_ref, a_ref, e_ref,
                    xc_ref, xdt_ref, dect_ref):
    conv = (cb_ref[...] + cw_ref[0:1, :] * s0_ref[...] + cw_ref[1:2, :] * s1_ref[...]
            + cw_ref[2:3, :] * s2_ref[...] + cw_ref[3:4, :] * xbc_ref[...])
    xc = _silu(conv)
    xc_ref[...] = xc
    dt = _softplus(dt_ref[...] + dtb_ref[...])
    dec = jnp.exp(dt * a_ref[...])
    hi, mid, lo = _split3(dt)
    e = e_ref[...]
    dt_full = (jnp.dot(hi, e, preferred_element_type=F32) + jnp.dot(mid, e, preferred_element_type=F32)
               + jnp.dot(lo, e, preferred_element_type=F32))
    hi, mid, lo = _split3(dec)
    dec_full = (jnp.dot(hi, e, preferred_element_type=F32) + jnp.dot(mid, e, preferred_element_type=F32)
                + jnp.dot(lo, e, preferred_element_type=F32))
    xdt_ref[...] = (dt_full * xc[:, :D_INNER]).T
    dect_ref[...] = dec_full.T


def _dec_state_kernel(h_ref, xdt_ref, dect_ref, bm_ref, cm_ref, hout_ref, y_ref):
    i = pl.program_id(0)
    B = xdt_ref.shape[1]
    lane = lax.broadcasted_iota(I32, (D_INNER, B), 1)
    sub = lax.broadcasted_iota(I32, (TB, D_INNER // N_BC_GROUPS), 0)
    ones = jnp.ones((B, D_STATE), BF16)
    half = D_INNER // N_BC_GROUPS
    nt = (((1,), (1,)), ((), ()))
    y_acc = [jnp.zeros((TB, half), F32) for _ in range(N_BC_GROUPS)]
    for j in range(TB):
        b = i * TB + j
        sel = lane == b
        xcol = jnp.where(sel, xdt_ref[...], 0.0)
        dcol = jnp.where(sel, dect_ref[...], 0.0)
        dec = sum(jnp.dot(part, ones, preferred_element_type=F32) for part in _split3(dcol))
        xhi, xlo = _split2(xcol)
        hnew = dec * h_ref[j]
        for g in range(N_BC_GROUPS):
            bmat = bm_ref[:, g * D_STATE:(g + 1) * D_STATE].astype(BF16)
            upd = (jnp.dot(xhi[g * half:(g + 1) * half], bmat, preferred_element_type=F32)
                   + jnp.dot(xlo[g * half:(g + 1) * half], bmat, preferred_element_type=F32))
            hg = hnew[g * half:(g + 1) * half] + upd
            hout_ref[j, g * half:(g + 1) * half, :] = hg
            start = pl.multiple_of(i * TB, TB)
            c8 = cm_ref[pl.ds(start, TB), g * D_STATE:(g + 1) * D_STATE].astype(BF16)
            yj = lax.dot_general(c8, hg.astype(BF16), nt, preferred_element_type=F32)
            y_acc[g] = y_acc[g] + jnp.where(sub == j, yj, 0.0)
    y_ref[...] = jnp.concatenate(y_acc, axis=1)


def _dec_post_kernel(y_ref, xc_ref, z_ref, dskip_ref, normw_ref, o_ref):
    o_ref[...] = _gated_out(y_ref[...], xc_ref[:, :D_INNER], z_ref[...], dskip_ref[...],
                            normw_ref[...]).astype(BF16)


def _decode_ssm(xbc_s, sc0, sc1, sc2, dt_s, z_s, state, conv_w, conv_b, dtb, a_row, dskip_full, normw, e_mat):
    B = xbc_s.shape[0]
    xc, xdt_t, dec_t = pl.pallas_call(
        _dec_pre_kernel,
        out_shape=[jax.ShapeDtypeStruct((B, XBC_DIM), F32),
                   jax.ShapeDtypeStruct((D_INNER, B), F32),
                   jax.ShapeDtypeStruct((D_INNER, B), F32)],
        compiler_params=pltpu.CompilerParams(vmem_limit_bytes=VMEM_LIMIT),
        name="decode_ssm_pre",
    )(xbc_s, sc0, sc1, sc2, dt_s, conv_w, conv_b, dtb, a_row, e_mat)
    bm = xc[:, D_INNER:D_INNER + N_BC_GROUPS * D_STATE]
    cm = xc[:, D_INNER + N_BC_GROUPS * D_STATE:]
    const = lambda shp: pl.BlockSpec(shp, lambda i: (0,) * len(shp))
    state_spec = pl.BlockSpec((TB, D_INNER, D_STATE), lambda i: (i, 0, 0))
    h_new, y = pl.pallas_call(
        _dec_state_kernel,
        grid=(B // TB,),
        in_specs=[state_spec, const((D_INNER, B)), const((D_INNER, B)),
                  const((B, N_BC_GROUPS * D_STATE)), const((B, N_BC_GROUPS * D_STATE))],
        out_specs=[state_spec, pl.BlockSpec((TB, D_INNER), lambda i: (i, 0))],
        out_shape=[jax.ShapeDtypeStruct((B, D_INNER, D_STATE), F32),
                   jax.ShapeDtypeStruct((B, D_INNER), F32)],
        compiler_params=_cparams(("arbitrary",)),
        name="decode_ssm_state",
    )(state, xdt_t, dec_t, bm, cm)
    ssm = pl.pallas_call(
        _dec_post_kernel,
        out_shape=jax.ShapeDtypeStruct((B, D_INNER), BF16),
        compiler_params=pltpu.CompilerParams(vmem_limit_bytes=VMEM_LIMIT),
        name="decode_ssm_post",
    )(y, xc, z_s, dskip_full, normw)
    return ssm, h_new


def _outproj_kernel(x_ref, att_ref, ssm_ref, wa_ref, ws_ref, nw_ref, wrh_ref, wrl_ref,
                    h_ref, u_ref, lg_ref):
    acc = jnp.dot(att_ref[...], wa_ref[...], preferred_element_type=F32)
    acc = acc + jnp.dot(ssm_ref[...], ws_ref[...], preferred_element_type=F32)
    h = x_ref[...] + acc
    h_ref[...] = h
    ms = jnp.mean(h * h, axis=-1, keepdims=True)
    u = h * lax.rsqrt(ms + EPS) * nw_ref[...]
    u_ref[...] = u
    uh, ul = _split2(u)
    wh, wl = wrh_ref[...], wrl_ref[...]
    lg_ref[...] = (jnp.dot(uh, wh, preferred_element_type=F32) + jnp.dot(uh, wl, preferred_element_type=F32)
                   + jnp.dot(ul, wh, preferred_element_type=F32))


def _outproj(xall, att, ssm, wa, ws, norm_w, wr_hi, wr_lo):
    R = xall.shape[0]
    row = lambda n: pl.BlockSpec((TM, n), lambda i: (i, 0))
    return pl.pallas_call(
        _outproj_kernel,
        grid=(R // TM,),
        in_specs=[row(D_MODEL), row(N_HEADS_A * DV), row(D_INNER),
                  _resident(wa.shape), _resident(ws.shape), _resident((1, D_MODEL)),
                  _resident(wr_hi.shape), _resident(wr_lo.shape)],
        out_specs=[row(D_MODEL), row(D_MODEL), row(LANES)],
        out_shape=[jax.ShapeDtypeStruct((R, D_MODEL), F32),
                   jax.ShapeDtypeStruct((R, D_MODEL), F32),
                   jax.ShapeDtypeStruct((R, LANES), F32)],
        compiler_params=_cparams(("arbitrary",)),
        name="outproj",
    )(xall, att, ssm, wa, ws, norm_w, wr_hi, wr_lo)


def _gather_kernel(idx_ref, src_ref, out_ref, sem):
    def row_copy(src_row, dst_row):
        return pltpu.make_async_copy(src_ref.at[pl.ds(src_row, 1)], out_ref.at[pl.ds(dst_row, 1)], sem)

    def start(r, carry):
        row_copy(idx_ref[0, r], r).start()
        return carry

    lax.fori_loop(0, TG, start, 0, unroll=8)
    pltpu.make_async_copy(src_ref.at[pl.ds(0, TG)], out_ref, sem).wait()


def _gather_rows(src, idx):
    n = idx.shape[0]
    assert n % TG == 0
    width = src.shape[1]
    return pl.pallas_call(
        _gather_kernel,
        grid=(n // TG,),
        in_specs=[pl.BlockSpec((None, 1, TG), lambda i: (i, 0, 0), memory_space=pltpu.SMEM),
                  pl.BlockSpec(memory_space=pl.ANY)],
        out_specs=pl.BlockSpec((TG, width), lambda i: (i, 0)),
        out_shape=jax.ShapeDtypeStruct((n, width), src.dtype),
        scratch_shapes=[pltpu.SemaphoreType.DMA(())],
        compiler_params=_cparams(("arbitrary",)),
        name="gather_rows",
    )(idx.reshape(n // TG, 1, TG), src)


def _moe_kernel(be_ref, nv_ref, x_ref, wg_ref, wu_ref, wd_ref, o_ref):
    del be_ref
    i = pl.program_id(0)
    f = pl.program_id(1)

    @pl.when(i < nv_ref[0])
    def _():
        x = x_ref[...].astype(BF16)
        g = jnp.dot(x, wg_ref[...].astype(BF16), preferred_element_type=F32)
        u = jnp.dot(x, wu_ref[...].astype(BF16), preferred_element_type=F32)
        hmid = (_silu(g) * u).astype(BF16)
        y = jnp.dot(hmid, wd_ref[...].astype(BF16), preferred_element_type=F32)

        @pl.when(f == 0)
        def _():
            o_ref[...] = y

        @pl.when(f > 0)
        def _():
            o_ref[...] += y

    @pl.when((i >= nv_ref[0]) & (f == 0))
    def _():
        o_ref[...] = jnp.zeros(o_ref.shape, F32)


def _moe(block_e, n_valid, xs, w_gate, w_up, w_down):
    rows = xs.shape[0]
    nblk = rows // TMOE
    nf = D_FF // TF

    def blk(i, nv):
        return jnp.minimum(i, nv[0] - 1)

    def fidx(i, f, nv):
        return jnp.where(i < nv[0], f, nf - 1)

    grid_spec = pltpu.PrefetchScalarGridSpec(
        num_scalar_prefetch=2,
        grid=(nblk, nf),
        in_specs=[pl.BlockSpec((TMOE, D_MODEL), lambda i, f, be, nv: (blk(i, nv), 0)),
                  pl.BlockSpec((None, D_MODEL, TF), lambda i, f, be, nv: (be[blk(i, nv)], 0, fidx(i, f, nv))),
                  pl.BlockSpec((None, D_MODEL, TF), lambda i, f, be, nv: (be[blk(i, nv)], 0, fidx(i, f, nv))),
                  pl.BlockSpec((None, TF, D_MODEL), lambda i, f, be, nv: (be[blk(i, nv)], fidx(i, f, nv), 0))],
        out_specs=pl.BlockSpec((TMOE, D_MODEL), lambda i, f, be, nv: (i, 0)),
    )
    return pl.pallas_call(
        _moe_kernel,
        grid_spec=grid_spec,
        out_shape=jax.ShapeDtypeStruct((rows, D_MODEL), F32),
        compiler_params=_cparams(("arbitrary", "arbitrary")),
        name="moe_experts",
    )(block_e, n_valid, xs, w_gate, w_up, w_down)


def _final_kernel(h_ref, y0_ref, y1_ref, w_ref, nw_ref, o_ref):
    w = w_ref[...]
    moe = y0_ref[...] * w[:, 0:1] + y1_ref[...] * w[:, 1:2]
    h = h_ref[...] + moe
    ms = jnp.mean(h * h, axis=-1, keepdims=True)
    o_ref[...] = h * lax.rsqrt(ms + EPS) * nw_ref[...]


def _final(h1, y0, y1, wts, norm_w):
    R = h1.shape[0]
    row = lambda n: pl.BlockSpec((TM, n), lambda i: (i, 0))
    return pl.pallas_call(
        _final_kernel,
        grid=(R // TM,),
        in_specs=[row(D_MODEL), row(D_MODEL), row(D_MODEL), row(TOP_K),
                  pl.BlockSpec((1, D_MODEL), lambda i: (0, 0))],
        out_specs=row(D_MODEL),
        out_shape=jax.ShapeDtypeStruct((R, D_MODEL), F32),
        compiler_params=_cparams(("arbitrary",)),
        name="final_norm",
    )(h1, y0, y1, wts, norm_w)


def _route(logits, row_valid):
    R = logits.shape[0]
    gl = logits[:, :N_EGROUPS]
    pg = jax.nn.softmax(gl, axis=-1)
    g = jnp.argmax(gl, axis=-1).astype(I32)
    el = logits[:, N_EGROUPS:N_EGROUPS + N_EXPERTS].reshape(R, N_EGROUPS, EXPERTS_PER_GROUP)
    el_g = jnp.take_along_axis(el, g[:, None, None], axis=1)[:, 0]
    top_v, top_i = lax.top_k(jax.nn.softmax(el_g, axis=-1), TOP_K)
    wts = top_v / jnp.sum(top_v, axis=-1, keepdims=True) * jnp.take_along_axis(pg, g[:, None], axis=1)
    eid = g[:, None] * EXPERTS_PER_GROUP + top_i.astype(I32)
    eid = jnp.where(row_valid[:, None], eid, N_EXPERTS)
    wts = jnp.where(row_valid[:, None], wts, 0.0)

    flat_e = eid.reshape(-1)
    onehot = (flat_e[:, None] == jnp.arange(N_EXPERTS, dtype=I32)[None, :]).astype(I32)
    rank = jnp.sum((jnp.cumsum(onehot, axis=0) - onehot) * onehot, axis=1)
    counts = jnp.sum(onehot, axis=0)
    padded = (counts + TMOE - 1) // TMOE * TMOE
    ends = jnp.cumsum(padded)
    pstart = ends - padded
    is_real = flat_e < N_EXPERTS
    dest = jnp.where(is_real, pstart[jnp.minimum(flat_e, N_EXPERTS - 1)] + rank, 0)

    a_max = TOP_K * R
    nblk = (a_max + N_EXPERTS * (TMOE - 1) + TMOE - 1) // TMOE
    rows = nblk * TMOE
    rows = (rows + TG - 1) // TG * TG
    nblk = rows // TMOE
    flat_tok = jnp.repeat(jnp.arange(R, dtype=I32), TOP_K)
    buf_tok = jnp.zeros((rows,), I32).at[jnp.where(is_real, dest, rows)].set(flat_tok, mode="drop")
    n_valid = (ends[-1] // TMOE).astype(I32)
    block_e = jnp.minimum(jnp.searchsorted(ends, jnp.arange(nblk, dtype=I32) * TMOE, side="right"),
                          N_EXPERTS - 1).astype(I32)
    return wts, dest.reshape(R, TOP_K), buf_tok, block_e, n_valid.reshape(1)


def kernel(x_prompt, x_sample, cache_k, cache_v, state_conv, state_ssm, page_table, meta_tokens, norm_mix_w,
           w_in, lambda_q1, lambda_k1, lambda_q2, lambda_k2, subln_w, conv_w, conv_b, dt_bias, a_log, d_skip,
           ssm_norm_w, w_out, norm_ffn_w, w_router_group, w_router_expert, w_gate, w_up, w_down, final_norm_w):
    depth = w_in.shape[0]
    assert depth == 1 and x_prompt.shape[0] == 1 and x_sample.shape[1] == 1
    seq = x_prompt.shape[1]
    B = x_sample.shape[0]
    n_pages = page_table.shape[1]
    n_phys = cache_k.shape[1]
    past = n_pages * cache_k.shape[2]
    assert cache_k.shape[2] == PAGE and seq % CHUNK == 0 and B == LANES and (FRONT + seq + B) % TM == 0
    n_t = FRONT + seq
    R = n_t + B
    l = 0
    lam_init = 0.8 - 0.6 * math.exp(-0.3 * l)

    xall = jnp.concatenate([jnp.zeros((PAD_ROWS, D_MODEL), F32), meta_tokens.astype(F32),
                            x_prompt[0], x_sample[:, 0]], axis=0)
    w = w_in[l]
    wq = w[:, :OFF_K].astype(BF16)
    wk = w[:, OFF_K:OFF_V].astype(BF16)
    wv = w[:, OFF_V:OFF_Z].astype(BF16)
    wz = w[:, OFF_Z:OFF_XBC].astype(BF16)
    wx = w[:, OFF_XBC:OFF_DT].astype(BF16)
    wdt = jnp.pad(w[:, OFF_DT:], ((0, 0), (0, LANES - N_HEADS_S))).astype(BF16)

    pos = jnp.concatenate([jnp.maximum(jnp.arange(n_t, dtype=I32) - PAD_ROWS, 0),
                           jnp.full((B,), past, I32)])
    inv = ROPE_THETA ** (-jnp.arange(0, ROT_DIM, 2, dtype=F32) / ROT_DIM)
    ang = pos.astype(F32)[:, None] * inv[None, :]
    cos, sin = jnp.cos(ang), jnp.sin(ang)
    half = ROT_DIM // 2
    ones = jnp.ones((R, DK - ROT_DIM), F32)
    zeros = jnp.zeros((R, DK - ROT_DIM), F32)
    zh = jnp.zeros((R, half), F32)
    ctab = jnp.tile(jnp.concatenate([cos, cos, ones], axis=1), (1, 2))
    s1tab = jnp.tile(jnp.concatenate([-sin, zh, zeros], axis=1), (1, 2))
    s2tab = jnp.tile(jnp.concatenate([zh, sin, zeros], axis=1), (1, 2))

    row1 = lambda v: v.reshape(1, -1).astype(F32)
    pad_heads = lambda v: jnp.pad(v.astype(F32), (0, LANES - N_HEADS_S)).reshape(1, LANES)
    a_row = pad_heads(-jnp.exp(a_log[l].astype(F32)))
    dtb = pad_heads(dt_bias[l])
    dskip_full = jnp.repeat(d_skip[l].astype(F32), SSM_HEAD_DIM).reshape(1, D_INNER)
    normw = row1(ssm_norm_w[l])
    head_of_col = jnp.arange(D_INNER, dtype=I32) // SSM_HEAD_DIM
    e_mat = (jnp.arange(LANES, dtype=I32)[:, None] == head_of_col[None, :]).astype(BF16)
    tri = (jnp.arange(CHUNK)[:, None] >= jnp.arange(CHUNK)[None, :]).astype(BF16)
    lq1, lk1, lq2, lk2 = (row1(v[l]) for v in (lambda_q1, lambda_k1, lambda_q2, lambda_k2))
    sub_w = row1(subln_w[l])

    q, k, v, kb, vb, z, xbc, dt_raw = _inproj(xall, row1(norm_mix_w[l]), wq, wk, wv, wz, wx, wdt,
                                              ctab, s1tab, s2tab)

    att_all = _prompt_attention(q, kb, vb, lq1, lk1, lq2, lk2, sub_w, lam_init)
    ssm_p, state_p = _prompt_ssd(xbc, z, dt_raw, conv_w[l].astype(F32), row1(conv_b[l]), dtb, a_row,
                                 dskip_full, normw, e_mat, e_mat.T.astype(F32), tri, n_t)

    k_s, v_s = k[n_t:], v[n_t:]
    pad8 = lambda a: jnp.pad(a.reshape(B, N_KV_HEADS, LANES), ((0, 0), (0, LANES - N_KV_HEADS), (0, 0)))
    att_s = _decode_attention(page_table, q[n_t:].reshape(B, N_HEADS_A, LANES), pad8(k_s), pad8(v_s),
                              cache_k[l].reshape(n_phys * PAGE * N_KV_HEADS, LANES),
                              cache_v[l].reshape(n_phys * PAGE * N_KV_HEADS, LANES),
                              lq1, lk1, lq2, lk2, sub_w, lam_init)
    sc = state_conv[l]
    xbc_s = xbc[n_t:]
    ssm_s, state_s = _decode_ssm(xbc_s, sc[:, 0], sc[:, 1], sc[:, 2], dt_raw[n_t:], z[n_t:],
                                 state_ssm[l].reshape(B, D_INNER, D_STATE), conv_w[l].astype(F32),
                                 row1(conv_b[l]), dtb, a_row, dskip_full, normw, e_mat)

    att = lax.dynamic_update_slice(att_all, att_s.reshape(B, N_HEADS_A * DV), (n_t, 0))
    ssm = jnp.concatenate([ssm_p, ssm_s], axis=0)

    wo = w_out[l].astype(BF16)
    wr = jnp.pad(jnp.concatenate([w_router_group[l], w_router_expert[l]], axis=1).astype(F32),
                 ((0, 0), (0, LANES - N_EGROUPS - N_EXPERTS)))
    wr_hi = wr.astype(BF16)
    wr_lo = (wr - wr_hi.astype(F32)).astype(BF16)
    h1, u2, logits = _outproj(xall, att, ssm, wo[:N_HEADS_A * DV], wo[N_HEADS_A * DV:],
                              row1(norm_ffn_w[l]), wr_hi, wr_lo)

    row_valid = jnp.arange(R, dtype=I32) >= PAD_ROWS
    wts, dest, buf_tok, block_e, n_valid = _route(logits, row_valid)
    xs = _gather_rows(u2, buf_tok)
    yb = _moe(block_e, n_valid, xs, w_gate[l], w_up[l], w_down[l])
    y0 = _gather_rows(yb, _pad_to(dest[:, 0], TG))[:R]
    y1 = _gather_rows(yb, _pad_to(dest[:, 1], TG))[:R]
    y = _final(h1, y0, y1, wts, row1(final_norm_w))

    t_real = N_META + seq
    y_prompt = y[FRONT:n_t].reshape(1, seq, D_MODEL)
    y_sample = y[n_t:].reshape(B, 1, D_MODEL)
    k_prompt = k[PAD_ROWS:n_t].reshape(1, 1, t_real, N_KV_HEADS, 2 * DK)
    v_prompt = v[PAD_ROWS:n_t].reshape(1, 1, t_real, N_KV_HEADS, DV)
    conv_prompt = xbc[n_t - (CONV_W - 1):n_t].reshape(1, 1, CONV_W - 1, XBC_DIM)
    ssm_prompt = state_p.reshape(1, 1, N_HEADS_S, SSM_HEAD_DIM, D_STATE)
    k_sample = k_s.reshape(1, B, 1, N_KV_HEADS, 2 * DK)
    v_sample = v_s.reshape(1, B, 1, N_KV_HEADS, DV)
    conv_sample = jnp.stack([sc[:, 1], sc[:, 2], xbc_s], axis=1)[None]
    ssm_sample = state_s.reshape(1, B, N_HEADS_S, SSM_HEAD_DIM, D_STATE)
    return (y_prompt, y_sample, k_prompt, v_prompt, conv_prompt, ssm_prompt,
            k_sample, v_sample, conv_sample, ssm_sample)


def _pad_to(idx, mult):
    n = idx.shape[0]
    return jnp.pad(idx, (0, (-n) % mult))
```

```python
import functools
import math

import jax
import jax.numpy as jnp
from jax import lax
from jax.experimental import pallas as pl
from jax.experimental.pallas import tpu as pltpu

F32 = jnp.float32
BF16 = jnp.bfloat16
I32 = jnp.int32

D_MODEL = 2048
N_META = 16
EPS = 1e-6
DV = 128
DK = 64
N_HEADS_A = 8
N_KV_HEADS = 4
ROT_DIM = 16
ROPE_THETA = 500000.0
NEG_INF = -1e30
LOG2E = 1.4426950408889634
D_INNER = 1024
SSM_HEAD_DIM = 64
N_HEADS_S = 16
N_BC_GROUPS = 2
D_STATE = 128
CONV_W = 4
XBC_DIM = D_INNER + 2 * N_BC_GROUPS * D_STATE
CHUNK = 128
Q_DIM = N_HEADS_A * 2 * DK
K_DIM = N_KV_HEADS * 2 * DK
V_DIM = N_KV_HEADS * DV
OFF_K = Q_DIM
OFF_V = OFF_K + K_DIM
OFF_Z = OFF_V + V_DIM
OFF_XBC = OFF_Z + D_INNER
OFF_DT = OFF_XBC + XBC_DIM
IN_DIM = OFF_DT + N_HEADS_S
N_EGROUPS = 4
EXPERTS_PER_GROUP = 8
N_EXPERTS = N_EGROUPS * EXPERTS_PER_GROUP
TOP_K = 2
D_FF = 1024
PAGE = 128

LANES = 128
VMEM_LIMIT = 56 * 1024 * 1024

FRONT = CHUNK
PAD_ROWS = FRONT - N_META
TM = 256
TQ = 256
TK = 512
TKF = 1024
TB = 8
TMOE = 512
TF = 256
TG = 256


def _cparams(sem):
    return pltpu.CompilerParams(dimension_semantics=sem, vmem_limit_bytes=VMEM_LIMIT)


def _resident(shape):
    nd = len(shape)
    return pl.BlockSpec(shape, lambda *_: (0,) * nd, pipeline_mode=pl.Buffered(1))


def _split2(x):
    hi = x.astype(BF16)
    lo = (x - hi.astype(F32)).astype(BF16)
    return hi, lo


def _split3(x):
    hi = x.astype(BF16)
    r = x - hi.astype(F32)
    mid = r.astype(BF16)
    lo = (r - mid.astype(F32)).astype(BF16)
    return hi, mid, lo


def _silu(x):
    return x * (1.0 / (1.0 + jnp.exp(-x)))


def _softplus(x):
    return jnp.maximum(x, 0.0) + jnp.log(1.0 + jnp.exp(-jnp.abs(x)))


def _inproj_kernel(x_ref, nw_ref, wq_ref, wk_ref, wv_ref, wz_ref, wx_ref, wdt_ref,
                   c_ref, s1_ref, s2_ref,
                   q_ref, k_ref, v_ref, kb_ref, vb_ref, z_ref, xbc_ref, dt_ref):
    x = x_ref[...]
    ms = jnp.mean(x * x, axis=-1, keepdims=True)
    u = (x * lax.rsqrt(ms + EPS) * nw_ref[...]).astype(BF16)
    c, s1, s2 = c_ref[...], s1_ref[...], s2_ref[...]

    def rope(p):
        outs = []
        for h in range(p.shape[1] // LANES):
            xh = p[:, h * LANES:(h + 1) * LANES]
            outs.append(xh * c + pltpu.roll(xh, LANES - ROT_DIM // 2, 1) * s1
                        + pltpu.roll(xh, ROT_DIM // 2, 1) * s2)
        return jnp.concatenate(outs, axis=1)

    q = rope(jnp.dot(u, wq_ref[...], preferred_element_type=F32))
    q_ref[...] = (q * (DK ** -0.5 * LOG2E)).astype(BF16)
    k = rope(jnp.dot(u, wk_ref[...], preferred_element_type=F32))
    k_ref[...] = k
    kb_ref[...] = k.astype(BF16)
    v = jnp.dot(u, wv_ref[...], preferred_element_type=F32)
    v_ref[...] = v
    vb = v.astype(BF16)
    ones = jnp.ones((vb.shape[0], DV), BF16)
    vb_ref[...] = jnp.concatenate(
        [piece for g in range(N_KV_HEADS) for piece in (vb[:, g * DV:(g + 1) * DV], ones)], axis=1)
    z_ref[...] = jnp.dot(u, wz_ref[...], preferred_element_type=F32)
    xbc_ref[...] = jnp.dot(u, wx_ref[...], preferred_element_type=F32)
    dt_ref[...] = jnp.dot(u, wdt_ref[...], preferred_element_type=F32)


def _inproj(xall, norm_w, wq, wk, wv, wz, wx, wdt, ctab, s1tab, s2tab):
    R = xall.shape[0]
    row = lambda n: pl.BlockSpec((TM, n), lambda i: (i, 0))
    return pl.pallas_call(
        _inproj_kernel,
        grid=(R // TM,),
        in_specs=[row(D_MODEL), _resident((1, D_MODEL)),
                  _resident(wq.shape), _resident(wk.shape), _resident(wv.shape),
                  _resident(wz.shape), _resident(wx.shape), _resident(wdt.shape),
                  row(LANES), row(LANES), row(LANES)],
        out_specs=[row(Q_DIM), row(K_DIM), row(V_DIM), row(K_DIM), row(2 * V_DIM),
                   row(D_INNER), row(XBC_DIM), row(LANES)],
        out_shape=[jax.ShapeDtypeStruct((R, Q_DIM), BF16),
                   jax.ShapeDtypeStruct((R, K_DIM), F32),
                   jax.ShapeDtypeStruct((R, V_DIM), F32),
                   jax.ShapeDtypeStruct((R, K_DIM), BF16),
                   jax.ShapeDtypeStruct((R, 2 * V_DIM), BF16),
                   jax.ShapeDtypeStruct((R, D_INNER), F32),
                   jax.ShapeDtypeStruct((R, XBC_DIM), F32),
                   jax.ShapeDtypeStruct((R, LANES), F32)],
        compiler_params=_cparams(("arbitrary",)),
        name="inproj",
    )(xall, norm_w, wq, wk, wv, wz, wx, wdt, ctab, s1tab, s2tab)


def _diff_lambda(lq1, lk1, lq2, lk2, lam_init):
    a = jnp.sum(lq1[...] * lk1[...], axis=-1, keepdims=True)
    b = jnp.sum(lq2[...] * lk2[...], axis=-1, keepdims=True)
    return jnp.exp(a) - jnp.exp(b) + lam_init


def _subln(o, sub_w, lam_init):
    ms = jnp.mean(o * o, axis=-1, keepdims=True)
    return (o * lax.rsqrt(ms + EPS) * sub_w) * (1.0 - lam_init)


def _attn_kernel(lq1, lk1, lq2, lk2, sub_ref, q_ref, k_ref, ve_ref, o_ref,
                 qs_scr, m_scr, acc_scr, *, lam_init, n_rows):
    qi = pl.program_id(1)
    lam = _diff_lambda(lq1, lk1, lq2, lk2, lam_init)

    q = q_ref[...]
    lane = lax.broadcasted_iota(I32, (TQ, LANES), 1)
    first = lane < DK
    zero = jnp.zeros((TQ, LANES), BF16)
    for hh in range(2):
        qh = q[:, hh * LANES:(hh + 1) * LANES]
        qs_scr[(2 * hh) * TQ:(2 * hh + 1) * TQ, :] = jnp.where(first, qh, zero)
        qs_scr[(2 * hh + 1) * TQ:(2 * hh + 2) * TQ, :] = jnp.where(first, zero, qh)
    m_scr[...] = jnp.full(m_scr.shape, NEG_INF, F32)
    acc_scr[...] = jnp.zeros(acc_scr.shape, F32)

    nt = (((1,), (1,)), ((), ()))

    def process(start, width, keep):
        for part in range(4):
            rows = slice(part * TQ, (part + 1) * TQ)
            s = lax.dot_general(qs_scr[rows, :], k_ref[pl.ds(start, width), :], nt,
                                preferred_element_type=F32)
            if keep is not None:
                s = jnp.where(keep, s, NEG_INF)
            m_prev = m_scr[rows, :]
            m_new = jnp.maximum(m_prev, jnp.max(s, axis=-1, keepdims=True))
            alpha = jnp.exp2(m_prev - m_new)
            p = jnp.concatenate([jnp.exp2(s[:, t * LANES:(t + 1) * LANES] - m_new)
                                 for t in range(width // LANES)], axis=1).astype(BF16)
            pv = jnp.dot(p, ve_ref[pl.ds(start, width), :], preferred_element_type=F32)
            acc_scr[rows, :LANES] = alpha * acc_scr[rows, :LANES] + pv[:, :LANES]
            acc_scr[rows, LANES:] = alpha * acc_scr[rows, LANES:] + pv[:, LANES:]
            m_scr[rows, :] = m_new

    q_lo = qi * TQ
    n_full = jnp.maximum(q_lo - PAD_ROWS, 0) // TKF
    tail_lo = PAD_ROWS + n_full * TKF
    n_tail = (q_lo + TQ - tail_lo + TK - 1) // TK

    def full(c, carry):
        process(pl.multiple_of(PAD_ROWS + c * TKF, 16), TKF, None)
        return carry

    lax.fori_loop(0, n_full, full, 0)

    def tail(c, carry):
        nominal = tail_lo + c * TK
        start = pl.multiple_of(jnp.minimum(nominal, n_rows - TK), 16)
        col = lax.broadcasted_iota(I32, (TQ, TK), 1)
        qpos = q_lo + lax.broadcasted_iota(I32, (TQ, TK), 0)
        process(start, TK, (col <= qpos - start) & (col >= nominal - start))
        return carry

    lax.fori_loop(0, n_tail, tail, 0)

    acc = acc_scr[...]
    o = acc[:, :LANES] / acc[:, LANES:]
    sub_w = sub_ref[...]
    for hh in range(2):
        o1 = o[(2 * hh) * TQ:(2 * hh + 1) * TQ, :]
        o2 = o[(2 * hh + 1) * TQ:(2 * hh + 2) * TQ, :]
        o_ref[:, hh * LANES:(hh + 1) * LANES] = _subln(o1 - lam * o2, sub_w, lam_init).astype(BF16)


def _prompt_attention(q, kb, vbe, lq1, lk1, lq2, lk2, sub_w, lam_init):
    R = q.shape[0]
    assert R % TQ == 0 and (R - TK) % 16 == 0 and R >= TK
    small = lambda n: pl.BlockSpec((1, n), lambda g, i: (0, 0))
    return pl.pallas_call(
        functools.partial(_attn_kernel, lam_init=lam_init, n_rows=R),
        grid=(N_KV_HEADS, R // TQ),
        in_specs=[small(DK), small(DK), small(DK), small(DK), small(DV),
                  pl.BlockSpec((TQ, 2 * LANES), lambda g, i: (i, g)),
                  pl.BlockSpec((R, LANES), lambda g, i: (0, g)),
                  pl.BlockSpec((R, 2 * LANES), lambda g, i: (0, g))],
        out_specs=pl.BlockSpec((TQ, 2 * LANES), lambda g, i: (i, g)),
        out_shape=jax.ShapeDtypeStruct((R, N_HEADS_A * DV), BF16),
        scratch_shapes=[pltpu.VMEM((4 * TQ, LANES), BF16),
                        pltpu.VMEM((4 * TQ, LANES), F32),
                        pltpu.VMEM((4 * TQ, 2 * LANES), F32)],
        compiler_params=_cparams(("arbitrary", "arbitrary")),
        name="prompt_attn",
    )(lq1, lk1, lq2, lk2, sub_w, q, kb, vbe)


def _expand_heads(x, e_ref):
    hi, lo = _split2(x)
    e = e_ref[...]
    return (jnp.dot(hi, e, preferred_element_type=F32)
            + jnp.dot(lo, e, preferred_element_type=F32))


def _gated_out(y, xs, z, dskip_full, normw):
    y = (y + dskip_full * xs) * _silu(z)
    half = D_INNER // N_BC_GROUPS
    outs = []
    for g in range(N_BC_GROUPS):
        yg = y[:, g * half:(g + 1) * half]
        ms = jnp.mean(yg * yg, axis=-1, keepdims=True)
        outs.append(yg * lax.rsqrt(ms + EPS))
    return jnp.concatenate(outs, axis=1) * normw


def _ssd_kernel(xbc_ref, z_ref, dt_ref, cw_ref, cb_ref, dtb_ref, a_ref, dskip_ref, normw_ref,
                e_ref, et_ref, tri_ref, ssm_ref, state_ref, ext_scr, h_scr):
    c = pl.program_id(0)
    L = CHUNK

    @pl.when(c == 0)
    def _():
        ext_scr[0:8, :] = jnp.zeros((8, XBC_DIM), F32)
        h_scr[...] = jnp.zeros(h_scr.shape, F32)

    x_new = xbc_ref[...]
    ext_scr[8:8 + L, :] = x_new
    conv = cb_ref[...] + cw_ref[CONV_W - 1:CONV_W, :] * x_new
    for j in range(CONV_W - 1):
        sh = CONV_W - 1 - j
        conv = conv + cw_ref[j:j + 1, :] * ext_scr[8 - sh:8 - sh + L, :]
    ext_scr[0:8, :] = x_new[L - 8:L, :]
    xc = _silu(conv)
    xs = xc[:, :D_INNER]

    rowi = lax.broadcasted_iota(I32, (L, LANES), 0)
    coli = lax.broadcasted_iota(I32, (L, LANES), 1)
    valid = (c * L + rowi) >= PAD_ROWS
    dt = jnp.where(valid, _softplus(dt_ref[...] + dtb_ref[...]), 0.0)
    da = dt * a_ref[...]
    tri = tri_ref[...]
    acs = sum(jnp.dot(tri, part, preferred_element_type=F32) for part in _split3(da))
    acs_t = acs.T
    dt_t = dt.T
    causal = coli <= rowi
    last = acs[L - 1:L, :]

    w_state = _expand_heads(dt * jnp.exp(last - acs), e_ref)
    dec_out = _expand_heads(jnp.exp(acs), e_ref)
    dec_rows = jnp.sum(et_ref[...] * jnp.exp(last), axis=-1, keepdims=True)
    xw = xs * w_state
    xs_b = xs.astype(BF16)

    half = D_INNER // N_BC_GROUPS
    hpg = N_HEADS_S // N_BC_GROUPS
    y_parts = []
    for g in range(N_BC_GROUPS):
        bm = xc[:, D_INNER + g * D_STATE:D_INNER + (g + 1) * D_STATE].astype(BF16)
        cm = xc[:, D_INNER + (N_BC_GROUPS + g) * D_STATE:
                D_INNER + (N_BC_GROUPS + g + 1) * D_STATE].astype(BF16)
        cb = lax.dot_general(cm, bm, (((1,), (1,)), ((), ())), preferred_element_type=F32)
        h_prev = h_scr[g * half:(g + 1) * half, :]
        y_off = lax.dot_general(cm, h_prev.astype(BF16), (((1,), (1,)), ((), ())),
                                preferred_element_type=F32)
        y_off = y_off * dec_out[:, g * half:(g + 1) * half]
        yd = []
        for r in range(0, hpg, 2):
            pair = []
            for hh in (r, r + 1):
                h = g * hpg + hh
                seg = jnp.broadcast_to(acs[:, h:h + 1], (L, L)) - acs_t[h:h + 1, :]
                lmat = jnp.where(causal, jnp.exp(jnp.where(causal, seg, 0.0)), 0.0)
                pair.append((cb * lmat * dt_t[h:h + 1, :]).astype(BF16))
            h0 = g * hpg + r
            xp = xs_b[:, h0 * SSM_HEAD_DIM:(h0 + 2) * SSM_HEAD_DIM]
            ya = jnp.dot(pair[0], xp, preferred_element_type=F32)
            yb = jnp.dot(pair[1], xp, preferred_element_type=F32)
            yd.append(jnp.where(coli < SSM_HEAD_DIM, ya, yb))
        y_parts.append(jnp.concatenate(yd, axis=1) + y_off)
        xw_t = xw[:, g * half:(g + 1) * half].T.astype(BF16)
        st = jnp.dot(xw_t, bm, preferred_element_type=F32)
        h_scr[g * half:(g + 1) * half, :] = dec_rows[g * half:(g + 1) * half, :] * h_prev + st
    y = jnp.concatenate(y_parts, axis=1)

    ssm_ref[...] = _gated_out(y, xs, z_ref[...], dskip_ref[...], normw_ref[...]).astype(BF16)

    @pl.when(c == pl.num_programs(0) - 1)
    def _():
        state_ref[...] = h_scr[...]


def _prompt_ssd(xbc, z, dt_raw, conv_w, conv_b, dtb, a_row, dskip_full, normw, e_mat, e_t, tri, n_t):
    row = lambda n: pl.BlockSpec((CHUNK, n), lambda c: (c, 0))
    const = lambda shp: pl.BlockSpec(shp, lambda c: (0, 0))
    return pl.pallas_call(
        _ssd_kernel,
        grid=(n_t // CHUNK,),
        in_specs=[row(XBC_DIM), row(D_INNER), row(LANES),
                  const((CONV_W, XBC_DIM)), const((1, XBC_DIM)), const((1, LANES)), const((1, LANES)),
                  const((1, D_INNER)), const((1, D_INNER)), const((LANES, D_INNER)),
                  const((D_INNER, LANES)), const((CHUNK, CHUNK))],
        out_specs=[row(D_INNER), pl.BlockSpec((D_INNER, D_STATE), lambda c: (0, 0))],
        out_shape=[jax.ShapeDtypeStruct((n_t, D_INNER), BF16),
                   jax.ShapeDtypeStruct((D_INNER, D_STATE), F32)],
        scratch_shapes=[pltpu.VMEM((8 + CHUNK, XBC_DIM), F32),
                        pltpu.VMEM((D_INNER, D_STATE), F32)],
        compiler_params=_cparams(("arbitrary",)),
        name="prompt_ssd",
    )(xbc, z, dt_raw, conv_w, conv_b, dtb, a_row, dskip_full, normw, e_mat, e_t, tri)


def _decode_attn_kernel(pt_ref, lq1, lk1, lq2, lk2, sub_ref, q_ref, kn_ref, vn_ref, *rest,
                        n_pages, lam_init):
    k_pages = rest[:n_pages]
    v_pages = rest[n_pages:2 * n_pages]
    o_ref = rest[2 * n_pages]
    del pt_ref
    lam = _diff_lambda(lq1, lk1, lq2, lk2, lam_init)
    nq = 2 * N_HEADS_A
    q8 = q_ref[...]
    lane = lax.broadcasted_iota(I32, (N_HEADS_A, LANES), 1)
    zero = jnp.zeros((N_HEADS_A, LANES), BF16)
    q16 = jnp.concatenate([jnp.where(lane < DK, q8, zero), jnp.where(lane < DK, zero, q8)], axis=0)

    pw = PAGE * N_KV_HEADS
    row_head = (lax.broadcasted_iota(I32, (nq, pw), 0) & (N_HEADS_A - 1)) >> 1
    col_head = lax.broadcasted_iota(I32, (nq, pw), 1) & (N_KV_HEADS - 1)
    own = row_head == col_head
    nt = (((1,), (1,)), ((), ()))
    s_pages = []
    for j in range(n_pages):
        kp = k_pages[j][...].astype(BF16)
        s = lax.dot_general(q16, kp, nt, preferred_element_type=F32)
        s_pages.append(jnp.where(own, s, NEG_INF))
    kn = kn_ref[...].astype(BF16)
    row_head_n = (lax.broadcasted_iota(I32, (nq, LANES), 0) & (N_HEADS_A - 1)) >> 1
    col_n = lax.broadcasted_iota(I32, (nq, LANES), 1)
    s_new = jnp.where(row_head_n == col_n, lax.dot_general(q16, kn, nt, preferred_element_type=F32), NEG_INF)

    m = jnp.max(s_new, axis=-1, keepdims=True)
    for s in s_pages:
        m = jnp.maximum(m, jnp.max(s, axis=-1, keepdims=True))
    p_new = jnp.exp2(s_new - m)
    l = jnp.sum(p_new, axis=-1, keepdims=True)
    acc = jnp.dot(p_new.astype(BF16), vn_ref[...].astype(BF16), preferred_element_type=F32)
    for j in range(n_pages):
        p = jnp.exp2(s_pages[j] - m)
        l = l + jnp.sum(p, axis=-1, keepdims=True)
        acc = acc + jnp.dot(p.astype(BF16), v_pages[j][...].astype(BF16), preferred_element_type=F32)
    o = acc / l
    res = o[:N_HEADS_A, :] - lam * o[N_HEADS_A:, :]
    o_ref[...] = _subln(res, sub_ref[...], lam_init).astype(BF16)


def _decode_attention(page_table, q_s, k_new8, v_new8, cache_k2, cache_v2, lq1, lk1, lq2, lk2, sub_w, lam_init):
    B, n_pages = page_table.shape
    pw = PAGE * N_KV_HEADS
    small = lambda n: pl.BlockSpec((1, n), lambda b, pt: (0, 0))
    per_b = pl.BlockSpec((None, 8, LANES), lambda b, pt: (b, 0, 0))
    new_tok = pl.BlockSpec((None, LANES, LANES), lambda b, pt: (b, 0, 0))
    page_spec = lambda j: pl.BlockSpec((pw, LANES), lambda b, pt: (pt[b, j], 0))
    grid_spec = pltpu.PrefetchScalarGridSpec(
        num_scalar_prefetch=1,
        grid=(B,),
        in_specs=[small(DK), small(DK), small(DK), small(DK), small(DV), per_b, new_tok, new_tok]
                 + [page_spec(j) for j in range(n_pages)] + [page_spec(j) for j in range(n_pages)],
        out_specs=per_b,
    )
    return pl.pallas_call(
        functools.partial(_decode_attn_kernel, n_pages=n_pages, lam_init=lam_init),
        grid_spec=grid_spec,
        out_shape=jax.ShapeDtypeStruct((B, 8, LANES), BF16),
        compiler_params=_cparams(("arbitrary",)),
        name="decode_attn",
    )(page_table, lq1, lk1, lq2, lk2, sub_w, q_s, k_new8, v_new8,
      *([cache_k2] * n_pages), *([cache_v2] * n_pages))


def _dec_pre_kernel(xbc_ref, s0_ref, s1_ref, s2_ref, dt_ref, cw_ref, cb_ref, dtb_ref, a_ref, e_ref,
                    xc_ref, xdt_ref, dect_ref):
    conv = (cb_ref[...] + cw_ref[0:1, :] * s0_ref[...] + cw_ref[1:2, :] * s1_ref[...]
            + cw_ref[2:3, :] * s2_ref[...] + cw_ref[3:4, :] * xbc_ref[...])
    xc = _silu(conv)
    xc_ref[...] = xc
    dt = _softplus(dt_ref[...] + dtb_ref[...])
    dec = jnp.exp(dt * a_ref[...])
    hi, mid, lo = _split3(dt)
    e = e_ref[...]
    dt_full = (jnp.dot(hi, e, preferred_element_type=F32) + jnp.dot(mid, e, preferred_element_type=F32)
               + jnp.dot(lo, e, preferred_element_type=F32))
    hi, mid, lo = _split3(dec)
    dec_full = (jnp.dot(hi, e, preferred_element_type=F32) + jnp.dot(mid, e, preferred_element_type=F32)
                + jnp.dot(lo, e, preferred_element_type=F32))
    xdt_ref[...] = (dt_full * xc[:, :D_INNER]).T
    dect_ref[...] = dec_full.T


def _dec_state_kernel(h_ref, xdt_ref, dect_ref, bm_ref, cm_ref, hout_ref, y_ref):
    i = pl.program_id(0)
    B = xdt_ref.shape[1]
    lane = lax.broadcasted_iota(I32, (D_INNER, B), 1)
    sub = lax.broadcasted_iota(I32, (TB, D_INNER // N_BC_GROUPS), 0)
    ones = jnp.ones((B, D_STATE), BF16)
    half = D_INNER // N_BC_GROUPS
    nt = (((1,), (1,)), ((), ()))
    y_acc = [jnp.zeros((TB, half), F32) for _ in range(N_BC_GROUPS)]
    for j in range(TB):
        b = i * TB + j
        sel = lane == b
        xcol = jnp.where(sel, xdt_ref[...], 0.0)
        dcol = jnp.where(sel, dect_ref[...], 0.0)
        dec = sum(jnp.dot(part, ones, preferred_element_type=F32) for part in _split3(dcol))
        xhi, xlo = _split2(xcol)
        hnew = dec * h_ref[j]
        for g in range(N_BC_GROUPS):
            bmat = bm_ref[:, g * D_STATE:(g + 1) * D_STATE].astype(BF16)
            upd = (jnp.dot(xhi[g * half:(g + 1) * half], bmat, preferred_element_type=F32)
                   + jnp.dot(xlo[g * half:(g + 1) * half], bmat, preferred_element_type=F32))
            hg = hnew[g * half:(g + 1) * half] + upd
            hout_ref[j, g * half:(g + 1) * half, :] = hg
            start = pl.multiple_of(i * TB, TB)
            c8 = cm_ref[pl.ds(start, TB), g * D_STATE:(g + 1) * D_STATE].astype(BF16)
            yj = lax.dot_general(c8, hg.astype(BF16), nt, preferred_element_type=F32)
            y_acc[g] = y_acc[g] + jnp.where(sub == j, yj, 0.0)
    y_ref[...] = jnp.concatenate(y_acc, axis=1)


def _dec_post_kernel(y_ref, xc_ref, z_ref, dskip_ref, normw_ref, o_ref):
    o_ref[...] = _gated_out(y_ref[...], xc_ref[:, :D_INNER], z_ref[...], dskip_ref[...],
                            normw_ref[...]).astype(BF16)


def _decode_ssm(xbc_s, sc0, sc1, sc2, dt_s, z_s, state, conv_w, conv_b, dtb, a_row, dskip_full, normw, e_mat):
    B = xbc_s.shape[0]
    xc, xdt_t, dec_t = pl.pallas_call(
        _dec_pre_kernel,
        out_shape=[jax.ShapeDtypeStruct((B, XBC_DIM), F32),
                   jax.ShapeDtypeStruct((D_INNER, B), F32),
                   jax.ShapeDtypeStruct((D_INNER, B), F32)],
        compiler_params=pltpu.CompilerParams(vmem_limit_bytes=VMEM_LIMIT),
        name="decode_ssm_pre",
    )(xbc_s, sc0, sc1, sc2, dt_s, conv_w, conv_b, dtb, a_row, e_mat)
    bm = xc[:, D_INNER:D_INNER + N_BC_GROUPS * D_STATE]
    cm = xc[:, D_INNER + N_BC_GROUPS * D_STATE:]
    const = lambda shp: pl.BlockSpec(shp, lambda i: (0,) * len(shp))
    state_spec = pl.BlockSpec((TB, D_INNER, D_STATE), lambda i: (i, 0, 0))
    h_new, y = pl.pallas_call(
        _dec_state_kernel,
        grid=(B // TB,),
        in_specs=[state_spec, const((D_INNER, B)), const((D_INNER, B)),
                  const((B, N_BC_GROUPS * D_STATE)), const((B, N_BC_GROUPS * D_STATE))],
        out_specs=[state_spec, pl.BlockSpec((TB, D_INNER), lambda i: (i, 0))],
        out_shape=[jax.ShapeDtypeStruct((B, D_INNER, D_STATE), F32),
                   jax.ShapeDtypeStruct((B, D_INNER), F32)],
        compiler_params=_cparams(("arbitrary",)),
        name="decode_ssm_state",
    )(state, xdt_t, dec_t, bm, cm)
    ssm = pl.pallas_call(
        _dec_post_kernel,
        out_shape=jax.ShapeDtypeStruct((B, D_INNER), BF16),
        compiler_params=pltpu.CompilerParams(vmem_limit_bytes=VMEM_LIMIT),
        name="decode_ssm_post",
    )(y, xc, z_s, dskip_full, normw)
    return ssm, h_new


def _outproj_kernel(x_ref, att_ref, ssm_ref, wa_ref, ws_ref, nw_ref, wrh_ref, wrl_ref,
                    h_ref, u_ref, lg_ref):
    acc = jnp.dot(att_ref[...], wa_ref[...], preferred_element_type=F32)
    acc = acc + jnp.dot(ssm_ref[...], ws_ref[...], preferred_element_type=F32)
    h = x_ref[...] + acc
    h_ref[...] = h
    ms = jnp.mean(h * h, axis=-1, keepdims=True)
    u = h * lax.rsqrt(ms + EPS) * nw_ref[...]
    u_ref[...] = u
    uh, ul = _split2(u)
    wh, wl = wrh_ref[...], wrl_ref[...]
    lg_ref[...] = (jnp.dot(uh, wh, preferred_element_type=F32) + jnp.dot(uh, wl, preferred_element_type=F32)
                   + jnp.dot(ul, wh, preferred_element_type=F32))


def _outproj(xall, att, ssm, wa, ws, norm_w, wr_hi, wr_lo):
    R = xall.shape[0]
    row = lambda n: pl.BlockSpec((TM, n), lambda i: (i, 0))
    return pl.pallas_call(
        _outproj_kernel,
        grid=(R // TM,),
        in_specs=[row(D_MODEL), row(N_HEADS_A * DV), row(D_INNER),
                  _resident(wa.shape), _resident(ws.shape), _resident((1, D_MODEL)),
                  _resident(wr_hi.shape), _resident(wr_lo.shape)],
        out_specs=[row(D_MODEL), row(D_MODEL), row(LANES)],
        out_shape=[jax.ShapeDtypeStruct((R, D_MODEL), F32),
                   jax.ShapeDtypeStruct((R, D_MODEL), F32),
                   jax.ShapeDtypeStruct((R, LANES), F32)],
        compiler_params=_cparams(("arbitrary",)),
        name="outproj",
    )(xall, att, ssm, wa, ws, norm_w, wr_hi, wr_lo)


GATHER_UNROLL = 8


def _gather_kernel(nlive_ref, idx_ref, src_ref, out_ref, sem):
    live = pl.program_id(0) * TG < nlive_ref[0]

    @pl.when(live)
    def _():
        def start(r8, carry):
            for j in range(GATHER_UNROLL):
                r = r8 * GATHER_UNROLL + j
                pltpu.make_async_copy(src_ref.at[pl.ds(idx_ref[0, r], 1)], out_ref.at[pl.ds(r, 1)],
                                      sem).start(priority=j % 2)
            return carry

        lax.fori_loop(0, TG // GATHER_UNROLL, start, 0)
        pltpu.make_async_copy(src_ref.at[pl.ds(0, TG)], out_ref, sem).wait()

    @pl.when(jnp.logical_not(live))
    def _():
        out_ref[...] = jnp.zeros(out_ref.shape, out_ref.dtype)


def _gather_rows(src, idx, n_live):
    n = idx.shape[0]
    assert n % TG == 0
    width = src.shape[1]
    grid_spec = pltpu.PrefetchScalarGridSpec(
        num_scalar_prefetch=1,
        grid=(n // TG,),
        in_specs=[pl.BlockSpec((None, 1, TG), lambda i, nl: (i, 0, 0), memory_space=pltpu.SMEM),
                  pl.BlockSpec(memory_space=pl.ANY)],
        out_specs=pl.BlockSpec((TG, width), lambda i, nl: (i, 0)),
        scratch_shapes=[pltpu.SemaphoreType.DMA(())],
    )
    return pl.pallas_call(
        _gather_kernel,
        grid_spec=grid_spec,
        out_shape=jax.ShapeDtypeStruct((n, width), src.dtype),
        compiler_params=_cparams(("arbitrary",)),
        name="gather_rows",
    )(n_live, idx.reshape(n // TG, 1, TG), src)


def _moe_up_kernel(be_ref, nx_ref, nv_ref, x_ref, wg_hbm, wu_hbm, h_ref,
                   stage_g, stage_u, wg_b, wu_b, sems):
    i = pl.program_id(0)
    valid = i < nv_ref[0]
    e = be_ref[i]
    changed = (i == 0) | (be_ref[jnp.maximum(i - 1, 0)] != e)

    def fetch(expert):
        return (pltpu.make_async_copy(wg_hbm.at[expert], stage_g, sems.at[0]),
                pltpu.make_async_copy(wu_hbm.at[expert], stage_u, sems.at[1]))

    @pl.when(i == 0)
    def _():
        for c in fetch(e):
            c.start()

    @pl.when(valid & changed)
    def _():
        for c in fetch(e):
            c.wait()
        wg_b[...] = stage_g[...].astype(BF16)
        wu_b[...] = stage_u[...].astype(BF16)
        nxt = nx_ref[i]

        @pl.when(nxt >= 0)
        def _():
            for c in fetch(nxt):
                c.start()

    @pl.when(valid)
    def _():
        x = x_ref[...].astype(BF16)
        g = jnp.dot(x, wg_b[...], preferred_element_type=F32)
        u = jnp.dot(x, wu_b[...], preferred_element_type=F32)
        h_ref[...] = (_silu(g) * u).astype(BF16)

    @pl.when(jnp.logical_not(valid))
    def _():
        h_ref[...] = jnp.zeros(h_ref.shape, BF16)


def _moe_down_kernel(be_ref, nv_ref, h_ref, wd_ref, o_ref, wd_b):
    i = pl.program_id(0)
    valid = i < nv_ref[0]
    changed = (i == 0) | (be_ref[jnp.maximum(i - 1, 0)] != be_ref[i])

    @pl.when(valid & changed)
    def _():
        wd_b[...] = wd_ref[...].astype(BF16)

    @pl.when(valid)
    def _():
        o_ref[...] = jnp.dot(h_ref[...], wd_b[...], preferred_element_type=F32)

    @pl.when(jnp.logical_not(valid))
    def _():
        o_ref[...] = jnp.zeros(o_ref.shape, F32)


def _moe(block_e, next_e, n_valid, xs, w_gate, w_up, w_down):
    rows = xs.shape[0]
    nblk = rows // TMOE

    def blk(i, nv):
        return jnp.minimum(i, nv[0] - 1)

    up_spec = pltpu.PrefetchScalarGridSpec(
        num_scalar_prefetch=3,
        grid=(nblk,),
        in_specs=[pl.BlockSpec((TMOE, D_MODEL), lambda i, be, nx, nv: (blk(i, nv), 0)),
                  pl.BlockSpec(memory_space=pl.ANY), pl.BlockSpec(memory_space=pl.ANY)],
        out_specs=pl.BlockSpec((TMOE, D_FF), lambda i, be, nx, nv: (i, 0)),
        scratch_shapes=[pltpu.VMEM((D_MODEL, D_FF), F32), pltpu.VMEM((D_MODEL, D_FF), F32),
                        pltpu.VMEM((D_MODEL, D_FF), BF16), pltpu.VMEM((D_MODEL, D_FF), BF16),
                        pltpu.SemaphoreType.DMA((2,))],
    )
    hmid = pl.pallas_call(
        _moe_up_kernel,
        grid_spec=up_spec,
        out_shape=jax.ShapeDtypeStruct((rows, D_FF), BF16),
        compiler_params=_cparams(("arbitrary",)),
        name="moe_up",
    )(block_e, next_e, n_valid, xs, w_gate, w_up)

    down_spec = pltpu.PrefetchScalarGridSpec(
        num_scalar_prefetch=2,
        grid=(nblk,),
        in_specs=[pl.BlockSpec((TMOE, D_FF), lambda i, be, nv: (blk(i, nv), 0)),
                  pl.BlockSpec((None, D_FF, D_MODEL), lambda i, be, nv: (be[blk(i, nv)], 0, 0))],
        out_specs=pl.BlockSpec((TMOE, D_MODEL), lambda i, be, nv: (i, 0)),
        scratch_shapes=[pltpu.VMEM((D_FF, D_MODEL), BF16)],
    )
    return pl.pallas_call(
        _moe_down_kernel,
        grid_spec=down_spec,
        out_shape=jax.ShapeDtypeStruct((rows, D_MODEL), F32),
        compiler_params=_cparams(("arbitrary",)),
        name="moe_down",
    )(block_e, n_valid, hmid, w_down)


def _final_kernel(h_ref, y0_ref, y1_ref, w_ref, nw_ref, o_ref):
    w = w_ref[...]
    moe = y0_ref[...] * w[:, 0:1] + y1_ref[...] * w[:, 1:2]
    h = h_ref[...] + moe
    ms = jnp.mean(h * h, axis=-1, keepdims=True)
    o_ref[...] = h * lax.rsqrt(ms + EPS) * nw_ref[...]


def _final(h1, y0, y1, wts, norm_w):
    R = h1.shape[0]
    row = lambda n: pl.BlockSpec((TM, n), lambda i: (i, 0))
    return pl.pallas_call(
        _final_kernel,
        grid=(R // TM,),
        in_specs=[row(D_MODEL), row(D_MODEL), row(D_MODEL), row(TOP_K),
                  pl.BlockSpec((1, D_MODEL), lambda i: (0, 0))],
        out_specs=row(D_MODEL),
        out_shape=jax.ShapeDtypeStruct((R, D_MODEL), F32),
        compiler_params=_cparams(("arbitrary",)),
        name="final_norm",
    )(h1, y0, y1, wts, norm_w)


def _route(logits, row_valid):
    R = logits.shape[0]
    gl = logits[:, :N_EGROUPS]
    pg = jax.nn.softmax(gl, axis=-1)
    g = jnp.argmax(gl, axis=-1).astype(I32)
    el = logits[:, N_EGROUPS:N_EGROUPS + N_EXPERTS].reshape(R, N_EGROUPS, EXPERTS_PER_GROUP)
    el_g = jnp.take_along_axis(el, g[:, None, None], axis=1)[:, 0]
    top_v, top_i = lax.top_k(jax.nn.softmax(el_g, axis=-1), TOP_K)
    wts = top_v / jnp.sum(top_v, axis=-1, keepdims=True) * jnp.take_along_axis(pg, g[:, None], axis=1)
    eid = g[:, None] * EXPERTS_PER_GROUP + top_i.astype(I32)
    eid = jnp.where(row_valid[:, None], eid, N_EXPERTS)
    wts = jnp.where(row_valid[:, None], wts, 0.0)

    flat_e = eid.reshape(-1)
    onehot = (flat_e[:, None] == jnp.arange(N_EXPERTS, dtype=I32)[None, :]).astype(I32)
    rank = jnp.sum((jnp.cumsum(onehot, axis=0) - onehot) * onehot, axis=1)
    counts = jnp.sum(onehot, axis=0)
    padded = (counts + TMOE - 1) // TMOE * TMOE
    ends = jnp.cumsum(padded)
    pstart = ends - padded
    is_real = flat_e < N_EXPERTS
    dest = jnp.where(is_real, pstart[jnp.minimum(flat_e, N_EXPERTS - 1)] + rank, 0)

    a_max = TOP_K * R
    nblk = (a_max + N_EXPERTS * (TMOE - 1) + TMOE - 1) // TMOE
    rows = nblk * TMOE
    rows = (rows + TG - 1) // TG * TG
    nblk = rows // TMOE
    flat_tok = jnp.repeat(jnp.arange(R, dtype=I32), TOP_K)
    buf_tok = (jnp.arange(rows, dtype=I32) % PAD_ROWS).at[jnp.where(is_real, dest, rows)].set(
        flat_tok, mode="drop")
    n_valid = (ends[-1] // TMOE).astype(I32)
    block_e = jnp.minimum(jnp.searchsorted(ends, jnp.arange(nblk, dtype=I32) * TMOE, side="right"),
                          N_EXPERTS - 1).astype(I32)
    ids = jnp.arange(N_EXPERTS, dtype=I32)
    cand = jnp.where(counts > 0, ids, N_EXPERTS)
    after = lax.cummin(cand[::-1])[::-1]
    nxt = jnp.concatenate([after[1:], jnp.full((1,), N_EXPERTS, I32)])
    nxt = jnp.where(nxt >= N_EXPERTS, -1, nxt)
    next_e = nxt[block_e].astype(I32)
    return wts, dest.reshape(R, TOP_K), buf_tok, block_e, next_e, n_valid.reshape(1)


def kernel(x_prompt, x_sample, cache_k, cache_v, state_conv, state_ssm, page_table, meta_tokens, norm_mix_w,
           w_in, lambda_q1, lambda_k1, lambda_q2, lambda_k2, subln_w, conv_w, conv_b, dt_bias, a_log, d_skip,
           ssm_norm_w, w_out, norm_ffn_w, w_router_group, w_router_expert, w_gate, w_up, w_down, final_norm_w):
    depth = w_in.shape[0]
    assert depth == 1 and x_prompt.shape[0] == 1 and x_sample.shape[1] == 1
    seq = x_prompt.shape[1]
    B = x_sample.shape[0]
    n_pages = page_table.shape[1]
    n_phys = cache_k.shape[1]
    past = n_pages * cache_k.shape[2]
    assert cache_k.shape[2] == PAGE and seq % CHUNK == 0 and B == LANES and (FRONT + seq + B) % TM == 0
    n_t = FRONT + seq
    R = n_t + B
    l = 0
    lam_init = 0.8 - 0.6 * math.exp(-0.3 * l)

    xall = jnp.concatenate([jnp.zeros((PAD_ROWS, D_MODEL), F32), meta_tokens.astype(F32),
                            x_prompt[0], x_sample[:, 0]], axis=0)
    w = w_in[l]
    wq = w[:, :OFF_K].astype(BF16)
    wk = w[:, OFF_K:OFF_V].astype(BF16)
    wv = w[:, OFF_V:OFF_Z].astype(BF16)
    wz = w[:, OFF_Z:OFF_XBC].astype(BF16)
    wx = w[:, OFF_XBC:OFF_DT].astype(BF16)
    wdt = jnp.pad(w[:, OFF_DT:], ((0, 0), (0, LANES - N_HEADS_S))).astype(BF16)

    pos = jnp.concatenate([jnp.maximum(jnp.arange(n_t, dtype=I32) - PAD_ROWS, 0),
                           jnp.full((B,), past, I32)])
    inv = ROPE_THETA ** (-jnp.arange(0, ROT_DIM, 2, dtype=F32) / ROT_DIM)
    ang = pos.astype(F32)[:, None] * inv[None, :]
    cos, sin = jnp.cos(ang), jnp.sin(ang)
    half = ROT_DIM // 2
    ones = jnp.ones((R, DK - ROT_DIM), F32)
    zeros = jnp.zeros((R, DK - ROT_DIM), F32)
    zh = jnp.zeros((R, half), F32)
    ctab = jnp.tile(jnp.concatenate([cos, cos, ones], axis=1), (1, 2))
    s1tab = jnp.tile(jnp.concatenate([-sin, zh, zeros], axis=1), (1, 2))
    s2tab = jnp.tile(jnp.concatenate([zh, sin, zeros], axis=1), (1, 2))

    row1 = lambda v: v.reshape(1, -1).astype(F32)
    pad_heads = lambda v: jnp.pad(v.astype(F32), (0, LANES - N_HEADS_S)).reshape(1, LANES)
    a_row = pad_heads(-jnp.exp(a_log[l].astype(F32)))
    dtb = pad_heads(dt_bias[l])
    dskip_full = jnp.repeat(d_skip[l].astype(F32), SSM_HEAD_DIM).reshape(1, D_INNER)
    normw = row1(ssm_norm_w[l])
    head_of_col = jnp.arange(D_INNER, dtype=I32) // SSM_HEAD_DIM
    e_mat = (jnp.arange(LANES, dtype=I32)[:, None] == head_of_col[None, :]).astype(BF16)
    tri = (jnp.arange(CHUNK)[:, None] >= jnp.arange(CHUNK)[None, :]).astype(BF16)
    lq1, lk1, lq2, lk2 = (row1(v[l]) for v in (lambda_q1, lambda_k1, lambda_q2, lambda_k2))
    sub_w = row1(subln_w[l])

    q, k, v, kb, vb, z, xbc, dt_raw = _inproj(xall, row1(norm_mix_w[l]), wq, wk, wv, wz, wx, wdt,
                                              ctab, s1tab, s2tab)

    att_all = _prompt_attention(q, kb, vb, lq1, lk1, lq2, lk2, sub_w, lam_init)
    ssm_p, state_p = _prompt_ssd(xbc, z, dt_raw, conv_w[l].astype(F32), row1(conv_b[l]), dtb, a_row,
                                 dskip_full, normw, e_mat, e_mat.T.astype(F32), tri, n_t)

    k_s, v_s = k[n_t:], v[n_t:]
    pad8 = lambda a: jnp.pad(a.reshape(B, N_KV_HEADS, LANES), ((0, 0), (0, LANES - N_KV_HEADS), (0, 0)))
    att_s = _decode_attention(page_table, q[n_t:].reshape(B, N_HEADS_A, LANES), pad8(k_s), pad8(v_s),
                              cache_k[l].reshape(n_phys * PAGE * N_KV_HEADS, LANES),
                              cache_v[l].reshape(n_phys * PAGE * N_KV_HEADS, LANES),
                              lq1, lk1, lq2, lk2, sub_w, lam_init)
    sc = state_conv[l]
    xbc_s = xbc[n_t:]
    ssm_s, state_s = _decode_ssm(xbc_s, sc[:, 0], sc[:, 1], sc[:, 2], dt_raw[n_t:], z[n_t:],
                                 state_ssm[l].reshape(B, D_INNER, D_STATE), conv_w[l].astype(F32),
                                 row1(conv_b[l]), dtb, a_row, dskip_full, normw, e_mat)

    att = lax.dynamic_update_slice(att_all, att_s.reshape(B, N_HEADS_A * DV), (n_t, 0))
    ssm = jnp.concatenate([ssm_p, ssm_s], axis=0)

    wo = w_out[l].astype(BF16)
    wr = jnp.pad(jnp.concatenate([w_router_group[l], w_router_expert[l]], axis=1).astype(F32),
                 ((0, 0), (0, LANES - N_EGROUPS - N_EXPERTS)))
    wr_hi = wr.astype(BF16)
    wr_lo = (wr - wr_hi.astype(F32)).astype(BF16)
    h1, u2, logits = _outproj(xall, att, ssm, wo[:N_HEADS_A * DV], wo[N_HEADS_A * DV:],
                              row1(norm_ffn_w[l]), wr_hi, wr_lo)

    row_valid = jnp.arange(R, dtype=I32) >= PAD_ROWS
    wts, dest, buf_tok, block_e, next_e, n_valid = _route(logits, row_valid)
    xs = _gather_rows(u2, buf_tok, n_valid * TMOE)
    yb = _moe(block_e, next_e, n_valid, xs, w_gate[l], w_up[l], w_down[l])
    n_tok = jnp.full((1,), R, I32)
    y0 = _gather_rows(yb, _pad_to(dest[:, 0], TG), n_tok)[:R]
    y1 = _gather_rows(yb, _pad_to(dest[:, 1], TG), n_tok)[:R]
    y = _final(h1, y0, y1, wts, row1(final_norm_w))

    t_real = N_META + seq
    y_prompt = y[FRONT:n_t].reshape(1, seq, D_MODEL)
    y_sample = y[n_t:].reshape(B, 1, D_MODEL)
    k_prompt = k[PAD_ROWS:n_t].reshape(1, 1, t_real, N_KV_HEADS, 2 * DK)
    v_prompt = v[PAD_ROWS:n_t].reshape(1, 1, t_real, N_KV_HEADS, DV)
    conv_prompt = xbc[n_t - (CONV_W - 1):n_t].reshape(1, 1, CONV_W - 1, XBC_DIM)
    ssm_prompt = state_p.reshape(1, 1, N_HEADS_S, SSM_HEAD_DIM, D_STATE)
    k_sample = k_s.reshape(1, B, 1, N_KV_HEADS, 2 * DK)
    v_sample = v_s.reshape(1, B, 1, N_KV_HEADS, DV)
    conv_sample = jnp.stack([sc[:, 1], sc[:, 2], xbc_s], axis=1)[None]
    ssm_sample = state_s.reshape(1, B, N_HEADS_S, SSM_HEAD_DIM, D_STATE)
    return (y_prompt, y_sample, k_prompt, v_prompt, conv_prompt, ssm_prompt,
            k_sample, v_sample, conv_sample, ssm_sample)


def _pad_to(idx, mult):
    n = idx.shape[0]
    return jnp.pad(idx, (0, (-n) % mult))
```

```python
import functools
import math

import jax
import jax.numpy as jnp
from jax import lax
from jax.experimental import pallas as pl
from jax.experimental.pallas import tpu as pltpu

F32 = jnp.float32
BF16 = jnp.bfloat16
I32 = jnp.int32

D_MODEL = 2048
N_META = 16
EPS = 1e-6
DV = 128
DK = 64
N_HEADS_A = 8
N_KV_HEADS = 4
ROT_DIM = 16
ROPE_THETA = 500000.0
NEG_INF = -1e30
LOG2E = 1.4426950408889634
D_INNER = 1024
SSM_HEAD_DIM = 64
N_HEADS_S = 16
N_BC_GROUPS = 2
D_STATE = 128
CONV_W = 4
XBC_DIM = D_INNER + 2 * N_BC_GROUPS * D_STATE
CHUNK = 128
Q_DIM = N_HEADS_A * 2 * DK
K_DIM = N_KV_HEADS * 2 * DK
V_DIM = N_KV_HEADS * DV
OFF_K = Q_DIM
OFF_V = OFF_K + K_DIM
OFF_Z = OFF_V + V_DIM
OFF_XBC = OFF_Z + D_INNER
OFF_DT = OFF_XBC + XBC_DIM
IN_DIM = OFF_DT + N_HEADS_S
N_EGROUPS = 4
EXPERTS_PER_GROUP = 8
N_EXPERTS = N_EGROUPS * EXPERTS_PER_GROUP
TOP_K = 2
D_FF = 1024
PAGE = 128

LANES = 128
VMEM_LIMIT = 56 * 1024 * 1024

FRONT = CHUNK
PAD_ROWS = FRONT - N_META
TM = 256
TQ = 256
TK = 512
TKF = 1024
TB = 8
TMOE = 512
TF = 256


def _cparams(sem):
    return pltpu.CompilerParams(dimension_semantics=sem, vmem_limit_bytes=VMEM_LIMIT)


def _resident(shape):
    nd = len(shape)
    return pl.BlockSpec(shape, lambda *_: (0,) * nd, pipeline_mode=pl.Buffered(1))


def _split2(x):
    hi = x.astype(BF16)
    lo = (x - hi.astype(F32)).astype(BF16)
    return hi, lo


def _split3(x):
    hi = x.astype(BF16)
    r = x - hi.astype(F32)
    mid = r.astype(BF16)
    lo = (r - mid.astype(F32)).astype(BF16)
    return hi, mid, lo


def _silu(x):
    return x * (1.0 / (1.0 + jnp.exp(-x)))


def _softplus(x):
    return jnp.maximum(x, 0.0) + jnp.log(1.0 + jnp.exp(-jnp.abs(x)))


def _inproj_kernel(x_ref, nw_ref, wq_ref, wk_ref, wv_ref, wz_ref, wx_ref, wdt_ref,
                   c_ref, s1_ref, s2_ref,
                   q_ref, k_ref, v_ref, kb_ref, vb_ref, z_ref, xbc_ref, dt_ref):
    x = x_ref[...]
    ms = jnp.mean(x * x, axis=-1, keepdims=True)
    u = (x * lax.rsqrt(ms + EPS) * nw_ref[...]).astype(BF16)
    c, s1, s2 = c_ref[...], s1_ref[...], s2_ref[...]

    def rope(p):
        outs = []
        for h in range(p.shape[1] // LANES):
            xh = p[:, h * LANES:(h + 1) * LANES]
            outs.append(xh * c + pltpu.roll(xh, LANES - ROT_DIM // 2, 1) * s1
                        + pltpu.roll(xh, ROT_DIM // 2, 1) * s2)
        return jnp.concatenate(outs, axis=1)

    q = rope(jnp.dot(u, wq_ref[...], preferred_element_type=F32))
    q_ref[...] = (q * (DK ** -0.5 * LOG2E)).astype(BF16)
    k = rope(jnp.dot(u, wk_ref[...], preferred_element_type=F32))
    k_ref[...] = k
    kb_ref[...] = k.astype(BF16)
    v = jnp.dot(u, wv_ref[...], preferred_element_type=F32)
    v_ref[...] = v
    vb = v.astype(BF16)
    ones = jnp.ones((vb.shape[0], DV), BF16)
    vb_ref[...] = jnp.concatenate(
        [piece for g in range(N_KV_HEADS) for piece in (vb[:, g * DV:(g + 1) * DV], ones)], axis=1)
    z_ref[...] = jnp.dot(u, wz_ref[...], preferred_element_type=F32)
    xbc_ref[...] = jnp.dot(u, wx_ref[...], preferred_element_type=F32)
    dt_ref[...] = jnp.dot(u, wdt_ref[...], preferred_element_type=F32)


def _inproj(xall, norm_w, wq, wk, wv, wz, wx, wdt, ctab, s1tab, s2tab):
    R = xall.shape[0]
    row = lambda n: pl.BlockSpec((TM, n), lambda i: (i, 0))
    return pl.pallas_call(
        _inproj_kernel,
        grid=(R // TM,),
        in_specs=[row(D_MODEL), _resident((1, D_MODEL)),
                  _resident(wq.shape), _resident(wk.shape), _resident(wv.shape),
                  _resident(wz.shape), _resident(wx.shape), _resident(wdt.shape),
                  row(LANES), row(LANES), row(LANES)],
        out_specs=[row(Q_DIM), row(K_DIM), row(V_DIM), row(K_DIM), row(2 * V_DIM),
                   row(D_INNER), row(XBC_DIM), row(LANES)],
        out_shape=[jax.ShapeDtypeStruct((R, Q_DIM), BF16),
                   jax.ShapeDtypeStruct((R, K_DIM), F32),
                   jax.ShapeDtypeStruct((R, V_DIM), F32),
                   jax.ShapeDtypeStruct((R, K_DIM), BF16),
                   jax.ShapeDtypeStruct((R, 2 * V_DIM), BF16),
                   jax.ShapeDtypeStruct((R, D_INNER), F32),
                   jax.ShapeDtypeStruct((R, XBC_DIM), F32),
                   jax.ShapeDtypeStruct((R, LANES), F32)],
        compiler_params=_cparams(("arbitrary",)),
        name="inproj",
    )(xall, norm_w, wq, wk, wv, wz, wx, wdt, ctab, s1tab, s2tab)


def _diff_lambda(lq1, lk1, lq2, lk2, lam_init):
    a = jnp.sum(lq1[...] * lk1[...], axis=-1, keepdims=True)
    b = jnp.sum(lq2[...] * lk2[...], axis=-1, keepdims=True)
    return jnp.exp(a) - jnp.exp(b) + lam_init


def _subln(o, sub_w, lam_init):
    ms = jnp.mean(o * o, axis=-1, keepdims=True)
    return (o * lax.rsqrt(ms + EPS) * sub_w) * (1.0 - lam_init)


def _attn_kernel(lq1, lk1, lq2, lk2, sub_ref, q_ref, k_ref, ve_ref, o_ref,
                 qs_scr, m_scr, acc_scr, *, lam_init, n_rows):
    qi = pl.program_id(1)
    lam = _diff_lambda(lq1, lk1, lq2, lk2, lam_init)

    q = q_ref[...]
    lane = lax.broadcasted_iota(I32, (TQ, LANES), 1)
    first = lane < DK
    zero = jnp.zeros((TQ, LANES), BF16)
    for hh in range(2):
        qh = q[:, hh * LANES:(hh + 1) * LANES]
        qs_scr[(2 * hh) * TQ:(2 * hh + 1) * TQ, :] = jnp.where(first, qh, zero)
        qs_scr[(2 * hh + 1) * TQ:(2 * hh + 2) * TQ, :] = jnp.where(first, zero, qh)
    m_scr[...] = jnp.full(m_scr.shape, NEG_INF, F32)
    acc_scr[...] = jnp.zeros(acc_scr.shape, F32)

    nt = (((1,), (1,)), ((), ()))

    def process(start, width, keep):
        for part in range(4):
            rows = slice(part * TQ, (part + 1) * TQ)
            s = lax.dot_general(qs_scr[rows, :], k_ref[pl.ds(start, width), :], nt,
                                preferred_element_type=F32)
            if keep is not None:
                s = jnp.where(keep, s, NEG_INF)
            m_prev = m_scr[rows, :]
            m_new = jnp.maximum(m_prev, jnp.max(s, axis=-1, keepdims=True))
            alpha = jnp.exp2(m_prev - m_new)
            p = jnp.concatenate([jnp.exp2(s[:, t * LANES:(t + 1) * LANES] - m_new)
                                 for t in range(width // LANES)], axis=1).astype(BF16)
            pv = jnp.dot(p, ve_ref[pl.ds(start, width), :], preferred_element_type=F32)
            acc_scr[rows, :LANES] = alpha * acc_scr[rows, :LANES] + pv[:, :LANES]
            acc_scr[rows, LANES:] = alpha * acc_scr[rows, LANES:] + pv[:, LANES:]
            m_scr[rows, :] = m_new

    q_lo = qi * TQ
    n_full = jnp.maximum(q_lo - PAD_ROWS, 0) // TKF
    tail_lo = PAD_ROWS + n_full * TKF
    n_tail = (q_lo + TQ - tail_lo + TK - 1) // TK

    def full(c, carry):
        process(pl.multiple_of(PAD_ROWS + c * TKF, 16), TKF, None)
        return carry

    lax.fori_loop(0, n_full, full, 0)

    def tail(c, carry):
        nominal = tail_lo + c * TK
        start = pl.multiple_of(jnp.minimum(nominal, n_rows - TK), 16)
        col = lax.broadcasted_iota(I32, (TQ, TK), 1)
        qpos = q_lo + lax.broadcasted_iota(I32, (TQ, TK), 0)
        process(start, TK, (col <= qpos - start) & (col >= nominal - start))
        return carry

    lax.fori_loop(0, n_tail, tail, 0)

    acc = acc_scr[...]
    o = acc[:, :LANES] / acc[:, LANES:]
    sub_w = sub_ref[...]
    for hh in range(2):
        o1 = o[(2 * hh) * TQ:(2 * hh + 1) * TQ, :]
        o2 = o[(2 * hh + 1) * TQ:(2 * hh + 2) * TQ, :]
        o_ref[:, hh * LANES:(hh + 1) * LANES] = _subln(o1 - lam * o2, sub_w, lam_init).astype(BF16)


def _prompt_attention(q, kb, vbe, lq1, lk1, lq2, lk2, sub_w, lam_init):
    R = q.shape[0]
    assert R % TQ == 0 and (R - TK) % 16 == 0 and R >= TK
    small = lambda n: pl.BlockSpec((1, n), lambda g, i: (0, 0))
    return pl.pallas_call(
        functools.partial(_attn_kernel, lam_init=lam_init, n_rows=R),
        grid=(N_KV_HEADS, R // TQ),
        in_specs=[small(DK), small(DK), small(DK), small(DK), small(DV),
                  pl.BlockSpec((TQ, 2 * LANES), lambda g, i: (i, g)),
                  pl.BlockSpec((R, LANES), lambda g, i: (0, g)),
                  pl.BlockSpec((R, 2 * LANES), lambda g, i: (0, g))],
        out_specs=pl.BlockSpec((TQ, 2 * LANES), lambda g, i: (i, g)),
        out_shape=jax.ShapeDtypeStruct((R, N_HEADS_A * DV), BF16),
        scratch_shapes=[pltpu.VMEM((4 * TQ, LANES), BF16),
                        pltpu.VMEM((4 * TQ, LANES), F32),
                        pltpu.VMEM((4 * TQ, 2 * LANES), F32)],
        compiler_params=_cparams(("arbitrary", "arbitrary")),
        name="prompt_attn",
    )(lq1, lk1, lq2, lk2, sub_w, q, kb, vbe)


def _expand_heads(x, e_ref):
    hi, lo = _split2(x)
    e = e_ref[...]
    return (jnp.dot(hi, e, preferred_element_type=F32)
            + jnp.dot(lo, e, preferred_element_type=F32))


def _gated_out(y, xs, z, dskip_full, normw):
    y = (y + dskip_full * xs) * _silu(z)
    half = D_INNER // N_BC_GROUPS
    outs = []
    for g in range(N_BC_GROUPS):
        yg = y[:, g * half:(g + 1) * half]
        ms = jnp.mean(yg * yg, axis=-1, keepdims=True)
        outs.append(yg * lax.rsqrt(ms + EPS))
    return jnp.concatenate(outs, axis=1) * normw


def _ssd_kernel(xbc_ref, z_ref, dt_ref, cw_ref, cb_ref, dtb_ref, a_ref, dskip_ref, normw_ref,
                e_ref, et_ref, tri_ref, ssm_ref, state_ref, ext_scr, h_scr):
    c = pl.program_id(0)
    L = CHUNK

    @pl.when(c == 0)
    def _():
        ext_scr[0:8, :] = jnp.zeros((8, XBC_DIM), F32)
        h_scr[...] = jnp.zeros(h_scr.shape, F32)

    x_new = xbc_ref[...]
    ext_scr[8:8 + L, :] = x_new
    conv = cb_ref[...] + cw_ref[CONV_W - 1:CONV_W, :] * x_new
    for j in range(CONV_W - 1):
        sh = CONV_W - 1 - j
        conv = conv + cw_ref[j:j + 1, :] * ext_scr[8 - sh:8 - sh + L, :]
    ext_scr[0:8, :] = x_new[L - 8:L, :]
    xc = _silu(conv)
    xs = xc[:, :D_INNER]

    rowi = lax.broadcasted_iota(I32, (L, LANES), 0)
    coli = lax.broadcasted_iota(I32, (L, LANES), 1)
    valid = (c * L + rowi) >= PAD_ROWS
    dt = jnp.where(valid, _softplus(dt_ref[...] + dtb_ref[...]), 0.0)
    da = dt * a_ref[...]
    tri = tri_ref[...]
    acs = sum(jnp.dot(tri, part, preferred_element_type=F32) for part in _split3(da))
    acs_t = acs.T
    dt_t = dt.T
    causal = coli <= rowi
    last = acs[L - 1:L, :]

    w_state = _expand_heads(dt * jnp.exp(last - acs), e_ref)
    dec_out = _expand_heads(jnp.exp(acs), e_ref)
    dec_rows = jnp.sum(et_ref[...] * jnp.exp(last), axis=-1, keepdims=True)
    xw = xs * w_state
    xs_b = xs.astype(BF16)

    half = D_INNER // N_BC_GROUPS
    hpg = N_HEADS_S // N_BC_GROUPS
    y_parts = []
    for g in range(N_BC_GROUPS):
        bm = xc[:, D_INNER + g * D_STATE:D_INNER + (g + 1) * D_STATE].astype(BF16)
        cm = xc[:, D_INNER + (N_BC_GROUPS + g) * D_STATE:
                D_INNER + (N_BC_GROUPS + g + 1) * D_STATE].astype(BF16)
        cb = lax.dot_general(cm, bm, (((1,), (1,)), ((), ())), preferred_element_type=F32)
        h_prev = h_scr[g * half:(g + 1) * half, :]
        y_off = lax.dot_general(cm, h_prev.astype(BF16), (((1,), (1,)), ((), ())),
                                preferred_element_type=F32)
        y_off = y_off * dec_out[:, g * half:(g + 1) * half]
        yd = []
        for r in range(0, hpg, 2):
            pair = []
            for hh in (r, r + 1):
                h = g * hpg + hh
                seg = jnp.broadcast_to(acs[:, h:h + 1], (L, L)) - acs_t[h:h + 1, :]
                lmat = jnp.where(causal, jnp.exp(jnp.where(causal, seg, 0.0)), 0.0)
                pair.append((cb * lmat * dt_t[h:h + 1, :]).astype(BF16))
            h0 = g * hpg + r
            xp = xs_b[:, h0 * SSM_HEAD_DIM:(h0 + 2) * SSM_HEAD_DIM]
            ya = jnp.dot(pair[0], xp, preferred_element_type=F32)
            yb = jnp.dot(pair[1], xp, preferred_element_type=F32)
            yd.append(jnp.where(coli < SSM_HEAD_DIM, ya, yb))
        y_parts.append(jnp.concatenate(yd, axis=1) + y_off)
        xw_t = xw[:, g * half:(g + 1) * half].T.astype(BF16)
        st = jnp.dot(xw_t, bm, preferred_element_type=F32)
        h_scr[g * half:(g + 1) * half, :] = dec_rows[g * half:(g + 1) * half, :] * h_prev + st
    y = jnp.concatenate(y_parts, axis=1)

    ssm_ref[...] = _gated_out(y, xs, z_ref[...], dskip_ref[...], normw_ref[...]).astype(BF16)

    @pl.when(c == pl.num_programs(0) - 1)
    def _():
        state_ref[...] = h_scr[...]


def _prompt_ssd(xbc, z, dt_raw, conv_w, conv_b, dtb, a_row, dskip_full, normw, e_mat, e_t, tri, n_t):
    row = lambda n: pl.BlockSpec((CHUNK, n), lambda c: (c, 0))
    const = lambda shp: pl.BlockSpec(shp, lambda c: (0, 0))
    return pl.pallas_call(
        _ssd_kernel,
        grid=(n_t // CHUNK,),
        in_specs=[row(XBC_DIM), row(D_INNER), row(LANES),
                  const((CONV_W, XBC_DIM)), const((1, XBC_DIM)), const((1, LANES)), const((1, LANES)),
                  const((1, D_INNER)), const((1, D_INNER)), const((LANES, D_INNER)),
                  const((D_INNER, LANES)), const((CHUNK, CHUNK))],
        out_specs=[row(D_INNER), pl.BlockSpec((D_INNER, D_STATE), lambda c: (0, 0))],
        out_shape=[jax.ShapeDtypeStruct((n_t, D_INNER), BF16),
                   jax.ShapeDtypeStruct((D_INNER, D_STATE), F32)],
        scratch_shapes=[pltpu.VMEM((8 + CHUNK, XBC_DIM), F32),
                        pltpu.VMEM((D_INNER, D_STATE), F32)],
        compiler_params=_cparams(("arbitrary",)),
        name="prompt_ssd",
    )(xbc, z, dt_raw, conv_w, conv_b, dtb, a_row, dskip_full, normw, e_mat, e_t, tri)


def _decode_attn_kernel(pt_ref, lq1, lk1, lq2, lk2, sub_ref, q_ref, kn_ref, vn_ref, *rest,
                        n_pages, lam_init):
    k_pages = rest[:n_pages]
    v_pages = rest[n_pages:2 * n_pages]
    o_ref = rest[2 * n_pages]
    del pt_ref
    lam = _diff_lambda(lq1, lk1, lq2, lk2, lam_init)
    nq = 2 * N_HEADS_A
    q8 = q_ref[...]
    lane = lax.broadcasted_iota(I32, (N_HEADS_A, LANES), 1)
    zero = jnp.zeros((N_HEADS_A, LANES), BF16)
    q16 = jnp.concatenate([jnp.where(lane < DK, q8, zero), jnp.where(lane < DK, zero, q8)], axis=0)

    pw = PAGE * N_KV_HEADS
    row_head = (lax.broadcasted_iota(I32, (nq, pw), 0) & (N_HEADS_A - 1)) >> 1
    col_head = lax.broadcasted_iota(I32, (nq, pw), 1) & (N_KV_HEADS - 1)
    own = row_head == col_head
    nt = (((1,), (1,)), ((), ()))
    s_pages = []
    for j in range(n_pages):
        kp = k_pages[j][...].astype(BF16)
        s = lax.dot_general(q16, kp, nt, preferred_element_type=F32)
        s_pages.append(jnp.where(own, s, NEG_INF))
    kn = kn_ref[...].astype(BF16)
    row_head_n = (lax.broadcasted_iota(I32, (nq, LANES), 0) & (N_HEADS_A - 1)) >> 1
    col_n = lax.broadcasted_iota(I32, (nq, LANES), 1)
    s_new = jnp.where(row_head_n == col_n, lax.dot_general(q16, kn, nt, preferred_element_type=F32), NEG_INF)

    m = jnp.max(s_new, axis=-1, keepdims=True)
    for s in s_pages:
        m = jnp.maximum(m, jnp.max(s, axis=-1, keepdims=True))
    p_new = jnp.exp2(s_new - m)
    l = jnp.sum(p_new, axis=-1, keepdims=True)
    acc = jnp.dot(p_new.astype(BF16), vn_ref[...].astype(BF16), preferred_element_type=F32)
    for j in range(n_pages):
        p = jnp.exp2(s_pages[j] - m)
        l = l + jnp.sum(p, axis=-1, keepdims=True)
        acc = acc + jnp.dot(p.astype(BF16), v_pages[j][...].astype(BF16), preferred_element_type=F32)
    o = acc / l
    res = o[:N_HEADS_A, :] - lam * o[N_HEADS_A:, :]
    o_ref[...] = _subln(res, sub_ref[...], lam_init).astype(BF16)


def _decode_attention(page_table, q_s, k_new8, v_new8, cache_k2, cache_v2, lq1, lk1, lq2, lk2, sub_w, lam_init):
    B, n_pages = page_table.shape
    pw = PAGE * N_KV_HEADS
    small = lambda n: pl.BlockSpec((1, n), lambda b, pt: (0, 0))
    per_b = pl.BlockSpec((None, 8, LANES), lambda b, pt: (b, 0, 0))
    new_tok = pl.BlockSpec((None, LANES, LANES), lambda b, pt: (b, 0, 0))
    page_spec = lambda j: pl.BlockSpec((pw, LANES), lambda b, pt: (pt[b, j], 0))
    grid_spec = pltpu.PrefetchScalarGridSpec(
        num_scalar_prefetch=1,
        grid=(B,),
        in_specs=[small(DK), small(DK), small(DK), small(DK), small(DV), per_b, new_tok, new_tok]
                 + [page_spec(j) for j in range(n_pages)] + [page_spec(j) for j in range(n_pages)],
        out_specs=per_b,
    )
    return pl.pallas_call(
        functools.partial(_decode_attn_kernel, n_pages=n_pages, lam_init=lam_init),
        grid_spec=grid_spec,
        out_shape=jax.ShapeDtypeStruct((B, 8, LANES), BF16),
        compiler_params=_cparams(("arbitrary",)),
        name="decode_attn",
    )(page_table, lq1, lk1, lq2, lk2, sub_w, q_s, k_new8, v_new8,
      *([cache_k2] * n_pages), *([cache_v2] * n_pages))


def _dec_pre_kernel(xbc_ref, s0_ref, s1_ref, s2_ref, dt_ref, cw_ref, cb_ref, dtb_ref, a_ref, e_ref,
                    xc_ref, xdt_ref, dect_ref):
    conv = (cb_ref[...] + cw_ref[0:1, :] * s0_ref[...] + cw_ref[1:2, :] * s1_ref[...]
            + cw_ref[2:3, :] * s2_ref[...] + cw_ref[3:4, :] * xbc_ref[...])
    xc = _silu(conv)
    xc_ref[...] = xc
    dt = _softplus(dt_ref[...] + dtb_ref[...])
    dec = jnp.exp(dt * a_ref[...])
    hi, mid, lo = _split3(dt)
    e = e_ref[...]
    dt_full = (jnp.dot(hi, e, preferred_element_type=F32) + jnp.dot(mid, e, preferred_element_type=F32)
               + jnp.dot(lo, e, preferred_element_type=F32))
    hi, mid, lo = _split3(dec)
    dec_full = (jnp.dot(hi, e, preferred_element_type=F32) + jnp.dot(mid, e, preferred_element_type=F32)
                + jnp.dot(lo, e, preferred_element_type=F32))
    xdt_ref[...] = (dt_full * xc[:, :D_INNER]).T
    dect_ref[...] = dec_full.T


def _dec_state_kernel(h_ref, xdt_ref, dect_ref, bm_ref, cm_ref, hout_ref, y_ref):
    i = pl.program_id(0)
    B = xdt_ref.shape[1]
    lane = lax.broadcasted_iota(I32, (D_INNER, B), 1)
    sub = lax.broadcasted_iota(I32, (TB, D_INNER // N_BC_GROUPS), 0)
    ones = jnp.ones((B, D_STATE), BF16)
    half = D_INNER // N_BC_GROUPS
    nt = (((1,), (1,)), ((), ()))
    y_acc = [jnp.zeros((TB, half), F32) for _ in range(N_BC_GROUPS)]
    for j in range(TB):
        b = i * TB + j
        sel = lane == b
        xcol = jnp.where(sel, xdt_ref[...], 0.0)
        dcol = jnp.where(sel, dect_ref[...], 0.0)
        dec = sum(jnp.dot(part, ones, preferred_element_type=F32) for part in _split3(dcol))
        xhi, xlo = _split2(xcol)
        hnew = dec * h_ref[j]
        for g in range(N_BC_GROUPS):
            bmat = bm_ref[:, g * D_STATE:(g + 1) * D_STATE].astype(BF16)
            upd = (jnp.dot(xhi[g * half:(g + 1) * half], bmat, preferred_element_type=F32)
                   + jnp.dot(xlo[g * half:(g + 1) * half], bmat, preferred_element_type=F32))
            hg = hnew[g * half:(g + 1) * half] + upd
            hout_ref[j, g * half:(g + 1) * half, :] = hg
            start = pl.multiple_of(i * TB, TB)
            c8 = cm_ref[pl.ds(start, TB), g * D_STATE:(g + 1) * D_STATE].astype(BF16)
            yj = lax.dot_general(c8, hg.astype(BF16), nt, preferred_element_type=F32)
            y_acc[g] = y_acc[g] + jnp.where(sub == j, yj, 0.0)
    y_ref[...] = jnp.concatenate(y_acc, axis=1)


def _dec_post_kernel(y_ref, xc_ref, z_ref, dskip_ref, normw_ref, o_ref):
    o_ref[...] = _gated_out(y_ref[...], xc_ref[:, :D_INNER], z_ref[...], dskip_ref[...],
                            normw_ref[...]).astype(BF16)


def _decode_ssm(xbc_s, sc0, sc1, sc2, dt_s, z_s, state, conv_w, conv_b, dtb, a_row, dskip_full, normw, e_mat):
    B = xbc_s.shape[0]
    xc, xdt_t, dec_t = pl.pallas_call(
        _dec_pre_kernel,
        out_shape=[jax.ShapeDtypeStruct((B, XBC_DIM), F32),
                   jax.ShapeDtypeStruct((D_INNER, B), F32),
                   jax.ShapeDtypeStruct((D_INNER, B), F32)],
        compiler_params=pltpu.CompilerParams(vmem_limit_bytes=VMEM_LIMIT),
        name="decode_ssm_pre",
    )(xbc_s, sc0, sc1, sc2, dt_s, conv_w, conv_b, dtb, a_row, e_mat)
    bm = xc[:, D_INNER:D_INNER + N_BC_GROUPS * D_STATE]
    cm = xc[:, D_INNER + N_BC_GROUPS * D_STATE:]
    const = lambda shp: pl.BlockSpec(shp, lambda i: (0,) * len(shp))
    state_spec = pl.BlockSpec((TB, D_INNER, D_STATE), lambda i: (i, 0, 0))
    h_new, y = pl.pallas_call(
        _dec_state_kernel,
        grid=(B // TB,),
        in_specs=[state_spec, const((D_INNER, B)), const((D_INNER, B)),
                  const((B, N_BC_GROUPS * D_STATE)), const((B, N_BC_GROUPS * D_STATE))],
        out_specs=[state_spec, pl.BlockSpec((TB, D_INNER), lambda i: (i, 0))],
        out_shape=[jax.ShapeDtypeStruct((B, D_INNER, D_STATE), F32),
                   jax.ShapeDtypeStruct((B, D_INNER), F32)],
        compiler_params=_cparams(("arbitrary",)),
        name="decode_ssm_state",
    )(state, xdt_t, dec_t, bm, cm)
    ssm = pl.pallas_call(
        _dec_post_kernel,
        out_shape=jax.ShapeDtypeStruct((B, D_INNER), BF16),
        compiler_params=pltpu.CompilerParams(vmem_limit_bytes=VMEM_LIMIT),
        name="decode_ssm_post",
    )(y, xc, z_s, dskip_full, normw)
    return ssm, h_new


def _outproj_kernel(x_ref, att_ref, ssm_ref, wa_ref, ws_ref, nw_ref, wrh_ref, wrl_ref,
                    h_ref, u_ref, lg_ref):
    acc = jnp.dot(att_ref[...], wa_ref[...], preferred_element_type=F32)
    acc = acc + jnp.dot(ssm_ref[...], ws_ref[...], preferred_element_type=F32)
    h = x_ref[...] + acc
    h_ref[...] = h
    ms = jnp.mean(h * h, axis=-1, keepdims=True)
    u = h * lax.rsqrt(ms + EPS) * nw_ref[...]
    u_ref[...] = u
    uh, ul = _split2(u)
    wh, wl = wrh_ref[...], wrl_ref[...]
    lg = (jnp.dot(uh, wh, preferred_element_type=F32) + jnp.dot(uh, wl, preferred_element_type=F32)
          + jnp.dot(ul, wh, preferred_element_type=F32))
    lg_ref[...] = _route_rows(lg, pl.program_id(0) * TM)


def _first_max(x, lane):
    m = jnp.max(x, axis=-1, keepdims=True)
    return m, jnp.min(jnp.where(x == m, lane, float(LANES)), axis=-1, keepdims=True)


def _route_rows(lg, row0):
    rows = lg.shape[0]
    lane_i = lax.broadcasted_iota(I32, (rows, LANES), 1)
    lane = lane_i.astype(F32)
    gl = jnp.where(lane_i < N_EGROUPS, lg, -jnp.inf)
    gmax, g = _first_max(gl, lane)
    pg = 1.0 / jnp.sum(jnp.exp(gl - gmax), axis=-1, keepdims=True)
    lo = N_EGROUPS + g * EXPERTS_PER_GROUP
    el = jnp.where((lane >= lo) & (lane < lo + EXPERTS_PER_GROUP), lg, -jnp.inf)
    m1, i1 = _first_max(el, lane)
    m2, i2 = _first_max(jnp.where(lane == i1, -jnp.inf, el), lane)
    z = jnp.sum(jnp.exp(el - m1), axis=-1, keepdims=True)
    p1 = 1.0 / z
    p2 = jnp.exp(m2 - m1) / z
    tot = p1 + p2
    valid = (row0 + lax.broadcasted_iota(I32, (rows, 1), 0)) >= PAD_ROWS
    e1 = jnp.where(valid, i1 - N_EGROUPS, -1.0)
    e2 = jnp.where(valid, i2 - N_EGROUPS, -1.0)
    w1 = jnp.where(valid, p1 / tot * pg, 0.0)
    w2 = jnp.where(valid, p2 / tot * pg, 0.0)
    return jnp.where(lane_i == 0, e1, jnp.where(lane_i == 1, e2, jnp.where(lane_i == 2, w1,
                                                                         jnp.where(lane_i == 3, w2, 0.0))))


def _rank_kernel(route_ref, tri_ref, utri_ref, rank_ref, meta_ref, carry, *, nblk):
    i = pl.program_id(0)

    @pl.when(i == 0)
    def _():
        carry[...] = jnp.zeros(carry.shape, F32)

    r = route_ref[...]
    lane = lax.broadcasted_iota(I32, r.shape, 1).astype(F32)
    oh0 = (lane == r[:, 0:1]).astype(F32)
    oh1 = (lane == r[:, 1:2]).astype(F32)
    oh = oh0 + oh1
    before = jnp.dot(tri_ref[...], oh.astype(BF16), preferred_element_type=F32) + carry[0:1, :]
    rank0 = jnp.sum(before * oh0, axis=-1, keepdims=True)
    rank1 = jnp.sum(before * oh1, axis=-1, keepdims=True)
    lane_i = lax.broadcasted_iota(I32, r.shape, 1)
    rank_ref[...] = jnp.where(lane_i == 0, rank0, jnp.where(lane_i == 1, rank1, 0.0))
    carry[0:1, :] = carry[0:1, :] + jnp.sum(oh, axis=0, keepdims=True)

    @pl.when(i == pl.num_programs(0) - 1)
    def _():
        nb_pad = meta_ref.shape[0] - 8
        counts = carry[0:1, :]
        pb = jnp.floor((counts + (TMOE - 1)) / TMOE)
        ends = jnp.dot(jnp.broadcast_to(pb, (8, LANES)).astype(BF16), utri_ref[...],
                       preferred_element_type=F32)[0:1, :]
        starts = ends - pb
        lane1 = lax.broadcasted_iota(I32, (1, LANES), 1)
        n_valid = jnp.sum(jnp.where(lane1 == N_EXPERTS - 1, ends, 0.0), axis=-1, keepdims=True)
        blk = lax.broadcasted_iota(I32, (nb_pad, LANES), 0).astype(F32)
        lane_b = lax.broadcasted_iota(I32, (nb_pad, LANES), 1)
        is_e = lane_b < N_EXPERTS
        block_e = jnp.minimum(jnp.sum(jnp.where(is_e & (ends <= blk), 1.0, 0.0), axis=-1, keepdims=True),
                              N_EXPERTS - 1.0)
        lane_bf = lane_b.astype(F32)
        has = is_e & (counts > 0.0) & (lane_bf > block_e)
        nxt = jnp.min(jnp.where(has, lane_bf, float(LANES)), axis=-1, keepdims=True)
        next_e = jnp.where(nxt >= LANES, -1.0, nxt)
        meta_ref[0:8, :] = jnp.where(lax.broadcasted_iota(I32, (8, LANES), 0) == 0, starts,
                                     jnp.broadcast_to(n_valid, (8, LANES)))
        meta_ref[8:, :] = jnp.where(lane_b == 0, block_e, jnp.where(lane_b == 1, next_e, 0.0))


def _rank(route, nblk):
    R = route.shape[0]
    nb_pad = (nblk + 7) // 8 * 8
    tri = (jnp.arange(TM)[:, None] > jnp.arange(TM)[None, :]).astype(BF16)
    utri = (jnp.arange(LANES)[:, None] <= jnp.arange(LANES)[None, :]).astype(BF16)
    return pl.pallas_call(
        functools.partial(_rank_kernel, nblk=nblk),
        grid=(R // TM,),
        in_specs=[pl.BlockSpec((TM, LANES), lambda i: (i, 0)),
                  pl.BlockSpec((TM, TM), lambda i: (0, 0)),
                  pl.BlockSpec((LANES, LANES), lambda i: (0, 0))],
        out_specs=[pl.BlockSpec((TM, LANES), lambda i: (i, 0)),
                   pl.BlockSpec((8 + nb_pad, LANES), lambda i: (0, 0))],
        out_shape=[jax.ShapeDtypeStruct((R, LANES), F32),
                   jax.ShapeDtypeStruct((8 + nb_pad, LANES), F32)],
        scratch_shapes=[pltpu.VMEM((8, LANES), F32)],
        compiler_params=_cparams(("arbitrary",)),
        name="expert_rank",
    )(route, tri, utri)


def _outproj(xall, att, ssm, wa, ws, norm_w, wr_hi, wr_lo):
    R = xall.shape[0]
    row = lambda n: pl.BlockSpec((TM, n), lambda i: (i, 0))
    return pl.pallas_call(
        _outproj_kernel,
        grid=(R // TM,),
        in_specs=[row(D_MODEL), row(N_HEADS_A * DV), row(D_INNER),
                  _resident(wa.shape), _resident(ws.shape), _resident((1, D_MODEL)),
                  _resident(wr_hi.shape), _resident(wr_lo.shape)],
        out_specs=[row(D_MODEL), row(D_MODEL), row(LANES)],
        out_shape=[jax.ShapeDtypeStruct((R, D_MODEL), F32),
                   jax.ShapeDtypeStruct((R, D_MODEL), F32),
                   jax.ShapeDtypeStruct((R, LANES), F32)],
        compiler_params=_cparams(("arbitrary",)),
        name="outproj",
    )(xall, att, ssm, wa, ws, norm_w, wr_hi, wr_lo)


GATHER_UNROLL = 8

def _start_row_gather(idx_ref, src_hbm, dst, sem):
    n = dst.shape[0]

    def start(r8, carry):
        for j in range(GATHER_UNROLL):
            r = r8 * GATHER_UNROLL + j
            pltpu.make_async_copy(src_hbm.at[pl.ds(idx_ref[0, r], 1)], dst.at[pl.ds(r, 1)],
                                  sem).start(priority=j % 2)
        return carry

    lax.fori_loop(0, n // GATHER_UNROLL, start, 0)


def _wait_row_gather(src_hbm, dst, sem):
    pltpu.make_async_copy(src_hbm.at[pl.ds(0, dst.shape[0])], dst, sem).wait()


def _moe_up_kernel(be_ref, nx_ref, nv_ref, idx_ref, idx_next_ref, x_hbm, wg_hbm, wu_hbm, h_ref,
                   xbuf, stage_g, stage_u, wg_b, wu_b, sems, row_sems):
    i = pl.program_id(0)
    valid = i < nv_ref[0]
    e = be_ref[i]
    changed = (i == 0) | (be_ref[jnp.maximum(i - 1, 0)] != e)
    slot = i % 2

    @pl.when(i == 0)
    def _():
        _start_row_gather(idx_ref, x_hbm, xbuf.at[0], row_sems.at[0])

    @pl.when(i + 1 < nv_ref[0])
    def _():
        _start_row_gather(idx_next_ref, x_hbm, xbuf.at[1 - slot], row_sems.at[1 - slot])

    def fetch(expert):
        return (pltpu.make_async_copy(wg_hbm.at[expert], stage_g, sems.at[0]),
                pltpu.make_async_copy(wu_hbm.at[expert], stage_u, sems.at[1]))

    @pl.when(i == 0)
    def _():
        for c in fetch(e):
            c.start()

    @pl.when(valid & changed)
    def _():
        for c in fetch(e):
            c.wait()
        wg_b[...] = stage_g[...].astype(BF16)
        wu_b[...] = stage_u[...].astype(BF16)
        nxt = nx_ref[i]

        @pl.when(nxt >= 0)
        def _():
            for c in fetch(nxt):
                c.start()

    @pl.when(valid)
    def _():
        _wait_row_gather(x_hbm, xbuf.at[slot], row_sems.at[slot])
        x = xbuf[slot].astype(BF16)
        g = jnp.dot(x, wg_b[...], preferred_element_type=F32)
        u = jnp.dot(x, wu_b[...], preferred_element_type=F32)
        h_ref[...] = (_silu(g) * u).astype(BF16)

    @pl.when(jnp.logical_not(valid))
    def _():
        h_ref[...] = jnp.zeros(h_ref.shape, BF16)


def _moe_down_kernel(be_ref, nv_ref, h_ref, wd_ref, o_ref, wd_b):
    i = pl.program_id(0)
    valid = i < nv_ref[0]
    changed = (i == 0) | (be_ref[jnp.maximum(i - 1, 0)] != be_ref[i])

    @pl.when(valid & changed)
    def _():
        wd_b[...] = wd_ref[...].astype(BF16)

    @pl.when(valid)
    def _():
        o_ref[...] = jnp.dot(h_ref[...], wd_b[...], preferred_element_type=F32)

    @pl.when(jnp.logical_not(valid))
    def _():
        o_ref[...] = jnp.zeros(o_ref.shape, F32)


def _moe(block_e, next_e, n_valid, buf_tok, x_tok, w_gate, w_up, w_down):
    rows = buf_tok.shape[0]
    nblk = rows // TMOE

    def blk(i, nv):
        return jnp.minimum(i, nv[0] - 1)

    up_spec = pltpu.PrefetchScalarGridSpec(
        num_scalar_prefetch=3,
        grid=(nblk,),
        in_specs=[pl.BlockSpec((None, 1, TMOE), lambda i, be, nx, nv: (blk(i, nv), 0, 0),
                               memory_space=pltpu.SMEM),
                  pl.BlockSpec((None, 1, TMOE), lambda i, be, nx, nv: (blk(i + 1, nv), 0, 0),
                               memory_space=pltpu.SMEM),
                  pl.BlockSpec(memory_space=pl.ANY),
                  pl.BlockSpec(memory_space=pl.ANY), pl.BlockSpec(memory_space=pl.ANY)],
        out_specs=pl.BlockSpec((TMOE, D_FF), lambda i, be, nx, nv: (i, 0)),
        scratch_shapes=[pltpu.VMEM((2, TMOE, D_MODEL), F32),
                        pltpu.VMEM((D_MODEL, D_FF), F32), pltpu.VMEM((D_MODEL, D_FF), F32),
                        pltpu.VMEM((D_MODEL, D_FF), BF16), pltpu.VMEM((D_MODEL, D_FF), BF16),
                        pltpu.SemaphoreType.DMA((2,)), pltpu.SemaphoreType.DMA((2,))],
    )
    slot_rows = buf_tok.reshape(nblk, 1, TMOE)
    hmid = pl.pallas_call(
        _moe_up_kernel,
        grid_spec=up_spec,
        out_shape=jax.ShapeDtypeStruct((rows, D_FF), BF16),
        compiler_params=_cparams(("arbitrary",)),
        name="moe_up",
    )(block_e, next_e, n_valid, slot_rows, slot_rows, x_tok, w_gate, w_up)

    down_spec = pltpu.PrefetchScalarGridSpec(
        num_scalar_prefetch=2,
        grid=(nblk,),
        in_specs=[pl.BlockSpec((TMOE, D_FF), lambda i, be, nv: (blk(i, nv), 0)),
                  pl.BlockSpec((None, D_FF, D_MODEL), lambda i, be, nv: (be[blk(i, nv)], 0, 0))],
        out_specs=pl.BlockSpec((TMOE, D_MODEL), lambda i, be, nv: (i, 0)),
        scratch_shapes=[pltpu.VMEM((D_FF, D_MODEL), BF16)],
    )
    return pl.pallas_call(
        _moe_down_kernel,
        grid_spec=down_spec,
        out_shape=jax.ShapeDtypeStruct((rows, D_MODEL), F32),
        compiler_params=_cparams(("arbitrary",)),
        name="moe_down",
    )(block_e, n_valid, hmid, w_down)


def _final_kernel(idx_ref, idx_next_ref, h_ref, w_ref, nw_ref, y_hbm, o_ref, ybuf, row_sems):
    i = pl.program_id(0)
    slot = i % 2

    @pl.when(i == 0)
    def _():
        _start_row_gather(idx_ref, y_hbm, ybuf.at[0], row_sems.at[0])

    @pl.when(i + 1 < pl.num_programs(0))
    def _():
        _start_row_gather(idx_next_ref, y_hbm, ybuf.at[1 - slot], row_sems.at[1 - slot])

    _wait_row_gather(y_hbm, ybuf.at[slot], row_sems.at[slot])
    w = w_ref[...]
    moe = ybuf[slot, :TM, :] * w[:, 0:1] + ybuf[slot, TM:, :] * w[:, 1:2]
    h = h_ref[...] + moe
    ms = jnp.mean(h * h, axis=-1, keepdims=True)
    o_ref[...] = h * lax.rsqrt(ms + EPS) * nw_ref[...]


def _final(h1, yb, dest, wts, norm_w):
    R = h1.shape[0]
    n = R // TM
    row = lambda w: pl.BlockSpec((TM, w), lambda i: (i, 0))
    slots = dest.reshape(n, TM, TOP_K).transpose(0, 2, 1).reshape(n, 1, TOP_K * TM)
    idx_spec = lambda off: pl.BlockSpec((None, 1, TOP_K * TM), lambda i: (jnp.minimum(i + off, n - 1), 0, 0),
                                        memory_space=pltpu.SMEM)
    return pl.pallas_call(
        _final_kernel,
        grid=(n,),
        in_specs=[idx_spec(0), idx_spec(1), row(D_MODEL), row(TOP_K),
                  pl.BlockSpec((1, D_MODEL), lambda i: (0, 0)),
                  pl.BlockSpec(memory_space=pl.ANY)],
        out_specs=row(D_MODEL),
        out_shape=jax.ShapeDtypeStruct((R, D_MODEL), F32),
        scratch_shapes=[pltpu.VMEM((2, TOP_K * TM, D_MODEL), F32), pltpu.SemaphoreType.DMA((2,))],
        compiler_params=_cparams(("arbitrary",)),
        name="final_norm",
    )(slots, slots, h1, wts, norm_w, yb)


def _route(route):
    R = route.shape[0]
    a_max = TOP_K * R
    nblk = (a_max + N_EXPERTS * (TMOE - 1) + TMOE - 1) // TMOE
    rows = nblk * TMOE
    rank, meta = _rank(route, nblk)
    eid = route[:, :TOP_K].astype(I32)
    wts = route[:, TOP_K:2 * TOP_K]
    starts = meta[0, :N_EXPERTS].astype(I32) * TMOE
    n_valid = meta[1, :1].astype(I32)
    block_e = meta[8:8 + nblk, 0].astype(I32)
    next_e = meta[8:8 + nblk, 1].astype(I32)
    is_real = eid >= 0
    dest = jnp.where(is_real, starts[jnp.maximum(eid, 0)] + rank[:, :TOP_K].astype(I32), 0)
    tok = jnp.broadcast_to(jnp.arange(R, dtype=I32)[:, None], (R, TOP_K))
    buf_tok = (jnp.arange(rows, dtype=I32) % PAD_ROWS).at[jnp.where(is_real, dest, rows).reshape(-1)].set(
        tok.reshape(-1), mode="drop")
    return wts, dest, buf_tok, block_e, next_e, n_valid


def kernel(x_prompt, x_sample, cache_k, cache_v, state_conv, state_ssm, page_table, meta_tokens, norm_mix_w,
           w_in, lambda_q1, lambda_k1, lambda_q2, lambda_k2, subln_w, conv_w, conv_b, dt_bias, a_log, d_skip,
           ssm_norm_w, w_out, norm_ffn_w, w_router_group, w_router_expert, w_gate, w_up, w_down, final_norm_w):
    depth = w_in.shape[0]
    assert depth == 1 and x_prompt.shape[0] == 1 and x_sample.shape[1] == 1
    seq = x_prompt.shape[1]
    B = x_sample.shape[0]
    n_pages = page_table.shape[1]
    n_phys = cache_k.shape[1]
    past = n_pages * cache_k.shape[2]
    assert cache_k.shape[2] == PAGE and seq % CHUNK == 0 and B == LANES and (FRONT + seq + B) % TM == 0
    n_t = FRONT + seq
    R = n_t + B
    l = 0
    lam_init = 0.8 - 0.6 * math.exp(-0.3 * l)

    xall = jnp.concatenate([jnp.zeros((PAD_ROWS, D_MODEL), F32), meta_tokens.astype(F32),
                            x_prompt[0], x_sample[:, 0]], axis=0)
    w = w_in[l]
    wq = w[:, :OFF_K].astype(BF16)
    wk = w[:, OFF_K:OFF_V].astype(BF16)
    wv = w[:, OFF_V:OFF_Z].astype(BF16)
    wz = w[:, OFF_Z:OFF_XBC].astype(BF16)
    wx = w[:, OFF_XBC:OFF_DT].astype(BF16)
    wdt = jnp.pad(w[:, OFF_DT:], ((0, 0), (0, LANES - N_HEADS_S))).astype(BF16)

    pos = jnp.concatenate([jnp.maximum(jnp.arange(n_t, dtype=I32) - PAD_ROWS, 0),
                           jnp.full((B,), past, I32)])
    inv = ROPE_THETA ** (-jnp.arange(0, ROT_DIM, 2, dtype=F32) / ROT_DIM)
    ang = pos.astype(F32)[:, None] * inv[None, :]
    cos, sin = jnp.cos(ang), jnp.sin(ang)
    half = ROT_DIM // 2
    ones = jnp.ones((R, DK - ROT_DIM), F32)
    zeros = jnp.zeros((R, DK - ROT_DIM), F32)
    zh = jnp.zeros((R, half), F32)
    ctab = jnp.tile(jnp.concatenate([cos, cos, ones], axis=1), (1, 2))
    s1tab = jnp.tile(jnp.concatenate([-sin, zh, zeros], axis=1), (1, 2))
    s2tab = jnp.tile(jnp.concatenate([zh, sin, zeros], axis=1), (1, 2))

    row1 = lambda v: v.reshape(1, -1).astype(F32)
    pad_heads = lambda v: jnp.pad(v.astype(F32), (0, LANES - N_HEADS_S)).reshape(1, LANES)
    a_row = pad_heads(-jnp.exp(a_log[l].astype(F32)))
    dtb = pad_heads(dt_bias[l])
    dskip_full = jnp.repeat(d_skip[l].astype(F32), SSM_HEAD_DIM).reshape(1, D_INNER)
    normw = row1(ssm_norm_w[l])
    head_of_col = jnp.arange(D_INNER, dtype=I32) // SSM_HEAD_DIM
    e_mat = (jnp.arange(LANES, dtype=I32)[:, None] == head_of_col[None, :]).astype(BF16)
    tri = (jnp.arange(CHUNK)[:, None] >= jnp.arange(CHUNK)[None, :]).astype(BF16)
    lq1, lk1, lq2, lk2 = (row1(v[l]) for v in (lambda_q1, lambda_k1, lambda_q2, lambda_k2))
    sub_w = row1(subln_w[l])

    q, k, v, kb, vb, z, xbc, dt_raw = _inproj(xall, row1(norm_mix_w[l]), wq, wk, wv, wz, wx, wdt,
                                              ctab, s1tab, s2tab)

    att_all = _prompt_attention(q, kb, vb, lq1, lk1, lq2, lk2, sub_w, lam_init)
    ssm_p, state_p = _prompt_ssd(xbc, z, dt_raw, conv_w[l].astype(F32), row1(conv_b[l]), dtb, a_row,
                                 dskip_full, normw, e_mat, e_mat.T.astype(F32), tri, n_t)

    k_s, v_s = k[n_t:], v[n_t:]
    pad8 = lambda a: jnp.pad(a.reshape(B, N_KV_HEADS, LANES), ((0, 0), (0, LANES - N_KV_HEADS), (0, 0)))
    att_s = _decode_attention(page_table, q[n_t:].reshape(B, N_HEADS_A, LANES), pad8(k_s), pad8(v_s),
                              cache_k[l].reshape(n_phys * PAGE * N_KV_HEADS, LANES),
                              cache_v[l].reshape(n_phys * PAGE * N_KV_HEADS, LANES),
                              lq1, lk1, lq2, lk2, sub_w, lam_init)
    sc = state_conv[l]
    xbc_s = xbc[n_t:]
    ssm_s, state_s = _decode_ssm(xbc_s, sc[:, 0], sc[:, 1], sc[:, 2], dt_raw[n_t:], z[n_t:],
                                 state_ssm[l].reshape(B, D_INNER, D_STATE), conv_w[l].astype(F32),
                                 row1(conv_b[l]), dtb, a_row, dskip_full, normw, e_mat)

    att = lax.dynamic_update_slice(att_all, att_s.reshape(B, N_HEADS_A * DV), (n_t, 0))
    ssm = jnp.concatenate([ssm_p, ssm_s], axis=0)

    wo = w_out[l].astype(BF16)
    wr = jnp.pad(jnp.concatenate([w_router_group[l], w_router_expert[l]], axis=1).astype(F32),
                 ((0, 0), (0, LANES - N_EGROUPS - N_EXPERTS)))
    wr_hi = wr.astype(BF16)
    wr_lo = (wr - wr_hi.astype(F32)).astype(BF16)
    h1, u2, route = _outproj(xall, att, ssm, wo[:N_HEADS_A * DV], wo[N_HEADS_A * DV:],
                             row1(norm_ffn_w[l]), wr_hi, wr_lo)

    wts, dest, buf_tok, block_e, next_e, n_valid = _route(route)
    yb = _moe(block_e, next_e, n_valid, buf_tok, u2, w_gate[l], w_up[l], w_down[l])
    y = _final(h1, yb, dest, wts, row1(final_norm_w))

    t_real = N_META + seq
    y_prompt = y[FRONT:n_t].reshape(1, seq, D_MODEL)
    y_sample = y[n_t:].reshape(B, 1, D_MODEL)
    k_prompt = k[PAD_ROWS:n_t].reshape(1, 1, t_real, N_KV_HEADS, 2 * DK)
    v_prompt = v[PAD_ROWS:n_t].reshape(1, 1, t_real, N_KV_HEADS, DV)
    conv_prompt = xbc[n_t - (CONV_W - 1):n_t].reshape(1, 1, CONV_W - 1, XBC_DIM)
    ssm_prompt = state_p.reshape(1, 1, N_HEADS_S, SSM_HEAD_DIM, D_STATE)
    k_sample = k_s.reshape(1, B, 1, N_KV_HEADS, 2 * DK)
    v_sample = v_s.reshape(1, B, 1, N_KV_HEADS, DV)
    conv_sample = jnp.stack([sc[:, 1], sc[:, 2], xbc_s], axis=1)[None]
    ssm_sample = state_s.reshape(1, B, N_HEADS_S, SSM_HEAD_DIM, D_STATE)
    return (y_prompt, y_sample, k_prompt, v_prompt, conv_prompt, ssm_prompt,
            k_sample, v_sample, conv_sample, ssm_sample)
```

```python
import functools
import math

import jax
import jax.numpy as jnp
from jax import lax
from jax.experimental import pallas as pl
from jax.experimental.pallas import tpu as pltpu

F32 = jnp.float32
BF16 = jnp.bfloat16
I32 = jnp.int32

D_MODEL = 2048
N_META = 16
EPS = 1e-6
DV = 128
DK = 64
N_HEADS_A = 8
N_KV_HEADS = 4
ROT_DIM = 16
ROPE_THETA = 500000.0
NEG_INF = -1e30
LOG2E = 1.4426950408889634
D_INNER = 1024
SSM_HEAD_DIM = 64
N_HEADS_S = 16
N_BC_GROUPS = 2
D_STATE = 128
CONV_W = 4
XBC_DIM = D_INNER + 2 * N_BC_GROUPS * D_STATE
CHUNK = 128
Q_DIM = N_HEADS_A * 2 * DK
K_DIM = N_KV_HEADS * 2 * DK
V_DIM = N_KV_HEADS * DV
OFF_K = Q_DIM
OFF_V = OFF_K + K_DIM
OFF_Z = OFF_V + V_DIM
OFF_XBC = OFF_Z + D_INNER
OFF_DT = OFF_XBC + XBC_DIM
IN_DIM = OFF_DT + N_HEADS_S
N_EGROUPS = 4
EXPERTS_PER_GROUP = 8
N_EXPERTS = N_EGROUPS * EXPERTS_PER_GROUP
TOP_K = 2
D_FF = 1024
PAGE = 128

LANES = 128
VMEM_LIMIT = 56 * 1024 * 1024

FRONT = CHUNK
PAD_ROWS = FRONT - N_META
TM = 256
TQ = 256
TK = 512
TKF = 1024
TB = 8
TMOE = 512
TSMALL = 128
TF = 256


def _cparams(sem):
    return pltpu.CompilerParams(dimension_semantics=sem, vmem_limit_bytes=VMEM_LIMIT)


def _resident(shape):
    nd = len(shape)
    return pl.BlockSpec(shape, lambda *_: (0,) * nd, pipeline_mode=pl.Buffered(1))


def _split2(x):
    hi = x.astype(BF16)
    lo = (x - hi.astype(F32)).astype(BF16)
    return hi, lo


def _split3(x):
    hi = x.astype(BF16)
    r = x - hi.astype(F32)
    mid = r.astype(BF16)
    lo = (r - mid.astype(F32)).astype(BF16)
    return hi, mid, lo


def _silu(x):
    return x * (1.0 / (1.0 + jnp.exp(-x)))


def _softplus(x):
    return jnp.maximum(x, 0.0) + jnp.log(1.0 + jnp.exp(-jnp.abs(x)))


HALF = TM // 2


def _token_specs(seq):
    nb = seq // HALF
    return [pl.BlockSpec((HALF, D_MODEL), lambda i: (0, 0)),
            pl.BlockSpec((HALF, D_MODEL), lambda i: (jnp.clip(2 * i - 1, 0, nb - 1), 0)),
            pl.BlockSpec((HALF, D_MODEL), lambda i: (jnp.clip(2 * i, 0, nb - 1), 0)),
            pl.BlockSpec((HALF, D_MODEL), lambda i: (0, 0))]


def _token_rows(front_ref, xa_ref, xb_ref, xs_ref):
    i = pl.program_id(0)
    top = jnp.where(i == 0, front_ref[...], xa_ref[...])
    bot = jnp.where(i == pl.num_programs(0) - 1, xs_ref[...], xb_ref[...])
    return jnp.concatenate([top, bot], axis=0)


def _inproj_kernel(front_ref, xa_ref, xb_ref, xs_ref, nw_ref, wq_ref, wk_ref, wv_ref, wz_ref, wx_ref, wdt_ref,
                   c_ref, s1_ref, s2_ref,
                   q_ref, k_ref, v_ref, kb_ref, vb_ref, z_ref, xbc_ref, dt_ref):
    x = _token_rows(front_ref, xa_ref, xb_ref, xs_ref)
    ms = jnp.mean(x * x, axis=-1, keepdims=True)
    u = (x * lax.rsqrt(ms + EPS) * nw_ref[...]).astype(BF16)
    c, s1, s2 = c_ref[...], s1_ref[...], s2_ref[...]

    def rope(p):
        outs = []
        for h in range(p.shape[1] // LANES):
            xh = p[:, h * LANES:(h + 1) * LANES]
            outs.append(xh * c + pltpu.roll(xh, LANES - ROT_DIM // 2, 1) * s1
                        + pltpu.roll(xh, ROT_DIM // 2, 1) * s2)
        return jnp.concatenate(outs, axis=1)

    q = rope(jnp.dot(u, wq_ref[...], preferred_element_type=F32))
    q_ref[...] = (q * (DK ** -0.5 * LOG2E)).astype(BF16)
    k = rope(jnp.dot(u, wk_ref[...], preferred_element_type=F32))
    k_ref[...] = k
    kb_ref[...] = k.astype(BF16)
    v = jnp.dot(u, wv_ref[...], preferred_element_type=F32)
    v_ref[...] = v
    vb = v.astype(BF16)
    ones = jnp.ones((vb.shape[0], DV), BF16)
    vb_ref[...] = jnp.concatenate(
        [piece for g in range(N_KV_HEADS) for piece in (vb[:, g * DV:(g + 1) * DV], ones)], axis=1)
    z_ref[...] = jnp.dot(u, wz_ref[...], preferred_element_type=F32)
    xbc_ref[...] = jnp.dot(u, wx_ref[...], preferred_element_type=F32)
    dt_ref[...] = jnp.dot(u, wdt_ref[...], preferred_element_type=F32)


def _inproj(front, xp, xs, norm_w, wq, wk, wv, wz, wx, wdt, ctab, s1tab, s2tab):
    R = front.shape[0] + xp.shape[0] + xs.shape[0]
    row = lambda n: pl.BlockSpec((TM, n), lambda i: (i, 0))
    return pl.pallas_call(
        _inproj_kernel,
        grid=(R // TM,),
        in_specs=_token_specs(xp.shape[0]) + [_resident((1, D_MODEL)),
                  _resident(wq.shape), _resident(wk.shape), _resident(wv.shape),
                  _resident(wz.shape), _resident(wx.shape), _resident(wdt.shape),
                  row(LANES), row(LANES), row(LANES)],
        out_specs=[row(Q_DIM), row(K_DIM), row(V_DIM), row(K_DIM), row(2 * V_DIM),
                   row(D_INNER), row(XBC_DIM), row(LANES)],
        out_shape=[jax.ShapeDtypeStruct((R, Q_DIM), BF16),
                   jax.ShapeDtypeStruct((R, K_DIM), F32),
                   jax.ShapeDtypeStruct((R, V_DIM), F32),
                   jax.ShapeDtypeStruct((R, K_DIM), BF16),
                   jax.ShapeDtypeStruct((R, 2 * V_DIM), BF16),
                   jax.ShapeDtypeStruct((R, D_INNER), F32),
                   jax.ShapeDtypeStruct((R, XBC_DIM), F32),
                   jax.ShapeDtypeStruct((R, LANES), F32)],
        compiler_params=_cparams(("arbitrary",)),
        name="inproj",
    )(front, xp, xp, xs, norm_w, wq, wk, wv, wz, wx, wdt, ctab, s1tab, s2tab)


def _diff_lambda(lq1, lk1, lq2, lk2, lam_init):
    a = jnp.sum(lq1[...] * lk1[...], axis=-1, keepdims=True)
    b = jnp.sum(lq2[...] * lk2[...], axis=-1, keepdims=True)
    return jnp.exp(a) - jnp.exp(b) + lam_init


def _subln(o, sub_w, lam_init):
    ms = jnp.mean(o * o, axis=-1, keepdims=True)
    return (o * lax.rsqrt(ms + EPS) * sub_w) * (1.0 - lam_init)


def _attn_kernel(lq1, lk1, lq2, lk2, sub_ref, q_ref, k_ref, ve_ref, o_ref,
                 qs_scr, m_scr, acc_scr, *, lam_init, n_rows):
    qi = pl.program_id(1)
    lam = _diff_lambda(lq1, lk1, lq2, lk2, lam_init)

    q = q_ref[...]
    lane = lax.broadcasted_iota(I32, (TQ, LANES), 1)
    first = lane < DK
    zero = jnp.zeros((TQ, LANES), BF16)
    for hh in range(2):
        qh = q[:, hh * LANES:(hh + 1) * LANES]
        qs_scr[(2 * hh) * TQ:(2 * hh + 1) * TQ, :] = jnp.where(first, qh, zero)
        qs_scr[(2 * hh + 1) * TQ:(2 * hh + 2) * TQ, :] = jnp.where(first, zero, qh)
    m_scr[...] = jnp.full(m_scr.shape, NEG_INF, F32)
    acc_scr[...] = jnp.zeros(acc_scr.shape, F32)

    nt = (((1,), (1,)), ((), ()))

    def process(start, width, keep):
        for part in range(4):
            rows = slice(part * TQ, (part + 1) * TQ)
            s = lax.dot_general(qs_scr[rows, :], k_ref[pl.ds(start, width), :], nt,
                                preferred_element_type=F32)
            if keep is not None:
                s = jnp.where(keep, s, NEG_INF)
            m_prev = m_scr[rows, :]
            m_new = jnp.maximum(m_prev, jnp.max(s, axis=-1, keepdims=True))
            alpha = jnp.exp2(m_prev - m_new)
            p = jnp.concatenate([jnp.exp2(s[:, t * LANES:(t + 1) * LANES] - m_new)
                                 for t in range(width // LANES)], axis=1).astype(BF16)
            pv = jnp.dot(p, ve_ref[pl.ds(start, width), :], preferred_element_type=F32)
            acc_scr[rows, :LANES] = alpha * acc_scr[rows, :LANES] + pv[:, :LANES]
            acc_scr[rows, LANES:] = alpha * acc_scr[rows, LANES:] + pv[:, LANES:]
            m_scr[rows, :] = m_new

    q_lo = qi * TQ
    n_full = jnp.maximum(q_lo - PAD_ROWS, 0) // TKF
    tail_lo = PAD_ROWS + n_full * TKF
    n_tail = (q_lo + TQ - tail_lo + TK - 1) // TK

    def full(c):
        process(pl.multiple_of(PAD_ROWS + c * TKF, 16), TKF, None)

    def full_pair(c2, carry):
        full(2 * c2)
        full(2 * c2 + 1)
        return carry

    lax.fori_loop(0, n_full // 2, full_pair, 0)

    @pl.when(n_full % 2 == 1)
    def _():
        full(n_full - 1)

    def tail(c, carry):
        nominal = tail_lo + c * TK
        start = pl.multiple_of(jnp.minimum(nominal, n_rows - TK), 16)
        col = lax.broadcasted_iota(I32, (TQ, TK), 1)
        qpos = q_lo + lax.broadcasted_iota(I32, (TQ, TK), 0)
        process(start, TK, (col <= qpos - start) & (col >= nominal - start))
        return carry

    lax.fori_loop(0, n_tail, tail, 0)

    acc = acc_scr[...]
    o = acc[:, :LANES] / acc[:, LANES:]
    sub_w = sub_ref[...]
    for hh in range(2):
        o1 = o[(2 * hh) * TQ:(2 * hh + 1) * TQ, :]
        o2 = o[(2 * hh + 1) * TQ:(2 * hh + 2) * TQ, :]
        o_ref[:, hh * LANES:(hh + 1) * LANES] = _subln(o1 - lam * o2, sub_w, lam_init).astype(BF16)


def _prompt_attention(q, kb, vbe, lq1, lk1, lq2, lk2, sub_w, lam_init):
    R = q.shape[0]
    assert R % TQ == 0 and (R - TK) % 16 == 0 and R >= TK
    small = lambda n: pl.BlockSpec((1, n), lambda g, i: (0, 0))
    return pl.pallas_call(
        functools.partial(_attn_kernel, lam_init=lam_init, n_rows=R),
        grid=(N_KV_HEADS, R // TQ),
        in_specs=[small(DK), small(DK), small(DK), small(DK), small(DV),
                  pl.BlockSpec((TQ, 2 * LANES), lambda g, i: (i, g)),
                  pl.BlockSpec((R, LANES), lambda g, i: (0, g)),
                  pl.BlockSpec((R, 2 * LANES), lambda g, i: (0, g))],
        out_specs=pl.BlockSpec((TQ, 2 * LANES), lambda g, i: (i, g)),
        out_shape=jax.ShapeDtypeStruct((R, N_HEADS_A * DV), BF16),
        scratch_shapes=[pltpu.VMEM((4 * TQ, LANES), BF16),
                        pltpu.VMEM((4 * TQ, LANES), F32),
                        pltpu.VMEM((4 * TQ, 2 * LANES), F32)],
        compiler_params=_cparams(("arbitrary", "arbitrary")),
        name="prompt_attn",
    )(lq1, lk1, lq2, lk2, sub_w, q, kb, vbe)


def _expand_heads(x, e_ref):
    hi, lo = _split2(x)
    e = e_ref[...]
    return (jnp.dot(hi, e, preferred_element_type=F32)
            + jnp.dot(lo, e, preferred_element_type=F32))


def _gated_out(y, xs, z, dskip_full, normw):
    y = (y + dskip_full * xs) * _silu(z)
    half = D_INNER // N_BC_GROUPS
    outs = []
    for g in range(N_BC_GROUPS):
        yg = y[:, g * half:(g + 1) * half]
        ms = jnp.mean(yg * yg, axis=-1, keepdims=True)
        outs.append(yg * lax.rsqrt(ms + EPS))
    return jnp.concatenate(outs, axis=1) * normw


def _ssd_kernel(xbc_ref, z_ref, dt_ref, cw_ref, cb_ref, dtb_ref, a_ref, dskip_ref, normw_ref,
                e_ref, et_ref, tri_ref, ssm_ref, state_ref, ext_scr, h_scr):
    c = pl.program_id(0)
    L = CHUNK

    @pl.when(c == 0)
    def _():
        ext_scr[0:8, :] = jnp.zeros((8, XBC_DIM), F32)
        h_scr[...] = jnp.zeros(h_scr.shape, F32)

    x_new = xbc_ref[...]
    ext_scr[8:8 + L, :] = x_new
    conv = cb_ref[...] + cw_ref[CONV_W - 1:CONV_W, :] * x_new
    for j in range(CONV_W - 1):
        sh = CONV_W - 1 - j
        conv = conv + cw_ref[j:j + 1, :] * ext_scr[8 - sh:8 - sh + L, :]
    ext_scr[0:8, :] = x_new[L - 8:L, :]
    xc = _silu(conv)
    xs = xc[:, :D_INNER]

    rowi = lax.broadcasted_iota(I32, (L, LANES), 0)
    coli = lax.broadcasted_iota(I32, (L, LANES), 1)
    valid = (c * L + rowi) >= PAD_ROWS
    dt = jnp.where(valid, _softplus(dt_ref[...] + dtb_ref[...]), 0.0)
    da = dt * a_ref[...]
    tri = tri_ref[...]
    acs = sum(jnp.dot(tri, part, preferred_element_type=F32) for part in _split3(da))
    acs_t = acs.T
    dt_t = dt.T
    causal = coli <= rowi
    last = acs[L - 1:L, :]

    w_state = _expand_heads(dt * jnp.exp(last - acs), e_ref)
    dec_out = _expand_heads(jnp.exp(acs), e_ref)
    dec_rows = jnp.sum(et_ref[...] * jnp.exp(last), axis=-1, keepdims=True)
    xw = xs * w_state
    xs_b = xs.astype(BF16)

    half = D_INNER // N_BC_GROUPS
    hpg = N_HEADS_S // N_BC_GROUPS
    y_parts = []
    for g in range(N_BC_GROUPS):
        bm = xc[:, D_INNER + g * D_STATE:D_INNER + (g + 1) * D_STATE].astype(BF16)
        cm = xc[:, D_INNER + (N_BC_GROUPS + g) * D_STATE:
                D_INNER + (N_BC_GROUPS + g + 1) * D_STATE].astype(BF16)
        cb = lax.dot_general(cm, bm, (((1,), (1,)), ((), ())), preferred_element_type=F32)
        h_prev = h_scr[g * half:(g + 1) * half, :]
        y_off = lax.dot_general(cm, h_prev.astype(BF16), (((1,), (1,)), ((), ())),
                                preferred_element_type=F32)
        y_off = y_off * dec_out[:, g * half:(g + 1) * half]
        yd = []
        for r in range(0, hpg, 2):
            pair = []
            for hh in (r, r + 1):
                h = g * hpg + hh
                seg = jnp.broadcast_to(acs[:, h:h + 1], (L, L)) - acs_t[h:h + 1, :]
                lmat = jnp.where(causal, jnp.exp(jnp.where(causal, seg, 0.0)), 0.0)
                pair.append((cb * lmat * dt_t[h:h + 1, :]).astype(BF16))
            h0 = g * hpg + r
            xp = xs_b[:, h0 * SSM_HEAD_DIM:(h0 + 2) * SSM_HEAD_DIM]
            ya = jnp.dot(pair[0], xp, preferred_element_type=F32)
            yb = jnp.dot(pair[1], xp, preferred_element_type=F32)
            yd.append(jnp.where(coli < SSM_HEAD_DIM, ya, yb))
        y_parts.append(jnp.concatenate(yd, axis=1) + y_off)
        xw_t = xw[:, g * half:(g + 1) * half].T.astype(BF16)
        st = jnp.dot(xw_t, bm, preferred_element_type=F32)
        h_scr[g * half:(g + 1) * half, :] = dec_rows[g * half:(g + 1) * half, :] * h_prev + st
    y = jnp.concatenate(y_parts, axis=1)

    ssm_ref[...] = _gated_out(y, xs, z_ref[...], dskip_ref[...], normw_ref[...]).astype(BF16)

    @pl.when(c == pl.num_programs(0) - 1)
    def _():
        state_ref[...] = h_scr[...]


def _prompt_ssd(xbc, z, dt_raw, conv_w, conv_b, dtb, a_row, dskip_full, normw, e_mat, e_t, tri, n_t):
    row = lambda n: pl.BlockSpec((CHUNK, n), lambda c: (c, 0))
    const = lambda shp: pl.BlockSpec(shp, lambda c: (0, 0))
    return pl.pallas_call(
        _ssd_kernel,
        grid=(n_t // CHUNK,),
        in_specs=[row(XBC_DIM), row(D_INNER), row(LANES),
                  const((CONV_W, XBC_DIM)), const((1, XBC_DIM)), const((1, LANES)), const((1, LANES)),
                  const((1, D_INNER)), const((1, D_INNER)), const((LANES, D_INNER)),
                  const((D_INNER, LANES)), const((CHUNK, CHUNK))],
        out_specs=[row(D_INNER), pl.BlockSpec((D_INNER, D_STATE), lambda c: (0, 0))],
        out_shape=[jax.ShapeDtypeStruct((n_t, D_INNER), BF16),
                   jax.ShapeDtypeStruct((D_INNER, D_STATE), F32)],
        scratch_shapes=[pltpu.VMEM((8 + CHUNK, XBC_DIM), F32),
                        pltpu.VMEM((D_INNER, D_STATE), F32)],
        compiler_params=_cparams(("arbitrary",)),
        name="prompt_ssd",
    )(xbc, z, dt_raw, conv_w, conv_b, dtb, a_row, dskip_full, normw, e_mat, e_t, tri)


def _decode_attn_kernel(pt_ref, lq1, lk1, lq2, lk2, sub_ref, q_ref, kn_ref, vn_ref, *rest,
                        n_pages, lam_init):
    k_pages = rest[:n_pages]
    v_pages = rest[n_pages:2 * n_pages]
    o_ref = rest[2 * n_pages]
    del pt_ref
    lam = _diff_lambda(lq1, lk1, lq2, lk2, lam_init)
    nq = 2 * N_HEADS_A
    q8 = q_ref[...]
    lane = lax.broadcasted_iota(I32, (N_HEADS_A, LANES), 1)
    zero = jnp.zeros((N_HEADS_A, LANES), BF16)
    q16 = jnp.concatenate([jnp.where(lane < DK, q8, zero), jnp.where(lane < DK, zero, q8)], axis=0)

    pw = PAGE * N_KV_HEADS
    row_head = (lax.broadcasted_iota(I32, (nq, pw), 0) & (N_HEADS_A - 1)) >> 1
    col_head = lax.broadcasted_iota(I32, (nq, pw), 1) & (N_KV_HEADS - 1)
    own = row_head == col_head
    nt = (((1,), (1,)), ((), ()))
    s_pages = []
    for j in range(n_pages):
        kp = k_pages[j][...].astype(BF16)
        s = lax.dot_general(q16, kp, nt, preferred_element_type=F32)
        s_pages.append(jnp.where(own, s, NEG_INF))
    kn = kn_ref[...].astype(BF16)
    row_head_n = (lax.broadcasted_iota(I32, (nq, LANES), 0) & (N_HEADS_A - 1)) >> 1
    col_n = lax.broadcasted_iota(I32, (nq, LANES), 1)
    s_new = jnp.where(row_head_n == col_n, lax.dot_general(q16, kn, nt, preferred_element_type=F32), NEG_INF)

    m = jnp.max(s_new, axis=-1, keepdims=True)
    for s in s_pages:
        m = jnp.maximum(m, jnp.max(s, axis=-1, keepdims=True))
    p_new = jnp.exp2(s_new - m)
    l = jnp.sum(p_new, axis=-1, keepdims=True)
    acc = jnp.dot(p_new.astype(BF16), vn_ref[...].astype(BF16), preferred_element_type=F32)
    for j in range(n_pages):
        p = jnp.exp2(s_pages[j] - m)
        l = l + jnp.sum(p, axis=-1, keepdims=True)
        acc = acc + jnp.dot(p.astype(BF16), v_pages[j][...].astype(BF16), preferred_element_type=F32)
    o = acc / l
    res = o[:N_HEADS_A, :] - lam * o[N_HEADS_A:, :]
    o_ref[...] = _subln(res, sub_ref[...], lam_init).astype(BF16)


def _decode_attention(page_table, q_s, k_new8, v_new8, cache_k2, cache_v2, lq1, lk1, lq2, lk2, sub_w, lam_init):
    B, n_pages = page_table.shape
    pw = PAGE * N_KV_HEADS
    small = lambda n: pl.BlockSpec((1, n), lambda b, pt: (0, 0))
    per_b = pl.BlockSpec((None, 8, LANES), lambda b, pt: (b, 0, 0))
    new_tok = pl.BlockSpec((None, LANES, LANES), lambda b, pt: (b, 0, 0))
    page_spec = lambda j: pl.BlockSpec((pw, LANES), lambda b, pt: (pt[b, j], 0))
    grid_spec = pltpu.PrefetchScalarGridSpec(
        num_scalar_prefetch=1,
        grid=(B,),
        in_specs=[small(DK), small(DK), small(DK), small(DK), small(DV), per_b, new_tok, new_tok]
                 + [page_spec(j) for j in range(n_pages)] + [page_spec(j) for j in range(n_pages)],
        out_specs=per_b,
    )
    return pl.pallas_call(
        functools.partial(_decode_attn_kernel, n_pages=n_pages, lam_init=lam_init),
        grid_spec=grid_spec,
        out_shape=jax.ShapeDtypeStruct((B, 8, LANES), BF16),
        compiler_params=_cparams(("arbitrary",)),
        name="decode_attn",
    )(page_table, lq1, lk1, lq2, lk2, sub_w, q_s, k_new8, v_new8,
      *([cache_k2] * n_pages), *([cache_v2] * n_pages))


def _dec_pre_kernel(xbc_ref, s0_ref, s1_ref, s2_ref, dt_ref, cw_ref, cb_ref, dtb_ref, a_ref, e_ref,
                    xc_ref, xdt_ref, dect_ref):
    conv = (cb_ref[...] + cw_ref[0:1, :] * s0_ref[...] + cw_ref[1:2, :] * s1_ref[...]
            + cw_ref[2:3, :] * s2_ref[...] + cw_ref[3:4, :] * xbc_ref[...])
    xc = _silu(conv)
    xc_ref[...] = xc
    dt = _softplus(dt_ref[...] + dtb_ref[...])
    dec = jnp.exp(dt * a_ref[...])
    hi, mid, lo = _split3(dt)
    e = e_ref[...]
    dt_full = (jnp.dot(hi, e, preferred_element_type=F32) + jnp.dot(mid, e, preferred_element_type=F32)
               + jnp.dot(lo, e, preferred_element_type=F32))
    hi, mid, lo = _split3(dec)
    dec_full = (jnp.dot(hi, e, preferred_element_type=F32) + jnp.dot(mid, e, preferred_element_type=F32)
                + jnp.dot(lo, e, preferred_element_type=F32))
    xdt_ref[...] = (dt_full * xc[:, :D_INNER]).T
    dect_ref[...] = dec_full.T


def _dec_state_kernel(h_ref, xdt_ref, dect_ref, bm_ref, cm_ref, hout_ref, y_ref):
    i = pl.program_id(0)
    B = xdt_ref.shape[1]
    lane = lax.broadcasted_iota(I32, (D_INNER, B), 1)
    sub = lax.broadcasted_iota(I32, (TB, D_INNER // N_BC_GROUPS), 0)
    ones = jnp.ones((B, D_STATE), BF16)
    half = D_INNER // N_BC_GROUPS
    nt = (((1,), (1,)), ((), ()))
    y_acc = [jnp.zeros((TB, half), F32) for _ in range(N_BC_GROUPS)]
    for j in range(TB):
        b = i * TB + j
        sel = lane == b
        xcol = jnp.where(sel, xdt_ref[...], 0.0)
        dcol = jnp.where(sel, dect_ref[...], 0.0)
        dec = sum(jnp.dot(part, ones, preferred_element_type=F32) for part in _split3(dcol))
        xhi, xlo = _split2(xcol)
        hnew = dec * h_ref[j]
        for g in range(N_BC_GROUPS):
            bmat = bm_ref[:, g * D_STATE:(g + 1) * D_STATE].astype(BF16)
            upd = (jnp.dot(xhi[g * half:(g + 1) * half], bmat, preferred_element_type=F32)
                   + jnp.dot(xlo[g * half:(g + 1) * half], bmat, preferred_element_type=F32))
            hg = hnew[g * half:(g + 1) * half] + upd
            hout_ref[j, g * half:(g + 1) * half, :] = hg
            start = pl.multiple_of(i * TB, TB)
            c8 = cm_ref[pl.ds(start, TB), g * D_STATE:(g + 1) * D_STATE].astype(BF16)
            yj = lax.dot_general(c8, hg.astype(BF16), nt, preferred_element_type=F32)
            y_acc[g] = y_acc[g] + jnp.where(sub == j, yj, 0.0)
    y_ref[...] = jnp.concatenate(y_acc, axis=1)


def _dec_post_kernel(y_ref, xc_ref, z_ref, dskip_ref, normw_ref, o_ref):
    o_ref[...] = _gated_out(y_ref[...], xc_ref[:, :D_INNER], z_ref[...], dskip_ref[...],
                            normw_ref[...]).astype(BF16)


def _decode_ssm(xbc_s, sc0, sc1, sc2, dt_s, z_s, state, conv_w, conv_b, dtb, a_row, dskip_full, normw, e_mat):
    B = xbc_s.shape[0]
    xc, xdt_t, dec_t = pl.pallas_call(
        _dec_pre_kernel,
        out_shape=[jax.ShapeDtypeStruct((B, XBC_DIM), F32),
                   jax.ShapeDtypeStruct((D_INNER, B), F32),
                   jax.ShapeDtypeStruct((D_INNER, B), F32)],
        compiler_params=pltpu.CompilerParams(vmem_limit_bytes=VMEM_LIMIT),
        name="decode_ssm_pre",
    )(xbc_s, sc0, sc1, sc2, dt_s, conv_w, conv_b, dtb, a_row, e_mat)
    bm = xc[:, D_INNER:D_INNER + N_BC_GROUPS * D_STATE]
    cm = xc[:, D_INNER + N_BC_GROUPS * D_STATE:]
    const = lambda shp: pl.BlockSpec(shp, lambda i: (0,) * len(shp))
    state_spec = pl.BlockSpec((TB, D_INNER, D_STATE), lambda i: (i, 0, 0))
    h_new, y = pl.pallas_call(
        _dec_state_kernel,
        grid=(B // TB,),
        in_specs=[state_spec, const((D_INNER, B)), const((D_INNER, B)),
                  const((B, N_BC_GROUPS * D_STATE)), const((B, N_BC_GROUPS * D_STATE))],
        out_specs=[state_spec, pl.BlockSpec((TB, D_INNER), lambda i: (i, 0))],
        out_shape=[jax.ShapeDtypeStruct((B, D_INNER, D_STATE), F32),
                   jax.ShapeDtypeStruct((B, D_INNER), F32)],
        compiler_params=_cparams(("arbitrary",)),
        name="decode_ssm_state",
    )(state, xdt_t, dec_t, bm, cm)
    ssm = pl.pallas_call(
        _dec_post_kernel,
        out_shape=jax.ShapeDtypeStruct((B, D_INNER), BF16),
        compiler_params=pltpu.CompilerParams(vmem_limit_bytes=VMEM_LIMIT),
        name="decode_ssm_post",
    )(y, xc, z_s, dskip_full, normw)
    return ssm, h_new


def _outproj_kernel(front_ref, xa_ref, xb_ref, xs_ref, att_ref, sa_ref, sb_ref, ss_ref,
                    wa_ref, ws_ref, nw_ref, wrh_ref, wrl_ref, h_ref, u_ref, lg_ref):
    last = pl.program_id(0) == pl.num_programs(0) - 1
    ssm = jnp.concatenate([sa_ref[...], jnp.where(last, ss_ref[...], sb_ref[...])], axis=0)
    acc = jnp.dot(att_ref[...], wa_ref[...], preferred_element_type=F32)
    acc = acc + jnp.dot(ssm, ws_ref[...], preferred_element_type=F32)
    h = _token_rows(front_ref, xa_ref, xb_ref, xs_ref) + acc
    h_ref[...] = h
    ms = jnp.mean(h * h, axis=-1, keepdims=True)
    u = h * lax.rsqrt(ms + EPS) * nw_ref[...]
    u_ref[...] = u
    uh, ul = _split2(u)
    wh, wl = wrh_ref[...], wrl_ref[...]
    lg = (jnp.dot(uh, wh, preferred_element_type=F32) + jnp.dot(uh, wl, preferred_element_type=F32)
          + jnp.dot(ul, wh, preferred_element_type=F32))
    lg_ref[...] = _route_rows(lg, pl.program_id(0) * TM)


def _first_max(x, lane):
    m = jnp.max(x, axis=-1, keepdims=True)
    return m, jnp.min(jnp.where(x == m, lane, float(LANES)), axis=-1, keepdims=True)


def _route_rows(lg, row0):
    rows = lg.shape[0]
    lane_i = lax.broadcasted_iota(I32, (rows, LANES), 1)
    lane = lane_i.astype(F32)
    gl = jnp.where(lane_i < N_EGROUPS, lg, -jnp.inf)
    gmax, g = _first_max(gl, lane)
    pg = 1.0 / jnp.sum(jnp.exp(gl - gmax), axis=-1, keepdims=True)
    lo = N_EGROUPS + g * EXPERTS_PER_GROUP
    el = jnp.where((lane >= lo) & (lane < lo + EXPERTS_PER_GROUP), lg, -jnp.inf)
    m1, i1 = _first_max(el, lane)
    m2, i2 = _first_max(jnp.where(lane == i1, -jnp.inf, el), lane)
    z = jnp.sum(jnp.exp(el - m1), axis=-1, keepdims=True)
    p1 = 1.0 / z
    p2 = jnp.exp(m2 - m1) / z
    tot = p1 + p2
    valid = (row0 + lax.broadcasted_iota(I32, (rows, 1), 0)) >= PAD_ROWS
    e1 = jnp.where(valid, i1 - N_EGROUPS, -1.0)
    e2 = jnp.where(valid, i2 - N_EGROUPS, -1.0)
    w1 = jnp.where(valid, p1 / tot * pg, 0.0)
    w2 = jnp.where(valid, p2 / tot * pg, 0.0)
    return jnp.where(lane_i == 0, e1, jnp.where(lane_i == 1, e2, jnp.where(lane_i == 2, w1,
                                                                         jnp.where(lane_i == 3, w2, 0.0))))


def _rank_kernel(route_ref, tri_ref, utri_ref, rank_ref, meta_ref, carry, *, nblk):
    i = pl.program_id(0)

    @pl.when(i == 0)
    def _():
        carry[...] = jnp.zeros(carry.shape, F32)

    r = route_ref[...]
    lane = lax.broadcasted_iota(I32, r.shape, 1).astype(F32)
    oh0 = (lane == r[:, 0:1]).astype(F32)
    oh1 = (lane == r[:, 1:2]).astype(F32)
    oh = oh0 + oh1
    before = jnp.dot(tri_ref[...], oh.astype(BF16), preferred_element_type=F32) + carry[0:1, :]
    rank0 = jnp.sum(before * oh0, axis=-1, keepdims=True)
    rank1 = jnp.sum(before * oh1, axis=-1, keepdims=True)
    lane_i = lax.broadcasted_iota(I32, r.shape, 1)
    rank_ref[...] = jnp.where(lane_i == 0, rank0, jnp.where(lane_i == 1, rank1, 0.0))
    carry[0:1, :] = carry[0:1, :] + jnp.sum(oh, axis=0, keepdims=True)

    @pl.when(i == pl.num_programs(0) - 1)
    def _():
        nb_pad = meta_ref.shape[0] - 8
        counts = carry[0:1, :]
        pb = jnp.floor((counts + (TMOE - 1)) / TMOE)
        ends = jnp.dot(jnp.broadcast_to(pb, (8, LANES)).astype(BF16), utri_ref[...],
                       preferred_element_type=F32)[0:1, :]
        starts = ends - pb
        lane1 = lax.broadcasted_iota(I32, (1, LANES), 1)
        n_valid = jnp.sum(jnp.where(lane1 == N_EXPERTS - 1, ends, 0.0), axis=-1, keepdims=True)
        blk = lax.broadcasted_iota(I32, (nb_pad, LANES), 0).astype(F32)
        lane_b = lax.broadcasted_iota(I32, (nb_pad, LANES), 1)
        is_e = lane_b < N_EXPERTS
        block_e = jnp.minimum(jnp.sum(jnp.where(is_e & (ends <= blk), 1.0, 0.0), axis=-1, keepdims=True),
                              N_EXPERTS - 1.0)
        lane_bf = lane_b.astype(F32)
        has = is_e & (counts > 0.0) & (lane_bf > block_e)
        nxt = jnp.min(jnp.where(has, lane_bf, float(LANES)), axis=-1, keepdims=True)
        next_e = jnp.where(nxt >= LANES, -1.0, nxt)
        mine = lane_bf == block_e
        cnt_b = jnp.sum(jnp.where(mine, counts, 0.0), axis=-1, keepdims=True)
        st_b = jnp.sum(jnp.where(mine, starts, 0.0), axis=-1, keepdims=True)
        n_real = jnp.clip(cnt_b - (blk[:, 0:1] - st_b) * TMOE, 0.0, float(TMOE))
        meta_ref[0:8, :] = jnp.where(lax.broadcasted_iota(I32, (8, LANES), 0) == 0, starts,
                                     jnp.broadcast_to(n_valid, (8, LANES)))
        meta_ref[8:, :] = jnp.where(lane_b == 0, block_e, jnp.where(lane_b == 1, next_e,
                                                                    jnp.where(lane_b == 2, n_real, 0.0)))


def _rank(route, nblk):
    R = route.shape[0]
    nb_pad = (nblk + 7) // 8 * 8
    tri = (jnp.arange(TM)[:, None] > jnp.arange(TM)[None, :]).astype(BF16)
    utri = (jnp.arange(LANES)[:, None] <= jnp.arange(LANES)[None, :]).astype(BF16)
    return pl.pallas_call(
        functools.partial(_rank_kernel, nblk=nblk),
        grid=(R // TM,),
        in_specs=[pl.BlockSpec((TM, LANES), lambda i: (i, 0)),
                  pl.BlockSpec((TM, TM), lambda i: (0, 0)),
                  pl.BlockSpec((LANES, LANES), lambda i: (0, 0))],
        out_specs=[pl.BlockSpec((TM, LANES), lambda i: (i, 0)),
                   pl.BlockSpec((8 + nb_pad, LANES), lambda i: (0, 0))],
        out_shape=[jax.ShapeDtypeStruct((R, LANES), F32),
                   jax.ShapeDtypeStruct((8 + nb_pad, LANES), F32)],
        scratch_shapes=[pltpu.VMEM((8, LANES), F32)],
        compiler_params=_cparams(("arbitrary",)),
        name="expert_rank",
    )(route, tri, utri)


def _outproj(front, xp, xs, att, ssm_p, ssm_s, wa, ws, norm_w, wr_hi, wr_lo):
    R = front.shape[0] + xp.shape[0] + xs.shape[0]
    row = lambda n: pl.BlockSpec((TM, n), lambda i: (i, 0))
    nbp = ssm_p.shape[0] // HALF
    half = lambda f: pl.BlockSpec((HALF, D_INNER), f)
    return pl.pallas_call(
        _outproj_kernel,
        grid=(R // TM,),
        in_specs=_token_specs(xp.shape[0]) + [
            row(N_HEADS_A * DV),
            half(lambda i: (2 * i, 0)), half(lambda i: (jnp.minimum(2 * i + 1, nbp - 1), 0)),
            half(lambda i: (0, 0)),
            _resident(wa.shape), _resident(ws.shape), _resident((1, D_MODEL)),
            _resident(wr_hi.shape), _resident(wr_lo.shape)],
        out_specs=[row(D_MODEL), row(D_MODEL), row(LANES)],
        out_shape=[jax.ShapeDtypeStruct((R, D_MODEL), F32),
                   jax.ShapeDtypeStruct((R, D_MODEL), F32),
                   jax.ShapeDtypeStruct((R, LANES), F32)],
        compiler_params=_cparams(("arbitrary",)),
        name="outproj",
    )(front, xp, xp, xs, att, ssm_p, ssm_p, ssm_s, wa, ws, norm_w, wr_hi, wr_lo)


GATHER_UNROLL = 8

def _start_row_gather(idx_ref, src_hbm, dst, sem, n_groups=None):
    def start(r8, carry):
        for j in range(GATHER_UNROLL):
            r = r8 * GATHER_UNROLL + j
            pltpu.make_async_copy(src_hbm.at[pl.ds(idx_ref[0, r], 1)], dst.at[pl.ds(r, 1)],
                                  sem).start(priority=j % 2)
        return carry

    lax.fori_loop(0, dst.shape[0] // GATHER_UNROLL if n_groups is None else n_groups, start, 0)


def _wait_row_gather(src_hbm, dst, sem, n_groups=None):
    if n_groups is None:
        pltpu.make_async_copy(src_hbm.at[pl.ds(0, dst.shape[0])], dst, sem).wait()
        return

    def wait(g, carry):
        pltpu.make_async_copy(src_hbm.at[pl.ds(0, GATHER_UNROLL)], dst.at[pl.ds(0, GATHER_UNROLL)], sem).wait()
        return carry

    lax.fori_loop(0, n_groups, wait, 0)


def _row_groups(n_real):
    return (n_real + (GATHER_UNROLL - 1)) // GATHER_UNROLL


def _moe_up_kernel(be_ref, nx_ref, nr_ref, nv_ref, idx_ref, idx_next_ref, x_hbm, wg_hbm, wu_hbm, h_ref,
                   xbuf, stage_g, stage_u, wg_b, wu_b, sems, row_sems):
    i = pl.program_id(0)
    valid = i < nv_ref[0]
    e = be_ref[i]
    changed = (i == 0) | (be_ref[jnp.maximum(i - 1, 0)] != e)
    slot = i % 2
    n_real = nr_ref[i]

    @pl.when(i == 0)
    def _():
        xbuf[...] = jnp.zeros(xbuf.shape, F32)
        _start_row_gather(idx_ref, x_hbm, xbuf.at[0], row_sems.at[0], _row_groups(n_real))

    @pl.when(i + 1 < nv_ref[0])
    def _():
        _start_row_gather(idx_next_ref, x_hbm, xbuf.at[1 - slot], row_sems.at[1 - slot],
                          _row_groups(nr_ref[jnp.minimum(i + 1, pl.num_programs(0) - 1)]))

    def fetch(expert):
        return (pltpu.make_async_copy(wg_hbm.at[expert], stage_g, sems.at[0]),
                pltpu.make_async_copy(wu_hbm.at[expert], stage_u, sems.at[1]))

    @pl.when(i == 0)
    def _():
        for c in fetch(e):
            c.start()

    @pl.when(valid & changed)
    def _():
        for c in fetch(e):
            c.wait()
        wg_b[...] = stage_g[...].astype(BF16)
        wu_b[...] = stage_u[...].astype(BF16)
        nxt = nx_ref[i]

        @pl.when(nxt >= 0)
        def _():
            for c in fetch(nxt):
                c.start()

    def mlp_up(rows):
        x = xbuf[slot, :rows, :].astype(BF16)
        g = jnp.dot(x, wg_b[...], preferred_element_type=F32)
        u = jnp.dot(x, wu_b[...], preferred_element_type=F32)
        h_ref[:rows, :] = (_silu(g) * u).astype(BF16)

    @pl.when(valid)
    def _():
        _wait_row_gather(x_hbm, xbuf.at[slot], row_sems.at[slot], _row_groups(n_real))

    @pl.when(valid & (n_real > TSMALL))
    def _():
        mlp_up(TMOE)

    @pl.when(valid & (n_real <= TSMALL))
    def _():
        mlp_up(TSMALL)
        h_ref[TSMALL:, :] = jnp.zeros((TMOE - TSMALL, D_FF), BF16)

    @pl.when(jnp.logical_not(valid))
    def _():
        h_ref[...] = jnp.zeros(h_ref.shape, BF16)


def _moe_down_kernel(be_ref, nr_ref, nv_ref, h_ref, wd_ref, o_ref, wd_b):
    i = pl.program_id(0)
    valid = i < nv_ref[0]
    changed = (i == 0) | (be_ref[jnp.maximum(i - 1, 0)] != be_ref[i])
    n_real = nr_ref[i]

    @pl.when(valid & changed)
    def _():
        wd_b[...] = wd_ref[...].astype(BF16)

    @pl.when(valid & (n_real > TSMALL))
    def _():
        o_ref[...] = jnp.dot(h_ref[...], wd_b[...], preferred_element_type=F32)

    @pl.when(valid & (n_real <= TSMALL))
    def _():
        o_ref[:TSMALL, :] = jnp.dot(h_ref[:TSMALL, :], wd_b[...], preferred_element_type=F32)
        o_ref[TSMALL:, :] = jnp.zeros((TMOE - TSMALL, D_MODEL), F32)

    @pl.when(jnp.logical_not(valid))
    def _():
        o_ref[...] = jnp.zeros(o_ref.shape, F32)


def _moe(block_e, next_e, n_real, n_valid, buf_tok, x_tok, w_gate, w_up, w_down):
    rows = buf_tok.shape[0]
    nblk = rows // TMOE

    def blk(i, nv):
        return jnp.minimum(i, nv[0] - 1)

    up_spec = pltpu.PrefetchScalarGridSpec(
        num_scalar_prefetch=4,
        grid=(nblk,),
        in_specs=[pl.BlockSpec((None, 1, TMOE), lambda i, be, nx, nr, nv: (blk(i, nv), 0, 0),
                               memory_space=pltpu.SMEM),
                  pl.BlockSpec((None, 1, TMOE), lambda i, be, nx, nr, nv: (blk(i + 1, nv), 0, 0),
                               memory_space=pltpu.SMEM),
                  pl.BlockSpec(memory_space=pl.ANY),
                  pl.BlockSpec(memory_space=pl.ANY), pl.BlockSpec(memory_space=pl.ANY)],
        out_specs=pl.BlockSpec((TMOE, D_FF), lambda i, be, nx, nr, nv: (i, 0)),
        scratch_shapes=[pltpu.VMEM((2, TMOE, D_MODEL), F32),
                        pltpu.VMEM((D_MODEL, D_FF), F32), pltpu.VMEM((D_MODEL, D_FF), F32),
                        pltpu.VMEM((D_MODEL, D_FF), BF16), pltpu.VMEM((D_MODEL, D_FF), BF16),
                        pltpu.SemaphoreType.DMA((2,)), pltpu.SemaphoreType.DMA((2,))],
    )
    slot_rows = buf_tok.reshape(nblk, 1, TMOE)
    hmid = pl.pallas_call(
        _moe_up_kernel,
        grid_spec=up_spec,
        out_shape=jax.ShapeDtypeStruct((rows, D_FF), BF16),
        compiler_params=_cparams(("arbitrary",)),
        name="moe_up",
    )(block_e, next_e, n_real, n_valid, slot_rows, slot_rows, x_tok, w_gate, w_up)

    down_spec = pltpu.PrefetchScalarGridSpec(
        num_scalar_prefetch=3,
        grid=(nblk,),
        in_specs=[pl.BlockSpec((TMOE, D_FF), lambda i, be, nr, nv: (blk(i, nv), 0)),
                  pl.BlockSpec((None, D_FF, D_MODEL), lambda i, be, nr, nv: (be[blk(i, nv)], 0, 0))],
        out_specs=pl.BlockSpec((TMOE, D_MODEL), lambda i, be, nr, nv: (i, 0)),
        scratch_shapes=[pltpu.VMEM((D_FF, D_MODEL), BF16)],
    )
    return pl.pallas_call(
        _moe_down_kernel,
        grid_spec=down_spec,
        out_shape=jax.ShapeDtypeStruct((rows, D_MODEL), F32),
        compiler_params=_cparams(("arbitrary",)),
        name="moe_down",
    )(block_e, n_real, n_valid, hmid, w_down)


def _final_kernel(idx_ref, idx_next_ref, h_ref, w_ref, nw_ref, y_hbm, op_ref, os_ref, ybuf, row_sems):
    i = pl.program_id(0)
    last = pl.num_programs(0) - 1
    slot = i % 2

    @pl.when(i == 0)
    def _():
        _start_row_gather(idx_ref, y_hbm, ybuf.at[0], row_sems.at[0])

    @pl.when(i + 1 < pl.num_programs(0))
    def _():
        _start_row_gather(idx_next_ref, y_hbm, ybuf.at[1 - slot], row_sems.at[1 - slot])

    _wait_row_gather(y_hbm, ybuf.at[slot], row_sems.at[slot])
    w = w_ref[...]
    moe = ybuf[slot, :HALF, :] * w[:, 0:1] + ybuf[slot, HALF:, :] * w[:, 1:2]
    h = h_ref[...] + moe
    ms = jnp.mean(h * h, axis=-1, keepdims=True)
    out = h * lax.rsqrt(ms + EPS) * nw_ref[...]

    @pl.when(i < last)
    def _():
        op_ref[...] = out

    @pl.when(i == last)
    def _():
        os_ref[...] = out


def _final(h1, yb, dest, wts, norm_w, seq):
    R = h1.shape[0]
    n = R // HALF
    nbp = seq // HALF
    assert R == (1 + nbp + 1) * HALF
    row = lambda w: pl.BlockSpec((HALF, w), lambda i: (i, 0))
    slots = dest.reshape(n, HALF, TOP_K).transpose(0, 2, 1).reshape(n, 1, TOP_K * HALF)
    idx_spec = lambda off: pl.BlockSpec((None, 1, TOP_K * HALF), lambda i: (jnp.minimum(i + off, n - 1), 0, 0),
                                        memory_space=pltpu.SMEM)
    return pl.pallas_call(
        _final_kernel,
        grid=(n,),
        in_specs=[idx_spec(0), idx_spec(1), row(D_MODEL), row(TOP_K),
                  pl.BlockSpec((1, D_MODEL), lambda i: (0, 0)),
                  pl.BlockSpec(memory_space=pl.ANY)],
        out_specs=[pl.BlockSpec((HALF, D_MODEL), lambda i: (jnp.clip(i - 1, 0, nbp - 1), 0)),
                   pl.BlockSpec((HALF, D_MODEL), lambda i: (0, 0))],
        out_shape=[jax.ShapeDtypeStruct((seq, D_MODEL), F32),
                   jax.ShapeDtypeStruct((HALF, D_MODEL), F32)],
        scratch_shapes=[pltpu.VMEM((2, TOP_K * HALF, D_MODEL), F32), pltpu.SemaphoreType.DMA((2,))],
        compiler_params=_cparams(("arbitrary",)),
        name="final_norm",
    )(slots, slots, h1, wts, norm_w, yb)


def _route(route):
    R = route.shape[0]
    a_max = TOP_K * R
    nblk = (a_max + N_EXPERTS * (TMOE - 1) + TMOE - 1) // TMOE
    rows = nblk * TMOE
    rank, meta = _rank(route, nblk)
    eid = route[:, :TOP_K].astype(I32)
    wts = route[:, TOP_K:2 * TOP_K]
    starts = meta[0, :N_EXPERTS].astype(I32) * TMOE
    n_valid = meta[1, :1].astype(I32)
    block_e = meta[8:8 + nblk, 0].astype(I32)
    next_e = meta[8:8 + nblk, 1].astype(I32)
    n_real = meta[8:8 + nblk, 2].astype(I32)
    is_real = eid >= 0
    dest = jnp.where(is_real, starts[jnp.maximum(eid, 0)] + rank[:, :TOP_K].astype(I32), 0)
    tok = jnp.broadcast_to(jnp.arange(R, dtype=I32)[:, None], (R, TOP_K))
    buf_tok = (jnp.arange(rows, dtype=I32) % PAD_ROWS).at[jnp.where(is_real, dest, rows).reshape(-1)].set(
        tok.reshape(-1), mode="drop")
    return wts, dest, buf_tok, block_e, next_e, n_real, n_valid


def kernel(x_prompt, x_sample, cache_k, cache_v, state_conv, state_ssm, page_table, meta_tokens, norm_mix_w,
           w_in, lambda_q1, lambda_k1, lambda_q2, lambda_k2, subln_w, conv_w, conv_b, dt_bias, a_log, d_skip,
           ssm_norm_w, w_out, norm_ffn_w, w_router_group, w_router_expert, w_gate, w_up, w_down, final_norm_w):
    depth = w_in.shape[0]
    assert depth == 1 and x_prompt.shape[0] == 1 and x_sample.shape[1] == 1
    seq = x_prompt.shape[1]
    B = x_sample.shape[0]
    n_pages = page_table.shape[1]
    n_phys = cache_k.shape[1]
    past = n_pages * cache_k.shape[2]
    assert cache_k.shape[2] == PAGE and seq % TM == 0 and B == HALF and FRONT == HALF
    n_t = FRONT + seq
    R = n_t + B
    l = 0
    lam_init = 0.8 - 0.6 * math.exp(-0.3 * l)

    front = jnp.concatenate([jnp.zeros((PAD_ROWS, D_MODEL), F32), meta_tokens.astype(F32)], axis=0)
    xp, xs = x_prompt[0], x_sample[:, 0]
    w = w_in[l]
    wq = w[:, :OFF_K].astype(BF16)
    wk = w[:, OFF_K:OFF_V].astype(BF16)
    wv = w[:, OFF_V:OFF_Z].astype(BF16)
    wz = w[:, OFF_Z:OFF_XBC].astype(BF16)
    wx = w[:, OFF_XBC:OFF_DT].astype(BF16)
    wdt = jnp.pad(w[:, OFF_DT:], ((0, 0), (0, LANES - N_HEADS_S))).astype(BF16)

    pos = jnp.concatenate([jnp.maximum(jnp.arange(n_t, dtype=I32) - PAD_ROWS, 0),
                           jnp.full((B,), past, I32)])
    inv = ROPE_THETA ** (-jnp.arange(0, ROT_DIM, 2, dtype=F32) / ROT_DIM)
    ang = pos.astype(F32)[:, None] * inv[None, :]
    cos, sin = jnp.cos(ang), jnp.sin(ang)
    half = ROT_DIM // 2
    ones = jnp.ones((R, DK - ROT_DIM), F32)
    zeros = jnp.zeros((R, DK - ROT_DIM), F32)
    zh = jnp.zeros((R, half), F32)
    ctab = jnp.tile(jnp.concatenate([cos, cos, ones], axis=1), (1, 2))
    s1tab = jnp.tile(jnp.concatenate([-sin, zh, zeros], axis=1), (1, 2))
    s2tab = jnp.tile(jnp.concatenate([zh, sin, zeros], axis=1), (1, 2))

    row1 = lambda v: v.reshape(1, -1).astype(F32)
    pad_heads = lambda v: jnp.pad(v.astype(F32), (0, LANES - N_HEADS_S)).reshape(1, LANES)
    a_row = pad_heads(-jnp.exp(a_log[l].astype(F32)))
    dtb = pad_heads(dt_bias[l])
    dskip_full = jnp.repeat(d_skip[l].astype(F32), SSM_HEAD_DIM).reshape(1, D_INNER)
    normw = row1(ssm_norm_w[l])
    head_of_col = jnp.arange(D_INNER, dtype=I32) // SSM_HEAD_DIM
    e_mat = (jnp.arange(LANES, dtype=I32)[:, None] == head_of_col[None, :]).astype(BF16)
    tri = (jnp.arange(CHUNK)[:, None] >= jnp.arange(CHUNK)[None, :]).astype(BF16)
    lq1, lk1, lq2, lk2 = (row1(v[l]) for v in (lambda_q1, lambda_k1, lambda_q2, lambda_k2))
    sub_w = row1(subln_w[l])

    q, k, v, kb, vb, z, xbc, dt_raw = _inproj(front, xp, xs, row1(norm_mix_w[l]), wq, wk, wv, wz, wx, wdt,
                                              ctab, s1tab, s2tab)

    att_all = _prompt_attention(q, kb, vb, lq1, lk1, lq2, lk2, sub_w, lam_init)
    ssm_p, state_p = _prompt_ssd(xbc, z, dt_raw, conv_w[l].astype(F32), row1(conv_b[l]), dtb, a_row,
                                 dskip_full, normw, e_mat, e_mat.T.astype(F32), tri, n_t)

    k_s, v_s = k[n_t:], v[n_t:]
    pad8 = lambda a: jnp.pad(a.reshape(B, N_KV_HEADS, LANES), ((0, 0), (0, LANES - N_KV_HEADS), (0, 0)))
    att_s = _decode_attention(page_table, q[n_t:].reshape(B, N_HEADS_A, LANES), pad8(k_s), pad8(v_s),
                              cache_k[l].reshape(n_phys * PAGE * N_KV_HEADS, LANES),
                              cache_v[l].reshape(n_phys * PAGE * N_KV_HEADS, LANES),
                              lq1, lk1, lq2, lk2, sub_w, lam_init)
    sc = state_conv[l]
    xbc_s = xbc[n_t:]
    ssm_s, state_s = _decode_ssm(xbc_s, sc[:, 0], sc[:, 1], sc[:, 2], dt_raw[n_t:], z[n_t:],
                                 state_ssm[l].reshape(B, D_INNER, D_STATE), conv_w[l].astype(F32),
                                 row1(conv_b[l]), dtb, a_row, dskip_full, normw, e_mat)

    att = lax.dynamic_update_slice(att_all, att_s.reshape(B, N_HEADS_A * DV), (n_t, 0))

    wo = w_out[l].astype(BF16)
    wr = jnp.pad(jnp.concatenate([w_router_group[l], w_router_expert[l]], axis=1).astype(F32),
                 ((0, 0), (0, LANES - N_EGROUPS - N_EXPERTS)))
    wr_hi = wr.astype(BF16)
    wr_lo = (wr - wr_hi.astype(F32)).astype(BF16)
    h1, u2, route = _outproj(front, xp, xs, att, ssm_p, ssm_s, wo[:N_HEADS_A * DV], wo[N_HEADS_A * DV:],
                             row1(norm_ffn_w[l]), wr_hi, wr_lo)

    wts, dest, buf_tok, block_e, next_e, n_real, n_valid = _route(route)
    yb = _moe(block_e, next_e, n_real, n_valid, buf_tok, u2, w_gate[l], w_up[l], w_down[l])
    y_p, y_s = _final(h1, yb, dest, wts, row1(final_norm_w), seq)

    t_real = N_META + seq
    y_prompt = y_p.reshape(1, seq, D_MODEL)
    y_sample = y_s.reshape(B, 1, D_MODEL)
    k_prompt = k[PAD_ROWS:n_t].reshape(1, 1, t_real, N_KV_HEADS, 2 * DK)
    v_prompt = v[PAD_ROWS:n_t].reshape(1, 1, t_real, N_KV_HEADS, DV)
    conv_prompt = xbc[n_t - (CONV_W - 1):n_t].reshape(1, 1, CONV_W - 1, XBC_DIM)
    ssm_prompt = state_p.reshape(1, 1, N_HEADS_S, SSM_HEAD_DIM, D_STATE)
    k_sample = k_s.reshape(1, B, 1, N_KV_HEADS, 2 * DK)
    v_sample = v_s.reshape(1, B, 1, N_KV_HEADS, DV)
    conv_sample = jnp.stack([sc[:, 1], sc[:, 2], xbc_s], axis=1)[None]
    ssm_sample = state_s.reshape(1, B, N_HEADS_S, SSM_HEAD_DIM, D_STATE)
    return (y_prompt, y_sample, k_prompt, v_prompt, conv_prompt, ssm_prompt,
            k_sample, v_sample, conv_sample, ssm_sample)
```

```python
import functools
import math

import jax
import jax.numpy as jnp
from jax import lax
from jax.experimental import pallas as pl
from jax.experimental.pallas import tpu as pltpu

F32 = jnp.float32
BF16 = jnp.bfloat16
I32 = jnp.int32

D_MODEL = 2048
N_META = 16
EPS = 1e-6
DV = 128
DK = 64
N_HEADS_A = 8
N_KV_HEADS = 4
ROT_DIM = 16
ROPE_THETA = 500000.0
NEG_INF = -1e30
LOG2E = 1.4426950408889634
D_INNER = 1024
SSM_HEAD_DIM = 64
N_HEADS_S = 16
N_BC_GROUPS = 2
D_STATE = 128
CONV_W = 4
XBC_DIM = D_INNER + 2 * N_BC_GROUPS * D_STATE
CHUNK = 128
Q_DIM = N_HEADS_A * 2 * DK
K_DIM = N_KV_HEADS * 2 * DK
V_DIM = N_KV_HEADS * DV
OFF_K = Q_DIM
OFF_V = OFF_K + K_DIM
OFF_Z = OFF_V + V_DIM
OFF_XBC = OFF_Z + D_INNER
OFF_DT = OFF_XBC + XBC_DIM
IN_DIM = OFF_DT + N_HEADS_S
N_EGROUPS = 4
EXPERTS_PER_GROUP = 8
N_EXPERTS = N_EGROUPS * EXPERTS_PER_GROUP
TOP_K = 2
D_FF = 1024
PAGE = 128

LANES = 128
VMEM_LIMIT = 56 * 1024 * 1024

FRONT = CHUNK
PAD_ROWS = FRONT - N_META
TM = 256
TQ = 256
TKF = 1024
TB = 8
TMOE = 512
TSMALL = 128
TF = 256


def _cparams(sem):
    return pltpu.CompilerParams(dimension_semantics=sem, vmem_limit_bytes=VMEM_LIMIT)


def _resident(shape):
    nd = len(shape)
    return pl.BlockSpec(shape, lambda *_: (0,) * nd, pipeline_mode=pl.Buffered(1))


def _split2(x):
    hi = x.astype(BF16)
    lo = (x - hi.astype(F32)).astype(BF16)
    return hi, lo


def _split3(x):
    hi = x.astype(BF16)
    r = x - hi.astype(F32)
    mid = r.astype(BF16)
    lo = (r - mid.astype(F32)).astype(BF16)
    return hi, mid, lo


def _silu(x):
    return x * (1.0 / (1.0 + jnp.exp(-x)))


def _softplus(x):
    return jnp.maximum(x, 0.0) + jnp.log(1.0 + jnp.exp(-jnp.abs(x)))


HALF = TM // 2


def _token_specs(seq):
    nb = seq // HALF
    return [pl.BlockSpec((HALF, D_MODEL), lambda i: (0, 0)),
            pl.BlockSpec((HALF, D_MODEL), lambda i: (jnp.clip(2 * i - 1, 0, nb - 1), 0)),
            pl.BlockSpec((HALF, D_MODEL), lambda i: (jnp.clip(2 * i, 0, nb - 1), 0)),
            pl.BlockSpec((HALF, D_MODEL), lambda i: (0, 0))]


def _token_rows(front_ref, xa_ref, xb_ref, xs_ref):
    i = pl.program_id(0)
    top = jnp.where(i == 0, front_ref[...], xa_ref[...])
    bot = jnp.where(i == pl.num_programs(0) - 1, xs_ref[...], xb_ref[...])
    return jnp.concatenate([top, bot], axis=0)


def _inproj_kernel(front_ref, xa_ref, xb_ref, xs_ref, nw_ref, wq_ref, wk_ref, wv_ref, wz_ref, wx_ref, wdt_ref,
                   c_ref, s1_ref, s2_ref,
                   q_ref, k_ref, v_ref, kb_ref, vb_ref, z_ref, xbc_ref, dt_ref):
    x = _token_rows(front_ref, xa_ref, xb_ref, xs_ref)
    ms = jnp.mean(x * x, axis=-1, keepdims=True)
    u = (x * lax.rsqrt(ms + EPS) * nw_ref[...]).astype(BF16)
    c, s1, s2 = c_ref[...], s1_ref[...], s2_ref[...]

    def rope(p):
        outs = []
        for h in range(p.shape[1] // LANES):
            xh = p[:, h * LANES:(h + 1) * LANES]
            outs.append(xh * c + pltpu.roll(xh, LANES - ROT_DIM // 2, 1) * s1
                        + pltpu.roll(xh, ROT_DIM // 2, 1) * s2)
        return jnp.concatenate(outs, axis=1)

    q = rope(jnp.dot(u, wq_ref[...], preferred_element_type=F32))
    q_ref[...] = (q * (DK ** -0.5 * LOG2E)).astype(BF16)
    k = rope(jnp.dot(u, wk_ref[...], preferred_element_type=F32))
    k_ref[...] = k
    kb_ref[...] = k.astype(BF16)
    v = jnp.dot(u, wv_ref[...], preferred_element_type=F32)
    v_ref[...] = v
    vb = v.astype(BF16)
    ones = jnp.ones((vb.shape[0], DV), BF16)
    vb_ref[...] = jnp.concatenate(
        [piece for g in range(N_KV_HEADS) for piece in (vb[:, g * DV:(g + 1) * DV], ones)], axis=1)
    z_ref[...] = jnp.dot(u, wz_ref[...], preferred_element_type=F32)
    xbc_ref[...] = jnp.dot(u, wx_ref[...], preferred_element_type=F32)
    dt_ref[...] = jnp.dot(u, wdt_ref[...], preferred_element_type=F32)


def _inproj(front, xp, xs, norm_w, wq, wk, wv, wz, wx, wdt, ctab, s1tab, s2tab):
    R = front.shape[0] + xp.shape[0] + xs.shape[0]
    row = lambda n: pl.BlockSpec((TM, n), lambda i: (i, 0))
    return pl.pallas_call(
        _inproj_kernel,
        grid=(R // TM,),
        in_specs=_token_specs(xp.shape[0]) + [_resident((1, D_MODEL)),
                  _resident(wq.shape), _resident(wk.shape), _resident(wv.shape),
                  _resident(wz.shape), _resident(wx.shape), _resident(wdt.shape),
                  row(LANES), row(LANES), row(LANES)],
        out_specs=[row(Q_DIM), row(K_DIM), row(V_DIM), row(K_DIM), row(2 * V_DIM),
                   row(D_INNER), row(XBC_DIM), row(LANES)],
        out_shape=[jax.ShapeDtypeStruct((R, Q_DIM), BF16),
                   jax.ShapeDtypeStruct((R, K_DIM), F32),
                   jax.ShapeDtypeStruct((R, V_DIM), F32),
                   jax.ShapeDtypeStruct((R, K_DIM), BF16),
                   jax.ShapeDtypeStruct((R, 2 * V_DIM), BF16),
                   jax.ShapeDtypeStruct((R, D_INNER), F32),
                   jax.ShapeDtypeStruct((R, XBC_DIM), F32),
                   jax.ShapeDtypeStruct((R, LANES), F32)],
        compiler_params=_cparams(("arbitrary",)),
        name="inproj",
    )(front, xp, xp, xs, norm_w, wq, wk, wv, wz, wx, wdt, ctab, s1tab, s2tab)


def _diff_lambda(lq1, lk1, lq2, lk2, lam_init):
    a = jnp.sum(lq1[...] * lk1[...], axis=-1, keepdims=True)
    b = jnp.sum(lq2[...] * lk2[...], axis=-1, keepdims=True)
    return jnp.exp(a) - jnp.exp(b) + lam_init


def _subln(o, sub_w, lam_init):
    ms = jnp.mean(o * o, axis=-1, keepdims=True)
    return (o * lax.rsqrt(ms + EPS) * sub_w) * (1.0 - lam_init)


def _attn_kernel(lq1, lk1, lq2, lk2, sub_ref, q_ref, k_ref, ve_ref, o_ref,
                 qs_scr, m_scr, acc_scr, *, lam_init, n_rows):
    qi = pl.program_id(1)
    lam = _diff_lambda(lq1, lk1, lq2, lk2, lam_init)

    q = q_ref[...]
    lane = lax.broadcasted_iota(I32, (TQ, LANES), 1)
    first = lane < DK
    zero = jnp.zeros((TQ, LANES), BF16)
    for hh in range(2):
        qh = q[:, hh * LANES:(hh + 1) * LANES]
        qs_scr[(2 * hh) * TQ:(2 * hh + 1) * TQ, :] = jnp.where(first, qh, zero)
        qs_scr[(2 * hh + 1) * TQ:(2 * hh + 2) * TQ, :] = jnp.where(first, zero, qh)
    m_scr[...] = jnp.full(m_scr.shape, NEG_INF, F32)
    acc_scr[...] = jnp.zeros(acc_scr.shape, F32)

    nt = (((1,), (1,)), ((), ()))

    def process(start, width, keep):
        for part in range(4):
            rows = slice(part * TQ, (part + 1) * TQ)
            s = lax.dot_general(qs_scr[rows, :], k_ref[pl.ds(start, width), :], nt,
                                preferred_element_type=F32)
            if keep is not None:
                s = jnp.where(keep, s, NEG_INF)
            m_prev = m_scr[rows, :]
            m_new = jnp.maximum(m_prev, jnp.max(s, axis=-1, keepdims=True))
            alpha = jnp.exp2(m_prev - m_new)
            p = jnp.concatenate([jnp.exp2(s[:, t * LANES:(t + 1) * LANES] - m_new)
                                 for t in range(width // LANES)], axis=1).astype(BF16)
            pv = jnp.dot(p, ve_ref[pl.ds(start, width), :], preferred_element_type=F32)
            acc_scr[rows, :LANES] = alpha * acc_scr[rows, :LANES] + pv[:, :LANES]
            acc_scr[rows, LANES:] = alpha * acc_scr[rows, LANES:] + pv[:, LANES:]
            m_scr[rows, :] = m_new

    q_lo = qi * TQ
    n_full = jnp.maximum(q_lo - PAD_ROWS, 0) // TKF
    tail_lo = PAD_ROWS + n_full * TKF
    n_tail = (q_lo + TQ - tail_lo + TQ - 1) // TQ

    def full(c):
        process(pl.multiple_of(PAD_ROWS + c * TKF, 16), TKF, None)

    def full_pair(c2, carry):
        full(2 * c2)
        full(2 * c2 + 1)
        return carry

    lax.fori_loop(0, n_full // 2, full_pair, 0)

    @pl.when(n_full % 2 == 1)
    def _():
        full(n_full - 1)

    def tail(w):
        width = w * TQ

        def run():
            start = pl.multiple_of(q_lo + TQ - width, 16)
            col = lax.broadcasted_iota(I32, (TQ, width), 1)
            qpos = q_lo + lax.broadcasted_iota(I32, (TQ, width), 0)
            process(start, width, (col <= qpos - start) & (col >= tail_lo - start))

        return run

    for w in range(1, TKF // TQ + 2):
        pl.when(n_tail == w)(tail(w))

    acc = acc_scr[...]
    o = acc[:, :LANES] / acc[:, LANES:]
    sub_w = sub_ref[...]
    for hh in range(2):
        o1 = o[(2 * hh) * TQ:(2 * hh + 1) * TQ, :]
        o2 = o[(2 * hh + 1) * TQ:(2 * hh + 2) * TQ, :]
        o_ref[:, hh * LANES:(hh + 1) * LANES] = _subln(o1 - lam * o2, sub_w, lam_init).astype(BF16)


def _expand_heads(x, e_ref):
    hi, lo = _split2(x)
    e = e_ref[...]
    return (jnp.dot(hi, e, preferred_element_type=F32)
            + jnp.dot(lo, e, preferred_element_type=F32))


def _gated_out(y, xs, z, dskip_full, normw):
    y = (y + dskip_full * xs) * _silu(z)
    half = D_INNER // N_BC_GROUPS
    outs = []
    for g in range(N_BC_GROUPS):
        yg = y[:, g * half:(g + 1) * half]
        ms = jnp.mean(yg * yg, axis=-1, keepdims=True)
        outs.append(yg * lax.rsqrt(ms + EPS))
    return jnp.concatenate(outs, axis=1) * normw


def _ssd_kernel(xbc_ref, z_ref, dt_ref, cw_ref, cb_ref, dtb_ref, a_ref, dskip_ref, normw_ref,
                e_ref, et_ref, tri_ref, ssm_ref, state_ref, ext_scr, h_scr):
    c = pl.program_id(0)
    L = CHUNK

    @pl.when(c == 0)
    def _():
        ext_scr[0:8, :] = jnp.zeros((8, XBC_DIM), F32)
        h_scr[...] = jnp.zeros(h_scr.shape, F32)

    x_new = xbc_ref[...]
    ext_scr[8:8 + L, :] = x_new
    conv = cb_ref[...] + cw_ref[CONV_W - 1:CONV_W, :] * x_new
    for j in range(CONV_W - 1):
        sh = CONV_W - 1 - j
        conv = conv + cw_ref[j:j + 1, :] * ext_scr[8 - sh:8 - sh + L, :]
    ext_scr[0:8, :] = x_new[L - 8:L, :]
    xc = _silu(conv)
    xs = xc[:, :D_INNER]

    rowi = lax.broadcasted_iota(I32, (L, LANES), 0)
    coli = lax.broadcasted_iota(I32, (L, LANES), 1)
    valid = (c * L + rowi) >= PAD_ROWS
    dt = jnp.where(valid, _softplus(dt_ref[...] + dtb_ref[...]), 0.0)
    da = dt * a_ref[...]
    tri = tri_ref[...]
    acs = sum(jnp.dot(tri, part, preferred_element_type=F32) for part in _split3(da))
    acs_t = acs.T
    dt_t = dt.T
    causal = coli <= rowi
    last = acs[L - 1:L, :]

    w_state = _expand_heads(dt * jnp.exp(last - acs), e_ref)
    dec_out = _expand_heads(jnp.exp(acs), e_ref)
    dec_rows = jnp.sum(et_ref[...] * jnp.exp(last), axis=-1, keepdims=True)
    xw = xs * w_state
    xs_b = xs.astype(BF16)

    half = D_INNER // N_BC_GROUPS
    hpg = N_HEADS_S // N_BC_GROUPS
    y_parts = []
    for g in range(N_BC_GROUPS):
        bm = xc[:, D_INNER + g * D_STATE:D_INNER + (g + 1) * D_STATE].astype(BF16)
        cm = xc[:, D_INNER + (N_BC_GROUPS + g) * D_STATE:
                D_INNER + (N_BC_GROUPS + g + 1) * D_STATE].astype(BF16)
        cb = lax.dot_general(cm, bm, (((1,), (1,)), ((), ())), preferred_element_type=F32)
        h_prev = h_scr[g * half:(g + 1) * half, :]
        y_off = lax.dot_general(cm, h_prev.astype(BF16), (((1,), (1,)), ((), ())),
                                preferred_element_type=F32)
        y_off = y_off * dec_out[:, g * half:(g + 1) * half]
        yd = []
        for r in range(0, hpg, 2):
            pair = []
            for hh in (r, r + 1):
                h = g * hpg + hh
                seg = jnp.broadcast_to(acs[:, h:h + 1], (L, L)) - acs_t[h:h + 1, :]
                lmat = jnp.where(causal, jnp.exp(jnp.where(causal, seg, 0.0)), 0.0)
                pair.append((cb * lmat * dt_t[h:h + 1, :]).astype(BF16))
            h0 = g * hpg + r
            xp = xs_b[:, h0 * SSM_HEAD_DIM:(h0 + 2) * SSM_HEAD_DIM]
            ya = jnp.dot(pair[0], xp, preferred_element_type=F32)
            yb = jnp.dot(pair[1], xp, preferred_element_type=F32)
            yd.append(jnp.where(coli < SSM_HEAD_DIM, ya, yb))
        y_parts.append(jnp.concatenate(yd, axis=1) + y_off)
        xw_t = xw[:, g * half:(g + 1) * half].T.astype(BF16)
        st = jnp.dot(xw_t, bm, preferred_element_type=F32)
        h_scr[g * half:(g + 1) * half, :] = dec_rows[g * half:(g + 1) * half, :] * h_prev + st
    y = jnp.concatenate(y_parts, axis=1)

    ssm_ref[...] = _gated_out(y, xs, z_ref[...], dskip_ref[...], normw_ref[...]).astype(BF16)

    @pl.when(c == pl.num_programs(0) - 1)
    def _():
        state_ref[...] = h_scr[...]


def _prompt_ssd(xbc, z, dt_raw, conv_w, conv_b, dtb, a_row, dskip_full, normw, e_mat, e_t, tri, n_t):
    row = lambda n: pl.BlockSpec((CHUNK, n), lambda c: (c, 0))
    const = lambda shp: pl.BlockSpec(shp, lambda c: (0, 0))
    return pl.pallas_call(
        _ssd_kernel,
        grid=(n_t // CHUNK,),
        in_specs=[row(XBC_DIM), row(D_INNER), row(LANES),
                  const((CONV_W, XBC_DIM)), const((1, XBC_DIM)), const((1, LANES)), const((1, LANES)),
                  const((1, D_INNER)), const((1, D_INNER)), const((LANES, D_INNER)),
                  const((D_INNER, LANES)), const((CHUNK, CHUNK))],
        out_specs=[row(D_INNER), pl.BlockSpec((D_INNER, D_STATE), lambda c: (0, 0))],
        out_shape=[jax.ShapeDtypeStruct((n_t, D_INNER), BF16),
                   jax.ShapeDtypeStruct((D_INNER, D_STATE), F32)],
        scratch_shapes=[pltpu.VMEM((8 + CHUNK, XBC_DIM), F32),
                        pltpu.VMEM((D_INNER, D_STATE), F32)],
        compiler_params=_cparams(("arbitrary",)),
        name="prompt_ssd",
    )(xbc, z, dt_raw, conv_w, conv_b, dtb, a_row, dskip_full, normw, e_mat, e_t, tri)


def _decode_attn_kernel(pt_ref, lq1, lk1, lq2, lk2, sub_ref, q_ref, kn_ref, vn_ref, *rest,
                        n_pages, lam_init):
    k_pages = rest[:n_pages]
    v_pages = rest[n_pages:2 * n_pages]
    o_ref = rest[2 * n_pages]
    del pt_ref
    lam = _diff_lambda(lq1, lk1, lq2, lk2, lam_init)
    nq = 2 * N_HEADS_A
    q8 = q_ref[...]
    lane = lax.broadcasted_iota(I32, (N_HEADS_A, LANES), 1)
    zero = jnp.zeros((N_HEADS_A, LANES), BF16)
    q16 = jnp.concatenate([jnp.where(lane < DK, q8, zero), jnp.where(lane < DK, zero, q8)], axis=0)

    pw = PAGE * N_KV_HEADS
    row_head = (lax.broadcasted_iota(I32, (nq, pw), 0) & (N_HEADS_A - 1)) >> 1
    col_head = lax.broadcasted_iota(I32, (nq, pw), 1) & (N_KV_HEADS - 1)
    own = row_head == col_head
    nt = (((1,), (1,)), ((), ()))
    s_pages = []
    for j in range(n_pages):
        kp = k_pages[j][...].astype(BF16)
        s = lax.dot_general(q16, kp, nt, preferred_element_type=F32)
        s_pages.append(jnp.where(own, s, NEG_INF))
    kn = kn_ref[...].astype(BF16)
    row_head_n = (lax.broadcasted_iota(I32, (nq, LANES), 0) & (N_HEADS_A - 1)) >> 1
    col_n = lax.broadcasted_iota(I32, (nq, LANES), 1)
    s_new = jnp.where(row_head_n == col_n, lax.dot_general(q16, kn, nt, preferred_element_type=F32), NEG_INF)

    m = jnp.max(s_new, axis=-1, keepdims=True)
    for s in s_pages:
        m = jnp.maximum(m, jnp.max(s, axis=-1, keepdims=True))
    p_new = jnp.exp2(s_new - m)
    l = jnp.sum(p_new, axis=-1, keepdims=True)
    acc = jnp.dot(p_new.astype(BF16), vn_ref[...].astype(BF16), preferred_element_type=F32)
    for j in range(n_pages):
        p = jnp.exp2(s_pages[j] - m)
        l = l + jnp.sum(p, axis=-1, keepdims=True)
        acc = acc + jnp.dot(p.astype(BF16), v_pages[j][...].astype(BF16), preferred_element_type=F32)
    o = acc / l
    res = o[:N_HEADS_A, :] - lam * o[N_HEADS_A:, :]
    o_ref[...] = _subln(res, sub_ref[...], lam_init).astype(BF16)


def _fused_attn_kernel(pt_ref, lq1, lk1, lq2, lk2, sub_ref, q_ref, k_ref, ve_ref, qd_ref, kn_ref, vn_ref, *rest,
                       n_pages, n_dec, lam_init, n_rows):
    pages = rest[:2 * n_pages]
    o_ref, od_ref = rest[2 * n_pages], rest[2 * n_pages + 1]
    scratch = rest[2 * n_pages + 2:]

    @pl.when(pl.program_id(1) < n_rows // TQ)
    def _():
        _attn_kernel(lq1, lk1, lq2, lk2, sub_ref, q_ref, k_ref, ve_ref, o_ref, *scratch,
                     lam_init=lam_init, n_rows=n_rows)

    step = pl.program_id(0) * pl.num_programs(1) + pl.program_id(1)

    @pl.when(step < n_dec)
    def _():
        _decode_attn_kernel(pt_ref, lq1, lk1, lq2, lk2, sub_ref, qd_ref, kn_ref, vn_ref, *pages, od_ref,
                            n_pages=n_pages, lam_init=lam_init)


def _attention(page_table, q, kb, vbe, q_s, k_new, v_new, cache_k2, cache_v2, lq1, lk1, lq2, lk2, sub_w,
               lam_init):
    R = q.shape[0]
    nq = R // TQ
    B, n_pages = page_table.shape
    assert R % TQ == 0 and TKF % TQ == 0
    n_inner = max(nq, pl.cdiv(B, N_KV_HEADS))
    pw = PAGE * N_KV_HEADS
    seq_of = lambda g, i: jnp.minimum(g * n_inner + i, B - 1)
    qblk = lambda i: jnp.minimum(i, nq - 1)
    small = lambda n: pl.BlockSpec((1, n), lambda g, i, pt: (0, 0))
    per_b = pl.BlockSpec((None, 8, LANES), lambda g, i, pt: (seq_of(g, i), 0, 0))
    new_tok = pl.BlockSpec((None, LANES, LANES), lambda g, i, pt: (seq_of(g, i), 0, 0))
    page_spec = lambda j: pl.BlockSpec((pw, LANES), lambda g, i, pt: (pt[seq_of(g, i), j], 0))
    grid_spec = pltpu.PrefetchScalarGridSpec(
        num_scalar_prefetch=1,
        grid=(N_KV_HEADS, n_inner),
        in_specs=[small(DK), small(DK), small(DK), small(DK), small(DV),
                  pl.BlockSpec((TQ, 2 * LANES), lambda g, i, pt: (qblk(i), g)),
                  pl.BlockSpec((R, LANES), lambda g, i, pt: (0, g)),
                  pl.BlockSpec((R, 2 * LANES), lambda g, i, pt: (0, g)),
                  per_b, new_tok, new_tok]
                 + [page_spec(j) for j in range(n_pages)] + [page_spec(j) for j in range(n_pages)],
        out_specs=[pl.BlockSpec((TQ, 2 * LANES), lambda g, i, pt: (qblk(i), g)), per_b],
        scratch_shapes=[pltpu.VMEM((4 * TQ, LANES), BF16),
                        pltpu.VMEM((4 * TQ, LANES), F32),
                        pltpu.VMEM((4 * TQ, 2 * LANES), F32)],
    )
    return pl.pallas_call(
        functools.partial(_fused_attn_kernel, n_pages=n_pages, n_dec=B, lam_init=lam_init, n_rows=R),
        grid_spec=grid_spec,
        out_shape=[jax.ShapeDtypeStruct((R, N_HEADS_A * DV), BF16),
                   jax.ShapeDtypeStruct((B, 8, LANES), BF16)],
        compiler_params=_cparams(("arbitrary", "arbitrary")),
        name="attention",
    )(page_table, lq1, lk1, lq2, lk2, sub_w, q, kb, vbe, q_s, k_new, v_new,
      *([cache_k2] * n_pages), *([cache_v2] * n_pages))


def _dec_pre_kernel(xbc_ref, s0_ref, s1_ref, s2_ref, dt_ref, cw_ref, cb_ref, dtb_ref, a_ref, e_ref,
                    xc_ref, xdt_ref, dect_ref):
    conv = (cb_ref[...] + cw_ref[0:1, :] * s0_ref[...] + cw_ref[1:2, :] * s1_ref[...]
            + cw_ref[2:3, :] * s2_ref[...] + cw_ref[3:4, :] * xbc_ref[...])
    xc = _silu(conv)
    xc_ref[...] = xc
    dt = _softplus(dt_ref[...] + dtb_ref[...])
    dec = jnp.exp(dt * a_ref[...])
    hi, mid, lo = _split3(dt)
    e = e_ref[...]
    dt_full = (jnp.dot(hi, e, preferred_element_type=F32) + jnp.dot(mid, e, preferred_element_type=F32)
               + jnp.dot(lo, e, preferred_element_type=F32))
    hi, mid, lo = _split3(dec)
    dec_full = (jnp.dot(hi, e, preferred_element_type=F32) + jnp.dot(mid, e, preferred_element_type=F32)
                + jnp.dot(lo, e, preferred_element_type=F32))
    xdt_ref[...] = (dt_full * xc[:, :D_INNER]).T
    dect_ref[...] = dec_full.T


def _dec_state_kernel(h_ref, xdt_ref, dect_ref, bm_ref, cm_ref, hout_ref, y_ref):
    i = pl.program_id(0)
    B = xdt_ref.shape[1]
    lane = lax.broadcasted_iota(I32, (D_INNER, B), 1)
    sub = lax.broadcasted_iota(I32, (TB, D_INNER // N_BC_GROUPS), 0)
    ones = jnp.ones((B, D_STATE), BF16)
    half = D_INNER // N_BC_GROUPS
    nt = (((1,), (1,)), ((), ()))
    y_acc = [jnp.zeros((TB, half), F32) for _ in range(N_BC_GROUPS)]
    for j in range(TB):
        b = i * TB + j
        sel = lane == b
        xcol = jnp.where(sel, xdt_ref[...], 0.0)
        dcol = jnp.where(sel, dect_ref[...], 0.0)
        dec = sum(jnp.dot(part, ones, preferred_element_type=F32) for part in _split3(dcol))
        xhi, xlo = _split2(xcol)
        hnew = dec * h_ref[j]
        for g in range(N_BC_GROUPS):
            bmat = bm_ref[:, g * D_STATE:(g + 1) * D_STATE].astype(BF16)
            upd = (jnp.dot(xhi[g * half:(g + 1) * half], bmat, preferred_element_type=F32)
                   + jnp.dot(xlo[g * half:(g + 1) * half], bmat, preferred_element_type=F32))
            hg = hnew[g * half:(g + 1) * half] + upd
            hout_ref[j, g * half:(g + 1) * half, :] = hg
            start = pl.multiple_of(i * TB, TB)
            c8 = cm_ref[pl.ds(start, TB), g * D_STATE:(g + 1) * D_STATE].astype(BF16)
            yj = lax.dot_general(c8, hg.astype(BF16), nt, preferred_element_type=F32)
            y_acc[g] = y_acc[g] + jnp.where(sub == j, yj, 0.0)
    y_ref[...] = jnp.concatenate(y_acc, axis=1)


def _dec_post_kernel(y_ref, xc_ref, z_ref, dskip_ref, normw_ref, o_ref):
    o_ref[...] = _gated_out(y_ref[...], xc_ref[:, :D_INNER], z_ref[...], dskip_ref[...],
                            normw_ref[...]).astype(BF16)


def _decode_ssm(xbc_s, sc0, sc1, sc2, dt_s, z_s, state, conv_w, conv_b, dtb, a_row, dskip_full, normw, e_mat):
    B = xbc_s.shape[0]
    xc, xdt_t, dec_t = pl.pallas_call(
        _dec_pre_kernel,
        out_shape=[jax.ShapeDtypeStruct((B, XBC_DIM), F32),
                   jax.ShapeDtypeStruct((D_INNER, B), F32),
                   jax.ShapeDtypeStruct((D_INNER, B), F32)],
        compiler_params=pltpu.CompilerParams(vmem_limit_bytes=VMEM_LIMIT),
        name="decode_ssm_pre",
    )(xbc_s, sc0, sc1, sc2, dt_s, conv_w, conv_b, dtb, a_row, e_mat)
    bm = xc[:, D_INNER:D_INNER + N_BC_GROUPS * D_STATE]
    cm = xc[:, D_INNER + N_BC_GROUPS * D_STATE:]
    const = lambda shp: pl.BlockSpec(shp, lambda i: (0,) * len(shp))
    state_spec = pl.BlockSpec((TB, D_INNER, D_STATE), lambda i: (i, 0, 0))
    h_new, y = pl.pallas_call(
        _dec_state_kernel,
        grid=(B // TB,),
        in_specs=[state_spec, const((D_INNER, B)), const((D_INNER, B)),
                  const((B, N_BC_GROUPS * D_STATE)), const((B, N_BC_GROUPS * D_STATE))],
        out_specs=[state_spec, pl.BlockSpec((TB, D_INNER), lambda i: (i, 0))],
        out_shape=[jax.ShapeDtypeStruct((B, D_INNER, D_STATE), F32),
                   jax.ShapeDtypeStruct((B, D_INNER), F32)],
        compiler_params=_cparams(("arbitrary",)),
        name="decode_ssm_state",
    )(state, xdt_t, dec_t, bm, cm)
    ssm = pl.pallas_call(
        _dec_post_kernel,
        out_shape=jax.ShapeDtypeStruct((B, D_INNER), BF16),
        compiler_params=pltpu.CompilerParams(vmem_limit_bytes=VMEM_LIMIT),
        name="decode_ssm_post",
    )(y, xc, z_s, dskip_full, normw)
    return ssm, h_new


def _outproj_kernel(front_ref, xa_ref, xb_ref, xs_ref, att_ref, sa_ref, sb_ref, ss_ref,
                    wa_ref, ws_ref, nw_ref, wrh_ref, wrl_ref, h_ref, u_ref, lg_ref):
    last = pl.program_id(0) == pl.num_programs(0) - 1
    ssm = jnp.concatenate([sa_ref[...], jnp.where(last, ss_ref[...], sb_ref[...])], axis=0)
    acc = jnp.dot(att_ref[...], wa_ref[...], preferred_element_type=F32)
    acc = acc + jnp.dot(ssm, ws_ref[...], preferred_element_type=F32)
    h = _token_rows(front_ref, xa_ref, xb_ref, xs_ref) + acc
    h_ref[...] = h
    ms = jnp.mean(h * h, axis=-1, keepdims=True)
    u = h * lax.rsqrt(ms + EPS) * nw_ref[...]
    u_ref[...] = u
    uh, ul = _split2(u)
    wh, wl = wrh_ref[...], wrl_ref[...]
    lg = (jnp.dot(uh, wh, preferred_element_type=F32) + jnp.dot(uh, wl, preferred_element_type=F32)
          + jnp.dot(ul, wh, preferred_element_type=F32))
    lg_ref[...] = _route_rows(lg, pl.program_id(0) * TM)


def _first_max(x, lane):
    m = jnp.max(x, axis=-1, keepdims=True)
    return m, jnp.min(jnp.where(x == m, lane, float(LANES)), axis=-1, keepdims=True)


def _route_rows(lg, row0):
    rows = lg.shape[0]
    lane_i = lax.broadcasted_iota(I32, (rows, LANES), 1)
    lane = lane_i.astype(F32)
    gl = jnp.where(lane_i < N_EGROUPS, lg, -jnp.inf)
    gmax, g = _first_max(gl, lane)
    pg = 1.0 / jnp.sum(jnp.exp(gl - gmax), axis=-1, keepdims=True)
    lo = N_EGROUPS + g * EXPERTS_PER_GROUP
    el = jnp.where((lane >= lo) & (lane < lo + EXPERTS_PER_GROUP), lg, -jnp.inf)
    m1, i1 = _first_max(el, lane)
    m2, i2 = _first_max(jnp.where(lane == i1, -jnp.inf, el), lane)
    z = jnp.sum(jnp.exp(el - m1), axis=-1, keepdims=True)
    p1 = 1.0 / z
    p2 = jnp.exp(m2 - m1) / z
    tot = p1 + p2
    valid = (row0 + lax.broadcasted_iota(I32, (rows, 1), 0)) >= PAD_ROWS
    e1 = jnp.where(valid, i1 - N_EGROUPS, -1.0)
    e2 = jnp.where(valid, i2 - N_EGROUPS, -1.0)
    w1 = jnp.where(valid, p1 / tot * pg, 0.0)
    w2 = jnp.where(valid, p2 / tot * pg, 0.0)
    return jnp.where(lane_i == 0, e1, jnp.where(lane_i == 1, e2, jnp.where(lane_i == 2, w1,
                                                                         jnp.where(lane_i == 3, w2, 0.0))))


def _rank_kernel(route_ref, tri_ref, utri_ref, rank_ref, meta_ref, carry, *, nblk):
    i = pl.program_id(0)

    @pl.when(i == 0)
    def _():
        carry[...] = jnp.zeros(carry.shape, F32)

    r = route_ref[...]
    lane = lax.broadcasted_iota(I32, r.shape, 1).astype(F32)
    oh0 = (lane == r[:, 0:1]).astype(F32)
    oh1 = (lane == r[:, 1:2]).astype(F32)
    oh = oh0 + oh1
    before = jnp.dot(tri_ref[...], oh.astype(BF16), preferred_element_type=F32) + carry[0:1, :]
    rank0 = jnp.sum(before * oh0, axis=-1, keepdims=True)
    rank1 = jnp.sum(before * oh1, axis=-1, keepdims=True)
    lane_i = lax.broadcasted_iota(I32, r.shape, 1)
    rank_ref[...] = jnp.where(lane_i == 0, rank0, jnp.where(lane_i == 1, rank1, 0.0))
    carry[0:1, :] = carry[0:1, :] + jnp.sum(oh, axis=0, keepdims=True)

    @pl.when(i == pl.num_programs(0) - 1)
    def _():
        nb_pad = meta_ref.shape[0] - 8
        counts = carry[0:1, :]
        pb = jnp.floor((counts + (TMOE - 1)) / TMOE)
        ends = jnp.dot(jnp.broadcast_to(pb, (8, LANES)).astype(BF16), utri_ref[...],
                       preferred_element_type=F32)[0:1, :]
        starts = ends - pb
        lane1 = lax.broadcasted_iota(I32, (1, LANES), 1)
        n_valid = jnp.sum(jnp.where(lane1 == N_EXPERTS - 1, ends, 0.0), axis=-1, keepdims=True)
        blk = lax.broadcasted_iota(I32, (nb_pad, LANES), 0).astype(F32)
        lane_b = lax.broadcasted_iota(I32, (nb_pad, LANES), 1)
        is_e = lane_b < N_EXPERTS
        block_e = jnp.minimum(jnp.sum(jnp.where(is_e & (ends <= blk), 1.0, 0.0), axis=-1, keepdims=True),
                              N_EXPERTS - 1.0)
        lane_bf = lane_b.astype(F32)
        has = is_e & (counts > 0.0) & (lane_bf > block_e)
        nxt = jnp.min(jnp.where(has, lane_bf, float(LANES)), axis=-1, keepdims=True)
        next_e = jnp.where(nxt >= LANES, -1.0, nxt)
        mine = lane_bf == block_e
        cnt_b = jnp.sum(jnp.where(mine, counts, 0.0), axis=-1, keepdims=True)
        st_b = jnp.sum(jnp.where(mine, starts, 0.0), axis=-1, keepdims=True)
        n_real = jnp.clip(cnt_b - (blk[:, 0:1] - st_b) * TMOE, 0.0, float(TMOE))
        meta_ref[0:8, :] = jnp.where(lax.broadcasted_iota(I32, (8, LANES), 0) == 0, starts,
                                     jnp.broadcast_to(n_valid, (8, LANES)))
        meta_ref[8:, :] = jnp.where(lane_b == 0, block_e, jnp.where(lane_b == 1, next_e,
                                                                    jnp.where(lane_b == 2, n_real, 0.0)))


def _rank(route, nblk):
    R = route.shape[0]
    nb_pad = (nblk + 7) // 8 * 8
    tri = (jnp.arange(TM)[:, None] > jnp.arange(TM)[None, :]).astype(BF16)
    utri = (jnp.arange(LANES)[:, None] <= jnp.arange(LANES)[None, :]).astype(BF16)
    return pl.pallas_call(
        functools.partial(_rank_kernel, nblk=nblk),
        grid=(R // TM,),
        in_specs=[pl.BlockSpec((TM, LANES), lambda i: (i, 0)),
                  pl.BlockSpec((TM, TM), lambda i: (0, 0)),
                  pl.BlockSpec((LANES, LANES), lambda i: (0, 0))],
        out_specs=[pl.BlockSpec((TM, LANES), lambda i: (i, 0)),
                   pl.BlockSpec((8 + nb_pad, LANES), lambda i: (0, 0))],
        out_shape=[jax.ShapeDtypeStruct((R, LANES), F32),
                   jax.ShapeDtypeStruct((8 + nb_pad, LANES), F32)],
        scratch_shapes=[pltpu.VMEM((8, LANES), F32)],
        compiler_params=_cparams(("arbitrary",)),
        name="expert_rank",
    )(route, tri, utri)


def _outproj(front, xp, xs, att, ssm_p, ssm_s, wa, ws, norm_w, wr_hi, wr_lo):
    R = front.shape[0] + xp.shape[0] + xs.shape[0]
    row = lambda n: pl.BlockSpec((TM, n), lambda i: (i, 0))
    nbp = ssm_p.shape[0] // HALF
    half = lambda f: pl.BlockSpec((HALF, D_INNER), f)
    return pl.pallas_call(
        _outproj_kernel,
        grid=(R // TM,),
        in_specs=_token_specs(xp.shape[0]) + [
            row(N_HEADS_A * DV),
            half(lambda i: (2 * i, 0)), half(lambda i: (jnp.minimum(2 * i + 1, nbp - 1), 0)),
            half(lambda i: (0, 0)),
            _resident(wa.shape), _resident(ws.shape), _resident((1, D_MODEL)),
            _resident(wr_hi.shape), _resident(wr_lo.shape)],
        out_specs=[row(D_MODEL), row(D_MODEL), row(LANES)],
        out_shape=[jax.ShapeDtypeStruct((R, D_MODEL), F32),
                   jax.ShapeDtypeStruct((R, D_MODEL), F32),
                   jax.ShapeDtypeStruct((R, LANES), F32)],
        compiler_params=_cparams(("arbitrary",)),
        name="outproj",
    )(front, xp, xp, xs, att, ssm_p, ssm_p, ssm_s, wa, ws, norm_w, wr_hi, wr_lo)


GATHER_UNROLL = 8

def _start_row_gather(idx_ref, src_hbm, dst, sem, n_groups=None):
    def start(r8, carry):
        for j in range(GATHER_UNROLL):
            r = r8 * GATHER_UNROLL + j
            pltpu.make_async_copy(src_hbm.at[pl.ds(idx_ref[0, r], 1)], dst.at[pl.ds(r, 1)],
                                  sem).start(priority=j % 2)
        return carry

    lax.fori_loop(0, dst.shape[0] // GATHER_UNROLL if n_groups is None else n_groups, start, 0)


def _wait_row_gather(src_hbm, dst, sem, n_groups=None):
    if n_groups is None:
        pltpu.make_async_copy(src_hbm.at[pl.ds(0, dst.shape[0])], dst, sem).wait()
        return

    def wait(g, carry):
        pltpu.make_async_copy(src_hbm.at[pl.ds(0, GATHER_UNROLL)], dst.at[pl.ds(0, GATHER_UNROLL)], sem).wait()
        return carry

    lax.fori_loop(0, n_groups, wait, 0)


def _row_groups(n_real):
    return (n_real + (GATHER_UNROLL - 1)) // GATHER_UNROLL


def _moe_up_kernel(be_ref, nx_ref, nr_ref, nv_ref, idx_ref, idx_next_ref, x_hbm, wg_hbm, wu_hbm, h_ref,
                   xbuf, stage_g, stage_u, wg_b, wu_b, sems, row_sems):
    i = pl.program_id(0)
    valid = i < nv_ref[0]
    e = be_ref[i]
    changed = (i == 0) | (be_ref[jnp.maximum(i - 1, 0)] != e)
    slot = i % 2
    n_real = nr_ref[i]

    @pl.when(i == 0)
    def _():
        xbuf[...] = jnp.zeros(xbuf.shape, F32)
        _start_row_gather(idx_ref, x_hbm, xbuf.at[0], row_sems.at[0], _row_groups(n_real))

    @pl.when(i + 1 < nv_ref[0])
    def _():
        _start_row_gather(idx_next_ref, x_hbm, xbuf.at[1 - slot], row_sems.at[1 - slot],
                          _row_groups(nr_ref[jnp.minimum(i + 1, pl.num_programs(0) - 1)]))

    def fetch(expert):
        return (pltpu.make_async_copy(wg_hbm.at[expert], stage_g, sems.at[0]),
                pltpu.make_async_copy(wu_hbm.at[expert], stage_u, sems.at[1]))

    @pl.when(i == 0)
    def _():
        for c in fetch(e):
            c.start()

    @pl.when(valid & changed)
    def _():
        for c in fetch(e):
            c.wait()
        wg_b[...] = stage_g[...].astype(BF16)
        wu_b[...] = stage_u[...].astype(BF16)
        nxt = nx_ref[i]

        @pl.when(nxt >= 0)
        def _():
            for c in fetch(nxt):
                c.start()

    def mlp_up(rows):
        x = xbuf[slot, :rows, :].astype(BF16)
        g = jnp.dot(x, wg_b[...], preferred_element_type=F32)
        u = jnp.dot(x, wu_b[...], preferred_element_type=F32)
        h_ref[:rows, :] = (_silu(g) * u).astype(BF16)

    @pl.when(valid)
    def _():
        _wait_row_gather(x_hbm, xbuf.at[slot], row_sems.at[slot], _row_groups(n_real))

    @pl.when(valid & (n_real > TSMALL))
    def _():
        mlp_up(TMOE)

    @pl.when(valid & (n_real <= TSMALL))
    def _():
        mlp_up(TSMALL)
        h_ref[TSMALL:, :] = jnp.zeros((TMOE - TSMALL, D_FF), BF16)

    @pl.when(jnp.logical_not(valid))
    def _():
        h_ref[...] = jnp.zeros(h_ref.shape, BF16)


def _moe_down_kernel(be_ref, nr_ref, nv_ref, h_ref, wd_ref, o_ref, wd_b):
    i = pl.program_id(0)
    valid = i < nv_ref[0]
    changed = (i == 0) | (be_ref[jnp.maximum(i - 1, 0)] != be_ref[i])
    n_real = nr_ref[i]

    @pl.when(valid & changed)
    def _():
        wd_b[...] = wd_ref[...].astype(BF16)

    @pl.when(valid & (n_real > TSMALL))
    def _():
        o_ref[...] = jnp.dot(h_ref[...], wd_b[...], preferred_element_type=F32)

    @pl.when(valid & (n_real <= TSMALL))
    def _():
        o_ref[:TSMALL, :] = jnp.dot(h_ref[:TSMALL, :], wd_b[...], preferred_element_type=F32)
        o_ref[TSMALL:, :] = jnp.zeros((TMOE - TSMALL, D_MODEL), F32)

    @pl.when(jnp.logical_not(valid))
    def _():
        o_ref[...] = jnp.zeros(o_ref.shape, F32)


def _moe(block_e, next_e, n_real, n_valid, buf_tok, x_tok, w_gate, w_up, w_down):
    rows = buf_tok.shape[0]
    nblk = rows // TMOE

    def blk(i, nv):
        return jnp.minimum(i, nv[0] - 1)

    up_spec = pltpu.PrefetchScalarGridSpec(
        num_scalar_prefetch=4,
        grid=(nblk,),
        in_specs=[pl.BlockSpec((None, 1, TMOE), lambda i, be, nx, nr, nv: (blk(i, nv), 0, 0),
                               memory_space=pltpu.SMEM),
                  pl.BlockSpec((None, 1, TMOE), lambda i, be, nx, nr, nv: (blk(i + 1, nv), 0, 0),
                               memory_space=pltpu.SMEM),
                  pl.BlockSpec(memory_space=pl.ANY),
                  pl.BlockSpec(memory_space=pl.ANY), pl.BlockSpec(memory_space=pl.ANY)],
        out_specs=pl.BlockSpec((TMOE, D_FF), lambda i, be, nx, nr, nv: (i, 0)),
        scratch_shapes=[pltpu.VMEM((2, TMOE, D_MODEL), F32),
                        pltpu.VMEM((D_MODEL, D_FF), F32), pltpu.VMEM((D_MODEL, D_FF), F32),
                        pltpu.VMEM((D_MODEL, D_FF), BF16), pltpu.VMEM((D_MODEL, D_FF), BF16),
                        pltpu.SemaphoreType.DMA((2,)), pltpu.SemaphoreType.DMA((2,))],
    )
    slot_rows = buf_tok.reshape(nblk, 1, TMOE)
    hmid = pl.pallas_call(
        _moe_up_kernel,
        grid_spec=up_spec,
        out_shape=jax.ShapeDtypeStruct((rows, D_FF), BF16),
        compiler_params=_cparams(("arbitrary",)),
        name="moe_up",
    )(block_e, next_e, n_real, n_valid, slot_rows, slot_rows, x_tok, w_gate, w_up)

    down_spec = pltpu.PrefetchScalarGridSpec(
        num_scalar_prefetch=3,
        grid=(nblk,),
        in_specs=[pl.BlockSpec((TMOE, D_FF), lambda i, be, nr, nv: (blk(i, nv), 0)),
                  pl.BlockSpec((None, D_FF, D_MODEL), lambda i, be, nr, nv: (be[blk(i, nv)], 0, 0))],
        out_specs=pl.BlockSpec((TMOE, D_MODEL), lambda i, be, nr, nv: (i, 0)),
        scratch_shapes=[pltpu.VMEM((D_FF, D_MODEL), BF16)],
    )
    return pl.pallas_call(
        _moe_down_kernel,
        grid_spec=down_spec,
        out_shape=jax.ShapeDtypeStruct((rows, D_MODEL), F32),
        compiler_params=_cparams(("arbitrary",)),
        name="moe_down",
    )(block_e, n_real, n_valid, hmid, w_down)


def _final_kernel(idx_ref, idx_next_ref, h_ref, w_ref, nw_ref, y_hbm, op_ref, os_ref, ybuf, row_sems):
    i = pl.program_id(0)
    last = pl.num_programs(0) - 1
    slot = i % 2

    @pl.when(i == 0)
    def _():
        _start_row_gather(idx_ref, y_hbm, ybuf.at[0], row_sems.at[0])

    @pl.when(i + 1 < pl.num_programs(0))
    def _():
        _start_row_gather(idx_next_ref, y_hbm, ybuf.at[1 - slot], row_sems.at[1 - slot])

    _wait_row_gather(y_hbm, ybuf.at[slot], row_sems.at[slot])
    w = w_ref[...]
    moe = ybuf[slot, :HALF, :] * w[:, 0:1] + ybuf[slot, HALF:, :] * w[:, 1:2]
    h = h_ref[...] + moe
    ms = jnp.mean(h * h, axis=-1, keepdims=True)
    out = h * lax.rsqrt(ms + EPS) * nw_ref[...]

    @pl.when(i < last)
    def _():
        op_ref[...] = out

    @pl.when(i == last)
    def _():
        os_ref[...] = out


def _final(h1, yb, dest, wts, norm_w, seq):
    R = h1.shape[0]
    n = R // HALF
    nbp = seq // HALF
    assert R == (1 + nbp + 1) * HALF
    row = lambda w: pl.BlockSpec((HALF, w), lambda i: (i, 0))
    slots = dest.reshape(n, HALF, TOP_K).transpose(0, 2, 1).reshape(n, 1, TOP_K * HALF)
    idx_spec = lambda off: pl.BlockSpec((None, 1, TOP_K * HALF), lambda i: (jnp.minimum(i + off, n - 1), 0, 0),
                                        memory_space=pltpu.SMEM)
    return pl.pallas_call(
        _final_kernel,
        grid=(n,),
        in_specs=[idx_spec(0), idx_spec(1), row(D_MODEL), row(TOP_K),
                  pl.BlockSpec((1, D_MODEL), lambda i: (0, 0)),
                  pl.BlockSpec(memory_space=pl.ANY)],
        out_specs=[pl.BlockSpec((HALF, D_MODEL), lambda i: (jnp.clip(i - 1, 0, nbp - 1), 0)),
                   pl.BlockSpec((HALF, D_MODEL), lambda i: (0, 0))],
        out_shape=[jax.ShapeDtypeStruct((seq, D_MODEL), F32),
                   jax.ShapeDtypeStruct((HALF, D_MODEL), F32)],
        scratch_shapes=[pltpu.VMEM((2, TOP_K * HALF, D_MODEL), F32), pltpu.SemaphoreType.DMA((2,))],
        compiler_params=_cparams(("arbitrary",)),
        name="final_norm",
    )(slots, slots, h1, wts, norm_w, yb)


def _route(route):
    R = route.shape[0]
    a_max = TOP_K * R
    nblk = (a_max + N_EXPERTS * (TMOE - 1) + TMOE - 1) // TMOE
    rows = nblk * TMOE
    rank, meta = _rank(route, nblk)
    eid = route[:, :TOP_K].astype(I32)
    wts = route[:, TOP_K:2 * TOP_K]
    starts = meta[0, :N_EXPERTS].astype(I32) * TMOE
    n_valid = meta[1, :1].astype(I32)
    block_e = meta[8:8 + nblk, 0].astype(I32)
    next_e = meta[8:8 + nblk, 1].astype(I32)
    n_real = meta[8:8 + nblk, 2].astype(I32)
    is_real = eid >= 0
    dest = jnp.where(is_real, starts[jnp.maximum(eid, 0)] + rank[:, :TOP_K].astype(I32), 0)
    tok = jnp.broadcast_to(jnp.arange(R, dtype=I32)[:, None], (R, TOP_K))
    buf_tok = (jnp.arange(rows, dtype=I32) % PAD_ROWS).at[jnp.where(is_real, dest, rows).reshape(-1)].set(
        tok.reshape(-1), mode="drop")
    return wts, dest, buf_tok, block_e, next_e, n_real, n_valid


def kernel(x_prompt, x_sample, cache_k, cache_v, state_conv, state_ssm, page_table, meta_tokens, norm_mix_w,
           w_in, lambda_q1, lambda_k1, lambda_q2, lambda_k2, subln_w, conv_w, conv_b, dt_bias, a_log, d_skip,
           ssm_norm_w, w_out, norm_ffn_w, w_router_group, w_router_expert, w_gate, w_up, w_down, final_norm_w):
    depth = w_in.shape[0]
    assert depth == 1 and x_prompt.shape[0] == 1 and x_sample.shape[1] == 1
    seq = x_prompt.shape[1]
    B = x_sample.shape[0]
    n_pages = page_table.shape[1]
    n_phys = cache_k.shape[1]
    past = n_pages * cache_k.shape[2]
    assert cache_k.shape[2] == PAGE and seq % TM == 0 and B == HALF and FRONT == HALF
    n_t = FRONT + seq
    R = n_t + B
    l = 0
    lam_init = 0.8 - 0.6 * math.exp(-0.3 * l)

    front = jnp.concatenate([jnp.zeros((PAD_ROWS, D_MODEL), F32), meta_tokens.astype(F32)], axis=0)
    xp, xs = x_prompt[0], x_sample[:, 0]
    w = w_in[l]
    wq = w[:, :OFF_K].astype(BF16)
    wk = w[:, OFF_K:OFF_V].astype(BF16)
    wv = w[:, OFF_V:OFF_Z].astype(BF16)
    wz = w[:, OFF_Z:OFF_XBC].astype(BF16)
    wx = w[:, OFF_XBC:OFF_DT].astype(BF16)
    wdt = jnp.pad(w[:, OFF_DT:], ((0, 0), (0, LANES - N_HEADS_S))).astype(BF16)

    pos = jnp.concatenate([jnp.maximum(jnp.arange(n_t, dtype=I32) - PAD_ROWS, 0),
                           jnp.full((B,), past, I32)])
    inv = ROPE_THETA ** (-jnp.arange(0, ROT_DIM, 2, dtype=F32) / ROT_DIM)
    ang = pos.astype(F32)[:, None] * inv[None, :]
    cos, sin = jnp.cos(ang), jnp.sin(ang)
    half = ROT_DIM // 2
    ones = jnp.ones((R, DK - ROT_DIM), F32)
    zeros = jnp.zeros((R, DK - ROT_DIM), F32)
    zh = jnp.zeros((R, half), F32)
    ctab = jnp.tile(jnp.concatenate([cos, cos, ones], axis=1), (1, 2))
    s1tab = jnp.tile(jnp.concatenate([-sin, zh, zeros], axis=1), (1, 2))
    s2tab = jnp.tile(jnp.concatenate([zh, sin, zeros], axis=1), (1, 2))

    row1 = lambda v: v.reshape(1, -1).astype(F32)
    pad_heads = lambda v: jnp.pad(v.astype(F32), (0, LANES - N_HEADS_S)).reshape(1, LANES)
    a_row = pad_heads(-jnp.exp(a_log[l].astype(F32)))
    dtb = pad_heads(dt_bias[l])
    dskip_full = jnp.repeat(d_skip[l].astype(F32), SSM_HEAD_DIM).reshape(1, D_INNER)
    normw = row1(ssm_norm_w[l])
    head_of_col = jnp.arange(D_INNER, dtype=I32) // SSM_HEAD_DIM
    e_mat = (jnp.arange(LANES, dtype=I32)[:, None] == head_of_col[None, :]).astype(BF16)
    tri = (jnp.arange(CHUNK)[:, None] >= jnp.arange(CHUNK)[None, :]).astype(BF16)
    lq1, lk1, lq2, lk2 = (row1(v[l]) for v in (lambda_q1, lambda_k1, lambda_q2, lambda_k2))
    sub_w = row1(subln_w[l])

    q, k, v, kb, vb, z, xbc, dt_raw = _inproj(front, xp, xs, row1(norm_mix_w[l]), wq, wk, wv, wz, wx, wdt,
                                              ctab, s1tab, s2tab)

    k_s, v_s = k[n_t:], v[n_t:]
    pad8 = lambda a: jnp.pad(a.reshape(B, N_KV_HEADS, LANES), ((0, 0), (0, LANES - N_KV_HEADS), (0, 0)))
    att_all, att_s = _attention(page_table, q, kb, vb, q[n_t:].reshape(B, N_HEADS_A, LANES), pad8(k_s), pad8(v_s),
                                cache_k[l].reshape(n_phys * PAGE * N_KV_HEADS, LANES),
                                cache_v[l].reshape(n_phys * PAGE * N_KV_HEADS, LANES),
                                lq1, lk1, lq2, lk2, sub_w, lam_init)
    ssm_p, state_p = _prompt_ssd(xbc, z, dt_raw, conv_w[l].astype(F32), row1(conv_b[l]), dtb, a_row,
                                 dskip_full, normw, e_mat, e_mat.T.astype(F32), tri, n_t)

    sc = state_conv[l]
    xbc_s = xbc[n_t:]
    ssm_s, state_s = _decode_ssm(xbc_s, sc[:, 0], sc[:, 1], sc[:, 2], dt_raw[n_t:], z[n_t:],
                                 state_ssm[l].reshape(B, D_INNER, D_STATE), conv_w[l].astype(F32),
                                 row1(conv_b[l]), dtb, a_row, dskip_full, normw, e_mat)

    att = lax.dynamic_update_slice(att_all, att_s.reshape(B, N_HEADS_A * DV), (n_t, 0))

    wo = w_out[l].astype(BF16)
    wr = jnp.pad(jnp.concatenate([w_router_group[l], w_router_expert[l]], axis=1).astype(F32),
                 ((0, 0), (0, LANES - N_EGROUPS - N_EXPERTS)))
    wr_hi = wr.astype(BF16)
    wr_lo = (wr - wr_hi.astype(F32)).astype(BF16)
    h1, u2, route = _outproj(front, xp, xs, att, ssm_p, ssm_s, wo[:N_HEADS_A * DV], wo[N_HEADS_A * DV:],
                             row1(norm_ffn_w[l]), wr_hi, wr_lo)

    wts, dest, buf_tok, block_e, next_e, n_real, n_valid = _route(route)
    yb = _moe(block_e, next_e, n_real, n_valid, buf_tok, u2, w_gate[l], w_up[l], w_down[l])
    y_p, y_s = _final(h1, yb, dest, wts, row1(final_norm_w), seq)

    t_real = N_META + seq
    y_prompt = y_p.reshape(1, seq, D_MODEL)
    y_sample = y_s.reshape(B, 1, D_MODEL)
    k_prompt = k[PAD_ROWS:n_t].reshape(1, 1, t_real, N_KV_HEADS, 2 * DK)
    v_prompt = v[PAD_ROWS:n_t].reshape(1, 1, t_real, N_KV_HEADS, DV)
    conv_prompt = xbc[n_t - (CONV_W - 1):n_t].reshape(1, 1, CONV_W - 1, XBC_DIM)
    ssm_prompt = state_p.reshape(1, 1, N_HEADS_S, SSM_HEAD_DIM, D_STATE)
    k_sample = k_s.reshape(1, B, 1, N_KV_HEADS, 2 * DK)
    v_sample = v_s.reshape(1, B, 1, N_KV_HEADS, DV)
    conv_sample = jnp.stack([sc[:, 1], sc[:, 2], xbc_s], axis=1)[None]
    ssm_sample = state_s.reshape(1, B, N_HEADS_S, SSM_HEAD_DIM, D_STATE)
    return (y_prompt, y_sample, k_prompt, v_prompt, conv_prompt, ssm_prompt,
            k_sample, v_sample, conv_sample, ssm_sample)
```

```python
import functools
import math

import jax
import jax.numpy as jnp
from jax import lax
from jax.experimental import pallas as pl
from jax.experimental.pallas import tpu as pltpu

F32 = jnp.float32
BF16 = jnp.bfloat16
I32 = jnp.int32

D_MODEL = 2048
N_META = 16
EPS = 1e-6
DV = 128
DK = 64
N_HEADS_A = 8
N_KV_HEADS = 4
ROT_DIM = 16
ROPE_THETA = 500000.0
NEG_INF = -1e30
LOG2E = 1.4426950408889634
D_INNER = 1024
SSM_HEAD_DIM = 64
N_HEADS_S = 16
N_BC_GROUPS = 2
D_STATE = 128
CONV_W = 4
XBC_DIM = D_INNER + 2 * N_BC_GROUPS * D_STATE
CHUNK = 128
Q_DIM = N_HEADS_A * 2 * DK
K_DIM = N_KV_HEADS * 2 * DK
V_DIM = N_KV_HEADS * DV
OFF_K = Q_DIM
OFF_V = OFF_K + K_DIM
OFF_Z = OFF_V + V_DIM
OFF_XBC = OFF_Z + D_INNER
OFF_DT = OFF_XBC + XBC_DIM
IN_DIM = OFF_DT + N_HEADS_S
N_EGROUPS = 4
EXPERTS_PER_GROUP = 8
N_EXPERTS = N_EGROUPS * EXPERTS_PER_GROUP
TOP_K = 2
D_FF = 1024
PAGE = 128

LANES = 128
VMEM_LIMIT = 56 * 1024 * 1024

FRONT = CHUNK
PAD_ROWS = FRONT - N_META
TM = 256
TQ = 256
TKF = 1024
TB = 8
TMOE = 256
TSMALL = 128
TF = 256


def _cparams(sem):
    return pltpu.CompilerParams(dimension_semantics=sem, vmem_limit_bytes=VMEM_LIMIT)


def _resident(shape):
    nd = len(shape)
    return pl.BlockSpec(shape, lambda *_: (0,) * nd, pipeline_mode=pl.Buffered(1))


def _split2(x):
    hi = x.astype(BF16)
    lo = (x - hi.astype(F32)).astype(BF16)
    return hi, lo


def _split3(x):
    hi = x.astype(BF16)
    r = x - hi.astype(F32)
    mid = r.astype(BF16)
    lo = (r - mid.astype(F32)).astype(BF16)
    return hi, mid, lo


def _silu(x):
    return x * (1.0 / (1.0 + jnp.exp(-x)))


def _softplus(x):
    return jnp.maximum(x, 0.0) + jnp.log(1.0 + jnp.exp(-jnp.abs(x)))


HALF = TM // 2


def _token_specs(seq):
    nb = seq // HALF
    return [pl.BlockSpec((HALF, D_MODEL), lambda i: (0, 0)),
            pl.BlockSpec((HALF, D_MODEL), lambda i: (jnp.clip(2 * i - 1, 0, nb - 1), 0)),
            pl.BlockSpec((HALF, D_MODEL), lambda i: (jnp.clip(2 * i, 0, nb - 1), 0)),
            pl.BlockSpec((HALF, D_MODEL), lambda i: (0, 0))]


def _token_rows(front_ref, xa_ref, xb_ref, xs_ref):
    i = pl.program_id(0)
    top = jnp.where(i == 0, front_ref[...], xa_ref[...])
    bot = jnp.where(i == pl.num_programs(0) - 1, xs_ref[...], xb_ref[...])
    return jnp.concatenate([top, bot], axis=0)


def _inproj_kernel(front_ref, xa_ref, xb_ref, xs_ref, nw_ref, wq_ref, wk_ref, wv_ref, wz_ref, wx_ref, wdt_ref,
                   c_ref, s1_ref, s2_ref,
                   q_ref, k_ref, v_ref, kb_ref, vb_ref, z_ref, xbc_ref, dt_ref):
    x = _token_rows(front_ref, xa_ref, xb_ref, xs_ref)
    ms = jnp.mean(x * x, axis=-1, keepdims=True)
    u = (x * lax.rsqrt(ms + EPS) * nw_ref[...]).astype(BF16)
    c, s1, s2 = c_ref[...], s1_ref[...], s2_ref[...]

    def rope(p):
        outs = []
        for h in range(p.shape[1] // LANES):
            xh = p[:, h * LANES:(h + 1) * LANES]
            outs.append(xh * c + pltpu.roll(xh, LANES - ROT_DIM // 2, 1) * s1
                        + pltpu.roll(xh, ROT_DIM // 2, 1) * s2)
        return jnp.concatenate(outs, axis=1)

    q = rope(jnp.dot(u, wq_ref[...], preferred_element_type=F32))
    q_ref[...] = (q * (DK ** -0.5 * LOG2E)).astype(BF16)
    k = rope(jnp.dot(u, wk_ref[...], preferred_element_type=F32))
    k_ref[...] = k
    kb_ref[...] = k.astype(BF16)
    v = jnp.dot(u, wv_ref[...], preferred_element_type=F32)
    v_ref[...] = v
    vb = v.astype(BF16)
    ones = jnp.ones((vb.shape[0], DV), BF16)
    vb_ref[...] = jnp.concatenate(
        [piece for g in range(N_KV_HEADS) for piece in (vb[:, g * DV:(g + 1) * DV], ones)], axis=1)
    z_ref[...] = jnp.dot(u, wz_ref[...], preferred_element_type=F32)
    xbc_ref[...] = jnp.dot(u, wx_ref[...], preferred_element_type=F32)
    dt_ref[...] = jnp.dot(u, wdt_ref[...], preferred_element_type=F32)


def _inproj(front, xp, xs, norm_w, wq, wk, wv, wz, wx, wdt, ctab, s1tab, s2tab):
    R = front.shape[0] + xp.shape[0] + xs.shape[0]
    row = lambda n: pl.BlockSpec((TM, n), lambda i: (i, 0))
    return pl.pallas_call(
        _inproj_kernel,
        grid=(R // TM,),
        in_specs=_token_specs(xp.shape[0]) + [_resident((1, D_MODEL)),
                  _resident(wq.shape), _resident(wk.shape), _resident(wv.shape),
                  _resident(wz.shape), _resident(wx.shape), _resident(wdt.shape),
                  row(LANES), row(LANES), row(LANES)],
        out_specs=[row(Q_DIM), row(K_DIM), row(V_DIM), row(K_DIM), row(2 * V_DIM),
                   row(D_INNER), row(XBC_DIM), row(LANES)],
        out_shape=[jax.ShapeDtypeStruct((R, Q_DIM), BF16),
                   jax.ShapeDtypeStruct((R, K_DIM), F32),
                   jax.ShapeDtypeStruct((R, V_DIM), F32),
                   jax.ShapeDtypeStruct((R, K_DIM), BF16),
                   jax.ShapeDtypeStruct((R, 2 * V_DIM), BF16),
                   jax.ShapeDtypeStruct((R, D_INNER), F32),
                   jax.ShapeDtypeStruct((R, XBC_DIM), F32),
                   jax.ShapeDtypeStruct((R, LANES), F32)],
        compiler_params=_cparams(("arbitrary",)),
        name="inproj",
    )(front, xp, xp, xs, norm_w, wq, wk, wv, wz, wx, wdt, ctab, s1tab, s2tab)


def _diff_lambda(lq1, lk1, lq2, lk2, lam_init):
    a = jnp.sum(lq1[...] * lk1[...], axis=-1, keepdims=True)
    b = jnp.sum(lq2[...] * lk2[...], axis=-1, keepdims=True)
    return jnp.exp(a) - jnp.exp(b) + lam_init


def _subln(o, sub_w, lam_init):
    ms = jnp.mean(o * o, axis=-1, keepdims=True)
    return (o * lax.rsqrt(ms + EPS) * sub_w) * (1.0 - lam_init)


def _attn_kernel(lq1, lk1, lq2, lk2, sub_ref, q_ref, k_ref, ve_ref, o_ref,
                 qs_scr, m_scr, acc_scr, *, lam_init, n_rows):
    qi = pl.program_id(1)
    lam = _diff_lambda(lq1, lk1, lq2, lk2, lam_init)

    q = q_ref[...]
    lane = lax.broadcasted_iota(I32, (TQ, LANES), 1)
    first = lane < DK
    zero = jnp.zeros((TQ, LANES), BF16)
    for hh in range(2):
        qh = q[:, hh * LANES:(hh + 1) * LANES]
        qs_scr[(2 * hh) * TQ:(2 * hh + 1) * TQ, :] = jnp.where(first, qh, zero)
        qs_scr[(2 * hh + 1) * TQ:(2 * hh + 2) * TQ, :] = jnp.where(first, zero, qh)
    m_scr[...] = jnp.full(m_scr.shape, NEG_INF, F32)
    acc_scr[...] = jnp.zeros(acc_scr.shape, F32)

    nt = (((1,), (1,)), ((), ()))

    def process(start, width, keep):
        for part in range(4):
            rows = slice(part * TQ, (part + 1) * TQ)
            s = lax.dot_general(qs_scr[rows, :], k_ref[pl.ds(start, width), :], nt,
                                preferred_element_type=F32)
            if keep is not None:
                s = jnp.where(keep, s, NEG_INF)
            m_prev = m_scr[rows, :]
            m_new = jnp.maximum(m_prev, jnp.max(s, axis=-1, keepdims=True))
            alpha = jnp.exp2(m_prev - m_new)
            p = jnp.concatenate([jnp.exp2(s[:, t * LANES:(t + 1) * LANES] - m_new)
                                 for t in range(width // LANES)], axis=1).astype(BF16)
            pv = jnp.dot(p, ve_ref[pl.ds(start, width), :], preferred_element_type=F32)
            acc_scr[rows, :LANES] = alpha * acc_scr[rows, :LANES] + pv[:, :LANES]
            acc_scr[rows, LANES:] = alpha * acc_scr[rows, LANES:] + pv[:, LANES:]
            m_scr[rows, :] = m_new

    q_lo = qi * TQ
    n_full = jnp.maximum(q_lo - PAD_ROWS, 0) // TKF
    tail_lo = PAD_ROWS + n_full * TKF
    n_tail = (q_lo + TQ - tail_lo + TQ - 1) // TQ

    def full(c):
        process(pl.multiple_of(PAD_ROWS + c * TKF, 16), TKF, None)

    def full_pair(c2, carry):
        full(2 * c2)
        full(2 * c2 + 1)
        return carry

    lax.fori_loop(0, n_full // 2, full_pair, 0)

    @pl.when(n_full % 2 == 1)
    def _():
        full(n_full - 1)

    def tail(w):
        width = w * TQ

        def run():
            start = pl.multiple_of(q_lo + TQ - width, 16)
            col = lax.broadcasted_iota(I32, (TQ, width), 1)
            qpos = q_lo + lax.broadcasted_iota(I32, (TQ, width), 0)
            process(start, width, (col <= qpos - start) & (col >= tail_lo - start))

        return run

    for w in range(1, TKF // TQ + 2):
        pl.when(n_tail == w)(tail(w))

    acc = acc_scr[...]
    o = acc[:, :LANES] / acc[:, LANES:]
    sub_w = sub_ref[...]
    for hh in range(2):
        o1 = o[(2 * hh) * TQ:(2 * hh + 1) * TQ, :]
        o2 = o[(2 * hh + 1) * TQ:(2 * hh + 2) * TQ, :]
        o_ref[:, hh * LANES:(hh + 1) * LANES] = _subln(o1 - lam * o2, sub_w, lam_init).astype(BF16)


def _expand_heads(x, e_ref):
    hi, lo = _split2(x)
    e = e_ref[...]
    return (jnp.dot(hi, e, preferred_element_type=F32)
            + jnp.dot(lo, e, preferred_element_type=F32))


def _gated_out(y, xs, z, dskip_full, normw):
    y = (y + dskip_full * xs) * _silu(z)
    half = D_INNER // N_BC_GROUPS
    outs = []
    for g in range(N_BC_GROUPS):
        yg = y[:, g * half:(g + 1) * half]
        ms = jnp.mean(yg * yg, axis=-1, keepdims=True)
        outs.append(yg * lax.rsqrt(ms + EPS))
    return jnp.concatenate(outs, axis=1) * normw


def _ssd_kernel(xbc_ref, z_ref, dt_ref, cw_ref, cb_ref, dtb_ref, a_ref, dskip_ref, normw_ref,
                e_ref, et_ref, tri_ref, ssm_ref, state_ref, ext_scr, h_scr):
    c = pl.program_id(0)
    L = CHUNK

    @pl.when(c == 0)
    def _():
        ext_scr[0:8, :] = jnp.zeros((8, XBC_DIM), F32)
        h_scr[...] = jnp.zeros(h_scr.shape, F32)

    x_new = xbc_ref[...]
    ext_scr[8:8 + L, :] = x_new
    conv = cb_ref[...] + cw_ref[CONV_W - 1:CONV_W, :] * x_new
    for j in range(CONV_W - 1):
        sh = CONV_W - 1 - j
        conv = conv + cw_ref[j:j + 1, :] * ext_scr[8 - sh:8 - sh + L, :]
    ext_scr[0:8, :] = x_new[L - 8:L, :]
    xc = _silu(conv)
    xs = xc[:, :D_INNER]

    rowi = lax.broadcasted_iota(I32, (L, LANES), 0)
    coli = lax.broadcasted_iota(I32, (L, LANES), 1)
    valid = (c * L + rowi) >= PAD_ROWS
    dt = jnp.where(valid, _softplus(dt_ref[...] + dtb_ref[...]), 0.0)
    da = dt * a_ref[...]
    tri = tri_ref[...]
    acs = sum(jnp.dot(tri, part, preferred_element_type=F32) for part in _split3(da))
    acs_t = acs.T
    dt_t = dt.T
    causal = coli <= rowi
    last = acs[L - 1:L, :]

    w_state = _expand_heads(dt * jnp.exp(last - acs), e_ref)
    dec_out = _expand_heads(jnp.exp(acs), e_ref)
    dec_rows = jnp.sum(et_ref[...] * jnp.exp(last), axis=-1, keepdims=True)
    xw = xs * w_state
    xs_b = xs.astype(BF16)

    half = D_INNER // N_BC_GROUPS
    hpg = N_HEADS_S // N_BC_GROUPS
    y_parts = []
    for g in range(N_BC_GROUPS):
        bm = xc[:, D_INNER + g * D_STATE:D_INNER + (g + 1) * D_STATE].astype(BF16)
        cm = xc[:, D_INNER + (N_BC_GROUPS + g) * D_STATE:
                D_INNER + (N_BC_GROUPS + g + 1) * D_STATE].astype(BF16)
        cb = lax.dot_general(cm, bm, (((1,), (1,)), ((), ())), preferred_element_type=F32)
        h_prev = h_scr[g * half:(g + 1) * half, :]
        y_off = lax.dot_general(cm, h_prev.astype(BF16), (((1,), (1,)), ((), ())),
                                preferred_element_type=F32)
        y_off = y_off * dec_out[:, g * half:(g + 1) * half]
        yd = []
        for r in range(0, hpg, 2):
            pair = []
            for hh in (r, r + 1):
                h = g * hpg + hh
                seg = jnp.broadcast_to(acs[:, h:h + 1], (L, L)) - acs_t[h:h + 1, :]
                lmat = jnp.where(causal, jnp.exp(jnp.where(causal, seg, 0.0)), 0.0)
                pair.append((cb * lmat * dt_t[h:h + 1, :]).astype(BF16))
            h0 = g * hpg + r
            xp = xs_b[:, h0 * SSM_HEAD_DIM:(h0 + 2) * SSM_HEAD_DIM]
            ya = jnp.dot(pair[0], xp, preferred_element_type=F32)
            yb = jnp.dot(pair[1], xp, preferred_element_type=F32)
            yd.append(jnp.where(coli < SSM_HEAD_DIM, ya, yb))
        y_parts.append(jnp.concatenate(yd, axis=1) + y_off)
        xw_t = xw[:, g * half:(g + 1) * half].T.astype(BF16)
        st = jnp.dot(xw_t, bm, preferred_element_type=F32)
        h_scr[g * half:(g + 1) * half, :] = dec_rows[g * half:(g + 1) * half, :] * h_prev + st
    y = jnp.concatenate(y_parts, axis=1)

    ssm_ref[...] = _gated_out(y, xs, z_ref[...], dskip_ref[...], normw_ref[...]).astype(BF16)

    @pl.when(c == pl.num_programs(0) - 1)
    def _():
        state_ref[...] = h_scr[...]


def _prompt_ssd(xbc, z, dt_raw, conv_w, conv_b, dtb, a_row, dskip_full, normw, e_mat, e_t, tri, n_t):
    row = lambda n: pl.BlockSpec((CHUNK, n), lambda c: (c, 0))
    const = lambda shp: pl.BlockSpec(shp, lambda c: (0, 0))
    return pl.pallas_call(
        _ssd_kernel,
        grid=(n_t // CHUNK,),
        in_specs=[row(XBC_DIM), row(D_INNER), row(LANES),
                  const((CONV_W, XBC_DIM)), const((1, XBC_DIM)), const((1, LANES)), const((1, LANES)),
                  const((1, D_INNER)), const((1, D_INNER)), const((LANES, D_INNER)),
                  const((D_INNER, LANES)), const((CHUNK, CHUNK))],
        out_specs=[row(D_INNER), pl.BlockSpec((D_INNER, D_STATE), lambda c: (0, 0))],
        out_shape=[jax.ShapeDtypeStruct((n_t, D_INNER), BF16),
                   jax.ShapeDtypeStruct((D_INNER, D_STATE), F32)],
        scratch_shapes=[pltpu.VMEM((8 + CHUNK, XBC_DIM), F32),
                        pltpu.VMEM((D_INNER, D_STATE), F32)],
        compiler_params=_cparams(("arbitrary",)),
        name="prompt_ssd",
    )(xbc, z, dt_raw, conv_w, conv_b, dtb, a_row, dskip_full, normw, e_mat, e_t, tri)


def _decode_attn_kernel(pt_ref, lq1, lk1, lq2, lk2, sub_ref, q_ref, kn_ref, vn_ref, *rest,
                        n_pages, lam_init):
    k_pages = rest[:n_pages]
    v_pages = rest[n_pages:2 * n_pages]
    o_ref = rest[2 * n_pages]
    del pt_ref
    lam = _diff_lambda(lq1, lk1, lq2, lk2, lam_init)
    nq = 2 * N_HEADS_A
    q8 = q_ref[...]
    lane = lax.broadcasted_iota(I32, (N_HEADS_A, LANES), 1)
    zero = jnp.zeros((N_HEADS_A, LANES), BF16)
    q16 = jnp.concatenate([jnp.where(lane < DK, q8, zero), jnp.where(lane < DK, zero, q8)], axis=0)

    pw = PAGE * N_KV_HEADS
    row_head = (lax.broadcasted_iota(I32, (nq, pw), 0) & (N_HEADS_A - 1)) >> 1
    col_head = lax.broadcasted_iota(I32, (nq, pw), 1) & (N_KV_HEADS - 1)
    own = row_head == col_head
    nt = (((1,), (1,)), ((), ()))
    s_pages = []
    for j in range(n_pages):
        kp = k_pages[j][...].astype(BF16)
        s = lax.dot_general(q16, kp, nt, preferred_element_type=F32)
        s_pages.append(jnp.where(own, s, NEG_INF))
    kn = kn_ref[...].astype(BF16)
    row_head_n = (lax.broadcasted_iota(I32, (nq, LANES), 0) & (N_HEADS_A - 1)) >> 1
    col_n = lax.broadcasted_iota(I32, (nq, LANES), 1)
    s_new = jnp.where(row_head_n == col_n, lax.dot_general(q16, kn, nt, preferred_element_type=F32), NEG_INF)

    m = jnp.max(s_new, axis=-1, keepdims=True)
    for s in s_pages:
        m = jnp.maximum(m, jnp.max(s, axis=-1, keepdims=True))
    p_new = jnp.exp2(s_new - m)
    l = jnp.sum(p_new, axis=-1, keepdims=True)
    acc = jnp.dot(p_new.astype(BF16), vn_ref[...].astype(BF16), preferred_element_type=F32)
    for j in range(n_pages):
        p = jnp.exp2(s_pages[j] - m)
        l = l + jnp.sum(p, axis=-1, keepdims=True)
        acc = acc + jnp.dot(p.astype(BF16), v_pages[j][...].astype(BF16), preferred_element_type=F32)
    o = acc / l
    res = o[:N_HEADS_A, :] - lam * o[N_HEADS_A:, :]
    o_ref[...] = _subln(res, sub_ref[...], lam_init).astype(BF16)


def _fused_attn_kernel(pt_ref, lq1, lk1, lq2, lk2, sub_ref, q_ref, k_ref, ve_ref, qd_ref, kn_ref, vn_ref, *rest,
                       n_pages, n_dec, lam_init, n_rows):
    pages = rest[:2 * n_pages]
    o_ref, od_ref = rest[2 * n_pages], rest[2 * n_pages + 1]
    scratch = rest[2 * n_pages + 2:]

    @pl.when(pl.program_id(1) < n_rows // TQ)
    def _():
        _attn_kernel(lq1, lk1, lq2, lk2, sub_ref, q_ref, k_ref, ve_ref, o_ref, *scratch,
                     lam_init=lam_init, n_rows=n_rows)

    step = pl.program_id(0) * pl.num_programs(1) + pl.program_id(1)

    @pl.when(step < n_dec)
    def _():
        _decode_attn_kernel(pt_ref, lq1, lk1, lq2, lk2, sub_ref, qd_ref, kn_ref, vn_ref, *pages, od_ref,
                            n_pages=n_pages, lam_init=lam_init)


def _attention(page_table, q, kb, vbe, q_s, k_new, v_new, cache_k2, cache_v2, lq1, lk1, lq2, lk2, sub_w,
               lam_init):
    R = q.shape[0]
    nq = R // TQ
    B, n_pages = page_table.shape
    assert R % TQ == 0 and TKF % TQ == 0
    n_inner = max(nq, pl.cdiv(B, N_KV_HEADS))
    pw = PAGE * N_KV_HEADS
    seq_of = lambda g, i: jnp.minimum(g * n_inner + i, B - 1)
    qblk = lambda i: jnp.minimum(i, nq - 1)
    small = lambda n: pl.BlockSpec((1, n), lambda g, i, pt: (0, 0))
    per_b = pl.BlockSpec((None, 8, LANES), lambda g, i, pt: (seq_of(g, i), 0, 0))
    new_tok = pl.BlockSpec((None, LANES, LANES), lambda g, i, pt: (seq_of(g, i), 0, 0))
    page_spec = lambda j: pl.BlockSpec((pw, LANES), lambda g, i, pt: (pt[seq_of(g, i), j], 0))
    grid_spec = pltpu.PrefetchScalarGridSpec(
        num_scalar_prefetch=1,
        grid=(N_KV_HEADS, n_inner),
        in_specs=[small(DK), small(DK), small(DK), small(DK), small(DV),
                  pl.BlockSpec((TQ, 2 * LANES), lambda g, i, pt: (qblk(i), g)),
                  pl.BlockSpec((R, LANES), lambda g, i, pt: (0, g)),
                  pl.BlockSpec((R, 2 * LANES), lambda g, i, pt: (0, g)),
                  per_b, new_tok, new_tok]
                 + [page_spec(j) for j in range(n_pages)] + [page_spec(j) for j in range(n_pages)],
        out_specs=[pl.BlockSpec((TQ, 2 * LANES), lambda g, i, pt: (qblk(i), g)), per_b],
        scratch_shapes=[pltpu.VMEM((4 * TQ, LANES), BF16),
                        pltpu.VMEM((4 * TQ, LANES), F32),
                        pltpu.VMEM((4 * TQ, 2 * LANES), F32)],
    )
    return pl.pallas_call(
        functools.partial(_fused_attn_kernel, n_pages=n_pages, n_dec=B, lam_init=lam_init, n_rows=R),
        grid_spec=grid_spec,
        out_shape=[jax.ShapeDtypeStruct((R, N_HEADS_A * DV), BF16),
                   jax.ShapeDtypeStruct((B, 8, LANES), BF16)],
        compiler_params=_cparams(("arbitrary", "arbitrary")),
        name="attention",
    )(page_table, lq1, lk1, lq2, lk2, sub_w, q, kb, vbe, q_s, k_new, v_new,
      *([cache_k2] * n_pages), *([cache_v2] * n_pages))


def _dec_pre_kernel(xbc_ref, s0_ref, s1_ref, s2_ref, dt_ref, cw_ref, cb_ref, dtb_ref, a_ref, e_ref,
                    xc_ref, xdt_ref, dect_ref):
    conv = (cb_ref[...] + cw_ref[0:1, :] * s0_ref[...] + cw_ref[1:2, :] * s1_ref[...]
            + cw_ref[2:3, :] * s2_ref[...] + cw_ref[3:4, :] * xbc_ref[...])
    xc = _silu(conv)
    xc_ref[...] = xc
    dt = _softplus(dt_ref[...] + dtb_ref[...])
    dec = jnp.exp(dt * a_ref[...])
    hi, mid, lo = _split3(dt)
    e = e_ref[...]
    dt_full = (jnp.dot(hi, e, preferred_element_type=F32) + jnp.dot(mid, e, preferred_element_type=F32)
               + jnp.dot(lo, e, preferred_element_type=F32))
    hi, mid, lo = _split3(dec)
    dec_full = (jnp.dot(hi, e, preferred_element_type=F32) + jnp.dot(mid, e, preferred_element_type=F32)
                + jnp.dot(lo, e, preferred_element_type=F32))
    xdt_ref[...] = (dt_full * xc[:, :D_INNER]).T
    dect_ref[...] = dec_full.T


def _dec_state_kernel(h_ref, xdt_ref, dect_ref, bm_ref, cm_ref, hout_ref, y_ref):
    i = pl.program_id(0)
    B = xdt_ref.shape[1]
    lane = lax.broadcasted_iota(I32, (D_INNER, B), 1)
    sub = lax.broadcasted_iota(I32, (TB, D_INNER // N_BC_GROUPS), 0)
    ones = jnp.ones((B, D_STATE), BF16)
    half = D_INNER // N_BC_GROUPS
    nt = (((1,), (1,)), ((), ()))
    y_acc = [jnp.zeros((TB, half), F32) for _ in range(N_BC_GROUPS)]
    for j in range(TB):
        b = i * TB + j
        sel = lane == b
        xcol = jnp.where(sel, xdt_ref[...], 0.0)
        dcol = jnp.where(sel, dect_ref[...], 0.0)
        dec = sum(jnp.dot(part, ones, preferred_element_type=F32) for part in _split3(dcol))
        xhi, xlo = _split2(xcol)
        hnew = dec * h_ref[j]
        for g in range(N_BC_GROUPS):
            bmat = bm_ref[:, g * D_STATE:(g + 1) * D_STATE].astype(BF16)
            upd = (jnp.dot(xhi[g * half:(g + 1) * half], bmat, preferred_element_type=F32)
                   + jnp.dot(xlo[g * half:(g + 1) * half], bmat, preferred_element_type=F32))
            hg = hnew[g * half:(g + 1) * half] + upd
            hout_ref[j, g * half:(g + 1) * half, :] = hg
            start = pl.multiple_of(i * TB, TB)
            c8 = cm_ref[pl.ds(start, TB), g * D_STATE:(g + 1) * D_STATE].astype(BF16)
            yj = lax.dot_general(c8, hg.astype(BF16), nt, preferred_element_type=F32)
            y_acc[g] = y_acc[g] + jnp.where(sub == j, yj, 0.0)
    y_ref[...] = jnp.concatenate(y_acc, axis=1)


def _dec_post_kernel(y_ref, xc_ref, z_ref, dskip_ref, normw_ref, o_ref):
    o_ref[...] = _gated_out(y_ref[...], xc_ref[:, :D_INNER], z_ref[...], dskip_ref[...],
                            normw_ref[...]).astype(BF16)


def _decode_ssm(xbc_s, sc0, sc1, sc2, dt_s, z_s, state, conv_w, conv_b, dtb, a_row, dskip_full, normw, e_mat):
    B = xbc_s.shape[0]
    xc, xdt_t, dec_t = pl.pallas_call(
        _dec_pre_kernel,
        out_shape=[jax.ShapeDtypeStruct((B, XBC_DIM), F32),
                   jax.ShapeDtypeStruct((D_INNER, B), F32),
                   jax.ShapeDtypeStruct((D_INNER, B), F32)],
        compiler_params=pltpu.CompilerParams(vmem_limit_bytes=VMEM_LIMIT),
        name="decode_ssm_pre",
    )(xbc_s, sc0, sc1, sc2, dt_s, conv_w, conv_b, dtb, a_row, e_mat)
    bm = xc[:, D_INNER:D_INNER + N_BC_GROUPS * D_STATE]
    cm = xc[:, D_INNER + N_BC_GROUPS * D_STATE:]
    const = lambda shp: pl.BlockSpec(shp, lambda i: (0,) * len(shp))
    state_spec = pl.BlockSpec((TB, D_INNER, D_STATE), lambda i: (i, 0, 0))
    h_new, y = pl.pallas_call(
        _dec_state_kernel,
        grid=(B // TB,),
        in_specs=[state_spec, const((D_INNER, B)), const((D_INNER, B)),
                  const((B, N_BC_GROUPS * D_STATE)), const((B, N_BC_GROUPS * D_STATE))],
        out_specs=[state_spec, pl.BlockSpec((TB, D_INNER), lambda i: (i, 0))],
        out_shape=[jax.ShapeDtypeStruct((B, D_INNER, D_STATE), F32),
                   jax.ShapeDtypeStruct((B, D_INNER), F32)],
        compiler_params=_cparams(("arbitrary",)),
        name="decode_ssm_state",
    )(state, xdt_t, dec_t, bm, cm)
    ssm = pl.pallas_call(
        _dec_post_kernel,
        out_shape=jax.ShapeDtypeStruct((B, D_INNER), BF16),
        compiler_params=pltpu.CompilerParams(vmem_limit_bytes=VMEM_LIMIT),
        name="decode_ssm_post",
    )(y, xc, z_s, dskip_full, normw)
    return ssm, h_new


def _outproj_kernel(front_ref, xa_ref, xb_ref, xs_ref, att_ref, sa_ref, sb_ref, ss_ref,
                    wa_ref, ws_ref, nw_ref, wrh_ref, wrl_ref, h_ref, u_ref, lg_ref):
    last = pl.program_id(0) == pl.num_programs(0) - 1
    ssm = jnp.concatenate([sa_ref[...], jnp.where(last, ss_ref[...], sb_ref[...])], axis=0)
    acc = jnp.dot(att_ref[...], wa_ref[...], preferred_element_type=F32)
    acc = acc + jnp.dot(ssm, ws_ref[...], preferred_element_type=F32)
    h = _token_rows(front_ref, xa_ref, xb_ref, xs_ref) + acc
    h_ref[...] = h
    ms = jnp.mean(h * h, axis=-1, keepdims=True)
    u = h * lax.rsqrt(ms + EPS) * nw_ref[...]
    u_ref[...] = u
    uh, ul = _split2(u)
    wh, wl = wrh_ref[...], wrl_ref[...]
    lg = (jnp.dot(uh, wh, preferred_element_type=F32) + jnp.dot(uh, wl, preferred_element_type=F32)
          + jnp.dot(ul, wh, preferred_element_type=F32))
    lg_ref[...] = _route_rows(lg, pl.program_id(0) * TM)


def _first_max(x, lane):
    m = jnp.max(x, axis=-1, keepdims=True)
    return m, jnp.min(jnp.where(x == m, lane, float(LANES)), axis=-1, keepdims=True)


def _route_rows(lg, row0):
    rows = lg.shape[0]
    lane_i = lax.broadcasted_iota(I32, (rows, LANES), 1)
    lane = lane_i.astype(F32)
    gl = jnp.where(lane_i < N_EGROUPS, lg, -jnp.inf)
    gmax, g = _first_max(gl, lane)
    pg = 1.0 / jnp.sum(jnp.exp(gl - gmax), axis=-1, keepdims=True)
    lo = N_EGROUPS + g * EXPERTS_PER_GROUP
    el = jnp.where((lane >= lo) & (lane < lo + EXPERTS_PER_GROUP), lg, -jnp.inf)
    m1, i1 = _first_max(el, lane)
    m2, i2 = _first_max(jnp.where(lane == i1, -jnp.inf, el), lane)
    z = jnp.sum(jnp.exp(el - m1), axis=-1, keepdims=True)
    p1 = 1.0 / z
    p2 = jnp.exp(m2 - m1) / z
    tot = p1 + p2
    valid = (row0 + lax.broadcasted_iota(I32, (rows, 1), 0)) >= PAD_ROWS
    e1 = jnp.where(valid, i1 - N_EGROUPS, -1.0)
    e2 = jnp.where(valid, i2 - N_EGROUPS, -1.0)
    w1 = jnp.where(valid, p1 / tot * pg, 0.0)
    w2 = jnp.where(valid, p2 / tot * pg, 0.0)
    return jnp.where(lane_i == 0, e1, jnp.where(lane_i == 1, e2, jnp.where(lane_i == 2, w1,
                                                                         jnp.where(lane_i == 3, w2, 0.0))))


def _rank_kernel(route_ref, tri_ref, utri_ref, rank_ref, meta_ref, carry, *, nblk):
    i = pl.program_id(0)

    @pl.when(i == 0)
    def _():
        carry[...] = jnp.zeros(carry.shape, F32)

    r = route_ref[...]
    lane = lax.broadcasted_iota(I32, r.shape, 1).astype(F32)
    oh0 = (lane == r[:, 0:1]).astype(F32)
    oh1 = (lane == r[:, 1:2]).astype(F32)
    oh = oh0 + oh1
    before = jnp.dot(tri_ref[...], oh.astype(BF16), preferred_element_type=F32) + carry[0:1, :]
    rank0 = jnp.sum(before * oh0, axis=-1, keepdims=True)
    rank1 = jnp.sum(before * oh1, axis=-1, keepdims=True)
    lane_i = lax.broadcasted_iota(I32, r.shape, 1)
    rank_ref[...] = jnp.where(lane_i == 0, rank0, jnp.where(lane_i == 1, rank1, 0.0))
    carry[0:1, :] = carry[0:1, :] + jnp.sum(oh, axis=0, keepdims=True)

    @pl.when(i == pl.num_programs(0) - 1)
    def _():
        nb_pad = meta_ref.shape[0] - 8
        counts = carry[0:1, :]
        pb = jnp.floor((counts + (TMOE - 1)) / TMOE)
        ends = jnp.dot(jnp.broadcast_to(pb, (8, LANES)).astype(BF16), utri_ref[...],
                       preferred_element_type=F32)[0:1, :]
        starts = ends - pb
        lane1 = lax.broadcasted_iota(I32, (1, LANES), 1)
        n_valid = jnp.sum(jnp.where(lane1 == N_EXPERTS - 1, ends, 0.0), axis=-1, keepdims=True)
        blk = lax.broadcasted_iota(I32, (nb_pad, LANES), 0).astype(F32)
        lane_b = lax.broadcasted_iota(I32, (nb_pad, LANES), 1)
        is_e = lane_b < N_EXPERTS
        block_e = jnp.minimum(jnp.sum(jnp.where(is_e & (ends <= blk), 1.0, 0.0), axis=-1, keepdims=True),
                              N_EXPERTS - 1.0)
        lane_bf = lane_b.astype(F32)
        has = is_e & (counts > 0.0) & (lane_bf > block_e)
        nxt = jnp.min(jnp.where(has, lane_bf, float(LANES)), axis=-1, keepdims=True)
        next_e = jnp.where(nxt >= LANES, -1.0, nxt)
        mine = lane_bf == block_e
        cnt_b = jnp.sum(jnp.where(mine, counts, 0.0), axis=-1, keepdims=True)
        st_b = jnp.sum(jnp.where(mine, starts, 0.0), axis=-1, keepdims=True)
        n_real = jnp.clip(cnt_b - (blk[:, 0:1] - st_b) * TMOE, 0.0, float(TMOE))
        meta_ref[0:8, :] = jnp.where(lax.broadcasted_iota(I32, (8, LANES), 0) == 0, starts,
                                     jnp.broadcast_to(n_valid, (8, LANES)))
        meta_ref[8:, :] = jnp.where(lane_b == 0, block_e, jnp.where(lane_b == 1, next_e,
                                                                    jnp.where(lane_b == 2, n_real, 0.0)))


def _rank(route, nblk):
    R = route.shape[0]
    nb_pad = (nblk + 7) // 8 * 8
    tri = (jnp.arange(TM)[:, None] > jnp.arange(TM)[None, :]).astype(BF16)
    utri = (jnp.arange(LANES)[:, None] <= jnp.arange(LANES)[None, :]).astype(BF16)
    return pl.pallas_call(
        functools.partial(_rank_kernel, nblk=nblk),
        grid=(R // TM,),
        in_specs=[pl.BlockSpec((TM, LANES), lambda i: (i, 0)),
                  pl.BlockSpec((TM, TM), lambda i: (0, 0)),
                  pl.BlockSpec((LANES, LANES), lambda i: (0, 0))],
        out_specs=[pl.BlockSpec((TM, LANES), lambda i: (i, 0)),
                   pl.BlockSpec((8 + nb_pad, LANES), lambda i: (0, 0))],
        out_shape=[jax.ShapeDtypeStruct((R, LANES), F32),
                   jax.ShapeDtypeStruct((8 + nb_pad, LANES), F32)],
        scratch_shapes=[pltpu.VMEM((8, LANES), F32)],
        compiler_params=_cparams(("arbitrary",)),
        name="expert_rank",
    )(route, tri, utri)


def _outproj(front, xp, xs, att, ssm_p, ssm_s, wa, ws, norm_w, wr_hi, wr_lo):
    R = front.shape[0] + xp.shape[0] + xs.shape[0]
    row = lambda n: pl.BlockSpec((TM, n), lambda i: (i, 0))
    nbp = ssm_p.shape[0] // HALF
    half = lambda f: pl.BlockSpec((HALF, D_INNER), f)
    return pl.pallas_call(
        _outproj_kernel,
        grid=(R // TM,),
        in_specs=_token_specs(xp.shape[0]) + [
            row(N_HEADS_A * DV),
            half(lambda i: (2 * i, 0)), half(lambda i: (jnp.minimum(2 * i + 1, nbp - 1), 0)),
            half(lambda i: (0, 0)),
            _resident(wa.shape), _resident(ws.shape), _resident((1, D_MODEL)),
            _resident(wr_hi.shape), _resident(wr_lo.shape)],
        out_specs=[row(D_MODEL), row(D_MODEL), row(LANES)],
        out_shape=[jax.ShapeDtypeStruct((R, D_MODEL), F32),
                   jax.ShapeDtypeStruct((R, D_MODEL), F32),
                   jax.ShapeDtypeStruct((R, LANES), F32)],
        compiler_params=_cparams(("arbitrary",)),
        name="outproj",
    )(front, xp, xp, xs, att, ssm_p, ssm_p, ssm_s, wa, ws, norm_w, wr_hi, wr_lo)


GATHER_UNROLL = 8

def _start_row_gather(idx_ref, src_hbm, dst, sem, n_groups=None):
    def start(r8, carry):
        for j in range(GATHER_UNROLL):
            r = r8 * GATHER_UNROLL + j
            pltpu.make_async_copy(src_hbm.at[pl.ds(idx_ref[0, r], 1)], dst.at[pl.ds(r, 1)],
                                  sem).start(priority=j % 2)
        return carry

    lax.fori_loop(0, dst.shape[0] // GATHER_UNROLL if n_groups is None else n_groups, start, 0)


def _wait_row_gather(src_hbm, dst, sem, n_groups=None):
    if n_groups is None:
        pltpu.make_async_copy(src_hbm.at[pl.ds(0, dst.shape[0])], dst, sem).wait()
        return

    def wait(g, carry):
        pltpu.make_async_copy(src_hbm.at[pl.ds(0, GATHER_UNROLL)], dst.at[pl.ds(0, GATHER_UNROLL)], sem).wait()
        return carry

    lax.fori_loop(0, n_groups, wait, 0)


def _row_groups(n_real):
    return (n_real + (GATHER_UNROLL - 1)) // GATHER_UNROLL


def _moe_kernel(be_ref, nx_ref, nr_ref, nv_ref, idx_ref, idx_next_ref, x_hbm, wg_hbm, wu_hbm, wd_hbm, o_ref,
                xbuf, stage_g, stage_u, stage_d, wg_b, wu_b, wd_b, sems, row_sems):
    i = pl.program_id(0)
    valid = i < nv_ref[0]
    e = be_ref[i]
    changed = (i == 0) | (be_ref[jnp.maximum(i - 1, 0)] != e)
    slot = i % 2
    n_real = nr_ref[i]

    @pl.when(i == 0)
    def _():
        xbuf[...] = jnp.zeros(xbuf.shape, F32)
        _start_row_gather(idx_ref, x_hbm, xbuf.at[0], row_sems.at[0], _row_groups(n_real))

    @pl.when(i + 1 < nv_ref[0])
    def _():
        _start_row_gather(idx_next_ref, x_hbm, xbuf.at[1 - slot], row_sems.at[1 - slot],
                          _row_groups(nr_ref[jnp.minimum(i + 1, pl.num_programs(0) - 1)]))

    def fetch(expert):
        return (pltpu.make_async_copy(wg_hbm.at[expert], stage_g, sems.at[0]),
                pltpu.make_async_copy(wu_hbm.at[expert], stage_u, sems.at[1]),
                pltpu.make_async_copy(wd_hbm.at[expert], stage_d, sems.at[2]))

    @pl.when(i == 0)
    def _():
        for c in fetch(e):
            c.start()

    @pl.when(valid & changed)
    def _():
        for c in fetch(e):
            c.wait()
        wg_b[...] = stage_g[...].astype(BF16)
        wu_b[...] = stage_u[...].astype(BF16)
        wd_b[...] = stage_d[...].astype(BF16)
        nxt = nx_ref[i]

        @pl.when(nxt >= 0)
        def _():
            for c in fetch(nxt):
                c.start()

    def mlp(rows):
        x = xbuf[slot, :rows, :].astype(BF16)
        g = jnp.dot(x, wg_b[...], preferred_element_type=F32)
        u = jnp.dot(x, wu_b[...], preferred_element_type=F32)
        h = (_silu(g) * u).astype(BF16)
        o_ref[:rows, :] = jnp.dot(h, wd_b[...], preferred_element_type=F32)

    @pl.when(valid)
    def _():
        _wait_row_gather(x_hbm, xbuf.at[slot], row_sems.at[slot], _row_groups(n_real))

    @pl.when(valid & (n_real > TSMALL))
    def _():
        mlp(TMOE)

    @pl.when(valid & (n_real <= TSMALL))
    def _():
        mlp(TSMALL)
        o_ref[TSMALL:, :] = jnp.zeros((TMOE - TSMALL, D_MODEL), F32)

    @pl.when(jnp.logical_not(valid))
    def _():
        o_ref[...] = jnp.zeros(o_ref.shape, F32)


def _moe(block_e, next_e, n_real, n_valid, buf_tok, x_tok, w_gate, w_up, w_down):
    rows = buf_tok.shape[0]
    nblk = rows // TMOE

    def blk(i, nv):
        return jnp.minimum(i, nv[0] - 1)

    grid_spec = pltpu.PrefetchScalarGridSpec(
        num_scalar_prefetch=4,
        grid=(nblk,),
        in_specs=[pl.BlockSpec((None, 1, TMOE), lambda i, be, nx, nr, nv: (blk(i, nv), 0, 0),
                               memory_space=pltpu.SMEM),
                  pl.BlockSpec((None, 1, TMOE), lambda i, be, nx, nr, nv: (blk(i + 1, nv), 0, 0),
                               memory_space=pltpu.SMEM),
                  pl.BlockSpec(memory_space=pl.ANY), pl.BlockSpec(memory_space=pl.ANY),
                  pl.BlockSpec(memory_space=pl.ANY), pl.BlockSpec(memory_space=pl.ANY)],
        out_specs=pl.BlockSpec((TMOE, D_MODEL), lambda i, be, nx, nr, nv: (i, 0)),
        scratch_shapes=[pltpu.VMEM((2, TMOE, D_MODEL), F32),
                        pltpu.VMEM((D_MODEL, D_FF), F32), pltpu.VMEM((D_MODEL, D_FF), F32),
                        pltpu.VMEM((D_FF, D_MODEL), F32),
                        pltpu.VMEM((D_MODEL, D_FF), BF16), pltpu.VMEM((D_MODEL, D_FF), BF16),
                        pltpu.VMEM((D_FF, D_MODEL), BF16),
                        pltpu.SemaphoreType.DMA((3,)), pltpu.SemaphoreType.DMA((2,))],
    )
    slot_rows = buf_tok.reshape(nblk, 1, TMOE)
    return pl.pallas_call(
        _moe_kernel,
        grid_spec=grid_spec,
        out_shape=jax.ShapeDtypeStruct((rows, D_MODEL), F32),
        compiler_params=_cparams(("arbitrary",)),
        name="moe_experts",
    )(block_e, next_e, n_real, n_valid, slot_rows, slot_rows, x_tok, w_gate, w_up, w_down)


def _final_kernel(idx_ref, idx_next_ref, h_ref, w_ref, nw_ref, y_hbm, op_ref, os_ref, ybuf, row_sems):
    i = pl.program_id(0)
    last = pl.num_programs(0) - 1
    slot = i % 2

    @pl.when(i == 0)
    def _():
        _start_row_gather(idx_ref, y_hbm, ybuf.at[0], row_sems.at[0])

    @pl.when(i + 1 < pl.num_programs(0))
    def _():
        _start_row_gather(idx_next_ref, y_hbm, ybuf.at[1 - slot], row_sems.at[1 - slot])

    _wait_row_gather(y_hbm, ybuf.at[slot], row_sems.at[slot])
    w = w_ref[...]
    moe = ybuf[slot, :HALF, :] * w[:, 0:1] + ybuf[slot, HALF:, :] * w[:, 1:2]
    h = h_ref[...] + moe
    ms = jnp.mean(h * h, axis=-1, keepdims=True)
    out = h * lax.rsqrt(ms + EPS) * nw_ref[...]

    @pl.when(i < last)
    def _():
        op_ref[...] = out

    @pl.when(i == last)
    def _():
        os_ref[...] = out


def _final(h1, yb, dest, wts, norm_w, seq):
    R = h1.shape[0]
    n = R // HALF
    nbp = seq // HALF
    assert R == (1 + nbp + 1) * HALF
    row = lambda w: pl.BlockSpec((HALF, w), lambda i: (i, 0))
    slots = dest.reshape(n, HALF, TOP_K).transpose(0, 2, 1).reshape(n, 1, TOP_K * HALF)
    idx_spec = lambda off: pl.BlockSpec((None, 1, TOP_K * HALF), lambda i: (jnp.minimum(i + off, n - 1), 0, 0),
                                        memory_space=pltpu.SMEM)
    return pl.pallas_call(
        _final_kernel,
        grid=(n,),
        in_specs=[idx_spec(0), idx_spec(1), row(D_MODEL), row(TOP_K),
                  pl.BlockSpec((1, D_MODEL), lambda i: (0, 0)),
                  pl.BlockSpec(memory_space=pl.ANY)],
        out_specs=[pl.BlockSpec((HALF, D_MODEL), lambda i: (jnp.clip(i - 1, 0, nbp - 1), 0)),
                   pl.BlockSpec((HALF, D_MODEL), lambda i: (0, 0))],
        out_shape=[jax.ShapeDtypeStruct((seq, D_MODEL), F32),
                   jax.ShapeDtypeStruct((HALF, D_MODEL), F32)],
        scratch_shapes=[pltpu.VMEM((2, TOP_K * HALF, D_MODEL), F32), pltpu.SemaphoreType.DMA((2,))],
        compiler_params=_cparams(("arbitrary",)),
        name="final_norm",
    )(slots, slots, h1, wts, norm_w, yb)


def _route(route):
    R = route.shape[0]
    a_max = TOP_K * R
    nblk = (a_max + N_EXPERTS * (TMOE - 1) + TMOE - 1) // TMOE
    rows = nblk * TMOE
    rank, meta = _rank(route, nblk)
    eid = route[:, :TOP_K].astype(I32)
    wts = route[:, TOP_K:2 * TOP_K]
    starts = meta[0, :N_EXPERTS].astype(I32) * TMOE
    n_valid = meta[1, :1].astype(I32)
    block_e = meta[8:8 + nblk, 0].astype(I32)
    next_e = meta[8:8 + nblk, 1].astype(I32)
    n_real = meta[8:8 + nblk, 2].astype(I32)
    is_real = eid >= 0
    dest = jnp.where(is_real, starts[jnp.maximum(eid, 0)] + rank[:, :TOP_K].astype(I32), 0)
    tok = jnp.broadcast_to(jnp.arange(R, dtype=I32)[:, None], (R, TOP_K))
    buf_tok = (jnp.arange(rows, dtype=I32) % PAD_ROWS).at[jnp.where(is_real, dest, rows).reshape(-1)].set(
        tok.reshape(-1), mode="drop")
    return wts, dest, buf_tok, block_e, next_e, n_real, n_valid


def kernel(x_prompt, x_sample, cache_k, cache_v, state_conv, state_ssm, page_table, meta_tokens, norm_mix_w,
           w_in, lambda_q1, lambda_k1, lambda_q2, lambda_k2, subln_w, conv_w, conv_b, dt_bias, a_log, d_skip,
           ssm_norm_w, w_out, norm_ffn_w, w_router_group, w_router_expert, w_gate, w_up, w_down, final_norm_w):
    depth = w_in.shape[0]
    assert depth == 1 and x_prompt.shape[0] == 1 and x_sample.shape[1] == 1
    seq = x_prompt.shape[1]
    B = x_sample.shape[0]
    n_pages = page_table.shape[1]
    n_phys = cache_k.shape[1]
    past = n_pages * cache_k.shape[2]
    assert cache_k.shape[2] == PAGE and seq % TM == 0 and B == HALF and FRONT == HALF
    n_t = FRONT + seq
    R = n_t + B
    l = 0
    lam_init = 0.8 - 0.6 * math.exp(-0.3 * l)

    front = jnp.concatenate([jnp.zeros((PAD_ROWS, D_MODEL), F32), meta_tokens.astype(F32)], axis=0)
    xp, xs = x_prompt[0], x_sample[:, 0]
    w = w_in[l]
    wq = w[:, :OFF_K].astype(BF16)
    wk = w[:, OFF_K:OFF_V].astype(BF16)
    wv = w[:, OFF_V:OFF_Z].astype(BF16)
    wz = w[:, OFF_Z:OFF_XBC].astype(BF16)
    wx = w[:, OFF_XBC:OFF_DT].astype(BF16)
    wdt = jnp.pad(w[:, OFF_DT:], ((0, 0), (0, LANES - N_HEADS_S))).astype(BF16)

    pos = jnp.concatenate([jnp.maximum(jnp.arange(n_t, dtype=I32) - PAD_ROWS, 0),
                           jnp.full((B,), past, I32)])
    inv = ROPE_THETA ** (-jnp.arange(0, ROT_DIM, 2, dtype=F32) / ROT_DIM)
    ang = pos.astype(F32)[:, None] * inv[None, :]
    cos, sin = jnp.cos(ang), jnp.sin(ang)
    half = ROT_DIM // 2
    ones = jnp.ones((R, DK - ROT_DIM), F32)
    zeros = jnp.zeros((R, DK - ROT_DIM), F32)
    zh = jnp.zeros((R, half), F32)
    ctab = jnp.tile(jnp.concatenate([cos, cos, ones], axis=1), (1, 2))
    s1tab = jnp.tile(jnp.concatenate([-sin, zh, zeros], axis=1), (1, 2))
    s2tab = jnp.tile(jnp.concatenate([zh, sin, zeros], axis=1), (1, 2))

    row1 = lambda v: v.reshape(1, -1).astype(F32)
    pad_heads = lambda v: jnp.pad(v.astype(F32), (0, LANES - N_HEADS_S)).reshape(1, LANES)
    a_row = pad_heads(-jnp.exp(a_log[l].astype(F32)))
    dtb = pad_heads(dt_bias[l])
    dskip_full = jnp.repeat(d_skip[l].astype(F32), SSM_HEAD_DIM).reshape(1, D_INNER)
    normw = row1(ssm_norm_w[l])
    head_of_col = jnp.arange(D_INNER, dtype=I32) // SSM_HEAD_DIM
    e_mat = (jnp.arange(LANES, dtype=I32)[:, None] == head_of_col[None, :]).astype(BF16)
    tri = (jnp.arange(CHUNK)[:, None] >= jnp.arange(CHUNK)[None, :]).astype(BF16)
    lq1, lk1, lq2, lk2 = (row1(v[l]) for v in (lambda_q1, lambda_k1, lambda_q2, lambda_k2))
    sub_w = row1(subln_w[l])

    q, k, v, kb, vb, z, xbc, dt_raw = _inproj(front, xp, xs, row1(norm_mix_w[l]), wq, wk, wv, wz, wx, wdt,
                                              ctab, s1tab, s2tab)

    k_s, v_s = k[n_t:], v[n_t:]
    pad8 = lambda a: jnp.pad(a.reshape(B, N_KV_HEADS, LANES), ((0, 0), (0, LANES - N_KV_HEADS), (0, 0)))
    att_all, att_s = _attention(page_table, q, kb, vb, q[n_t:].reshape(B, N_HEADS_A, LANES), pad8(k_s), pad8(v_s),
                                cache_k[l].reshape(n_phys * PAGE * N_KV_HEADS, LANES),
                                cache_v[l].reshape(n_phys * PAGE * N_KV_HEADS, LANES),
                                lq1, lk1, lq2, lk2, sub_w, lam_init)
    ssm_p, state_p = _prompt_ssd(xbc, z, dt_raw, conv_w[l].astype(F32), row1(conv_b[l]), dtb, a_row,
                                 dskip_full, normw, e_mat, e_mat.T.astype(F32), tri, n_t)

    sc = state_conv[l]
    xbc_s = xbc[n_t:]
    ssm_s, state_s = _decode_ssm(xbc_s, sc[:, 0], sc[:, 1], sc[:, 2], dt_raw[n_t:], z[n_t:],
                                 state_ssm[l].reshape(B, D_INNER, D_STATE), conv_w[l].astype(F32),
                                 row1(conv_b[l]), dtb, a_row, dskip_full, normw, e_mat)

    att = lax.dynamic_update_slice(att_all, att_s.reshape(B, N_HEADS_A * DV), (n_t, 0))

    wo = w_out[l].astype(BF16)
    wr = jnp.pad(jnp.concatenate([w_router_group[l], w_router_expert[l]], axis=1).astype(F32),
                 ((0, 0), (0, LANES - N_EGROUPS - N_EXPERTS)))
    wr_hi = wr.astype(BF16)
    wr_lo = (wr - wr_hi.astype(F32)).astype(BF16)
    h1, u2, route = _outproj(front, xp, xs, att, ssm_p, ssm_s, wo[:N_HEADS_A * DV], wo[N_HEADS_A * DV:],
                             row1(norm_ffn_w[l]), wr_hi, wr_lo)

    wts, dest, buf_tok, block_e, next_e, n_real, n_valid = _route(route)
    yb = _moe(block_e, next_e, n_real, n_valid, buf_tok, u2, w_gate[l], w_up[l], w_down[l])
    y_p, y_s = _final(h1, yb, dest, wts, row1(final_norm_w), seq)

    t_real = N_META + seq
    y_prompt = y_p.reshape(1, seq, D_MODEL)
    y_sample = y_s.reshape(B, 1, D_MODEL)
    k_prompt = k[PAD_ROWS:n_t].reshape(1, 1, t_real, N_KV_HEADS, 2 * DK)
    v_prompt = v[PAD_ROWS:n_t].reshape(1, 1, t_real, N_KV_HEADS, DV)
    conv_prompt = xbc[n_t - (CONV_W - 1):n_t].reshape(1, 1, CONV_W - 1, XBC_DIM)
    ssm_prompt = state_p.reshape(1, 1, N_HEADS_S, SSM_HEAD_DIM, D_STATE)
    k_sample = k_s.reshape(1, B, 1, N_KV_HEADS, 2 * DK)
    v_sample = v_s.reshape(1, B, 1, N_KV_HEADS, DV)
    conv_sample = jnp.stack([sc[:, 1], sc[:, 2], xbc_s], axis=1)[None]
    ssm_sample = state_s.reshape(1, B, N_HEADS_S, SSM_HEAD_DIM, D_STATE)
    return (y_prompt, y_sample, k_prompt, v_prompt, conv_prompt, ssm_prompt,
            k_sample, v_sample, conv_sample, ssm_sample)
```

```python
import functools
import math

import jax
import jax.numpy as jnp
from jax import lax
from jax.experimental import pallas as pl
from jax.experimental.pallas import tpu as pltpu

F32 = jnp.float32
BF16 = jnp.bfloat16
I32 = jnp.int32

D_MODEL = 2048
N_META = 16
EPS = 1e-6
DV = 128
DK = 64
N_HEADS_A = 8
N_KV_HEADS = 4
ROT_DIM = 16
ROPE_THETA = 500000.0
NEG_INF = -1e30
LOG2E = 1.4426950408889634
D_INNER = 1024
SSM_HEAD_DIM = 64
N_HEADS_S = 16
N_BC_GROUPS = 2
D_STATE = 128
CONV_W = 4
XBC_DIM = D_INNER + 2 * N_BC_GROUPS * D_STATE
CHUNK = 128
Q_DIM = N_HEADS_A * 2 * DK
K_DIM = N_KV_HEADS * 2 * DK
V_DIM = N_KV_HEADS * DV
OFF_K = Q_DIM
OFF_V = OFF_K + K_DIM
OFF_Z = OFF_V + V_DIM
OFF_XBC = OFF_Z + D_INNER
OFF_DT = OFF_XBC + XBC_DIM
IN_DIM = OFF_DT + N_HEADS_S
N_EGROUPS = 4
EXPERTS_PER_GROUP = 8
N_EXPERTS = N_EGROUPS * EXPERTS_PER_GROUP
TOP_K = 2
D_FF = 1024
PAGE = 128

LANES = 128
VMEM_LIMIT = 56 * 1024 * 1024

FRONT = CHUNK
PAD_ROWS = FRONT - N_META
TM = 256
TQ = 256
TKF = 1024
TB = 8
TMOE = 256
TSMALL = 128
TF = 256


def _cparams(sem):
    return pltpu.CompilerParams(dimension_semantics=sem, vmem_limit_bytes=VMEM_LIMIT)


def _resident(shape):
    nd = len(shape)
    return pl.BlockSpec(shape, lambda *_: (0,) * nd, pipeline_mode=pl.Buffered(1))


def _split2(x):
    hi = x.astype(BF16)
    lo = (x - hi.astype(F32)).astype(BF16)
    return hi, lo


def _split3(x):
    hi = x.astype(BF16)
    r = x - hi.astype(F32)
    mid = r.astype(BF16)
    lo = (r - mid.astype(F32)).astype(BF16)
    return hi, mid, lo


def _silu(x):
    return x * (1.0 / (1.0 + jnp.exp(-x)))


def _softplus(x):
    return jnp.maximum(x, 0.0) + jnp.log(1.0 + jnp.exp(-jnp.abs(x)))


HALF = TM // 2


def _token_specs(seq):
    nb = seq // HALF
    return [pl.BlockSpec((HALF, D_MODEL), lambda i: (0, 0)),
            pl.BlockSpec((HALF, D_MODEL), lambda i: (jnp.clip(2 * i - 1, 0, nb - 1), 0)),
            pl.BlockSpec((HALF, D_MODEL), lambda i: (jnp.clip(2 * i, 0, nb - 1), 0)),
            pl.BlockSpec((HALF, D_MODEL), lambda i: (0, 0))]


def _token_rows(front_ref, xa_ref, xb_ref, xs_ref):
    i = pl.program_id(0)
    top = jnp.where(i == 0, front_ref[...], xa_ref[...])
    bot = jnp.where(i == pl.num_programs(0) - 1, xs_ref[...], xb_ref[...])
    return jnp.concatenate([top, bot], axis=0)


def _inproj_kernel(front_ref, xa_ref, xb_ref, xs_ref, nw_ref, wq_ref, wk_ref, wv_ref, wz_ref, wx_ref, wdt_ref,
                   c_ref, s1_ref, s2_ref,
                   q_ref, k_ref, v_ref, kb_ref, vb_ref, z_ref, xbc_ref, dt_ref):
    x = _token_rows(front_ref, xa_ref, xb_ref, xs_ref)
    ms = jnp.mean(x * x, axis=-1, keepdims=True)
    u = (x * lax.rsqrt(ms + EPS) * nw_ref[...]).astype(BF16)
    c, s1, s2 = c_ref[...], s1_ref[...], s2_ref[...]

    def rope(p):
        outs = []
        for h in range(p.shape[1] // LANES):
            xh = p[:, h * LANES:(h + 1) * LANES]
            outs.append(xh * c + pltpu.roll(xh, LANES - ROT_DIM // 2, 1) * s1
                        + pltpu.roll(xh, ROT_DIM // 2, 1) * s2)
        return jnp.concatenate(outs, axis=1)

    q = rope(jnp.dot(u, wq_ref[...], preferred_element_type=F32))
    q_ref[...] = (q * (DK ** -0.5 * LOG2E)).astype(BF16)
    k = rope(jnp.dot(u, wk_ref[...], preferred_element_type=F32))
    kb_ref[...] = k.astype(BF16)
    v = jnp.dot(u, wv_ref[...], preferred_element_type=F32)
    for g in range(N_KV_HEADS):
        k_ref[pl.ds(g, TM, stride=N_KV_HEADS), :] = k[:, g * LANES:(g + 1) * LANES]
        v_ref[pl.ds(g, TM, stride=N_KV_HEADS), :] = v[:, g * LANES:(g + 1) * LANES]
    vb = v.astype(BF16)
    ones = jnp.ones((vb.shape[0], DV), BF16)
    vb_ref[...] = jnp.concatenate(
        [piece for g in range(N_KV_HEADS) for piece in (vb[:, g * DV:(g + 1) * DV], ones)], axis=1)
    z_ref[...] = jnp.dot(u, wz_ref[...], preferred_element_type=F32)
    xbc_ref[...] = jnp.dot(u, wx_ref[...], preferred_element_type=F32)
    dt_ref[...] = jnp.dot(u, wdt_ref[...], preferred_element_type=F32)


def _inproj(front, xp, xs, norm_w, wq, wk, wv, wz, wx, wdt, ctab, s1tab, s2tab):
    R = front.shape[0] + xp.shape[0] + xs.shape[0]
    row = lambda n: pl.BlockSpec((TM, n), lambda i: (i, 0))
    head_rows = pl.BlockSpec((TM * N_KV_HEADS, LANES), lambda i: (i, 0))
    return pl.pallas_call(
        _inproj_kernel,
        grid=(R // TM,),
        in_specs=_token_specs(xp.shape[0]) + [_resident((1, D_MODEL)),
                  _resident(wq.shape), _resident(wk.shape), _resident(wv.shape),
                  _resident(wz.shape), _resident(wx.shape), _resident(wdt.shape),
                  row(LANES), row(LANES), row(LANES)],
        out_specs=[row(Q_DIM), head_rows, head_rows, row(K_DIM), row(2 * V_DIM),
                   row(D_INNER), row(XBC_DIM), row(LANES)],
        out_shape=[jax.ShapeDtypeStruct((R, Q_DIM), BF16),
                   jax.ShapeDtypeStruct((R * N_KV_HEADS, LANES), F32),
                   jax.ShapeDtypeStruct((R * N_KV_HEADS, LANES), F32),
                   jax.ShapeDtypeStruct((R, K_DIM), BF16),
                   jax.ShapeDtypeStruct((R, 2 * V_DIM), BF16),
                   jax.ShapeDtypeStruct((R, D_INNER), F32),
                   jax.ShapeDtypeStruct((R, XBC_DIM), F32),
                   jax.ShapeDtypeStruct((R, LANES), F32)],
        compiler_params=_cparams(("arbitrary",)),
        name="inproj",
    )(front, xp, xp, xs, norm_w, wq, wk, wv, wz, wx, wdt, ctab, s1tab, s2tab)


def _diff_lambda(lq1, lk1, lq2, lk2, lam_init):
    a = jnp.sum(lq1[...] * lk1[...], axis=-1, keepdims=True)
    b = jnp.sum(lq2[...] * lk2[...], axis=-1, keepdims=True)
    return jnp.exp(a) - jnp.exp(b) + lam_init


def _subln(o, sub_w, lam_init):
    ms = jnp.mean(o * o, axis=-1, keepdims=True)
    return (o * lax.rsqrt(ms + EPS) * sub_w) * (1.0 - lam_init)


def _attn_kernel(lq1, lk1, lq2, lk2, sub_ref, q_ref, k_ref, ve_ref, o_ref,
                 qs_scr, m_scr, acc_scr, *, lam_init, n_rows):
    qi = pl.program_id(1)
    lam = _diff_lambda(lq1, lk1, lq2, lk2, lam_init)

    q = q_ref[...]
    lane = lax.broadcasted_iota(I32, (TQ, LANES), 1)
    first = lane < DK
    zero = jnp.zeros((TQ, LANES), BF16)
    for hh in range(2):
        qh = q[:, hh * LANES:(hh + 1) * LANES]
        qs_scr[(2 * hh) * TQ:(2 * hh + 1) * TQ, :] = jnp.where(first, qh, zero)
        qs_scr[(2 * hh + 1) * TQ:(2 * hh + 2) * TQ, :] = jnp.where(first, zero, qh)
    m_scr[...] = jnp.full(m_scr.shape, NEG_INF, F32)
    acc_scr[...] = jnp.zeros(acc_scr.shape, F32)

    nt = (((1,), (1,)), ((), ()))

    def process(start, width, keep):
        for part in range(4):
            rows = slice(part * TQ, (part + 1) * TQ)
            s = lax.dot_general(qs_scr[rows, :], k_ref[pl.ds(start, width), :], nt,
                                preferred_element_type=F32)
            if keep is not None:
                s = jnp.where(keep, s, NEG_INF)
            m_prev = m_scr[rows, :]
            m_new = jnp.maximum(m_prev, jnp.max(s, axis=-1, keepdims=True))
            alpha = jnp.exp2(m_prev - m_new)
            p = jnp.concatenate([jnp.exp2(s[:, t * LANES:(t + 1) * LANES] - m_new)
                                 for t in range(width // LANES)], axis=1).astype(BF16)
            pv = jnp.dot(p, ve_ref[pl.ds(start, width), :], preferred_element_type=F32)
            acc_scr[rows, :LANES] = alpha * acc_scr[rows, :LANES] + pv[:, :LANES]
            acc_scr[rows, LANES:] = alpha * acc_scr[rows, LANES:] + pv[:, LANES:]
            m_scr[rows, :] = m_new

    q_lo = qi * TQ
    n_full = jnp.maximum(q_lo - PAD_ROWS, 0) // TKF
    tail_lo = PAD_ROWS + n_full * TKF
    n_tail = (q_lo + TQ - tail_lo + TQ - 1) // TQ

    def full(c):
        process(pl.multiple_of(PAD_ROWS + c * TKF, 16), TKF, None)

    def full_pair(c2, carry):
        full(2 * c2)
        full(2 * c2 + 1)
        return carry

    lax.fori_loop(0, n_full // 2, full_pair, 0)

    @pl.when(n_full % 2 == 1)
    def _():
        full(n_full - 1)

    def tail(w):
        width = w * TQ

        def run():
            start = pl.multiple_of(q_lo + TQ - width, 16)
            col = lax.broadcasted_iota(I32, (TQ, width), 1)
            qpos = q_lo + lax.broadcasted_iota(I32, (TQ, width), 0)
            process(start, width, (col <= qpos - start) & (col >= tail_lo - start))

        return run

    for w in range(1, TKF // TQ + 2):
        pl.when(n_tail == w)(tail(w))

    acc = acc_scr[...]
    o = acc[:, :LANES] / acc[:, LANES:]
    sub_w = sub_ref[...]
    for hh in range(2):
        o1 = o[(2 * hh) * TQ:(2 * hh + 1) * TQ, :]
        o2 = o[(2 * hh + 1) * TQ:(2 * hh + 2) * TQ, :]
        o_ref[:, hh * LANES:(hh + 1) * LANES] = _subln(o1 - lam * o2, sub_w, lam_init).astype(BF16)


def _expand_heads(x, e_ref):
    hi, lo = _split2(x)
    e = e_ref[...]
    return (jnp.dot(hi, e, preferred_element_type=F32)
            + jnp.dot(lo, e, preferred_element_type=F32))


def _gated_out(y, xs, z, dskip_full, normw):
    y = (y + dskip_full * xs) * _silu(z)
    half = D_INNER // N_BC_GROUPS
    outs = []
    for g in range(N_BC_GROUPS):
        yg = y[:, g * half:(g + 1) * half]
        ms = jnp.mean(yg * yg, axis=-1, keepdims=True)
        outs.append(yg * lax.rsqrt(ms + EPS))
    return jnp.concatenate(outs, axis=1) * normw


def _ssd_kernel(xbc_ref, z_ref, dt_ref, cw_ref, cb_ref, dtb_ref, a_ref, dskip_ref, normw_ref,
                e_ref, et_ref, tri_ref, ssm_ref, state_ref, ext_scr, h_scr):
    c = pl.program_id(0)
    L = CHUNK

    @pl.when(c == 0)
    def _():
        ext_scr[0:8, :] = jnp.zeros((8, XBC_DIM), F32)
        h_scr[...] = jnp.zeros(h_scr.shape, F32)

    x_new = xbc_ref[...]
    ext_scr[8:8 + L, :] = x_new
    conv = cb_ref[...] + cw_ref[CONV_W - 1:CONV_W, :] * x_new
    for j in range(CONV_W - 1):
        sh = CONV_W - 1 - j
        conv = conv + cw_ref[j:j + 1, :] * ext_scr[8 - sh:8 - sh + L, :]
    ext_scr[0:8, :] = x_new[L - 8:L, :]
    xc = _silu(conv)
    xs = xc[:, :D_INNER]

    rowi = lax.broadcasted_iota(I32, (L, LANES), 0)
    coli = lax.broadcasted_iota(I32, (L, LANES), 1)
    valid = (c * L + rowi) >= PAD_ROWS
    dt = jnp.where(valid, _softplus(dt_ref[...] + dtb_ref[...]), 0.0)
    da = dt * a_ref[...]
    tri = tri_ref[...]
    acs = sum(jnp.dot(tri, part, preferred_element_type=F32) for part in _split3(da))
    acs_t = acs.T
    dt_t = dt.T
    causal = coli <= rowi
    last = acs[L - 1:L, :]

    w_state = _expand_heads(dt * jnp.exp(last - acs), e_ref)
    dec_out = _expand_heads(jnp.exp(acs), e_ref)
    dec_rows = jnp.sum(et_ref[...] * jnp.exp(last), axis=-1, keepdims=True)
    xw = xs * w_state
    xs_b = xs.astype(BF16)

    half = D_INNER // N_BC_GROUPS
    hpg = N_HEADS_S // N_BC_GROUPS
    y_parts = []
    for g in range(N_BC_GROUPS):
        bm = xc[:, D_INNER + g * D_STATE:D_INNER + (g + 1) * D_STATE].astype(BF16)
        cm = xc[:, D_INNER + (N_BC_GROUPS + g) * D_STATE:
                D_INNER + (N_BC_GROUPS + g + 1) * D_STATE].astype(BF16)
        cb = lax.dot_general(cm, bm, (((1,), (1,)), ((), ())), preferred_element_type=F32)
        h_prev = h_scr[g * half:(g + 1) * half, :]
        y_off = lax.dot_general(cm, h_prev.astype(BF16), (((1,), (1,)), ((), ())),
                                preferred_element_type=F32)
        y_off = y_off * dec_out[:, g * half:(g + 1) * half]
        yd = []
        for r in range(0, hpg, 2):
            pair = []
            for hh in (r, r + 1):
                h = g * hpg + hh
                seg = jnp.broadcast_to(acs[:, h:h + 1], (L, L)) - acs_t[h:h + 1, :]
                lmat = jnp.where(causal, jnp.exp(jnp.where(causal, seg, 0.0)), 0.0)
                pair.append((cb * lmat * dt_t[h:h + 1, :]).astype(BF16))
            h0 = g * hpg + r
            xp = xs_b[:, h0 * SSM_HEAD_DIM:(h0 + 2) * SSM_HEAD_DIM]
            ya = jnp.dot(pair[0], xp, preferred_element_type=F32)
            yb = jnp.dot(pair[1], xp, preferred_element_type=F32)
            yd.append(jnp.where(coli < SSM_HEAD_DIM, ya, yb))
        y_parts.append(jnp.concatenate(yd, axis=1) + y_off)
        xw_t = xw[:, g * half:(g + 1) * half].T.astype(BF16)
        st = jnp.dot(xw_t, bm, preferred_element_type=F32)
        h_scr[g * half:(g + 1) * half, :] = dec_rows[g * half:(g + 1) * half, :] * h_prev + st
    y = jnp.concatenate(y_parts, axis=1)

    ssm_ref[...] = _gated_out(y, xs, z_ref[...], dskip_ref[...], normw_ref[...]).astype(BF16)

    @pl.when(c == pl.num_programs(0) - 1)
    def _():
        state_ref[...] = h_scr[...]


def _prompt_ssd(xbc, z, dt_raw, conv_w, conv_b, dtb, a_row, dskip_full, normw, e_mat, e_t, tri, n_t):
    row = lambda n: pl.BlockSpec((CHUNK, n), lambda c: (c, 0))
    const = lambda shp: pl.BlockSpec(shp, lambda c: (0, 0))
    return pl.pallas_call(
        _ssd_kernel,
        grid=(n_t // CHUNK,),
        in_specs=[row(XBC_DIM), row(D_INNER), row(LANES),
                  const((CONV_W, XBC_DIM)), const((1, XBC_DIM)), const((1, LANES)), const((1, LANES)),
                  const((1, D_INNER)), const((1, D_INNER)), const((LANES, D_INNER)),
                  const((D_INNER, LANES)), const((CHUNK, CHUNK))],
        out_specs=[row(D_INNER), pl.BlockSpec((D_INNER, D_STATE), lambda c: (0, 0))],
        out_shape=[jax.ShapeDtypeStruct((n_t, D_INNER), BF16),
                   jax.ShapeDtypeStruct((D_INNER, D_STATE), F32)],
        scratch_shapes=[pltpu.VMEM((8 + CHUNK, XBC_DIM), F32),
                        pltpu.VMEM((D_INNER, D_STATE), F32)],
        compiler_params=_cparams(("arbitrary",)),
        name="prompt_ssd",
    )(xbc, z, dt_raw, conv_w, conv_b, dtb, a_row, dskip_full, normw, e_mat, e_t, tri)


def _decode_attn_kernel(pt_ref, lq1, lk1, lq2, lk2, sub_ref, q_ref, kn_ref, vn_ref, *rest,
                        n_pages, lam_init):
    k_pages = rest[:n_pages]
    v_pages = rest[n_pages:2 * n_pages]
    o_ref = rest[2 * n_pages]
    del pt_ref
    lam = _diff_lambda(lq1, lk1, lq2, lk2, lam_init)
    nq = 2 * N_HEADS_A
    q8 = q_ref[...]
    lane = lax.broadcasted_iota(I32, (N_HEADS_A, LANES), 1)
    zero = jnp.zeros((N_HEADS_A, LANES), BF16)
    q16 = jnp.concatenate([jnp.where(lane < DK, q8, zero), jnp.where(lane < DK, zero, q8)], axis=0)

    pw = PAGE * N_KV_HEADS
    row_head = (lax.broadcasted_iota(I32, (nq, pw), 0) & (N_HEADS_A - 1)) >> 1
    col_head = lax.broadcasted_iota(I32, (nq, pw), 1) & (N_KV_HEADS - 1)
    own = row_head == col_head
    nt = (((1,), (1,)), ((), ()))
    s_pages = []
    for j in range(n_pages):
        kp = k_pages[j][...].astype(BF16)
        s = lax.dot_general(q16, kp, nt, preferred_element_type=F32)
        s_pages.append(jnp.where(own, s, NEG_INF))
    kn = kn_ref[...].astype(BF16)
    row_head_n = (lax.broadcasted_iota(I32, (nq, LANES), 0) & (N_HEADS_A - 1)) >> 1
    col_n = lax.broadcasted_iota(I32, (nq, LANES), 1)
    s_new = jnp.where(row_head_n == col_n, lax.dot_general(q16, kn, nt, preferred_element_type=F32), NEG_INF)

    m = jnp.max(s_new, axis=-1, keepdims=True)
    for s in s_pages:
        m = jnp.maximum(m, jnp.max(s, axis=-1, keepdims=True))
    p_new = jnp.exp2(s_new - m)
    l = jnp.sum(p_new, axis=-1, keepdims=True)
    acc = jnp.dot(p_new.astype(BF16), vn_ref[...].astype(BF16), preferred_element_type=F32)
    for j in range(n_pages):
        p = jnp.exp2(s_pages[j] - m)
        l = l + jnp.sum(p, axis=-1, keepdims=True)
        acc = acc + jnp.dot(p.astype(BF16), v_pages[j][...].astype(BF16), preferred_element_type=F32)
    o = acc / l
    res = o[:N_HEADS_A, :] - lam * o[N_HEADS_A:, :]
    o_ref[...] = _subln(res, sub_ref[...], lam_init).astype(BF16)


def _fused_attn_kernel(pt_ref, lq1, lk1, lq2, lk2, sub_ref, q_ref, k_ref, ve_ref, qd_ref, kn_ref, vn_ref, *rest,
                       n_pages, n_dec, lam_init, n_rows):
    pages = rest[:2 * n_pages]
    o_ref, od_ref = rest[2 * n_pages], rest[2 * n_pages + 1]
    scratch = rest[2 * n_pages + 2:]

    @pl.when(pl.program_id(1) < n_rows // TQ)
    def _():
        _attn_kernel(lq1, lk1, lq2, lk2, sub_ref, q_ref, k_ref, ve_ref, o_ref, *scratch,
                     lam_init=lam_init, n_rows=n_rows)

    step = pl.program_id(0) * pl.num_programs(1) + pl.program_id(1)

    @pl.when(step < n_dec)
    def _():
        _decode_attn_kernel(pt_ref, lq1, lk1, lq2, lk2, sub_ref, qd_ref, kn_ref, vn_ref, *pages, od_ref,
                            n_pages=n_pages, lam_init=lam_init)


def _attention(page_table, q, kb, vbe, q_s, k_new, v_new, cache_k2, cache_v2, lq1, lk1, lq2, lk2, sub_w,
               lam_init):
    R = q.shape[0]
    nq = R // TQ
    B, n_pages = page_table.shape
    assert R % TQ == 0 and TKF % TQ == 0
    n_inner = max(nq, pl.cdiv(B, N_KV_HEADS))
    pw = PAGE * N_KV_HEADS
    seq_of = lambda g, i: jnp.minimum(g * n_inner + i, B - 1)
    qblk = lambda i: jnp.minimum(i, nq - 1)
    small = lambda n: pl.BlockSpec((1, n), lambda g, i, pt: (0, 0))
    per_b = pl.BlockSpec((None, 8, LANES), lambda g, i, pt: (seq_of(g, i), 0, 0))
    new_tok = pl.BlockSpec((None, LANES, LANES), lambda g, i, pt: (seq_of(g, i), 0, 0))
    page_spec = lambda j: pl.BlockSpec((pw, LANES), lambda g, i, pt: (pt[seq_of(g, i), j], 0))
    grid_spec = pltpu.PrefetchScalarGridSpec(
        num_scalar_prefetch=1,
        grid=(N_KV_HEADS, n_inner),
        in_specs=[small(DK), small(DK), small(DK), small(DK), small(DV),
                  pl.BlockSpec((TQ, 2 * LANES), lambda g, i, pt: (qblk(i), g)),
                  pl.BlockSpec((R, LANES), lambda g, i, pt: (0, g)),
                  pl.BlockSpec((R, 2 * LANES), lambda g, i, pt: (0, g)),
                  per_b, new_tok, new_tok]
                 + [page_spec(j) for j in range(n_pages)] + [page_spec(j) for j in range(n_pages)],
        out_specs=[pl.BlockSpec((TQ, 2 * LANES), lambda g, i, pt: (qblk(i), g)), per_b],
        scratch_shapes=[pltpu.VMEM((4 * TQ, LANES), BF16),
                        pltpu.VMEM((4 * TQ, LANES), F32),
                        pltpu.VMEM((4 * TQ, 2 * LANES), F32)],
    )
    return pl.pallas_call(
        functools.partial(_fused_attn_kernel, n_pages=n_pages, n_dec=B, lam_init=lam_init, n_rows=R),
        grid_spec=grid_spec,
        out_shape=[jax.ShapeDtypeStruct((R, N_HEADS_A * DV), BF16),
                   jax.ShapeDtypeStruct((B, 8, LANES), BF16)],
        compiler_params=_cparams(("arbitrary", "arbitrary")),
        name="attention",
    )(page_table, lq1, lk1, lq2, lk2, sub_w, q, kb, vbe, q_s, k_new, v_new,
      *([cache_k2] * n_pages), *([cache_v2] * n_pages))


def _dec_pre_kernel(xbc_ref, s0_ref, s1_ref, s2_ref, dt_ref, cw_ref, cb_ref, dtb_ref, a_ref, e_ref,
                    xc_ref, xdt_ref, dect_ref):
    conv = (cb_ref[...] + cw_ref[0:1, :] * s0_ref[...] + cw_ref[1:2, :] * s1_ref[...]
            + cw_ref[2:3, :] * s2_ref[...] + cw_ref[3:4, :] * xbc_ref[...])
    xc = _silu(conv)
    xc_ref[...] = xc
    dt = _softplus(dt_ref[...] + dtb_ref[...])
    dec = jnp.exp(dt * a_ref[...])
    hi, mid, lo = _split3(dt)
    e = e_ref[...]
    dt_full = (jnp.dot(hi, e, preferred_element_type=F32) + jnp.dot(mid, e, preferred_element_type=F32)
               + jnp.dot(lo, e, preferred_element_type=F32))
    hi, mid, lo = _split3(dec)
    dec_full = (jnp.dot(hi, e, preferred_element_type=F32) + jnp.dot(mid, e, preferred_element_type=F32)
                + jnp.dot(lo, e, preferred_element_type=F32))
    xdt_ref[...] = (dt_full * xc[:, :D_INNER]).T
    dect_ref[...] = dec_full.T


def _dec_state_kernel(h_ref, xdt_ref, dect_ref, bm_ref, cm_ref, hout_ref, y_ref):
    i = pl.program_id(0)
    B = xdt_ref.shape[1]
    lane = lax.broadcasted_iota(I32, (D_INNER, B), 1)
    sub = lax.broadcasted_iota(I32, (TB, D_INNER // N_BC_GROUPS), 0)
    ones = jnp.ones((B, D_STATE), BF16)
    half = D_INNER // N_BC_GROUPS
    nt = (((1,), (1,)), ((), ()))
    y_acc = [jnp.zeros((TB, half), F32) for _ in range(N_BC_GROUPS)]
    for j in range(TB):
        b = i * TB + j
        sel = lane == b
        xcol = jnp.where(sel, xdt_ref[...], 0.0)
        dcol = jnp.where(sel, dect_ref[...], 0.0)
        dec = sum(jnp.dot(part, ones, preferred_element_type=F32) for part in _split3(dcol))
        xhi, xlo = _split2(xcol)
        hnew = dec * h_ref[j]
        for g in range(N_BC_GROUPS):
            bmat = bm_ref[:, g * D_STATE:(g + 1) * D_STATE].astype(BF16)
            upd = (jnp.dot(xhi[g * half:(g + 1) * half], bmat, preferred_element_type=F32)
                   + jnp.dot(xlo[g * half:(g + 1) * half], bmat, preferred_element_type=F32))
            hg = hnew[g * half:(g + 1) * half] + upd
            hout_ref[j, g * half:(g + 1) * half, :] = hg
            start = pl.multiple_of(i * TB, TB)
            c8 = cm_ref[pl.ds(start, TB), g * D_STATE:(g + 1) * D_STATE].astype(BF16)
            yj = lax.dot_general(c8, hg.astype(BF16), nt, preferred_element_type=F32)
            y_acc[g] = y_acc[g] + jnp.where(sub == j, yj, 0.0)
    y_ref[...] = jnp.concatenate(y_acc, axis=1)


def _dec_post_kernel(y_ref, xc_ref, z_ref, dskip_ref, normw_ref, o_ref):
    o_ref[...] = _gated_out(y_ref[...], xc_ref[:, :D_INNER], z_ref[...], dskip_ref[...],
                            normw_ref[...]).astype(BF16)


def _decode_ssm(xbc_s, sc0, sc1, sc2, dt_s, z_s, state, conv_w, conv_b, dtb, a_row, dskip_full, normw, e_mat):
    B = xbc_s.shape[0]
    xc, xdt_t, dec_t = pl.pallas_call(
        _dec_pre_kernel,
        out_shape=[jax.ShapeDtypeStruct((B, XBC_DIM), F32),
                   jax.ShapeDtypeStruct((D_INNER, B), F32),
                   jax.ShapeDtypeStruct((D_INNER, B), F32)],
        compiler_params=pltpu.CompilerParams(vmem_limit_bytes=VMEM_LIMIT),
        name="decode_ssm_pre",
    )(xbc_s, sc0, sc1, sc2, dt_s, conv_w, conv_b, dtb, a_row, e_mat)
    bm = xc[:, D_INNER:D_INNER + N_BC_GROUPS * D_STATE]
    cm = xc[:, D_INNER + N_BC_GROUPS * D_STATE:]
    const = lambda shp: pl.BlockSpec(shp, lambda i: (0,) * len(shp))
    state_spec = pl.BlockSpec((TB, D_INNER, D_STATE), lambda i: (i, 0, 0))
    h_new, y = pl.pallas_call(
        _dec_state_kernel,
        grid=(B // TB,),
        in_specs=[state_spec, const((D_INNER, B)), const((D_INNER, B)),
                  const((B, N_BC_GROUPS * D_STATE)), const((B, N_BC_GROUPS * D_STATE))],
        out_specs=[state_spec, pl.BlockSpec((TB, D_INNER), lambda i: (i, 0))],
        out_shape=[jax.ShapeDtypeStruct((B, D_INNER, D_STATE), F32),
                   jax.ShapeDtypeStruct((B, D_INNER), F32)],
        compiler_params=_cparams(("arbitrary",)),
        name="decode_ssm_state",
    )(state, xdt_t, dec_t, bm, cm)
    ssm = pl.pallas_call(
        _dec_post_kernel,
        out_shape=jax.ShapeDtypeStruct((B, D_INNER), BF16),
        compiler_params=pltpu.CompilerParams(vmem_limit_bytes=VMEM_LIMIT),
        name="decode_ssm_post",
    )(y, xc, z_s, dskip_full, normw)
    return ssm, h_new


def _outproj_kernel(front_ref, xa_ref, xb_ref, xs_ref, att_ref, sa_ref, sb_ref, ss_ref,
                    wa_ref, ws_ref, nw_ref, wrh_ref, wrl_ref, h_ref, u_ref, lg_ref):
    last = pl.program_id(0) == pl.num_programs(0) - 1
    ssm = jnp.concatenate([sa_ref[...], jnp.where(last, ss_ref[...], sb_ref[...])], axis=0)
    acc = jnp.dot(att_ref[...], wa_ref[...], preferred_element_type=F32)
    acc = acc + jnp.dot(ssm, ws_ref[...], preferred_element_type=F32)
    h = _token_rows(front_ref, xa_ref, xb_ref, xs_ref) + acc
    h_ref[...] = h
    ms = jnp.mean(h * h, axis=-1, keepdims=True)
    u = h * lax.rsqrt(ms + EPS) * nw_ref[...]
    u_ref[...] = u
    uh, ul = _split2(u)
    wh, wl = wrh_ref[...], wrl_ref[...]
    lg = (jnp.dot(uh, wh, preferred_element_type=F32) + jnp.dot(uh, wl, preferred_element_type=F32)
          + jnp.dot(ul, wh, preferred_element_type=F32))
    lg_ref[...] = _route_rows(lg, pl.program_id(0) * TM)


def _first_max(x, lane):
    m = jnp.max(x, axis=-1, keepdims=True)
    return m, jnp.min(jnp.where(x == m, lane, float(LANES)), axis=-1, keepdims=True)


def _route_rows(lg, row0):
    rows = lg.shape[0]
    lane_i = lax.broadcasted_iota(I32, (rows, LANES), 1)
    lane = lane_i.astype(F32)
    gl = jnp.where(lane_i < N_EGROUPS, lg, -jnp.inf)
    gmax, g = _first_max(gl, lane)
    pg = 1.0 / jnp.sum(jnp.exp(gl - gmax), axis=-1, keepdims=True)
    lo = N_EGROUPS + g * EXPERTS_PER_GROUP
    el = jnp.where((lane >= lo) & (lane < lo + EXPERTS_PER_GROUP), lg, -jnp.inf)
    m1, i1 = _first_max(el, lane)
    m2, i2 = _first_max(jnp.where(lane == i1, -jnp.inf, el), lane)
    z = jnp.sum(jnp.exp(el - m1), axis=-1, keepdims=True)
    p1 = 1.0 / z
    p2 = jnp.exp(m2 - m1) / z
    tot = p1 + p2
    valid = (row0 + lax.broadcasted_iota(I32, (rows, 1), 0)) >= PAD_ROWS
    e1 = jnp.where(valid, i1 - N_EGROUPS, -1.0)
    e2 = jnp.where(valid, i2 - N_EGROUPS, -1.0)
    w1 = jnp.where(valid, p1 / tot * pg, 0.0)
    w2 = jnp.where(valid, p2 / tot * pg, 0.0)
    return jnp.where(lane_i == 0, e1, jnp.where(lane_i == 1, e2, jnp.where(lane_i == 2, w1,
                                                                         jnp.where(lane_i == 3, w2, 0.0))))


def _rank_kernel(route_ref, tri_ref, utri_ref, rank_ref, meta_ref, carry, *, nblk):
    i = pl.program_id(0)

    @pl.when(i == 0)
    def _():
        carry[...] = jnp.zeros(carry.shape, F32)

    r = route_ref[...]
    lane = lax.broadcasted_iota(I32, r.shape, 1).astype(F32)
    oh0 = (lane == r[:, 0:1]).astype(F32)
    oh1 = (lane == r[:, 1:2]).astype(F32)
    oh = oh0 + oh1
    before = jnp.dot(tri_ref[...], oh.astype(BF16), preferred_element_type=F32) + carry[0:1, :]
    rank0 = jnp.sum(before * oh0, axis=-1, keepdims=True)
    rank1 = jnp.sum(before * oh1, axis=-1, keepdims=True)
    lane_i = lax.broadcasted_iota(I32, r.shape, 1)
    rank_ref[...] = jnp.where(lane_i == 0, rank0, jnp.where(lane_i == 1, rank1, 0.0))
    carry[0:1, :] = carry[0:1, :] + jnp.sum(oh, axis=0, keepdims=True)

    @pl.when(i == pl.num_programs(0) - 1)
    def _():
        nb_pad = meta_ref.shape[0] - 8
        counts = carry[0:1, :]
        pb = jnp.floor((counts + (TMOE - 1)) / TMOE)
        ends = jnp.dot(jnp.broadcast_to(pb, (8, LANES)).astype(BF16), utri_ref[...],
                       preferred_element_type=F32)[0:1, :]
        starts = ends - pb
        lane1 = lax.broadcasted_iota(I32, (1, LANES), 1)
        n_valid = jnp.sum(jnp.where(lane1 == N_EXPERTS - 1, ends, 0.0), axis=-1, keepdims=True)
        blk = lax.broadcasted_iota(I32, (nb_pad, LANES), 0).astype(F32)
        lane_b = lax.broadcasted_iota(I32, (nb_pad, LANES), 1)
        is_e = lane_b < N_EXPERTS
        block_e = jnp.minimum(jnp.sum(jnp.where(is_e & (ends <= blk), 1.0, 0.0), axis=-1, keepdims=True),
                              N_EXPERTS - 1.0)
        lane_bf = lane_b.astype(F32)
        has = is_e & (counts > 0.0) & (lane_bf > block_e)
        nxt = jnp.min(jnp.where(has, lane_bf, float(LANES)), axis=-1, keepdims=True)
        next_e = jnp.where(nxt >= LANES, -1.0, nxt)
        mine = lane_bf == block_e
        cnt_b = jnp.sum(jnp.where(mine, counts, 0.0), axis=-1, keepdims=True)
        st_b = jnp.sum(jnp.where(mine, starts, 0.0), axis=-1, keepdims=True)
        n_real = jnp.clip(cnt_b - (blk[:, 0:1] - st_b) * TMOE, 0.0, float(TMOE))
        meta_ref[0:8, :] = jnp.where(lax.broadcasted_iota(I32, (8, LANES), 0) == 0, starts,
                                     jnp.broadcast_to(n_valid, (8, LANES)))
        meta_ref[8:, :] = jnp.where(lane_b == 0, block_e, jnp.where(lane_b == 1, next_e,
                                                                    jnp.where(lane_b == 2, n_real, 0.0)))


def _rank(route, nblk):
    R = route.shape[0]
    nb_pad = (nblk + 7) // 8 * 8
    tri = (jnp.arange(TM)[:, None] > jnp.arange(TM)[None, :]).astype(BF16)
    utri = (jnp.arange(LANES)[:, None] <= jnp.arange(LANES)[None, :]).astype(BF16)
    return pl.pallas_call(
        functools.partial(_rank_kernel, nblk=nblk),
        grid=(R // TM,),
        in_specs=[pl.BlockSpec((TM, LANES), lambda i: (i, 0)),
                  pl.BlockSpec((TM, TM), lambda i: (0, 0)),
                  pl.BlockSpec((LANES, LANES), lambda i: (0, 0))],
        out_specs=[pl.BlockSpec((TM, LANES), lambda i: (i, 0)),
                   pl.BlockSpec((8 + nb_pad, LANES), lambda i: (0, 0))],
        out_shape=[jax.ShapeDtypeStruct((R, LANES), F32),
                   jax.ShapeDtypeStruct((8 + nb_pad, LANES), F32)],
        scratch_shapes=[pltpu.VMEM((8, LANES), F32)],
        compiler_params=_cparams(("arbitrary",)),
        name="expert_rank",
    )(route, tri, utri)


def _outproj(front, xp, xs, att, ssm_p, ssm_s, wa, ws, norm_w, wr_hi, wr_lo):
    R = front.shape[0] + xp.shape[0] + xs.shape[0]
    row = lambda n: pl.BlockSpec((TM, n), lambda i: (i, 0))
    nbp = ssm_p.shape[0] // HALF
    half = lambda f: pl.BlockSpec((HALF, D_INNER), f)
    return pl.pallas_call(
        _outproj_kernel,
        grid=(R // TM,),
        in_specs=_token_specs(xp.shape[0]) + [
            row(N_HEADS_A * DV),
            half(lambda i: (2 * i, 0)), half(lambda i: (jnp.minimum(2 * i + 1, nbp - 1), 0)),
            half(lambda i: (0, 0)),
            _resident(wa.shape), _resident(ws.shape), _resident((1, D_MODEL)),
            _resident(wr_hi.shape), _resident(wr_lo.shape)],
        out_specs=[row(D_MODEL), row(D_MODEL), row(LANES)],
        out_shape=[jax.ShapeDtypeStruct((R, D_MODEL), F32),
                   jax.ShapeDtypeStruct((R, D_MODEL), F32),
                   jax.ShapeDtypeStruct((R, LANES), F32)],
        compiler_params=_cparams(("arbitrary",)),
        name="outproj",
    )(front, xp, xp, xs, att, ssm_p, ssm_p, ssm_s, wa, ws, norm_w, wr_hi, wr_lo)


GATHER_UNROLL = 8

def _start_row_gather(idx_ref, src_hbm, dst, sem, n_groups=None):
    def start(r8, carry):
        for j in range(GATHER_UNROLL):
            r = r8 * GATHER_UNROLL + j
            pltpu.make_async_copy(src_hbm.at[pl.ds(idx_ref[0, r], 1)], dst.at[pl.ds(r, 1)],
                                  sem).start(priority=j % 2)
        return carry

    lax.fori_loop(0, dst.shape[0] // GATHER_UNROLL if n_groups is None else n_groups, start, 0)


def _wait_row_gather(src_hbm, dst, sem, n_groups=None):
    if n_groups is None:
        pltpu.make_async_copy(src_hbm.at[pl.ds(0, dst.shape[0])], dst, sem).wait()
        return

    def wait(g, carry):
        pltpu.make_async_copy(src_hbm.at[pl.ds(0, GATHER_UNROLL)], dst.at[pl.ds(0, GATHER_UNROLL)], sem).wait()
        return carry

    lax.fori_loop(0, n_groups, wait, 0)


def _row_groups(n_real):
    return (n_real + (GATHER_UNROLL - 1)) // GATHER_UNROLL


def _moe_kernel(be_ref, nx_ref, nr_ref, nv_ref, idx_ref, idx_next_ref, x_hbm, wg_hbm, wu_hbm, wd_hbm, o_ref,
                xbuf, stage_g, stage_u, stage_d, wg_b, wu_b, wd_b, sems, row_sems):
    i = pl.program_id(0)
    valid = i < nv_ref[0]
    e = be_ref[i]
    changed = (i == 0) | (be_ref[jnp.maximum(i - 1, 0)] != e)
    slot = i % 2
    n_real = nr_ref[i]

    @pl.when(i == 0)
    def _():
        xbuf[...] = jnp.zeros(xbuf.shape, F32)
        _start_row_gather(idx_ref, x_hbm, xbuf.at[0], row_sems.at[0], _row_groups(n_real))

    @pl.when(i + 1 < nv_ref[0])
    def _():
        _start_row_gather(idx_next_ref, x_hbm, xbuf.at[1 - slot], row_sems.at[1 - slot],
                          _row_groups(nr_ref[jnp.minimum(i + 1, pl.num_programs(0) - 1)]))

    def fetch(expert):
        return (pltpu.make_async_copy(wg_hbm.at[expert], stage_g, sems.at[0]),
                pltpu.make_async_copy(wu_hbm.at[expert], stage_u, sems.at[1]),
                pltpu.make_async_copy(wd_hbm.at[expert], stage_d, sems.at[2]))

    @pl.when(i == 0)
    def _():
        for c in fetch(e):
            c.start()

    @pl.when(valid & changed)
    def _():
        for c in fetch(e):
            c.wait()
        wg_b[...] = stage_g[...].astype(BF16)
        wu_b[...] = stage_u[...].astype(BF16)
        wd_b[...] = stage_d[...].astype(BF16)
        nxt = nx_ref[i]

        @pl.when(nxt >= 0)
        def _():
            for c in fetch(nxt):
                c.start()

    def mlp(rows):
        x = xbuf[slot, :rows, :].astype(BF16)
        g = jnp.dot(x, wg_b[...], preferred_element_type=F32)
        u = jnp.dot(x, wu_b[...], preferred_element_type=F32)
        h = (_silu(g) * u).astype(BF16)
        o_ref[:rows, :] = jnp.dot(h, wd_b[...], preferred_element_type=F32)

    @pl.when(valid)
    def _():
        _wait_row_gather(x_hbm, xbuf.at[slot], row_sems.at[slot], _row_groups(n_real))

    @pl.when(valid & (n_real > TSMALL))
    def _():
        mlp(TMOE)

    @pl.when(valid & (n_real <= TSMALL))
    def _():
        mlp(TSMALL)
        o_ref[TSMALL:, :] = jnp.zeros((TMOE - TSMALL, D_MODEL), F32)

    @pl.when(jnp.logical_not(valid))
    def _():
        o_ref[...] = jnp.zeros(o_ref.shape, F32)


def _moe(block_e, next_e, n_real, n_valid, buf_tok, x_tok, w_gate, w_up, w_down):
    rows = buf_tok.shape[0]
    nblk = rows // TMOE

    def blk(i, nv):
        return jnp.minimum(i, nv[0] - 1)

    grid_spec = pltpu.PrefetchScalarGridSpec(
        num_scalar_prefetch=4,
        grid=(nblk,),
        in_specs=[pl.BlockSpec((None, 1, TMOE), lambda i, be, nx, nr, nv: (blk(i, nv), 0, 0),
                               memory_space=pltpu.SMEM),
                  pl.BlockSpec((None, 1, TMOE), lambda i, be, nx, nr, nv: (blk(i + 1, nv), 0, 0),
                               memory_space=pltpu.SMEM),
                  pl.BlockSpec(memory_space=pl.ANY), pl.BlockSpec(memory_space=pl.ANY),
                  pl.BlockSpec(memory_space=pl.ANY), pl.BlockSpec(memory_space=pl.ANY)],
        out_specs=pl.BlockSpec((TMOE, D_MODEL), lambda i, be, nx, nr, nv: (i, 0)),
        scratch_shapes=[pltpu.VMEM((2, TMOE, D_MODEL), F32),
                        pltpu.VMEM((D_MODEL, D_FF), F32), pltpu.VMEM((D_MODEL, D_FF), F32),
                        pltpu.VMEM((D_FF, D_MODEL), F32),
                        pltpu.VMEM((D_MODEL, D_FF), BF16), pltpu.VMEM((D_MODEL, D_FF), BF16),
                        pltpu.VMEM((D_FF, D_MODEL), BF16),
                        pltpu.SemaphoreType.DMA((3,)), pltpu.SemaphoreType.DMA((2,))],
    )
    slot_rows = buf_tok.reshape(nblk, 1, TMOE)
    return pl.pallas_call(
        _moe_kernel,
        grid_spec=grid_spec,
        out_shape=jax.ShapeDtypeStruct((rows, D_MODEL), F32),
        compiler_params=_cparams(("arbitrary",)),
        name="moe_experts",
    )(block_e, next_e, n_real, n_valid, slot_rows, slot_rows, x_tok, w_gate, w_up, w_down)


def _final_kernel(idx_ref, idx_next_ref, h_ref, w_ref, nw_ref, y_hbm, op_ref, os_ref, ybuf, row_sems):
    i = pl.program_id(0)
    last = pl.num_programs(0) - 1
    slot = i % 2

    @pl.when(i == 0)
    def _():
        _start_row_gather(idx_ref, y_hbm, ybuf.at[0], row_sems.at[0])

    @pl.when(i + 1 < pl.num_programs(0))
    def _():
        _start_row_gather(idx_next_ref, y_hbm, ybuf.at[1 - slot], row_sems.at[1 - slot])

    _wait_row_gather(y_hbm, ybuf.at[slot], row_sems.at[slot])
    w = w_ref[...]
    moe = ybuf[slot, :HALF, :] * w[:, 0:1] + ybuf[slot, HALF:, :] * w[:, 1:2]
    h = h_ref[...] + moe
    ms = jnp.mean(h * h, axis=-1, keepdims=True)
    out = h * lax.rsqrt(ms + EPS) * nw_ref[...]

    @pl.when(i < last)
    def _():
        op_ref[...] = out

    @pl.when(i == last)
    def _():
        os_ref[...] = out


def _final(h1, yb, dest, wts, norm_w, seq):
    R = h1.shape[0]
    n = R // HALF
    nbp = seq // HALF
    assert R == (1 + nbp + 1) * HALF
    row = lambda w: pl.BlockSpec((HALF, w), lambda i: (i, 0))
    slots = dest.reshape(n, HALF, TOP_K).transpose(0, 2, 1).reshape(n, 1, TOP_K * HALF)
    idx_spec = lambda off: pl.BlockSpec((None, 1, TOP_K * HALF), lambda i: (jnp.minimum(i + off, n - 1), 0, 0),
                                        memory_space=pltpu.SMEM)
    return pl.pallas_call(
        _final_kernel,
        grid=(n,),
        in_specs=[idx_spec(0), idx_spec(1), row(D_MODEL), row(TOP_K),
                  pl.BlockSpec((1, D_MODEL), lambda i: (0, 0)),
                  pl.BlockSpec(memory_space=pl.ANY)],
        out_specs=[pl.BlockSpec((HALF, D_MODEL), lambda i: (jnp.clip(i - 1, 0, nbp - 1), 0)),
                   pl.BlockSpec((HALF, D_MODEL), lambda i: (0, 0))],
        out_shape=[jax.ShapeDtypeStruct((seq, D_MODEL), F32),
                   jax.ShapeDtypeStruct((HALF, D_MODEL), F32)],
        scratch_shapes=[pltpu.VMEM((2, TOP_K * HALF, D_MODEL), F32), pltpu.SemaphoreType.DMA((2,))],
        compiler_params=_cparams(("arbitrary",)),
        name="final_norm",
    )(slots, slots, h1, wts, norm_w, yb)


def _route(route):
    R = route.shape[0]
    a_max = TOP_K * R
    nblk = (a_max + N_EXPERTS * (TMOE - 1) + TMOE - 1) // TMOE
    rows = nblk * TMOE
    rank, meta = _rank(route, nblk)
    eid = route[:, :TOP_K].astype(I32)
    wts = route[:, TOP_K:2 * TOP_K]
    starts = meta[0, :N_EXPERTS].astype(I32) * TMOE
    n_valid = meta[1, :1].astype(I32)
    block_e = meta[8:8 + nblk, 0].astype(I32)
    next_e = meta[8:8 + nblk, 1].astype(I32)
    n_real = meta[8:8 + nblk, 2].astype(I32)
    is_real = eid >= 0
    dest = jnp.where(is_real, starts[jnp.maximum(eid, 0)] + rank[:, :TOP_K].astype(I32), 0)
    buf_tok = _slot_table(jnp.where(is_real, dest, rows), rows)
    return wts, dest, buf_tok, block_e, next_e, n_real, n_valid


def _slot_table_kernel(dest_ref, out_ref, *, n_assign, n_slots):
    def clear(r, carry):
        for c in range(LANES):
            out_ref[r, c] = 0
        return carry

    lax.fori_loop(0, out_ref.shape[0], clear, 0)

    def place(r, carry):
        for c in range(LANES):
            d = dest_ref[r, c]
            out_ref[d >> 7, d & (LANES - 1)] = (r * LANES + c) >> 1
        return carry

    lax.fori_loop(0, n_assign // LANES, place, 0)


def _slot_table(dest, n_slots):
    n_assign = dest.size
    assert n_assign % LANES == 0 and n_slots % LANES == 0 and TOP_K == 2
    smem = pl.BlockSpec(memory_space=pltpu.SMEM)
    out = pl.pallas_call(
        functools.partial(_slot_table_kernel, n_assign=n_assign, n_slots=n_slots),
        in_specs=[smem],
        out_specs=smem,
        out_shape=jax.ShapeDtypeStruct((n_slots // LANES + 1, LANES), I32),
        name="slot_table",
    )(dest.reshape(n_assign // LANES, LANES))
    return out.reshape(-1)[:n_slots]


def kernel(x_prompt, x_sample, cache_k, cache_v, state_conv, state_ssm, page_table, meta_tokens, norm_mix_w,
           w_in, lambda_q1, lambda_k1, lambda_q2, lambda_k2, subln_w, conv_w, conv_b, dt_bias, a_log, d_skip,
           ssm_norm_w, w_out, norm_ffn_w, w_router_group, w_router_expert, w_gate, w_up, w_down, final_norm_w):
    depth = w_in.shape[0]
    assert depth == 1 and x_prompt.shape[0] == 1 and x_sample.shape[1] == 1
    seq = x_prompt.shape[1]
    B = x_sample.shape[0]
    n_pages = page_table.shape[1]
    n_phys = cache_k.shape[1]
    past = n_pages * cache_k.shape[2]
    assert cache_k.shape[2] == PAGE and seq % TM == 0 and B == HALF and FRONT == HALF
    n_t = FRONT + seq
    R = n_t + B
    l = 0
    lam_init = 0.8 - 0.6 * math.exp(-0.3 * l)

    front = jnp.concatenate([jnp.zeros((PAD_ROWS, D_MODEL), F32), meta_tokens.astype(F32)], axis=0)
    xp, xs = x_prompt[0], x_sample[:, 0]
    w = w_in[l]
    wq = w[:, :OFF_K].astype(BF16)
    wk = w[:, OFF_K:OFF_V].astype(BF16)
    wv = w[:, OFF_V:OFF_Z].astype(BF16)
    wz = w[:, OFF_Z:OFF_XBC].astype(BF16)
    wx = w[:, OFF_XBC:OFF_DT].astype(BF16)
    wdt = jnp.pad(w[:, OFF_DT:], ((0, 0), (0, LANES - N_HEADS_S))).astype(BF16)

    pos = jnp.concatenate([jnp.maximum(jnp.arange(n_t, dtype=I32) - PAD_ROWS, 0),
                           jnp.full((B,), past, I32)])
    inv = ROPE_THETA ** (-jnp.arange(0, ROT_DIM, 2, dtype=F32) / ROT_DIM)
    ang = pos.astype(F32)[:, None] * inv[None, :]
    cos, sin = jnp.cos(ang), jnp.sin(ang)
    half = ROT_DIM // 2
    ones = jnp.ones((R, DK - ROT_DIM), F32)
    zeros = jnp.zeros((R, DK - ROT_DIM), F32)
    zh = jnp.zeros((R, half), F32)
    ctab = jnp.tile(jnp.concatenate([cos, cos, ones], axis=1), (1, 2))
    s1tab = jnp.tile(jnp.concatenate([-sin, zh, zeros], axis=1), (1, 2))
    s2tab = jnp.tile(jnp.concatenate([zh, sin, zeros], axis=1), (1, 2))

    row1 = lambda v: v.reshape(1, -1).astype(F32)
    pad_heads = lambda v: jnp.pad(v.astype(F32), (0, LANES - N_HEADS_S)).reshape(1, LANES)
    a_row = pad_heads(-jnp.exp(a_log[l].astype(F32)))
    dtb = pad_heads(dt_bias[l])
    dskip_full = jnp.repeat(d_skip[l].astype(F32), SSM_HEAD_DIM).reshape(1, D_INNER)
    normw = row1(ssm_norm_w[l])
    head_of_col = jnp.arange(D_INNER, dtype=I32) // SSM_HEAD_DIM
    e_mat = (jnp.arange(LANES, dtype=I32)[:, None] == head_of_col[None, :]).astype(BF16)
    tri = (jnp.arange(CHUNK)[:, None] >= jnp.arange(CHUNK)[None, :]).astype(BF16)
    lq1, lk1, lq2, lk2 = (row1(v[l]) for v in (lambda_q1, lambda_k1, lambda_q2, lambda_k2))
    sub_w = row1(subln_w[l])

    q, k, v, kb, vb, z, xbc, dt_raw = _inproj(front, xp, xs, row1(norm_mix_w[l]), wq, wk, wv, wz, wx, wdt,
                                              ctab, s1tab, s2tab)

    k_s, v_s = k[n_t * N_KV_HEADS:], v[n_t * N_KV_HEADS:]
    pad8 = lambda a: jnp.pad(a.reshape(B, N_KV_HEADS, LANES), ((0, 0), (0, LANES - N_KV_HEADS), (0, 0)))
    att_all, att_s = _attention(page_table, q, kb, vb, q[n_t:].reshape(B, N_HEADS_A, LANES), pad8(k_s), pad8(v_s),
                                cache_k[l].reshape(n_phys * PAGE * N_KV_HEADS, LANES),
                                cache_v[l].reshape(n_phys * PAGE * N_KV_HEADS, LANES),
                                lq1, lk1, lq2, lk2, sub_w, lam_init)
    ssm_p, state_p = _prompt_ssd(xbc, z, dt_raw, conv_w[l].astype(F32), row1(conv_b[l]), dtb, a_row,
                                 dskip_full, normw, e_mat, e_mat.T.astype(F32), tri, n_t)

    sc = state_conv[l]
    xbc_s = xbc[n_t:]
    ssm_s, state_s = _decode_ssm(xbc_s, sc[:, 0], sc[:, 1], sc[:, 2], dt_raw[n_t:], z[n_t:],
                                 state_ssm[l].reshape(B, D_INNER, D_STATE), conv_w[l].astype(F32),
                                 row1(conv_b[l]), dtb, a_row, dskip_full, normw, e_mat)

    att = lax.dynamic_update_slice(att_all, att_s.reshape(B, N_HEADS_A * DV), (n_t, 0))

    wo = w_out[l].astype(BF16)
    wr = jnp.pad(jnp.concatenate([w_router_group[l], w_router_expert[l]], axis=1).astype(F32),
                 ((0, 0), (0, LANES - N_EGROUPS - N_EXPERTS)))
    wr_hi = wr.astype(BF16)
    wr_lo = (wr - wr_hi.astype(F32)).astype(BF16)
    h1, u2, route = _outproj(front, xp, xs, att, ssm_p, ssm_s, wo[:N_HEADS_A * DV], wo[N_HEADS_A * DV:],
                             row1(norm_ffn_w[l]), wr_hi, wr_lo)

    wts, dest, buf_tok, block_e, next_e, n_real, n_valid = _route(route)
    yb = _moe(block_e, next_e, n_real, n_valid, buf_tok, u2, w_gate[l], w_up[l], w_down[l])
    y_p, y_s = _final(h1, yb, dest, wts, row1(final_norm_w), seq)

    t_real = N_META + seq
    y_prompt = y_p.reshape(1, seq, D_MODEL)
    y_sample = y_s.reshape(B, 1, D_MODEL)
    k_prompt = k[PAD_ROWS * N_KV_HEADS:n_t * N_KV_HEADS].reshape(1, 1, t_real, N_KV_HEADS, 2 * DK)
    v_prompt = v[PAD_ROWS * N_KV_HEADS:n_t * N_KV_HEADS].reshape(1, 1, t_real, N_KV_HEADS, DV)
    conv_prompt = xbc[n_t - (CONV_W - 1):n_t].reshape(1, 1, CONV_W - 1, XBC_DIM)
    ssm_prompt = state_p.reshape(1, 1, N_HEADS_S, SSM_HEAD_DIM, D_STATE)
    k_sample = k_s.reshape(1, B, 1, N_KV_HEADS, 2 * DK)
    v_sample = v_s.reshape(1, B, 1, N_KV_HEADS, DV)
    conv_sample = jnp.stack([sc[:, 1], sc[:, 2], xbc_s], axis=1)[None]
    ssm_sample = state_s.reshape(1, B, N_HEADS_S, SSM_HEAD_DIM, D_STATE)
    return (y_prompt, y_sample, k_prompt, v_prompt, conv_prompt, ssm_prompt,
            k_sample, v_sample, conv_sample, ssm_sample)
```

```python
import functools
import math

import jax
import jax.numpy as jnp
from jax import lax
from jax.experimental import pallas as pl
from jax.experimental.pallas import tpu as pltpu

F32 = jnp.float32
BF16 = jnp.bfloat16
I32 = jnp.int32

D_MODEL = 2048
N_META = 16
EPS = 1e-6
DV = 128
DK = 64
N_HEADS_A = 8
N_KV_HEADS = 4
ROT_DIM = 16
ROPE_THETA = 500000.0
NEG_INF = -1e30
LOG2E = 1.4426950408889634
D_INNER = 1024
SSM_HEAD_DIM = 64
N_HEADS_S = 16
N_BC_GROUPS = 2
D_STATE = 128
CONV_W = 4
XBC_DIM = D_INNER + 2 * N_BC_GROUPS * D_STATE
CHUNK = 128
Q_DIM = N_HEADS_A * 2 * DK
K_DIM = N_KV_HEADS * 2 * DK
V_DIM = N_KV_HEADS * DV
OFF_K = Q_DIM
OFF_V = OFF_K + K_DIM
OFF_Z = OFF_V + V_DIM
OFF_XBC = OFF_Z + D_INNER
OFF_DT = OFF_XBC + XBC_DIM
IN_DIM = OFF_DT + N_HEADS_S
N_EGROUPS = 4
EXPERTS_PER_GROUP = 8
N_EXPERTS = N_EGROUPS * EXPERTS_PER_GROUP
TOP_K = 2
D_FF = 1024
PAGE = 128

LANES = 128
VMEM_LIMIT = 56 * 1024 * 1024

FRONT = CHUNK
PAD_ROWS = FRONT - N_META
TM = 256
TQ = 256
TKF = 1024
TB = 8
TMOE = 256
TSMALL = 128
TF = 256


def _cparams(sem):
    return pltpu.CompilerParams(dimension_semantics=sem, vmem_limit_bytes=VMEM_LIMIT)


def _resident(shape):
    nd = len(shape)
    return pl.BlockSpec(shape, lambda *_: (0,) * nd, pipeline_mode=pl.Buffered(1))


def _split2(x):
    hi = x.astype(BF16)
    lo = (x - hi.astype(F32)).astype(BF16)
    return hi, lo


def _split3(x):
    hi = x.astype(BF16)
    r = x - hi.astype(F32)
    mid = r.astype(BF16)
    lo = (r - mid.astype(F32)).astype(BF16)
    return hi, mid, lo


def _silu(x):
    return x * (1.0 / (1.0 + jnp.exp(-x)))


def _softplus(x):
    return jnp.maximum(x, 0.0) + jnp.log(1.0 + jnp.exp(-jnp.abs(x)))


HALF = TM // 2


def _token_specs(seq):
    nb = seq // HALF
    return [pl.BlockSpec((HALF, D_MODEL), lambda i: (0, 0)),
            pl.BlockSpec((HALF, D_MODEL), lambda i: (jnp.clip(2 * i - 1, 0, nb - 1), 0)),
            pl.BlockSpec((HALF, D_MODEL), lambda i: (jnp.clip(2 * i, 0, nb - 1), 0)),
            pl.BlockSpec((HALF, D_MODEL), lambda i: (0, 0))]


def _token_rows(front_ref, xa_ref, xb_ref, xs_ref):
    i = pl.program_id(0)
    top = jnp.where(i == 0, front_ref[...], xa_ref[...])
    bot = jnp.where(i == pl.num_programs(0) - 1, xs_ref[...], xb_ref[...])
    return jnp.concatenate([top, bot], axis=0)


def _inproj_kernel(front_ref, xa_ref, xb_ref, xs_ref, nw_ref, wq_ref, wk_ref, wv_ref, wz_ref, wx_ref, wdt_ref,
                   c_ref, s1_ref, s2_ref,
                   q_ref, k_ref, v_ref, kb_ref, vb_ref, z_ref, xbc_ref, dt_ref):
    x = _token_rows(front_ref, xa_ref, xb_ref, xs_ref)
    ms = jnp.mean(x * x, axis=-1, keepdims=True)
    u = (x * lax.rsqrt(ms + EPS) * nw_ref[...]).astype(BF16)
    c, s1, s2 = c_ref[...], s1_ref[...], s2_ref[...]

    def rope(p):
        outs = []
        for h in range(p.shape[1] // LANES):
            xh = p[:, h * LANES:(h + 1) * LANES]
            outs.append(xh * c + pltpu.roll(xh, LANES - ROT_DIM // 2, 1) * s1
                        + pltpu.roll(xh, ROT_DIM // 2, 1) * s2)
        return jnp.concatenate(outs, axis=1)

    q = rope(jnp.dot(u, wq_ref[...], preferred_element_type=F32))
    q_ref[...] = (q * (DK ** -0.5 * LOG2E)).astype(BF16)
    k = rope(jnp.dot(u, wk_ref[...], preferred_element_type=F32))
    kb_ref[...] = k.astype(BF16)
    v = jnp.dot(u, wv_ref[...], preferred_element_type=F32)
    for g in range(N_KV_HEADS):
        k_ref[pl.ds(g, TM, stride=N_KV_HEADS), :] = k[:, g * LANES:(g + 1) * LANES]
        v_ref[pl.ds(g, TM, stride=N_KV_HEADS), :] = v[:, g * LANES:(g + 1) * LANES]
    vb = v.astype(BF16)
    ones = jnp.ones((vb.shape[0], DV), BF16)
    vb_ref[...] = jnp.concatenate(
        [piece for g in range(N_KV_HEADS) for piece in (vb[:, g * DV:(g + 1) * DV], ones)], axis=1)
    z_ref[...] = jnp.dot(u, wz_ref[...], preferred_element_type=F32)
    xbc_ref[...] = jnp.dot(u, wx_ref[...], preferred_element_type=F32)
    dt_ref[...] = jnp.dot(u, wdt_ref[...], preferred_element_type=F32)


def _inproj(front, xp, xs, norm_w, wq, wk, wv, wz, wx, wdt, ctab, s1tab, s2tab):
    R = front.shape[0] + xp.shape[0] + xs.shape[0]
    row = lambda n: pl.BlockSpec((TM, n), lambda i: (i, 0))
    head_rows = pl.BlockSpec((TM * N_KV_HEADS, LANES), lambda i: (i, 0))
    return pl.pallas_call(
        _inproj_kernel,
        grid=(R // TM,),
        in_specs=_token_specs(xp.shape[0]) + [_resident((1, D_MODEL)),
                  _resident(wq.shape), _resident(wk.shape), _resident(wv.shape),
                  _resident(wz.shape), _resident(wx.shape), _resident(wdt.shape),
                  row(LANES), row(LANES), row(LANES)],
        out_specs=[row(Q_DIM), head_rows, head_rows, row(K_DIM), row(2 * V_DIM),
                   row(D_INNER), row(XBC_DIM), row(LANES)],
        out_shape=[jax.ShapeDtypeStruct((R, Q_DIM), BF16),
                   jax.ShapeDtypeStruct((R * N_KV_HEADS, LANES), F32),
                   jax.ShapeDtypeStruct((R * N_KV_HEADS, LANES), F32),
                   jax.ShapeDtypeStruct((R, K_DIM), BF16),
                   jax.ShapeDtypeStruct((R, 2 * V_DIM), BF16),
                   jax.ShapeDtypeStruct((R, D_INNER), F32),
                   jax.ShapeDtypeStruct((R, XBC_DIM), F32),
                   jax.ShapeDtypeStruct((R, LANES), F32)],
        compiler_params=_cparams(("arbitrary",)),
        name="inproj",
    )(front, xp, xp, xs, norm_w, wq, wk, wv, wz, wx, wdt, ctab, s1tab, s2tab)


def _diff_lambda(lq1, lk1, lq2, lk2, lam_init):
    a = jnp.sum(lq1[...] * lk1[...], axis=-1, keepdims=True)
    b = jnp.sum(lq2[...] * lk2[...], axis=-1, keepdims=True)
    return jnp.exp(a) - jnp.exp(b) + lam_init


def _subln(o, sub_w, lam_init):
    ms = jnp.mean(o * o, axis=-1, keepdims=True)
    return (o * lax.rsqrt(ms + EPS) * sub_w) * (1.0 - lam_init)


def _attn_kernel(lq1, lk1, lq2, lk2, sub_ref, q_ref, k_ref, ve_ref, o_ref,
                 qs_scr, m_scr, acc_scr, *, lam_init, n_rows):
    qi = pl.program_id(1)
    lam = _diff_lambda(lq1, lk1, lq2, lk2, lam_init)

    q = q_ref[...]
    lane = lax.broadcasted_iota(I32, (TQ, LANES), 1)
    first = lane < DK
    zero = jnp.zeros((TQ, LANES), BF16)
    for hh in range(2):
        qh = q[:, hh * LANES:(hh + 1) * LANES]
        qs_scr[(2 * hh) * TQ:(2 * hh + 1) * TQ, :] = jnp.where(first, qh, zero)
        qs_scr[(2 * hh + 1) * TQ:(2 * hh + 2) * TQ, :] = jnp.where(first, zero, qh)
    m_scr[...] = jnp.full(m_scr.shape, NEG_INF, F32)
    acc_scr[...] = jnp.zeros(acc_scr.shape, F32)

    nt = (((1,), (1,)), ((), ()))

    def process(start, width, keep):
        for part in range(4):
            rows = slice(part * TQ, (part + 1) * TQ)
            s = lax.dot_general(qs_scr[rows, :], k_ref[pl.ds(start, width), :], nt,
                                preferred_element_type=F32)
            if keep is not None:
                s = jnp.where(keep, s, NEG_INF)
            m_prev = m_scr[rows, :]
            m_new = jnp.maximum(m_prev, jnp.max(s, axis=-1, keepdims=True))
            alpha = jnp.exp2(m_prev - m_new)
            p = jnp.concatenate([jnp.exp2(s[:, t * LANES:(t + 1) * LANES] - m_new)
                                 for t in range(width // LANES)], axis=1).astype(BF16)
            pv = jnp.dot(p, ve_ref[pl.ds(start, width), :], preferred_element_type=F32)
            acc_scr[rows, :LANES] = alpha * acc_scr[rows, :LANES] + pv[:, :LANES]
            acc_scr[rows, LANES:] = alpha * acc_scr[rows, LANES:] + pv[:, LANES:]
            m_scr[rows, :] = m_new

    q_lo = qi * TQ
    n_full = jnp.maximum(q_lo - PAD_ROWS, 0) // TKF
    tail_lo = PAD_ROWS + n_full * TKF
    n_tail = (q_lo + TQ - tail_lo + TQ - 1) // TQ

    def full(c):
        process(pl.multiple_of(PAD_ROWS + c * TKF, 16), TKF, None)

    def full_pair(c2, carry):
        full(2 * c2)
        full(2 * c2 + 1)
        return carry

    lax.fori_loop(0, n_full // 2, full_pair, 0)

    @pl.when(n_full % 2 == 1)
    def _():
        full(n_full - 1)

    def tail(w):
        width = w * TQ

        def run():
            start = pl.multiple_of(q_lo + TQ - width, 16)
            col = lax.broadcasted_iota(I32, (TQ, width), 1)
            qpos = q_lo + lax.broadcasted_iota(I32, (TQ, width), 0)
            process(start, width, (col <= qpos - start) & (col >= tail_lo - start))

        return run

    for w in range(1, TKF // TQ + 2):
        pl.when(n_tail == w)(tail(w))

    acc = acc_scr[...]
    o = acc[:, :LANES] / acc[:, LANES:]
    sub_w = sub_ref[...]
    for hh in range(2):
        o1 = o[(2 * hh) * TQ:(2 * hh + 1) * TQ, :]
        o2 = o[(2 * hh + 1) * TQ:(2 * hh + 2) * TQ, :]
        o_ref[:, hh * LANES:(hh + 1) * LANES] = _subln(o1 - lam * o2, sub_w, lam_init).astype(BF16)


def _expand_heads(x, e_ref):
    hi, lo = _split2(x)
    e = e_ref[...]
    return (jnp.dot(hi, e, preferred_element_type=F32)
            + jnp.dot(lo, e, preferred_element_type=F32))


def _gated_out(y, xs, z, dskip_full, normw):
    y = (y + dskip_full * xs) * _silu(z)
    half = D_INNER // N_BC_GROUPS
    outs = []
    for g in range(N_BC_GROUPS):
        yg = y[:, g * half:(g + 1) * half]
        ms = jnp.mean(yg * yg, axis=-1, keepdims=True)
        outs.append(yg * lax.rsqrt(ms + EPS))
    return jnp.concatenate(outs, axis=1) * normw


def _ssd_kernel(xbc_ref, z_ref, dt_ref, cw_ref, cb_ref, dtb_ref, a_ref, dskip_ref, normw_ref,
                e_ref, et_ref, tri_ref, ssm_ref, state_ref, ext_scr, h_scr):
    c = pl.program_id(0)
    L = CHUNK

    @pl.when(c == 0)
    def _():
        ext_scr[0:8, :] = jnp.zeros((8, XBC_DIM), F32)
        h_scr[...] = jnp.zeros(h_scr.shape, F32)

    x_new = xbc_ref[...]
    ext_scr[8:8 + L, :] = x_new
    conv = cb_ref[...] + cw_ref[CONV_W - 1:CONV_W, :] * x_new
    for j in range(CONV_W - 1):
        sh = CONV_W - 1 - j
        conv = conv + cw_ref[j:j + 1, :] * ext_scr[8 - sh:8 - sh + L, :]
    ext_scr[0:8, :] = x_new[L - 8:L, :]
    xc = _silu(conv)
    xs = xc[:, :D_INNER]

    rowi = lax.broadcasted_iota(I32, (L, LANES), 0)
    coli = lax.broadcasted_iota(I32, (L, LANES), 1)
    valid = (c * L + rowi) >= PAD_ROWS
    dt = jnp.where(valid, _softplus(dt_ref[...] + dtb_ref[...]), 0.0)
    da = dt * a_ref[...]
    tri = tri_ref[...]
    acs = sum(jnp.dot(tri, part, preferred_element_type=F32) for part in _split3(da))
    acs_t = acs.T
    dt_t = dt.T
    causal = coli <= rowi
    last = acs[L - 1:L, :]

    w_state = _expand_heads(dt * jnp.exp(last - acs), e_ref)
    dec_out = _expand_heads(jnp.exp(acs), e_ref)
    dec_rows = jnp.sum(et_ref[...] * jnp.exp(last), axis=-1, keepdims=True)
    xw = xs * w_state
    xs_b = xs.astype(BF16)

    half = D_INNER // N_BC_GROUPS
    hpg = N_HEADS_S // N_BC_GROUPS
    y_parts = []
    for g in range(N_BC_GROUPS):
        bm = xc[:, D_INNER + g * D_STATE:D_INNER + (g + 1) * D_STATE].astype(BF16)
        cm = xc[:, D_INNER + (N_BC_GROUPS + g) * D_STATE:
                D_INNER + (N_BC_GROUPS + g + 1) * D_STATE].astype(BF16)
        cb = lax.dot_general(cm, bm, (((1,), (1,)), ((), ())), preferred_element_type=F32)
        h_prev = h_scr[g * half:(g + 1) * half, :]
        y_off = lax.dot_general(cm, h_prev.astype(BF16), (((1,), (1,)), ((), ())),
                                preferred_element_type=F32)
        y_off = y_off * dec_out[:, g * half:(g + 1) * half]
        yd = []
        for r in range(0, hpg, 2):
            pair = []
            for hh in (r, r + 1):
                h = g * hpg + hh
                seg = jnp.broadcast_to(acs[:, h:h + 1], (L, L)) - acs_t[h:h + 1, :]
                lmat = jnp.where(causal, jnp.exp(jnp.where(causal, seg, 0.0)), 0.0)
                pair.append((cb * lmat * dt_t[h:h + 1, :]).astype(BF16))
            h0 = g * hpg + r
            xp = xs_b[:, h0 * SSM_HEAD_DIM:(h0 + 2) * SSM_HEAD_DIM]
            ya = jnp.dot(pair[0], xp, preferred_element_type=F32)
            yb = jnp.dot(pair[1], xp, preferred_element_type=F32)
            yd.append(jnp.where(coli < SSM_HEAD_DIM, ya, yb))
        y_parts.append(jnp.concatenate(yd, axis=1) + y_off)
        xw_t = xw[:, g * half:(g + 1) * half].T.astype(BF16)
        st = jnp.dot(xw_t, bm, preferred_element_type=F32)
        h_scr[g * half:(g + 1) * half, :] = dec_rows[g * half:(g + 1) * half, :] * h_prev + st
    y = jnp.concatenate(y_parts, axis=1)

    ssm_ref[...] = _gated_out(y, xs, z_ref[...], dskip_ref[...], normw_ref[...]).astype(BF16)

    @pl.when(c == pl.num_programs(0) - 1)
    def _():
        state_ref[...] = h_scr[...]


def _prompt_ssd(xbc, z, dt_raw, conv_w, conv_b, dtb, a_row, dskip_full, normw, e_mat, e_t, tri, n_t):
    row = lambda n: pl.BlockSpec((CHUNK, n), lambda c: (c, 0))
    const = lambda shp: pl.BlockSpec(shp, lambda c: (0, 0))
    return pl.pallas_call(
        _ssd_kernel,
        grid=(n_t // CHUNK,),
        in_specs=[row(XBC_DIM), row(D_INNER), row(LANES),
                  const((CONV_W, XBC_DIM)), const((1, XBC_DIM)), const((1, LANES)), const((1, LANES)),
                  const((1, D_INNER)), const((1, D_INNER)), const((LANES, D_INNER)),
                  const((D_INNER, LANES)), const((CHUNK, CHUNK))],
        out_specs=[row(D_INNER), pl.BlockSpec((D_INNER, D_STATE), lambda c: (0, 0))],
        out_shape=[jax.ShapeDtypeStruct((n_t, D_INNER), BF16),
                   jax.ShapeDtypeStruct((D_INNER, D_STATE), F32)],
        scratch_shapes=[pltpu.VMEM((8 + CHUNK, XBC_DIM), F32),
                        pltpu.VMEM((D_INNER, D_STATE), F32)],
        compiler_params=_cparams(("arbitrary",)),
        name="prompt_ssd",
    )(xbc, z, dt_raw, conv_w, conv_b, dtb, a_row, dskip_full, normw, e_mat, e_t, tri)


def _decode_attn_kernel(pt_ref, lq1, lk1, lq2, lk2, sub_ref, q_ref, kn_ref, vn_ref, *rest,
                        n_pages, lam_init):
    k_pages = rest[:n_pages]
    v_pages = rest[n_pages:2 * n_pages]
    o_ref = rest[2 * n_pages]
    del pt_ref
    lam = _diff_lambda(lq1, lk1, lq2, lk2, lam_init)
    nq = 2 * N_HEADS_A
    q8 = q_ref[...]
    lane = lax.broadcasted_iota(I32, (N_HEADS_A, LANES), 1)
    zero = jnp.zeros((N_HEADS_A, LANES), BF16)
    q16 = jnp.concatenate([jnp.where(lane < DK, q8, zero), jnp.where(lane < DK, zero, q8)], axis=0)

    pw = PAGE * N_KV_HEADS
    row_head = (lax.broadcasted_iota(I32, (nq, pw), 0) & (N_HEADS_A - 1)) >> 1
    col_head = lax.broadcasted_iota(I32, (nq, pw), 1) & (N_KV_HEADS - 1)
    own = row_head == col_head
    nt = (((1,), (1,)), ((), ()))
    s_pages = []
    for j in range(n_pages):
        kp = k_pages[j][...].astype(BF16)
        s = lax.dot_general(q16, kp, nt, preferred_element_type=F32)
        s_pages.append(jnp.where(own, s, NEG_INF))
    kn = kn_ref[...].astype(BF16)
    row_head_n = (lax.broadcasted_iota(I32, (nq, LANES), 0) & (N_HEADS_A - 1)) >> 1
    col_n = lax.broadcasted_iota(I32, (nq, LANES), 1)
    s_new = jnp.where(row_head_n == col_n, lax.dot_general(q16, kn, nt, preferred_element_type=F32), NEG_INF)

    m = jnp.max(s_new, axis=-1, keepdims=True)
    for s in s_pages:
        m = jnp.maximum(m, jnp.max(s, axis=-1, keepdims=True))
    p_new = jnp.exp2(s_new - m)
    l = jnp.sum(p_new, axis=-1, keepdims=True)
    acc = jnp.dot(p_new.astype(BF16), vn_ref[...].astype(BF16), preferred_element_type=F32)
    for j in range(n_pages):
        p = jnp.exp2(s_pages[j] - m)
        l = l + jnp.sum(p, axis=-1, keepdims=True)
        acc = acc + jnp.dot(p.astype(BF16), v_pages[j][...].astype(BF16), preferred_element_type=F32)
    o = acc / l
    res = o[:N_HEADS_A, :] - lam * o[N_HEADS_A:, :]
    o_ref[...] = _subln(res, sub_ref[...], lam_init).astype(BF16)


def _fused_attn_kernel(pt_ref, lq1, lk1, lq2, lk2, sub_ref, q_ref, k_ref, ve_ref, qd_ref, kn_ref, vn_ref, *rest,
                       n_pages, n_dec, lam_init, n_rows):
    pages = rest[:2 * n_pages]
    o_ref, od_ref = rest[2 * n_pages], rest[2 * n_pages + 1]
    scratch = rest[2 * n_pages + 2:]

    @pl.when(pl.program_id(1) < n_rows // TQ)
    def _():
        _attn_kernel(lq1, lk1, lq2, lk2, sub_ref, q_ref, k_ref, ve_ref, o_ref, *scratch,
                     lam_init=lam_init, n_rows=n_rows)

    step = pl.program_id(0) * pl.num_programs(1) + pl.program_id(1)

    @pl.when(step < n_dec)
    def _():
        _decode_attn_kernel(pt_ref, lq1, lk1, lq2, lk2, sub_ref, qd_ref, kn_ref, vn_ref, *pages, od_ref,
                            n_pages=n_pages, lam_init=lam_init)


def _attention(page_table, q, kb, vbe, q_s, k_new, v_new, cache_k2, cache_v2, lq1, lk1, lq2, lk2, sub_w,
               lam_init):
    R = q.shape[0]
    nq = R // TQ
    B, n_pages = page_table.shape
    assert R % TQ == 0 and TKF % TQ == 0
    n_inner = max(nq, pl.cdiv(B, N_KV_HEADS))
    pw = PAGE * N_KV_HEADS
    seq_of = lambda g, i: jnp.minimum(g * n_inner + i, B - 1)
    qblk = lambda i: jnp.minimum(i, nq - 1)
    small = lambda n: pl.BlockSpec((1, n), lambda g, i, pt: (0, 0))
    per_b = pl.BlockSpec((None, 8, LANES), lambda g, i, pt: (seq_of(g, i), 0, 0))
    new_tok = pl.BlockSpec((None, LANES, LANES), lambda g, i, pt: (seq_of(g, i), 0, 0))
    page_spec = lambda j: pl.BlockSpec((pw, LANES), lambda g, i, pt: (pt[seq_of(g, i), j], 0))
    grid_spec = pltpu.PrefetchScalarGridSpec(
        num_scalar_prefetch=1,
        grid=(N_KV_HEADS, n_inner),
        in_specs=[small(DK), small(DK), small(DK), small(DK), small(DV),
                  pl.BlockSpec((TQ, 2 * LANES), lambda g, i, pt: (qblk(i), g)),
                  pl.BlockSpec((R, LANES), lambda g, i, pt: (0, g)),
                  pl.BlockSpec((R, 2 * LANES), lambda g, i, pt: (0, g)),
                  per_b, new_tok, new_tok]
                 + [page_spec(j) for j in range(n_pages)] + [page_spec(j) for j in range(n_pages)],
        out_specs=[pl.BlockSpec((TQ, 2 * LANES), lambda g, i, pt: (qblk(i), g)), per_b],
        scratch_shapes=[pltpu.VMEM((4 * TQ, LANES), BF16),
                        pltpu.VMEM((4 * TQ, LANES), F32),
                        pltpu.VMEM((4 * TQ, 2 * LANES), F32)],
    )
    return pl.pallas_call(
        functools.partial(_fused_attn_kernel, n_pages=n_pages, n_dec=B, lam_init=lam_init, n_rows=R),
        grid_spec=grid_spec,
        out_shape=[jax.ShapeDtypeStruct((R, N_HEADS_A * DV), BF16),
                   jax.ShapeDtypeStruct((B, 8, LANES), BF16)],
        compiler_params=_cparams(("arbitrary", "arbitrary")),
        name="attention",
    )(page_table, lq1, lk1, lq2, lk2, sub_w, q, kb, vbe, q_s, k_new, v_new,
      *([cache_k2] * n_pages), *([cache_v2] * n_pages))


def _dec_pre_kernel(xbc_ref, s0_ref, s1_ref, s2_ref, dt_ref, cw_ref, cb_ref, dtb_ref, a_ref, e_ref,
                    xc_ref, xdt_ref, dect_ref):
    conv = (cb_ref[...] + cw_ref[0:1, :] * s0_ref[...] + cw_ref[1:2, :] * s1_ref[...]
            + cw_ref[2:3, :] * s2_ref[...] + cw_ref[3:4, :] * xbc_ref[...])
    xc = _silu(conv)
    xc_ref[...] = xc
    dt = _softplus(dt_ref[...] + dtb_ref[...])
    dec = jnp.exp(dt * a_ref[...])
    hi, mid, lo = _split3(dt)
    e = e_ref[...]
    dt_full = (jnp.dot(hi, e, preferred_element_type=F32) + jnp.dot(mid, e, preferred_element_type=F32)
               + jnp.dot(lo, e, preferred_element_type=F32))
    hi, mid, lo = _split3(dec)
    dec_full = (jnp.dot(hi, e, preferred_element_type=F32) + jnp.dot(mid, e, preferred_element_type=F32)
                + jnp.dot(lo, e, preferred_element_type=F32))
    xdt_ref[...] = (dt_full * xc[:, :D_INNER]).T
    dect_ref[...] = dec_full.T


def _dec_state_kernel(h_ref, xdt_ref, dect_ref, bm_ref, cm_ref, hout_ref, y_ref):
    i = pl.program_id(0)
    B = xdt_ref.shape[1]
    lane = lax.broadcasted_iota(I32, (D_INNER, B), 1)
    sub = lax.broadcasted_iota(I32, (TB, D_INNER // N_BC_GROUPS), 0)
    half = D_INNER // N_BC_GROUPS
    nt = (((1,), (1,)), ((), ()))
    y_acc = [jnp.zeros((TB, half), F32) for _ in range(N_BC_GROUPS)]
    start = pl.multiple_of(i * TB, TB)
    for j in range(TB):
        b = i * TB + j
        sel = lane == b
        xcol = jnp.sum(jnp.where(sel, xdt_ref[...], 0.0), axis=-1, keepdims=True)
        dcol = jnp.sum(jnp.where(sel, dect_ref[...], 0.0), axis=-1, keepdims=True)
        hnew = dcol * h_ref[j]
        for g in range(N_BC_GROUPS):
            brow = bm_ref[pl.ds(start, TB), g * D_STATE:(g + 1) * D_STATE][j:j + 1, :]
            hg = hnew[g * half:(g + 1) * half] + xcol[g * half:(g + 1) * half] * brow
            hout_ref[j, g * half:(g + 1) * half, :] = hg
            c8 = cm_ref[pl.ds(start, TB), g * D_STATE:(g + 1) * D_STATE].astype(BF16)
            yj = lax.dot_general(c8, hg.astype(BF16), nt, preferred_element_type=F32)
            y_acc[g] = y_acc[g] + jnp.where(sub == j, yj, 0.0)
    y_ref[...] = jnp.concatenate(y_acc, axis=1)


def _dec_post_kernel(y_ref, xc_ref, z_ref, dskip_ref, normw_ref, o_ref):
    o_ref[...] = _gated_out(y_ref[...], xc_ref[:, :D_INNER], z_ref[...], dskip_ref[...],
                            normw_ref[...]).astype(BF16)


def _decode_ssm(xbc_s, sc0, sc1, sc2, dt_s, z_s, state, conv_w, conv_b, dtb, a_row, dskip_full, normw, e_mat):
    B = xbc_s.shape[0]
    xc, xdt_t, dec_t = pl.pallas_call(
        _dec_pre_kernel,
        out_shape=[jax.ShapeDtypeStruct((B, XBC_DIM), F32),
                   jax.ShapeDtypeStruct((D_INNER, B), F32),
                   jax.ShapeDtypeStruct((D_INNER, B), F32)],
        compiler_params=pltpu.CompilerParams(vmem_limit_bytes=VMEM_LIMIT),
        name="decode_ssm_pre",
    )(xbc_s, sc0, sc1, sc2, dt_s, conv_w, conv_b, dtb, a_row, e_mat)
    bm = xc[:, D_INNER:D_INNER + N_BC_GROUPS * D_STATE]
    cm = xc[:, D_INNER + N_BC_GROUPS * D_STATE:]
    const = lambda shp: pl.BlockSpec(shp, lambda i: (0,) * len(shp))
    state_spec = pl.BlockSpec((TB, D_INNER, D_STATE), lambda i: (i, 0, 0))
    h_new, y = pl.pallas_call(
        _dec_state_kernel,
        grid=(B // TB,),
        in_specs=[state_spec, const((D_INNER, B)), const((D_INNER, B)),
                  const((B, N_BC_GROUPS * D_STATE)), const((B, N_BC_GROUPS * D_STATE))],
        out_specs=[state_spec, pl.BlockSpec((TB, D_INNER), lambda i: (i, 0))],
        out_shape=[jax.ShapeDtypeStruct((B, D_INNER, D_STATE), F32),
                   jax.ShapeDtypeStruct((B, D_INNER), F32)],
        compiler_params=_cparams(("arbitrary",)),
        name="decode_ssm_state",
    )(state, xdt_t, dec_t, bm, cm)
    ssm = pl.pallas_call(
        _dec_post_kernel,
        out_shape=jax.ShapeDtypeStruct((B, D_INNER), BF16),
        compiler_params=pltpu.CompilerParams(vmem_limit_bytes=VMEM_LIMIT),
        name="decode_ssm_post",
    )(y, xc, z_s, dskip_full, normw)
    return ssm, h_new


def _outproj_kernel(front_ref, xa_ref, xb_ref, xs_ref, att_ref, sa_ref, sb_ref, ss_ref,
                    wa_ref, ws_ref, nw_ref, wrh_ref, wrl_ref, h_ref, u_ref, lg_ref):
    last = pl.program_id(0) == pl.num_programs(0) - 1
    ssm = jnp.concatenate([sa_ref[...], jnp.where(last, ss_ref[...], sb_ref[...])], axis=0)
    acc = jnp.dot(att_ref[...], wa_ref[...], preferred_element_type=F32)
    acc = acc + jnp.dot(ssm, ws_ref[...], preferred_element_type=F32)
    h = _token_rows(front_ref, xa_ref, xb_ref, xs_ref) + acc
    h_ref[...] = h
    ms = jnp.mean(h * h, axis=-1, keepdims=True)
    u = h * lax.rsqrt(ms + EPS) * nw_ref[...]
    u_ref[...] = u
    uh, ul = _split2(u)
    wh, wl = wrh_ref[...], wrl_ref[...]
    lg = (jnp.dot(uh, wh, preferred_element_type=F32) + jnp.dot(uh, wl, preferred_element_type=F32)
          + jnp.dot(ul, wh, preferred_element_type=F32))
    lg_ref[...] = _route_rows(lg, pl.program_id(0) * TM)


def _first_max(x, lane):
    m = jnp.max(x, axis=-1, keepdims=True)
    return m, jnp.min(jnp.where(x == m, lane, float(LANES)), axis=-1, keepdims=True)


def _route_rows(lg, row0):
    rows = lg.shape[0]
    lane_i = lax.broadcasted_iota(I32, (rows, LANES), 1)
    lane = lane_i.astype(F32)
    gl = jnp.where(lane_i < N_EGROUPS, lg, -jnp.inf)
    gmax, g = _first_max(gl, lane)
    pg = 1.0 / jnp.sum(jnp.exp(gl - gmax), axis=-1, keepdims=True)
    lo = N_EGROUPS + g * EXPERTS_PER_GROUP
    el = jnp.where((lane >= lo) & (lane < lo + EXPERTS_PER_GROUP), lg, -jnp.inf)
    m1, i1 = _first_max(el, lane)
    m2, i2 = _first_max(jnp.where(lane == i1, -jnp.inf, el), lane)
    z = jnp.sum(jnp.exp(el - m1), axis=-1, keepdims=True)
    p1 = 1.0 / z
    p2 = jnp.exp(m2 - m1) / z
    tot = p1 + p2
    valid = (row0 + lax.broadcasted_iota(I32, (rows, 1), 0)) >= PAD_ROWS
    e1 = jnp.where(valid, i1 - N_EGROUPS, -1.0)
    e2 = jnp.where(valid, i2 - N_EGROUPS, -1.0)
    w1 = jnp.where(valid, p1 / tot * pg, 0.0)
    w2 = jnp.where(valid, p2 / tot * pg, 0.0)
    return jnp.where(lane_i == 0, e1, jnp.where(lane_i == 1, e2, jnp.where(lane_i == 2, w1,
                                                                         jnp.where(lane_i == 3, w2, 0.0))))


def _rank_kernel(route_ref, tri_ref, utri_ref, rank_ref, meta_ref, carry, *, nblk):
    i = pl.program_id(0)

    @pl.when(i == 0)
    def _():
        carry[...] = jnp.zeros(carry.shape, F32)

    r = route_ref[...]
    lane = lax.broadcasted_iota(I32, r.shape, 1).astype(F32)
    oh0 = (lane == r[:, 0:1]).astype(F32)
    oh1 = (lane == r[:, 1:2]).astype(F32)
    oh = oh0 + oh1
    before = jnp.dot(tri_ref[...], oh.astype(BF16), preferred_element_type=F32) + carry[0:1, :]
    rank0 = jnp.sum(before * oh0, axis=-1, keepdims=True)
    rank1 = jnp.sum(before * oh1, axis=-1, keepdims=True)
    lane_i = lax.broadcasted_iota(I32, r.shape, 1)
    rank_ref[...] = jnp.where(lane_i == 0, rank0, jnp.where(lane_i == 1, rank1, 0.0))
    carry[0:1, :] = carry[0:1, :] + jnp.sum(oh, axis=0, keepdims=True)

    @pl.when(i == pl.num_programs(0) - 1)
    def _():
        nb_pad = meta_ref.shape[0] - 8
        counts = carry[0:1, :]
        pb = jnp.floor((counts + (TMOE - 1)) / TMOE)
        ends = jnp.dot(jnp.broadcast_to(pb, (8, LANES)).astype(BF16), utri_ref[...],
                       preferred_element_type=F32)[0:1, :]
        starts = ends - pb
        lane1 = lax.broadcasted_iota(I32, (1, LANES), 1)
        n_valid = jnp.sum(jnp.where(lane1 == N_EXPERTS - 1, ends, 0.0), axis=-1, keepdims=True)
        blk = lax.broadcasted_iota(I32, (nb_pad, LANES), 0).astype(F32)
        lane_b = lax.broadcasted_iota(I32, (nb_pad, LANES), 1)
        is_e = lane_b < N_EXPERTS
        block_e = jnp.minimum(jnp.sum(jnp.where(is_e & (ends <= blk), 1.0, 0.0), axis=-1, keepdims=True),
                              N_EXPERTS - 1.0)
        lane_bf = lane_b.astype(F32)
        has = is_e & (counts > 0.0) & (lane_bf > block_e)
        nxt = jnp.min(jnp.where(has, lane_bf, float(LANES)), axis=-1, keepdims=True)
        next_e = jnp.where(nxt >= LANES, -1.0, nxt)
        mine = lane_bf == block_e
        cnt_b = jnp.sum(jnp.where(mine, counts, 0.0), axis=-1, keepdims=True)
        st_b = jnp.sum(jnp.where(mine, starts, 0.0), axis=-1, keepdims=True)
        n_real = jnp.clip(cnt_b - (blk[:, 0:1] - st_b) * TMOE, 0.0, float(TMOE))
        meta_ref[0:8, :] = jnp.where(lax.broadcasted_iota(I32, (8, LANES), 0) == 0, starts,
                                     jnp.broadcast_to(n_valid, (8, LANES)))
        meta_ref[8:, :] = jnp.where(lane_b == 0, block_e, jnp.where(lane_b == 1, next_e,
                                                                    jnp.where(lane_b == 2, n_real, 0.0)))


def _rank(route, nblk):
    R = route.shape[0]
    nb_pad = (nblk + 7) // 8 * 8
    tri = (jnp.arange(TM)[:, None] > jnp.arange(TM)[None, :]).astype(BF16)
    utri = (jnp.arange(LANES)[:, None] <= jnp.arange(LANES)[None, :]).astype(BF16)
    return pl.pallas_call(
        functools.partial(_rank_kernel, nblk=nblk),
        grid=(R // TM,),
        in_specs=[pl.BlockSpec((TM, LANES), lambda i: (i, 0)),
                  pl.BlockSpec((TM, TM), lambda i: (0, 0)),
                  pl.BlockSpec((LANES, LANES), lambda i: (0, 0))],
        out_specs=[pl.BlockSpec((TM, LANES), lambda i: (i, 0)),
                   pl.BlockSpec((8 + nb_pad, LANES), lambda i: (0, 0))],
        out_shape=[jax.ShapeDtypeStruct((R, LANES), F32),
                   jax.ShapeDtypeStruct((8 + nb_pad, LANES), F32)],
        scratch_shapes=[pltpu.VMEM((8, LANES), F32)],
        compiler_params=_cparams(("arbitrary",)),
        name="expert_rank",
    )(route, tri, utri)


def _outproj(front, xp, xs, att, ssm_p, ssm_s, wa, ws, norm_w, wr_hi, wr_lo):
    R = front.shape[0] + xp.shape[0] + xs.shape[0]
    row = lambda n: pl.BlockSpec((TM, n), lambda i: (i, 0))
    nbp = ssm_p.shape[0] // HALF
    half = lambda f: pl.BlockSpec((HALF, D_INNER), f)
    return pl.pallas_call(
        _outproj_kernel,
        grid=(R // TM,),
        in_specs=_token_specs(xp.shape[0]) + [
            row(N_HEADS_A * DV),
            half(lambda i: (2 * i, 0)), half(lambda i: (jnp.minimum(2 * i + 1, nbp - 1), 0)),
            half(lambda i: (0, 0)),
            _resident(wa.shape), _resident(ws.shape), _resident((1, D_MODEL)),
            _resident(wr_hi.shape), _resident(wr_lo.shape)],
        out_specs=[row(D_MODEL), row(D_MODEL), row(LANES)],
        out_shape=[jax.ShapeDtypeStruct((R, D_MODEL), F32),
                   jax.ShapeDtypeStruct((R, D_MODEL), F32),
                   jax.ShapeDtypeStruct((R, LANES), F32)],
        compiler_params=_cparams(("arbitrary",)),
        name="outproj",
    )(front, xp, xp, xs, att, ssm_p, ssm_p, ssm_s, wa, ws, norm_w, wr_hi, wr_lo)


GATHER_UNROLL = 8

def _start_row_gather(idx_ref, src_hbm, dst, sem, n_groups=None):
    def start(g, carry):
        for j in range(GATHER_UNROLL):
            pltpu.make_async_copy(src_hbm.at[pl.ds(idx_ref[0, g * GATHER_UNROLL + j], 1)],
                                  dst.at[g, pl.ds(j, 1)], sem).start(priority=j % 2)
        return carry

    lax.fori_loop(0, dst.shape[0] if n_groups is None else n_groups, start, 0)


def _wait_row_gather(src_hbm, dst, sem, n_groups=None):
    def wait(g, carry):
        pltpu.make_async_copy(src_hbm.at[pl.ds(0, GATHER_UNROLL)], dst.at[0], sem).wait()
        return carry

    lax.fori_loop(0, dst.shape[0] if n_groups is None else n_groups, wait, 0)


def _row_groups(n_real):
    return (n_real + (GATHER_UNROLL - 1)) // GATHER_UNROLL


def _moe_kernel(be_ref, nx_ref, nr_ref, nv_ref, idx_ref, idx_next_ref, x_hbm, wg_hbm, wu_hbm, wd_hbm, o_ref,
                xbuf, stage_g, stage_u, stage_d, wg_b, wu_b, wd_b, sems, row_sems):
    i = pl.program_id(0)
    valid = i < nv_ref[0]
    e = be_ref[i]
    changed = (i == 0) | (be_ref[jnp.maximum(i - 1, 0)] != e)
    slot = i % 2
    n_real = nr_ref[i]

    @pl.when(i == 0)
    def _():
        xbuf[...] = jnp.zeros(xbuf.shape, F32)
        _start_row_gather(idx_ref, x_hbm, xbuf.at[0], row_sems.at[0], _row_groups(n_real))

    @pl.when(i + 1 < nv_ref[0])
    def _():
        _start_row_gather(idx_next_ref, x_hbm, xbuf.at[1 - slot], row_sems.at[1 - slot],
                          _row_groups(nr_ref[jnp.minimum(i + 1, pl.num_programs(0) - 1)]))

    def fetch(expert):
        return (pltpu.make_async_copy(wg_hbm.at[expert], stage_g, sems.at[0]),
                pltpu.make_async_copy(wu_hbm.at[expert], stage_u, sems.at[1]),
                pltpu.make_async_copy(wd_hbm.at[expert], stage_d, sems.at[2]))

    @pl.when(i == 0)
    def _():
        for c in fetch(e):
            c.start()

    @pl.when(valid & changed)
    def _():
        for c in fetch(e):
            c.wait()
        wg_b[...] = stage_g[...].astype(BF16)
        wu_b[...] = stage_u[...].astype(BF16)
        wd_b[...] = stage_d[...].astype(BF16)
        nxt = nx_ref[i]

        @pl.when(nxt >= 0)
        def _():
            for c in fetch(nxt):
                c.start()

    def mlp(rows):
        x = xbuf[slot, :rows // GATHER_UNROLL].reshape(rows, D_MODEL).astype(BF16)
        g = jnp.dot(x, wg_b[...], preferred_element_type=F32)
        u = jnp.dot(x, wu_b[...], preferred_element_type=F32)
        h = (_silu(g) * u).astype(BF16)
        o_ref[:rows, :] = jnp.dot(h, wd_b[...], preferred_element_type=F32)

    @pl.when(valid)
    def _():
        _wait_row_gather(x_hbm, xbuf.at[slot], row_sems.at[slot], _row_groups(n_real))

    @pl.when(valid & (n_real > TSMALL))
    def _():
        mlp(TMOE)

    @pl.when(valid & (n_real <= TSMALL))
    def _():
        mlp(TSMALL)
        o_ref[TSMALL:, :] = jnp.zeros((TMOE - TSMALL, D_MODEL), F32)

    @pl.when(jnp.logical_not(valid))
    def _():
        o_ref[...] = jnp.zeros(o_ref.shape, F32)


def _moe(block_e, next_e, n_real, n_valid, buf_tok, x_tok, w_gate, w_up, w_down):
    rows = buf_tok.shape[0]
    nblk = rows // TMOE

    def blk(i, nv):
        return jnp.minimum(i, nv[0] - 1)

    grid_spec = pltpu.PrefetchScalarGridSpec(
        num_scalar_prefetch=4,
        grid=(nblk,),
        in_specs=[pl.BlockSpec((None, 1, TMOE), lambda i, be, nx, nr, nv: (blk(i, nv), 0, 0),
                               memory_space=pltpu.SMEM),
                  pl.BlockSpec((None, 1, TMOE), lambda i, be, nx, nr, nv: (blk(i + 1, nv), 0, 0),
                               memory_space=pltpu.SMEM),
                  pl.BlockSpec(memory_space=pl.ANY), pl.BlockSpec(memory_space=pl.ANY),
                  pl.BlockSpec(memory_space=pl.ANY), pl.BlockSpec(memory_space=pl.ANY)],
        out_specs=pl.BlockSpec((TMOE, D_MODEL), lambda i, be, nx, nr, nv: (i, 0)),
        scratch_shapes=[pltpu.VMEM((2, TMOE // GATHER_UNROLL, GATHER_UNROLL, D_MODEL), F32),
                        pltpu.VMEM((D_MODEL, D_FF), F32), pltpu.VMEM((D_MODEL, D_FF), F32),
                        pltpu.VMEM((D_FF, D_MODEL), F32),
                        pltpu.VMEM((D_MODEL, D_FF), BF16), pltpu.VMEM((D_MODEL, D_FF), BF16),
                        pltpu.VMEM((D_FF, D_MODEL), BF16),
                        pltpu.SemaphoreType.DMA((3,)), pltpu.SemaphoreType.DMA((2,))],
    )
    slot_rows = buf_tok.reshape(nblk, 1, TMOE)
    return pl.pallas_call(
        _moe_kernel,
        grid_spec=grid_spec,
        out_shape=jax.ShapeDtypeStruct((rows, D_MODEL), F32),
        compiler_params=_cparams(("arbitrary",)),
        name="moe_experts",
    )(block_e, next_e, n_real, n_valid, slot_rows, slot_rows, x_tok, w_gate, w_up, w_down)


def _final_kernel(idx_ref, idx_next_ref, h_ref, w_ref, nw_ref, y_hbm, op_ref, os_ref, ybuf, row_sems):
    i = pl.program_id(0)
    last = pl.num_programs(0) - 1
    slot = i % 2

    @pl.when(i == 0)
    def _():
        _start_row_gather(idx_ref, y_hbm, ybuf.at[0], row_sems.at[0])

    @pl.when(i + 1 < pl.num_programs(0))
    def _():
        _start_row_gather(idx_next_ref, y_hbm, ybuf.at[1 - slot], row_sems.at[1 - slot])

    _wait_row_gather(y_hbm, ybuf.at[slot], row_sems.at[slot])
    w = w_ref[...]
    hg = HALF // GATHER_UNROLL
    moe = (ybuf[slot, :hg].reshape(HALF, D_MODEL) * w[:, 0:1]
           + ybuf[slot, hg:].reshape(HALF, D_MODEL) * w[:, 1:2])
    h = h_ref[...] + moe
    ms = jnp.mean(h * h, axis=-1, keepdims=True)
    out = h * lax.rsqrt(ms + EPS) * nw_ref[...]

    @pl.when(i < last)
    def _():
        op_ref[...] = out

    @pl.when(i == last)
    def _():
        os_ref[...] = out


def _final(h1, yb, dest, wts, norm_w, seq):
    R = h1.shape[0]
    n = R // HALF
    nbp = seq // HALF
    assert R == (1 + nbp + 1) * HALF
    row = lambda w: pl.BlockSpec((HALF, w), lambda i: (i, 0))
    slots = dest.reshape(TOP_K, n, HALF).transpose(1, 0, 2).reshape(n, 1, TOP_K * HALF)
    idx_spec = lambda off: pl.BlockSpec((None, 1, TOP_K * HALF), lambda i: (jnp.minimum(i + off, n - 1), 0, 0),
                                        memory_space=pltpu.SMEM)
    return pl.pallas_call(
        _final_kernel,
        grid=(n,),
        in_specs=[idx_spec(0), idx_spec(1), row(D_MODEL), row(TOP_K),
                  pl.BlockSpec((1, D_MODEL), lambda i: (0, 0)),
                  pl.BlockSpec(memory_space=pl.ANY)],
        out_specs=[pl.BlockSpec((HALF, D_MODEL), lambda i: (jnp.clip(i - 1, 0, nbp - 1), 0)),
                   pl.BlockSpec((HALF, D_MODEL), lambda i: (0, 0))],
        out_shape=[jax.ShapeDtypeStruct((seq, D_MODEL), F32),
                   jax.ShapeDtypeStruct((HALF, D_MODEL), F32)],
        scratch_shapes=[pltpu.VMEM((2, TOP_K * HALF // GATHER_UNROLL, GATHER_UNROLL, D_MODEL), F32),
                        pltpu.SemaphoreType.DMA((2,))],
        compiler_params=_cparams(("arbitrary",)),
        name="final_norm",
    )(slots, slots, h1, wts, norm_w, yb)


def _route(route):
    R = route.shape[0]
    a_max = TOP_K * R
    nblk = (a_max + N_EXPERTS * (TMOE - 1) + TMOE - 1) // TMOE
    rows = nblk * TMOE
    rank, meta = _rank(route, nblk)
    eid = route[:, :TOP_K].T.astype(I32)
    wts = route[:, TOP_K:2 * TOP_K]
    starts = meta[0, :N_EXPERTS].astype(I32) * TMOE
    n_valid = meta[1, :1].astype(I32)
    block_e = meta[8:8 + nblk, 0].astype(I32)
    next_e = meta[8:8 + nblk, 1].astype(I32)
    n_real = meta[8:8 + nblk, 2].astype(I32)
    is_real = eid >= 0
    dest = jnp.where(is_real, starts[jnp.maximum(eid, 0)] + rank[:, :TOP_K].T.astype(I32), 0)
    buf_tok = _slot_table(jnp.where(is_real, dest, rows), rows)
    return wts, dest, buf_tok, block_e, next_e, n_real, n_valid


def _slot_table_kernel(dest_ref, out_ref, *, rows_per_choice):
    def clear(r, carry):
        for c in range(LANES):
            out_ref[r, c] = 0
        return carry

    lax.fori_loop(0, out_ref.shape[0], clear, 0)

    def place(r, carry):
        tok0 = jnp.where(r >= rows_per_choice, r - rows_per_choice, r) * LANES
        for c in range(LANES):
            d = dest_ref[r, c]
            out_ref[d >> 7, d & (LANES - 1)] = tok0 + c
        return carry

    lax.fori_loop(0, TOP_K * rows_per_choice, place, 0)


def _slot_table(dest, n_slots):
    n_tok = dest.shape[1]
    assert n_tok % LANES == 0 and n_slots % LANES == 0 and dest.shape[0] == TOP_K == 2
    smem = pl.BlockSpec(memory_space=pltpu.SMEM)
    out = pl.pallas_call(
        functools.partial(_slot_table_kernel, rows_per_choice=n_tok // LANES),
        in_specs=[smem],
        out_specs=smem,
        out_shape=jax.ShapeDtypeStruct((n_slots // LANES + 1, LANES), I32),
        name="slot_table",
    )(dest.reshape(TOP_K * n_tok // LANES, LANES))
    return out.reshape(-1)[:n_slots]


def kernel(x_prompt, x_sample, cache_k, cache_v, state_conv, state_ssm, page_table, meta_tokens, norm_mix_w,
           w_in, lambda_q1, lambda_k1, lambda_q2, lambda_k2, subln_w, conv_w, conv_b, dt_bias, a_log, d_skip,
           ssm_norm_w, w_out, norm_ffn_w, w_router_group, w_router_expert, w_gate, w_up, w_down, final_norm_w):
    depth = w_in.shape[0]
    assert depth == 1 and x_prompt.shape[0] == 1 and x_sample.shape[1] == 1
    seq = x_prompt.shape[1]
    B = x_sample.shape[0]
    n_pages = page_table.shape[1]
    n_phys = cache_k.shape[1]
    past = n_pages * cache_k.shape[2]
    assert cache_k.shape[2] == PAGE and seq % TM == 0 and B == HALF and FRONT == HALF
    n_t = FRONT + seq
    R = n_t + B
    l = 0
    lam_init = 0.8 - 0.6 * math.exp(-0.3 * l)

    front = jnp.concatenate([jnp.zeros((PAD_ROWS, D_MODEL), F32), meta_tokens.astype(F32)], axis=0)
    xp, xs = x_prompt[0], x_sample[:, 0]
    w = w_in[l]
    wq = w[:, :OFF_K].astype(BF16)
    wk = w[:, OFF_K:OFF_V].astype(BF16)
    wv = w[:, OFF_V:OFF_Z].astype(BF16)
    wz = w[:, OFF_Z:OFF_XBC].astype(BF16)
    wx = w[:, OFF_XBC:OFF_DT].astype(BF16)
    wdt = jnp.pad(w[:, OFF_DT:], ((0, 0), (0, LANES - N_HEADS_S))).astype(BF16)

    pos = jnp.concatenate([jnp.maximum(jnp.arange(n_t, dtype=I32) - PAD_ROWS, 0),
                           jnp.full((B,), past, I32)])
    inv = ROPE_THETA ** (-jnp.arange(0, ROT_DIM, 2, dtype=F32) / ROT_DIM)
    ang = pos.astype(F32)[:, None] * inv[None, :]
    cos, sin = jnp.cos(ang), jnp.sin(ang)
    half = ROT_DIM // 2
    ones = jnp.ones((R, DK - ROT_DIM), F32)
    zeros = jnp.zeros((R, DK - ROT_DIM), F32)
    zh = jnp.zeros((R, half), F32)
    ctab = jnp.tile(jnp.concatenate([cos, cos, ones], axis=1), (1, 2))
    s1tab = jnp.tile(jnp.concatenate([-sin, zh, zeros], axis=1), (1, 2))
    s2tab = jnp.tile(jnp.concatenate([zh, sin, zeros], axis=1), (1, 2))

    row1 = lambda v: v.reshape(1, -1).astype(F32)
    pad_heads = lambda v: jnp.pad(v.astype(F32), (0, LANES - N_HEADS_S)).reshape(1, LANES)
    a_row = pad_heads(-jnp.exp(a_log[l].astype(F32)))
    dtb = pad_heads(dt_bias[l])
    dskip_full = jnp.repeat(d_skip[l].astype(F32), SSM_HEAD_DIM).reshape(1, D_INNER)
    normw = row1(ssm_norm_w[l])
    head_of_col = jnp.arange(D_INNER, dtype=I32) // SSM_HEAD_DIM
    e_mat = (jnp.arange(LANES, dtype=I32)[:, None] == head_of_col[None, :]).astype(BF16)
    tri = (jnp.arange(CHUNK)[:, None] >= jnp.arange(CHUNK)[None, :]).astype(BF16)
    lq1, lk1, lq2, lk2 = (row1(v[l]) for v in (lambda_q1, lambda_k1, lambda_q2, lambda_k2))
    sub_w = row1(subln_w[l])

    q, k, v, kb, vb, z, xbc, dt_raw = _inproj(front, xp, xs, row1(norm_mix_w[l]), wq, wk, wv, wz, wx, wdt,
                                              ctab, s1tab, s2tab)

    k_s, v_s = k[n_t * N_KV_HEADS:], v[n_t * N_KV_HEADS:]
    pad8 = lambda a: jnp.pad(a.reshape(B, N_KV_HEADS, LANES), ((0, 0), (0, LANES - N_KV_HEADS), (0, 0)))
    att_all, att_s = _attention(page_table, q, kb, vb, q[n_t:].reshape(B, N_HEADS_A, LANES), pad8(k_s), pad8(v_s),
                                cache_k[l].reshape(n_phys * PAGE * N_KV_HEADS, LANES),
                                cache_v[l].reshape(n_phys * PAGE * N_KV_HEADS, LANES),
                                lq1, lk1, lq2, lk2, sub_w, lam_init)
    ssm_p, state_p = _prompt_ssd(xbc, z, dt_raw, conv_w[l].astype(F32), row1(conv_b[l]), dtb, a_row,
                                 dskip_full, normw, e_mat, e_mat.T.astype(F32), tri, n_t)

    sc = state_conv[l]
    xbc_s = xbc[n_t:]
    ssm_s, state_s = _decode_ssm(xbc_s, sc[:, 0], sc[:, 1], sc[:, 2], dt_raw[n_t:], z[n_t:],
                                 state_ssm[l].reshape(B, D_INNER, D_STATE), conv_w[l].astype(F32),
                                 row1(conv_b[l]), dtb, a_row, dskip_full, normw, e_mat)

    att = lax.dynamic_update_slice(att_all, att_s.reshape(B, N_HEADS_A * DV), (n_t, 0))

    wo = w_out[l].astype(BF16)
    wr = jnp.pad(jnp.concatenate([w_router_group[l], w_router_expert[l]], axis=1).astype(F32),
                 ((0, 0), (0, LANES - N_EGROUPS - N_EXPERTS)))
    wr_hi = wr.astype(BF16)
    wr_lo = (wr - wr_hi.astype(F32)).astype(BF16)
    h1, u2, route = _outproj(front, xp, xs, att, ssm_p, ssm_s, wo[:N_HEADS_A * DV], wo[N_HEADS_A * DV:],
                             row1(norm_ffn_w[l]), wr_hi, wr_lo)

    wts, dest, buf_tok, block_e, next_e, n_real, n_valid = _route(route)
    yb = _moe(block_e, next_e, n_real, n_valid, buf_tok, u2, w_gate[l], w_up[l], w_down[l])
    y_p, y_s = _final(h1, yb, dest, wts, row1(final_norm_w), seq)

    t_real = N_META + seq
    y_prompt = y_p.reshape(1, seq, D_MODEL)
    y_sample = y_s.reshape(B, 1, D_MODEL)
    k_prompt = k[PAD_ROWS * N_KV_HEADS:n_t * N_KV_HEADS].reshape(1, 1, t_real, N_KV_HEADS, 2 * DK)
    v_prompt = v[PAD_ROWS * N_KV_HEADS:n_t * N_KV_HEADS].reshape(1, 1, t_real, N_KV_HEADS, DV)
    conv_prompt = xbc[n_t - (CONV_W - 1):n_t].reshape(1, 1, CONV_W - 1, XBC_DIM)
    ssm_prompt = state_p.reshape(1, 1, N_HEADS_S, SSM_HEAD_DIM, D_STATE)
    k_sample = k_s.reshape(1, B, 1, N_KV_HEADS, 2 * DK)
    v_sample = v_s.reshape(1, B, 1, N_KV_HEADS, DV)
    conv_sample = jnp.stack([sc[:, 1], sc[:, 2], xbc_s], axis=1)[None]
    ssm_sample = state_s.reshape(1, B, N_HEADS_S, SSM_HEAD_DIM, D_STATE)
    return (y_prompt, y_sample, k_prompt, v_prompt, conv_prompt, ssm_prompt,
            k_sample, v_sample, conv_sample, ssm_sample)
```

```python
import functools
import math

import jax
import jax.numpy as jnp
from jax import lax
from jax.experimental import pallas as pl
from jax.experimental.pallas import tpu as pltpu

F32 = jnp.float32
BF16 = jnp.bfloat16
I32 = jnp.int32

D_MODEL = 2048
N_META = 16
EPS = 1e-6
DV = 128
DK = 64
N_HEADS_A = 8
N_KV_HEADS = 4
ROT_DIM = 16
ROPE_THETA = 500000.0
NEG_INF = -1e30
LOG2E = 1.4426950408889634
D_INNER = 1024
SSM_HEAD_DIM = 64
N_HEADS_S = 16
N_BC_GROUPS = 2
D_STATE = 128
CONV_W = 4
XBC_DIM = D_INNER + 2 * N_BC_GROUPS * D_STATE
CHUNK = 128
Q_DIM = N_HEADS_A * 2 * DK
K_DIM = N_KV_HEADS * 2 * DK
V_DIM = N_KV_HEADS * DV
OFF_K = Q_DIM
OFF_V = OFF_K + K_DIM
OFF_Z = OFF_V + V_DIM
OFF_XBC = OFF_Z + D_INNER
OFF_DT = OFF_XBC + XBC_DIM
IN_DIM = OFF_DT + N_HEADS_S
N_EGROUPS = 4
EXPERTS_PER_GROUP = 8
N_EXPERTS = N_EGROUPS * EXPERTS_PER_GROUP
TOP_K = 2
D_FF = 1024
PAGE = 128

LANES = 128
VMEM_LIMIT = 56 * 1024 * 1024

FRONT = CHUNK
PAD_ROWS = FRONT - N_META
TM = 256
TQ = 256
TKF = 1024
TB = 8
TMOE = 256
TSMALL = 128
TF = 256


def _cparams(sem):
    return pltpu.CompilerParams(dimension_semantics=sem, vmem_limit_bytes=VMEM_LIMIT)


def _resident(shape):
    nd = len(shape)
    return pl.BlockSpec(shape, lambda *_: (0,) * nd, pipeline_mode=pl.Buffered(1))


def _split2(x):
    hi = x.astype(BF16)
    lo = (x - hi.astype(F32)).astype(BF16)
    return hi, lo


def _split3(x):
    hi = x.astype(BF16)
    r = x - hi.astype(F32)
    mid = r.astype(BF16)
    lo = (r - mid.astype(F32)).astype(BF16)
    return hi, mid, lo


def _silu(x):
    return x * (1.0 / (1.0 + jnp.exp(-x)))


def _softplus(x):
    return jnp.maximum(x, 0.0) + jnp.log(1.0 + jnp.exp(-jnp.abs(x)))


HALF = TM // 2


def _token_specs(seq):
    nb = seq // HALF
    return [pl.BlockSpec((HALF, D_MODEL), lambda i: (0, 0)),
            pl.BlockSpec((HALF, D_MODEL), lambda i: (jnp.clip(2 * i - 1, 0, nb - 1), 0)),
            pl.BlockSpec((HALF, D_MODEL), lambda i: (jnp.clip(2 * i, 0, nb - 1), 0)),
            pl.BlockSpec((HALF, D_MODEL), lambda i: (0, 0))]


def _token_rows(front_ref, xa_ref, xb_ref, xs_ref):
    i = pl.program_id(0)
    top = jnp.where(i == 0, front_ref[...], xa_ref[...])
    bot = jnp.where(i == pl.num_programs(0) - 1, xs_ref[...], xb_ref[...])
    return jnp.concatenate([top, bot], axis=0)


def _inproj_kernel(front_ref, xa_ref, xb_ref, xs_ref, nw_ref, wq_ref, wk_ref, wv_ref, wz_ref, wx_ref, wdt_ref,
                   c_ref, s1_ref, s2_ref,
                   q_ref, k_ref, v_ref, kb_ref, vb_ref, z_ref, xbc_ref, dt_ref):
    x = _token_rows(front_ref, xa_ref, xb_ref, xs_ref)
    ms = jnp.mean(x * x, axis=-1, keepdims=True)
    u = (x * lax.rsqrt(ms + EPS) * nw_ref[...]).astype(BF16)
    c, s1, s2 = c_ref[...], s1_ref[...], s2_ref[...]

    def rope(p):
        outs = []
        for h in range(p.shape[1] // LANES):
            xh = p[:, h * LANES:(h + 1) * LANES]
            outs.append(xh * c + pltpu.roll(xh, LANES - ROT_DIM // 2, 1) * s1
                        + pltpu.roll(xh, ROT_DIM // 2, 1) * s2)
        return jnp.concatenate(outs, axis=1)

    q = rope(jnp.dot(u, wq_ref[...], preferred_element_type=F32))
    q_ref[...] = (q * (DK ** -0.5 * LOG2E)).astype(BF16)
    k = rope(jnp.dot(u, wk_ref[...], preferred_element_type=F32))
    kb_ref[...] = k.astype(BF16)
    v = jnp.dot(u, wv_ref[...], preferred_element_type=F32)
    for g in range(N_KV_HEADS):
        k_ref[pl.ds(g, TM, stride=N_KV_HEADS), :] = k[:, g * LANES:(g + 1) * LANES]
        v_ref[pl.ds(g, TM, stride=N_KV_HEADS), :] = v[:, g * LANES:(g + 1) * LANES]
    vb = v.astype(BF16)
    ones = jnp.ones((vb.shape[0], DV), BF16)
    vb_ref[...] = jnp.concatenate(
        [piece for g in range(N_KV_HEADS) for piece in (vb[:, g * DV:(g + 1) * DV], ones)], axis=1)
    z_ref[...] = jnp.dot(u, wz_ref[...], preferred_element_type=F32)
    xbc_ref[...] = jnp.dot(u, wx_ref[...], preferred_element_type=F32)
    dt_ref[...] = jnp.dot(u, wdt_ref[...], preferred_element_type=F32)


def _inproj(front, xp, xs, norm_w, wq, wk, wv, wz, wx, wdt, ctab, s1tab, s2tab):
    R = front.shape[0] + xp.shape[0] + xs.shape[0]
    row = lambda n: pl.BlockSpec((TM, n), lambda i: (i, 0))
    head_rows = pl.BlockSpec((TM * N_KV_HEADS, LANES), lambda i: (i, 0))
    return pl.pallas_call(
        _inproj_kernel,
        grid=(R // TM,),
        in_specs=_token_specs(xp.shape[0]) + [_resident((1, D_MODEL)),
                  _resident(wq.shape), _resident(wk.shape), _resident(wv.shape),
                  _resident(wz.shape), _resident(wx.shape), _resident(wdt.shape),
                  row(LANES), row(LANES), row(LANES)],
        out_specs=[row(Q_DIM), head_rows, head_rows, row(K_DIM), row(2 * V_DIM),
                   row(D_INNER), row(XBC_DIM), row(LANES)],
        out_shape=[jax.ShapeDtypeStruct((R, Q_DIM), BF16),
                   jax.ShapeDtypeStruct((R * N_KV_HEADS, LANES), F32),
                   jax.ShapeDtypeStruct((R * N_KV_HEADS, LANES), F32),
                   jax.ShapeDtypeStruct((R, K_DIM), BF16),
                   jax.ShapeDtypeStruct((R, 2 * V_DIM), BF16),
                   jax.ShapeDtypeStruct((R, D_INNER), F32),
                   jax.ShapeDtypeStruct((R, XBC_DIM), F32),
                   jax.ShapeDtypeStruct((R, LANES), F32)],
        compiler_params=_cparams(("arbitrary",)),
        name="inproj",
    )(front, xp, xp, xs, norm_w, wq, wk, wv, wz, wx, wdt, ctab, s1tab, s2tab)


def _diff_lambda(lq1, lk1, lq2, lk2, lam_init):
    a = jnp.sum(lq1[...] * lk1[...], axis=-1, keepdims=True)
    b = jnp.sum(lq2[...] * lk2[...], axis=-1, keepdims=True)
    return jnp.exp(a) - jnp.exp(b) + lam_init


def _subln(o, sub_w, lam_init):
    ms = jnp.mean(o * o, axis=-1, keepdims=True)
    return (o * lax.rsqrt(ms + EPS) * sub_w) * (1.0 - lam_init)


def _attn_kernel(lq1, lk1, lq2, lk2, sub_ref, q_ref, k_ref, ve_ref, o_ref,
                 qs_scr, m_scr, acc_scr, *, lam_init, n_rows):
    qi = pl.program_id(1)
    lam = _diff_lambda(lq1, lk1, lq2, lk2, lam_init)

    q = q_ref[...]
    lane = lax.broadcasted_iota(I32, (TQ, LANES), 1)
    first = lane < DK
    zero = jnp.zeros((TQ, LANES), BF16)
    for hh in range(2):
        qh = q[:, hh * LANES:(hh + 1) * LANES]
        qs_scr[(2 * hh) * TQ:(2 * hh + 1) * TQ, :] = jnp.where(first, qh, zero)
        qs_scr[(2 * hh + 1) * TQ:(2 * hh + 2) * TQ, :] = jnp.where(first, zero, qh)
    m_scr[...] = jnp.full(m_scr.shape, NEG_INF, F32)
    acc_scr[...] = jnp.zeros(acc_scr.shape, F32)

    nt = (((1,), (1,)), ((), ()))

    def process(start, width, keep):
        for part in range(4):
            rows = slice(part * TQ, (part + 1) * TQ)
            s = lax.dot_general(qs_scr[rows, :], k_ref[pl.ds(start, width), :], nt,
                                preferred_element_type=F32)
            if keep is not None:
                s = jnp.where(keep, s, NEG_INF)
            m_prev = m_scr[rows, :]
            m_new = jnp.maximum(m_prev, jnp.max(s, axis=-1, keepdims=True))
            alpha = jnp.exp2(m_prev - m_new)
            p = jnp.concatenate([jnp.exp2(s[:, t * LANES:(t + 1) * LANES] - m_new)
                                 for t in range(width // LANES)], axis=1).astype(BF16)
            pv = jnp.dot(p, ve_ref[pl.ds(start, width), :], preferred_element_type=F32)
            acc_scr[rows, :LANES] = alpha * acc_scr[rows, :LANES] + pv[:, :LANES]
            acc_scr[rows, LANES:] = alpha * acc_scr[rows, LANES:] + pv[:, LANES:]
            m_scr[rows, :] = m_new

    q_lo = qi * TQ
    n_full = jnp.maximum(q_lo - PAD_ROWS, 0) // TKF
    tail_lo = PAD_ROWS + n_full * TKF
    n_tail = (q_lo + TQ - tail_lo + TQ - 1) // TQ

    def full(c):
        process(pl.multiple_of(PAD_ROWS + c * TKF, 16), TKF, None)

    def full_pair(c2, carry):
        full(2 * c2)
        full(2 * c2 + 1)
        return carry

    lax.fori_loop(0, n_full // 2, full_pair, 0)

    @pl.when(n_full % 2 == 1)
    def _():
        full(n_full - 1)

    def tail(w):
        width = w * TQ

        def run():
            start = pl.multiple_of(q_lo + TQ - width, 16)
            col = lax.broadcasted_iota(I32, (TQ, width), 1)
            qpos = q_lo + lax.broadcasted_iota(I32, (TQ, width), 0)
            process(start, width, (col <= qpos - start) & (col >= tail_lo - start))

        return run

    for w in range(1, TKF // TQ + 2):
        pl.when(n_tail == w)(tail(w))

    acc = acc_scr[...]
    o = acc[:, :LANES] / acc[:, LANES:]
    sub_w = sub_ref[...]
    for hh in range(2):
        o1 = o[(2 * hh) * TQ:(2 * hh + 1) * TQ, :]
        o2 = o[(2 * hh + 1) * TQ:(2 * hh + 2) * TQ, :]
        o_ref[:, hh * LANES:(hh + 1) * LANES] = _subln(o1 - lam * o2, sub_w, lam_init).astype(BF16)


def _expand_heads(x, e_ref):
    hi, lo = _split2(x)
    e = e_ref[...]
    return (jnp.dot(hi, e, preferred_element_type=F32)
            + jnp.dot(lo, e, preferred_element_type=F32))


def _gated_out(y, xs, z, dskip_full, normw):
    y = (y + dskip_full * xs) * _silu(z)
    half = D_INNER // N_BC_GROUPS
    outs = []
    for g in range(N_BC_GROUPS):
        yg = y[:, g * half:(g + 1) * half]
        ms = jnp.mean(yg * yg, axis=-1, keepdims=True)
        outs.append(yg * lax.rsqrt(ms + EPS))
    return jnp.concatenate(outs, axis=1) * normw


def _ssd_kernel(xbc_ref, z_ref, dt_ref, cw_ref, cb_ref, dtb_ref, a_ref, dskip_ref, normw_ref,
                e_ref, et_ref, tri_ref, ssm_ref, state_ref, ext_scr, h_scr):
    c = pl.program_id(0)
    L = CHUNK

    @pl.when(c == 0)
    def _():
        ext_scr[0:8, :] = jnp.zeros((8, XBC_DIM), F32)
        h_scr[...] = jnp.zeros(h_scr.shape, F32)

    x_new = xbc_ref[...]
    ext_scr[8:8 + L, :] = x_new
    conv = cb_ref[...] + cw_ref[CONV_W - 1:CONV_W, :] * x_new
    for j in range(CONV_W - 1):
        sh = CONV_W - 1 - j
        conv = conv + cw_ref[j:j + 1, :] * ext_scr[8 - sh:8 - sh + L, :]
    ext_scr[0:8, :] = x_new[L - 8:L, :]
    xc = _silu(conv)
    xs = xc[:, :D_INNER]

    rowi = lax.broadcasted_iota(I32, (L, LANES), 0)
    coli = lax.broadcasted_iota(I32, (L, LANES), 1)
    valid = (c * L + rowi) >= PAD_ROWS
    dt = jnp.where(valid, _softplus(dt_ref[...] + dtb_ref[...]), 0.0)
    da = dt * a_ref[...]
    tri = tri_ref[...]
    acs = sum(jnp.dot(tri, part, preferred_element_type=F32) for part in _split3(da))
    acs_t = acs.T
    dt_t = dt.T
    causal = coli <= rowi
    last = acs[L - 1:L, :]

    w_state = _expand_heads(dt * jnp.exp(last - acs), e_ref)
    dec_out = _expand_heads(jnp.exp(acs), e_ref)
    dec_rows = jnp.sum(et_ref[...] * jnp.exp(last), axis=-1, keepdims=True)
    xw = xs * w_state
    xs_b = xs.astype(BF16)

    half = D_INNER // N_BC_GROUPS
    hpg = N_HEADS_S // N_BC_GROUPS
    y_parts = []
    for g in range(N_BC_GROUPS):
        bm = xc[:, D_INNER + g * D_STATE:D_INNER + (g + 1) * D_STATE].astype(BF16)
        cm = xc[:, D_INNER + (N_BC_GROUPS + g) * D_STATE:
                D_INNER + (N_BC_GROUPS + g + 1) * D_STATE].astype(BF16)
        cb = lax.dot_general(cm, bm, (((1,), (1,)), ((), ())), preferred_element_type=F32)
        h_prev = h_scr[g * half:(g + 1) * half, :]
        y_off = lax.dot_general(cm, h_prev.astype(BF16), (((1,), (1,)), ((), ())),
                                preferred_element_type=F32)
        y_off = y_off * dec_out[:, g * half:(g + 1) * half]
        yd = []
        for r in range(0, hpg, 2):
            pair = []
            for hh in (r, r + 1):
                h = g * hpg + hh
                seg = jnp.broadcast_to(acs[:, h:h + 1], (L, L)) - acs_t[h:h + 1, :]
                lmat = jnp.where(causal, jnp.exp(jnp.where(causal, seg, 0.0)), 0.0)
                pair.append((cb * lmat * dt_t[h:h + 1, :]).astype(BF16))
            h0 = g * hpg + r
            xp = xs_b[:, h0 * SSM_HEAD_DIM:(h0 + 2) * SSM_HEAD_DIM]
            ya = jnp.dot(pair[0], xp, preferred_element_type=F32)
            yb = jnp.dot(pair[1], xp, preferred_element_type=F32)
            yd.append(jnp.where(coli < SSM_HEAD_DIM, ya, yb))
        y_parts.append(jnp.concatenate(yd, axis=1) + y_off)
        xw_t = xw[:, g * half:(g + 1) * half].T.astype(BF16)
        st = jnp.dot(xw_t, bm, preferred_element_type=F32)
        h_scr[g * half:(g + 1) * half, :] = dec_rows[g * half:(g + 1) * half, :] * h_prev + st
    y = jnp.concatenate(y_parts, axis=1)

    ssm_ref[...] = _gated_out(y, xs, z_ref[...], dskip_ref[...], normw_ref[...]).astype(BF16)

    @pl.when(c == pl.num_programs(0) - 1)
    def _():
        state_ref[...] = h_scr[...]


def _prompt_ssd(xbc, z, dt_raw, conv_w, conv_b, dtb, a_row, dskip_full, normw, e_mat, e_t, tri, n_t):
    row = lambda n: pl.BlockSpec((CHUNK, n), lambda c: (c, 0))
    const = lambda shp: pl.BlockSpec(shp, lambda c: (0, 0))
    return pl.pallas_call(
        _ssd_kernel,
        grid=(n_t // CHUNK,),
        in_specs=[row(XBC_DIM), row(D_INNER), row(LANES),
                  const((CONV_W, XBC_DIM)), const((1, XBC_DIM)), const((1, LANES)), const((1, LANES)),
                  const((1, D_INNER)), const((1, D_INNER)), const((LANES, D_INNER)),
                  const((D_INNER, LANES)), const((CHUNK, CHUNK))],
        out_specs=[row(D_INNER), pl.BlockSpec((D_INNER, D_STATE), lambda c: (0, 0))],
        out_shape=[jax.ShapeDtypeStruct((n_t, D_INNER), BF16),
                   jax.ShapeDtypeStruct((D_INNER, D_STATE), F32)],
        scratch_shapes=[pltpu.VMEM((8 + CHUNK, XBC_DIM), F32),
                        pltpu.VMEM((D_INNER, D_STATE), F32)],
        compiler_params=_cparams(("arbitrary",)),
        name="prompt_ssd",
    )(xbc, z, dt_raw, conv_w, conv_b, dtb, a_row, dskip_full, normw, e_mat, e_t, tri)


def _decode_attn_kernel(pt_ref, lq1, lk1, lq2, lk2, sub_ref, q_ref, kn_ref, vn_ref, *rest,
                        n_pages, lam_init):
    k_pages = rest[:n_pages]
    v_pages = rest[n_pages:2 * n_pages]
    o_ref = rest[2 * n_pages]
    del pt_ref
    lam = _diff_lambda(lq1, lk1, lq2, lk2, lam_init)
    nq = 2 * N_HEADS_A
    q8 = q_ref[...]
    lane = lax.broadcasted_iota(I32, (N_HEADS_A, LANES), 1)
    zero = jnp.zeros((N_HEADS_A, LANES), BF16)
    q16 = jnp.concatenate([jnp.where(lane < DK, q8, zero), jnp.where(lane < DK, zero, q8)], axis=0)

    pw = PAGE * N_KV_HEADS
    row_head = (lax.broadcasted_iota(I32, (nq, pw), 0) & (N_HEADS_A - 1)) >> 1
    col_head = lax.broadcasted_iota(I32, (nq, pw), 1) & (N_KV_HEADS - 1)
    own = row_head == col_head
    nt = (((1,), (1,)), ((), ()))
    s_pages = []
    for j in range(n_pages):
        kp = k_pages[j][...].astype(BF16)
        s = lax.dot_general(q16, kp, nt, preferred_element_type=F32)
        s_pages.append(jnp.where(own, s, NEG_INF))
    kn = kn_ref[...].astype(BF16)
    row_head_n = (lax.broadcasted_iota(I32, (nq, LANES), 0) & (N_HEADS_A - 1)) >> 1
    col_n = lax.broadcasted_iota(I32, (nq, LANES), 1)
    s_new = jnp.where(row_head_n == col_n, lax.dot_general(q16, kn, nt, preferred_element_type=F32), NEG_INF)

    m = jnp.max(s_new, axis=-1, keepdims=True)
    for s in s_pages:
        m = jnp.maximum(m, jnp.max(s, axis=-1, keepdims=True))
    p_new = jnp.exp2(s_new - m)
    l = jnp.sum(p_new, axis=-1, keepdims=True)
    acc = jnp.dot(p_new.astype(BF16), vn_ref[...].astype(BF16), preferred_element_type=F32)
    for j in range(n_pages):
        p = jnp.exp2(s_pages[j] - m)
        l = l + jnp.sum(p, axis=-1, keepdims=True)
        acc = acc + jnp.dot(p.astype(BF16), v_pages[j][...].astype(BF16), preferred_element_type=F32)
    o = acc / l
    res = o[:N_HEADS_A, :] - lam * o[N_HEADS_A:, :]
    o_ref[...] = _subln(res, sub_ref[...], lam_init).astype(BF16)


def _fused_attn_kernel(pt_ref, lq1, lk1, lq2, lk2, sub_ref, q_ref, k_ref, ve_ref, qd_ref, kn_ref, vn_ref, *rest,
                       n_pages, n_dec, lam_init, n_rows):
    pages = rest[:2 * n_pages]
    o_ref, od_ref = rest[2 * n_pages], rest[2 * n_pages + 1]
    scratch = rest[2 * n_pages + 2:]

    @pl.when(pl.program_id(1) < n_rows // TQ)
    def _():
        _attn_kernel(lq1, lk1, lq2, lk2, sub_ref, q_ref, k_ref, ve_ref, o_ref, *scratch,
                     lam_init=lam_init, n_rows=n_rows)

    step = pl.program_id(0) * pl.num_programs(1) + pl.program_id(1)

    @pl.when(step < n_dec)
    def _():
        _decode_attn_kernel(pt_ref, lq1, lk1, lq2, lk2, sub_ref, qd_ref, kn_ref, vn_ref, *pages, od_ref,
                            n_pages=n_pages, lam_init=lam_init)


def _attention(page_table, q, kb, vbe, q_s, k_new, v_new, cache_k2, cache_v2, lq1, lk1, lq2, lk2, sub_w,
               lam_init):
    R = q.shape[0]
    nq = R // TQ
    B, n_pages = page_table.shape
    assert R % TQ == 0 and TKF % TQ == 0
    n_inner = max(nq, pl.cdiv(B, N_KV_HEADS))
    pw = PAGE * N_KV_HEADS
    seq_of = lambda g, i: jnp.minimum(g * n_inner + i, B - 1)
    qblk = lambda i: jnp.minimum(i, nq - 1)
    small = lambda n: pl.BlockSpec((1, n), lambda g, i, pt: (0, 0))
    per_b = pl.BlockSpec((None, 8, LANES), lambda g, i, pt: (seq_of(g, i), 0, 0))
    new_tok = pl.BlockSpec((None, LANES, LANES), lambda g, i, pt: (seq_of(g, i), 0, 0))
    page_spec = lambda j: pl.BlockSpec((pw, LANES), lambda g, i, pt: (pt[seq_of(g, i), j], 0))
    grid_spec = pltpu.PrefetchScalarGridSpec(
        num_scalar_prefetch=1,
        grid=(N_KV_HEADS, n_inner),
        in_specs=[small(DK), small(DK), small(DK), small(DK), small(DV),
                  pl.BlockSpec((TQ, 2 * LANES), lambda g, i, pt: (qblk(i), g)),
                  pl.BlockSpec((R, LANES), lambda g, i, pt: (0, g)),
                  pl.BlockSpec((R, 2 * LANES), lambda g, i, pt: (0, g)),
                  per_b, new_tok, new_tok]
                 + [page_spec(j) for j in range(n_pages)] + [page_spec(j) for j in range(n_pages)],
        out_specs=[pl.BlockSpec((TQ, 2 * LANES), lambda g, i, pt: (qblk(i), g)), per_b],
        scratch_shapes=[pltpu.VMEM((4 * TQ, LANES), BF16),
                        pltpu.VMEM((4 * TQ, LANES), F32),
                        pltpu.VMEM((4 * TQ, 2 * LANES), F32)],
    )
    return pl.pallas_call(
        functools.partial(_fused_attn_kernel, n_pages=n_pages, n_dec=B, lam_init=lam_init, n_rows=R),
        grid_spec=grid_spec,
        out_shape=[jax.ShapeDtypeStruct((R, N_HEADS_A * DV), BF16),
                   jax.ShapeDtypeStruct((B, 8, LANES), BF16)],
        compiler_params=_cparams(("arbitrary", "arbitrary")),
        name="attention",
    )(page_table, lq1, lk1, lq2, lk2, sub_w, q, kb, vbe, q_s, k_new, v_new,
      *([cache_k2] * n_pages), *([cache_v2] * n_pages))


def _dec_pre_kernel(xbc_ref, s0_ref, s1_ref, s2_ref, dt_ref, cw_ref, cb_ref, dtb_ref, a_ref, e_ref,
                    xc_ref, xdt_ref, dect_ref):
    conv = (cb_ref[...] + cw_ref[0:1, :] * s0_ref[...] + cw_ref[1:2, :] * s1_ref[...]
            + cw_ref[2:3, :] * s2_ref[...] + cw_ref[3:4, :] * xbc_ref[...])
    xc = _silu(conv)
    xc_ref[...] = xc
    dt = _softplus(dt_ref[...] + dtb_ref[...])
    dec = jnp.exp(dt * a_ref[...])
    hi, mid, lo = _split3(dt)
    e = e_ref[...]
    dt_full = (jnp.dot(hi, e, preferred_element_type=F32) + jnp.dot(mid, e, preferred_element_type=F32)
               + jnp.dot(lo, e, preferred_element_type=F32))
    hi, mid, lo = _split3(dec)
    dec_full = (jnp.dot(hi, e, preferred_element_type=F32) + jnp.dot(mid, e, preferred_element_type=F32)
                + jnp.dot(lo, e, preferred_element_type=F32))
    xdt_ref[...] = (dt_full * xc[:, :D_INNER]).T
    dect_ref[...] = dec_full.T


def _dec_state_kernel(h_ref, xdt_ref, dect_ref, bm_ref, cm_ref, hout_ref, y_ref):
    i = pl.program_id(0)
    B = xdt_ref.shape[1]
    lane = lax.broadcasted_iota(I32, (D_INNER, B), 1)
    sub = lax.broadcasted_iota(I32, (TB, D_INNER // N_BC_GROUPS), 0)
    half = D_INNER // N_BC_GROUPS
    nt = (((1,), (1,)), ((), ()))
    y_acc = [jnp.zeros((TB, half), F32) for _ in range(N_BC_GROUPS)]
    start = pl.multiple_of(i * TB, TB)
    for j in range(TB):
        b = i * TB + j
        sel = lane == b
        xcol = jnp.sum(jnp.where(sel, xdt_ref[...], 0.0), axis=-1, keepdims=True)
        dcol = jnp.sum(jnp.where(sel, dect_ref[...], 0.0), axis=-1, keepdims=True)
        hnew = dcol * h_ref[j]
        for g in range(N_BC_GROUPS):
            brow = bm_ref[pl.ds(start, TB), g * D_STATE:(g + 1) * D_STATE][j:j + 1, :]
            hg = hnew[g * half:(g + 1) * half] + xcol[g * half:(g + 1) * half] * brow
            hout_ref[j, g * half:(g + 1) * half, :] = hg
            c8 = cm_ref[pl.ds(start, TB), g * D_STATE:(g + 1) * D_STATE].astype(BF16)
            yj = lax.dot_general(c8, hg.astype(BF16), nt, preferred_element_type=F32)
            y_acc[g] = y_acc[g] + jnp.where(sub == j, yj, 0.0)
    y_ref[...] = jnp.concatenate(y_acc, axis=1)


def _dec_post_kernel(y_ref, xc_ref, z_ref, dskip_ref, normw_ref, o_ref):
    o_ref[...] = _gated_out(y_ref[...], xc_ref[:, :D_INNER], z_ref[...], dskip_ref[...],
                            normw_ref[...]).astype(BF16)


def _decode_ssm(xbc_s, sc0, sc1, sc2, dt_s, z_s, state, conv_w, conv_b, dtb, a_row, dskip_full, normw, e_mat):
    B = xbc_s.shape[0]
    xc, xdt_t, dec_t = pl.pallas_call(
        _dec_pre_kernel,
        out_shape=[jax.ShapeDtypeStruct((B, XBC_DIM), F32),
                   jax.ShapeDtypeStruct((D_INNER, B), F32),
                   jax.ShapeDtypeStruct((D_INNER, B), F32)],
        compiler_params=pltpu.CompilerParams(vmem_limit_bytes=VMEM_LIMIT),
        name="decode_ssm_pre",
    )(xbc_s, sc0, sc1, sc2, dt_s, conv_w, conv_b, dtb, a_row, e_mat)
    bm = xc[:, D_INNER:D_INNER + N_BC_GROUPS * D_STATE]
    cm = xc[:, D_INNER + N_BC_GROUPS * D_STATE:]
    const = lambda shp: pl.BlockSpec(shp, lambda i: (0,) * len(shp))
    state_spec = pl.BlockSpec((TB, D_INNER, D_STATE), lambda i: (i, 0, 0))
    h_new, y = pl.pallas_call(
        _dec_state_kernel,
        grid=(B // TB,),
        in_specs=[state_spec, const((D_INNER, B)), const((D_INNER, B)),
                  const((B, N_BC_GROUPS * D_STATE)), const((B, N_BC_GROUPS * D_STATE))],
        out_specs=[state_spec, pl.BlockSpec((TB, D_INNER), lambda i: (i, 0))],
        out_shape=[jax.ShapeDtypeStruct((B, D_INNER, D_STATE), F32),
                   jax.ShapeDtypeStruct((B, D_INNER), F32)],
        compiler_params=_cparams(("arbitrary",)),
        name="decode_ssm_state",
    )(state, xdt_t, dec_t, bm, cm)
    ssm = pl.pallas_call(
        _dec_post_kernel,
        out_shape=jax.ShapeDtypeStruct((B, D_INNER), BF16),
        compiler_params=pltpu.CompilerParams(vmem_limit_bytes=VMEM_LIMIT),
        name="decode_ssm_post",
    )(y, xc, z_s, dskip_full, normw)
    return ssm, h_new


def _outproj_kernel(front_ref, xa_ref, xb_ref, xs_ref, att_ref, sa_ref, sb_ref, ss_ref,
                    wa_ref, ws_ref, nw_ref, wrh_ref, wrl_ref, h_ref, u_ref, lg_ref):
    last = pl.program_id(0) == pl.num_programs(0) - 1
    ssm = jnp.concatenate([sa_ref[...], jnp.where(last, ss_ref[...], sb_ref[...])], axis=0)
    acc = jnp.dot(att_ref[...], wa_ref[...], preferred_element_type=F32)
    acc = acc + jnp.dot(ssm, ws_ref[...], preferred_element_type=F32)
    h = _token_rows(front_ref, xa_ref, xb_ref, xs_ref) + acc
    h_ref[...] = h
    ms = jnp.mean(h * h, axis=-1, keepdims=True)
    u = h * lax.rsqrt(ms + EPS) * nw_ref[...]
    u_ref[...] = u
    uh, ul = _split2(u)
    wh, wl = wrh_ref[...], wrl_ref[...]
    lg = (jnp.dot(uh, wh, preferred_element_type=F32) + jnp.dot(uh, wl, preferred_element_type=F32)
          + jnp.dot(ul, wh, preferred_element_type=F32))
    lg_ref[...] = _route_rows(lg, pl.program_id(0) * TM)


def _first_max(x, lane):
    m = jnp.max(x, axis=-1, keepdims=True)
    return m, jnp.min(jnp.where(x == m, lane, float(LANES)), axis=-1, keepdims=True)


def _route_rows(lg, row0):
    rows = lg.shape[0]
    lane_i = lax.broadcasted_iota(I32, (rows, LANES), 1)
    lane = lane_i.astype(F32)
    gl = jnp.where(lane_i < N_EGROUPS, lg, -jnp.inf)
    gmax, g = _first_max(gl, lane)
    pg = 1.0 / jnp.sum(jnp.exp(gl - gmax), axis=-1, keepdims=True)
    lo = N_EGROUPS + g * EXPERTS_PER_GROUP
    el = jnp.where((lane >= lo) & (lane < lo + EXPERTS_PER_GROUP), lg, -jnp.inf)
    m1, i1 = _first_max(el, lane)
    m2, i2 = _first_max(jnp.where(lane == i1, -jnp.inf, el), lane)
    z = jnp.sum(jnp.exp(el - m1), axis=-1, keepdims=True)
    p1 = 1.0 / z
    p2 = jnp.exp(m2 - m1) / z
    tot = p1 + p2
    valid = (row0 + lax.broadcasted_iota(I32, (rows, 1), 0)) >= PAD_ROWS
    e1 = jnp.where(valid, i1 - N_EGROUPS, -1.0)
    e2 = jnp.where(valid, i2 - N_EGROUPS, -1.0)
    w1 = jnp.where(valid, p1 / tot * pg, 0.0)
    w2 = jnp.where(valid, p2 / tot * pg, 0.0)
    return jnp.where(lane_i == 0, e1, jnp.where(lane_i == 1, e2, jnp.where(lane_i == 2, w1,
                                                                         jnp.where(lane_i == 3, w2, 0.0))))


def _rank_kernel(route_ref, tri_ref, utri_ref, rank_ref, meta_ref, carry, *, nblk):
    i = pl.program_id(0)

    @pl.when(i == 0)
    def _():
        carry[...] = jnp.zeros(carry.shape, F32)

    r = route_ref[...]
    lane = lax.broadcasted_iota(I32, r.shape, 1).astype(F32)
    oh0 = (lane == r[:, 0:1]).astype(F32)
    oh1 = (lane == r[:, 1:2]).astype(F32)
    oh = oh0 + oh1
    before = jnp.dot(tri_ref[...], oh.astype(BF16), preferred_element_type=F32) + carry[0:1, :]
    rank0 = jnp.sum(before * oh0, axis=-1, keepdims=True)
    rank1 = jnp.sum(before * oh1, axis=-1, keepdims=True)
    lane_i = lax.broadcasted_iota(I32, r.shape, 1)
    rank_ref[...] = jnp.where(lane_i == 0, rank0, jnp.where(lane_i == 1, rank1, 0.0))
    carry[0:1, :] = carry[0:1, :] + jnp.sum(oh, axis=0, keepdims=True)

    @pl.when(i == pl.num_programs(0) - 1)
    def _():
        nb_pad = meta_ref.shape[0] - 8
        counts = carry[0:1, :]
        pb = jnp.floor((counts + (TMOE - 1)) / TMOE)
        ends = jnp.dot(jnp.broadcast_to(pb, (8, LANES)).astype(BF16), utri_ref[...],
                       preferred_element_type=F32)[0:1, :]
        starts = ends - pb
        lane1 = lax.broadcasted_iota(I32, (1, LANES), 1)
        n_valid = jnp.sum(jnp.where(lane1 == N_EXPERTS - 1, ends, 0.0), axis=-1, keepdims=True)
        blk = lax.broadcasted_iota(I32, (nb_pad, LANES), 0).astype(F32)
        lane_b = lax.broadcasted_iota(I32, (nb_pad, LANES), 1)
        is_e = lane_b < N_EXPERTS
        block_e = jnp.minimum(jnp.sum(jnp.where(is_e & (ends <= blk), 1.0, 0.0), axis=-1, keepdims=True),
                              N_EXPERTS - 1.0)
        lane_bf = lane_b.astype(F32)
        has = is_e & (counts > 0.0) & (lane_bf > block_e)
        nxt = jnp.min(jnp.where(has, lane_bf, float(LANES)), axis=-1, keepdims=True)
        next_e = jnp.where(nxt >= LANES, -1.0, nxt)
        mine = lane_bf == block_e
        cnt_b = jnp.sum(jnp.where(mine, counts, 0.0), axis=-1, keepdims=True)
        st_b = jnp.sum(jnp.where(mine, starts, 0.0), axis=-1, keepdims=True)
        n_real = jnp.clip(cnt_b - (blk[:, 0:1] - st_b) * TMOE, 0.0, float(TMOE))
        meta_ref[0:8, :] = jnp.where(lax.broadcasted_iota(I32, (8, LANES), 0) == 0, starts,
                                     jnp.broadcast_to(n_valid, (8, LANES)))
        meta_ref[8:, :] = jnp.where(lane_b == 0, block_e, jnp.where(lane_b == 1, next_e,
                                                                    jnp.where(lane_b == 2, n_real, 0.0)))


def _rank(route, nblk):
    R = route.shape[0]
    nb_pad = (nblk + 7) // 8 * 8
    tri = (jnp.arange(TM)[:, None] > jnp.arange(TM)[None, :]).astype(BF16)
    utri = (jnp.arange(LANES)[:, None] <= jnp.arange(LANES)[None, :]).astype(BF16)
    return pl.pallas_call(
        functools.partial(_rank_kernel, nblk=nblk),
        grid=(R // TM,),
        in_specs=[pl.BlockSpec((TM, LANES), lambda i: (i, 0)),
                  pl.BlockSpec((TM, TM), lambda i: (0, 0)),
                  pl.BlockSpec((LANES, LANES), lambda i: (0, 0))],
        out_specs=[pl.BlockSpec((TM, LANES), lambda i: (i, 0)),
                   pl.BlockSpec((8 + nb_pad, LANES), lambda i: (0, 0))],
        out_shape=[jax.ShapeDtypeStruct((R, LANES), F32),
                   jax.ShapeDtypeStruct((8 + nb_pad, LANES), F32)],
        scratch_shapes=[pltpu.VMEM((8, LANES), F32)],
        compiler_params=_cparams(("arbitrary",)),
        name="expert_rank",
    )(route, tri, utri)


def _outproj(front, xp, xs, att, ssm_p, ssm_s, wa, ws, norm_w, wr_hi, wr_lo):
    R = front.shape[0] + xp.shape[0] + xs.shape[0]
    row = lambda n: pl.BlockSpec((TM, n), lambda i: (i, 0))
    nbp = ssm_p.shape[0] // HALF
    half = lambda f: pl.BlockSpec((HALF, D_INNER), f)
    return pl.pallas_call(
        _outproj_kernel,
        grid=(R // TM,),
        in_specs=_token_specs(xp.shape[0]) + [
            row(N_HEADS_A * DV),
            half(lambda i: (2 * i, 0)), half(lambda i: (jnp.minimum(2 * i + 1, nbp - 1), 0)),
            half(lambda i: (0, 0)),
            _resident(wa.shape), _resident(ws.shape), _resident((1, D_MODEL)),
            _resident(wr_hi.shape), _resident(wr_lo.shape)],
        out_specs=[row(D_MODEL), row(D_MODEL), row(LANES)],
        out_shape=[jax.ShapeDtypeStruct((R, D_MODEL), F32),
                   jax.ShapeDtypeStruct((R, D_MODEL), F32),
                   jax.ShapeDtypeStruct((R, LANES), F32)],
        compiler_params=_cparams(("arbitrary",)),
        name="outproj",
    )(front, xp, xp, xs, att, ssm_p, ssm_p, ssm_s, wa, ws, norm_w, wr_hi, wr_lo)


GATHER_UNROLL = 8

def _start_row_gather(idx_ref, src_hbm, dst, sem, n_groups=None):
    def start(g, carry):
        for j in range(GATHER_UNROLL):
            pltpu.make_async_copy(src_hbm.at[pl.ds(idx_ref[0, g * GATHER_UNROLL + j], 1)],
                                  dst.at[g, pl.ds(j, 1)], sem).start(priority=j % 2)
        return carry

    lax.fori_loop(0, dst.shape[0] if n_groups is None else n_groups, start, 0)


def _wait_row_gather(src_hbm, dst, sem, n_groups=None):
    def wait(g, carry):
        pltpu.make_async_copy(src_hbm.at[pl.ds(0, GATHER_UNROLL)], dst.at[0], sem).wait()
        return carry

    lax.fori_loop(0, dst.shape[0] if n_groups is None else n_groups, wait, 0)


def _row_groups(n_real):
    return (n_real + (GATHER_UNROLL - 1)) // GATHER_UNROLL


def _moe_kernel(be_ref, nx_ref, nr_ref, nv_ref, idx_ref, idx_next_ref, x_hbm, wg_hbm, wu_hbm, wd_hbm, o_ref,
                xbuf, stage_g, stage_u, stage_d, wg_b, wu_b, wd_b, sems, row_sems):
    i = pl.program_id(0)
    valid = i < nv_ref[0]
    e = be_ref[i]
    changed = (i == 0) | (be_ref[jnp.maximum(i - 1, 0)] != e)
    slot = i % 2
    n_real = nr_ref[i]

    @pl.when(i == 0)
    def _():
        xbuf[...] = jnp.zeros(xbuf.shape, F32)
        _start_row_gather(idx_ref, x_hbm, xbuf.at[0], row_sems.at[0], _row_groups(n_real))

    @pl.when(i + 1 < nv_ref[0])
    def _():
        _start_row_gather(idx_next_ref, x_hbm, xbuf.at[1 - slot], row_sems.at[1 - slot],
                          _row_groups(nr_ref[jnp.minimum(i + 1, pl.num_programs(0) - 1)]))

    def fetch(expert):
        return (pltpu.make_async_copy(wg_hbm.at[expert], stage_g, sems.at[0]),
                pltpu.make_async_copy(wu_hbm.at[expert], stage_u, sems.at[1]),
                pltpu.make_async_copy(wd_hbm.at[expert], stage_d, sems.at[2]))

    @pl.when(i == 0)
    def _():
        for c in fetch(e):
            c.start()

    @pl.when(valid & changed)
    def _():
        for c in fetch(e):
            c.wait()
        wg_b[...] = stage_g[...].astype(BF16)
        wu_b[...] = stage_u[...].astype(BF16)
        wd_b[...] = stage_d[...].astype(BF16)
        nxt = nx_ref[i]

        @pl.when(nxt >= 0)
        def _():
            for c in fetch(nxt):
                c.start()

    def mlp(rows):
        x = xbuf[slot, :rows // GATHER_UNROLL].reshape(rows, D_MODEL).astype(BF16)
        g = jnp.dot(x, wg_b[...], preferred_element_type=F32)
        u = jnp.dot(x, wu_b[...], preferred_element_type=F32)
        h = (_silu(g) * u).astype(BF16)
        o_ref[:rows, :] = jnp.dot(h, wd_b[...], preferred_element_type=F32)

    @pl.when(valid)
    def _():
        _wait_row_gather(x_hbm, xbuf.at[slot], row_sems.at[slot], _row_groups(n_real))

    @pl.when(valid & (n_real > TSMALL))
    def _():
        mlp(TMOE)

    @pl.when(valid & (n_real <= TSMALL))
    def _():
        mlp(TSMALL)
        o_ref[TSMALL:, :] = jnp.zeros((TMOE - TSMALL, D_MODEL), F32)

    @pl.when(jnp.logical_not(valid))
    def _():
        o_ref[...] = jnp.zeros(o_ref.shape, F32)


def _moe(block_e, next_e, n_real, n_valid, buf_tok, x_tok, w_gate, w_up, w_down):
    rows = buf_tok.shape[0]
    nblk = rows // TMOE

    def blk(i, nv):
        return jnp.minimum(i, nv[0] - 1)

    grid_spec = pltpu.PrefetchScalarGridSpec(
        num_scalar_prefetch=4,
        grid=(nblk,),
        in_specs=[pl.BlockSpec((None, 1, TMOE), lambda i, be, nx, nr, nv: (blk(i, nv), 0, 0),
                               memory_space=pltpu.SMEM),
                  pl.BlockSpec((None, 1, TMOE), lambda i, be, nx, nr, nv: (blk(i + 1, nv), 0, 0),
                               memory_space=pltpu.SMEM),
                  pl.BlockSpec(memory_space=pl.ANY), pl.BlockSpec(memory_space=pl.ANY),
                  pl.BlockSpec(memory_space=pl.ANY), pl.BlockSpec(memory_space=pl.ANY)],
        out_specs=pl.BlockSpec((TMOE, D_MODEL), lambda i, be, nx, nr, nv: (i, 0)),
        scratch_shapes=[pltpu.VMEM((2, TMOE // GATHER_UNROLL, GATHER_UNROLL, D_MODEL), F32),
                        pltpu.VMEM((D_MODEL, D_FF), F32), pltpu.VMEM((D_MODEL, D_FF), F32),
                        pltpu.VMEM((D_FF, D_MODEL), F32),
                        pltpu.VMEM((D_MODEL, D_FF), BF16), pltpu.VMEM((D_MODEL, D_FF), BF16),
                        pltpu.VMEM((D_FF, D_MODEL), BF16),
                        pltpu.SemaphoreType.DMA((3,)), pltpu.SemaphoreType.DMA((2,))],
    )
    slot_rows = buf_tok.reshape(nblk, 1, TMOE)
    return pl.pallas_call(
        _moe_kernel,
        grid_spec=grid_spec,
        out_shape=jax.ShapeDtypeStruct((rows, D_MODEL), F32),
        compiler_params=_cparams(("arbitrary",)),
        name="moe_experts",
    )(block_e, next_e, n_real, n_valid, slot_rows, slot_rows, x_tok, w_gate, w_up, w_down)


def _final_kernel(idx_ref, idx_next_ref, h_ref, w_ref, nw_ref, y_hbm, op_ref, os_ref, ybuf, row_sems):
    i = pl.program_id(0)
    last = pl.num_programs(0) - 1
    slot = i % 2

    @pl.when(i == 0)
    def _():
        _start_row_gather(idx_ref, y_hbm, ybuf.at[0], row_sems.at[0])

    @pl.when(i + 1 < pl.num_programs(0))
    def _():
        _start_row_gather(idx_next_ref, y_hbm, ybuf.at[1 - slot], row_sems.at[1 - slot])

    _wait_row_gather(y_hbm, ybuf.at[slot], row_sems.at[slot])
    w = w_ref[...]
    hg = HALF // GATHER_UNROLL
    moe = (ybuf[slot, :hg].reshape(HALF, D_MODEL) * w[:, 0:1]
           + ybuf[slot, hg:].reshape(HALF, D_MODEL) * w[:, 1:2])
    h = h_ref[...] + moe
    ms = jnp.mean(h * h, axis=-1, keepdims=True)
    out = h * lax.rsqrt(ms + EPS) * nw_ref[...]

    @pl.when(i < last)
    def _():
        op_ref[...] = out

    @pl.when(i == last)
    def _():
        os_ref[...] = out


def _final(h1, yb, dest, wts, norm_w, seq):
    R = h1.shape[0]
    n = R // HALF
    nbp = seq // HALF
    assert R == (1 + nbp + 1) * HALF
    row = lambda w: pl.BlockSpec((HALF, w), lambda i: (i, 0))
    slots = dest.reshape(TOP_K, n, HALF).transpose(1, 0, 2).reshape(n, 1, TOP_K * HALF)
    idx_spec = lambda off: pl.BlockSpec((None, 1, TOP_K * HALF), lambda i: (jnp.minimum(i + off, n - 1), 0, 0),
                                        memory_space=pltpu.SMEM)
    return pl.pallas_call(
        _final_kernel,
        grid=(n,),
        in_specs=[idx_spec(0), idx_spec(1), row(D_MODEL), row(TOP_K),
                  pl.BlockSpec((1, D_MODEL), lambda i: (0, 0)),
                  pl.BlockSpec(memory_space=pl.ANY)],
        out_specs=[pl.BlockSpec((HALF, D_MODEL), lambda i: (jnp.clip(i - 1, 0, nbp - 1), 0)),
                   pl.BlockSpec((HALF, D_MODEL), lambda i: (0, 0))],
        out_shape=[jax.ShapeDtypeStruct((seq, D_MODEL), F32),
                   jax.ShapeDtypeStruct((HALF, D_MODEL), F32)],
        scratch_shapes=[pltpu.VMEM((2, TOP_K * HALF // GATHER_UNROLL, GATHER_UNROLL, D_MODEL), F32),
                        pltpu.SemaphoreType.DMA((2,))],
        compiler_params=_cparams(("arbitrary",)),
        name="final_norm",
    )(slots, slots, h1, wts, norm_w, yb)


def _route(route):
    R = route.shape[0]
    a_max = TOP_K * R
    nblk = (a_max + N_EXPERTS * (TMOE - 1) + TMOE - 1) // TMOE
    rows = nblk * TMOE
    rank, meta = _rank(route, nblk)
    eid = route[:, :TOP_K].T.astype(I32)
    wts = route[:, TOP_K:2 * TOP_K]
    starts = meta[0, :N_EXPERTS].astype(I32) * TMOE
    n_valid = meta[1, :1].astype(I32)
    block_e = meta[8:8 + nblk, 0].astype(I32)
    next_e = meta[8:8 + nblk, 1].astype(I32)
    n_real = meta[8:8 + nblk, 2].astype(I32)
    is_real = eid >= 0
    experts = jnp.arange(N_EXPERTS, dtype=I32)[:, None, None]
    first_slot = jnp.sum(jnp.where(eid[None] == experts, starts[:, None, None], 0), axis=0)
    dest = jnp.where(is_real, first_slot + rank[:, :TOP_K].T.astype(I32), 0)
    buf_tok = _slot_table(jnp.where(is_real, dest, rows), rows)
    return wts, dest, buf_tok, block_e, next_e, n_real, n_valid


def _slot_table_kernel(dest_ref, out_ref, *, rows_per_choice):
    def clear(r, carry):
        for c in range(LANES):
            out_ref[r, c] = 0
        return carry

    lax.fori_loop(0, out_ref.shape[0], clear, 0)

    def place(r, carry):
        tok0 = jnp.where(r >= rows_per_choice, r - rows_per_choice, r) * LANES
        for c in range(LANES):
            d = dest_ref[r, c]
            out_ref[d >> 7, d & (LANES - 1)] = tok0 + c
        return carry

    lax.fori_loop(0, TOP_K * rows_per_choice, place, 0)


def _slot_table(dest, n_slots):
    n_tok = dest.shape[1]
    assert n_tok % LANES == 0 and n_slots % LANES == 0 and dest.shape[0] == TOP_K == 2
    smem = pl.BlockSpec(memory_space=pltpu.SMEM)
    out = pl.pallas_call(
        functools.partial(_slot_table_kernel, rows_per_choice=n_tok // LANES),
        in_specs=[smem],
        out_specs=smem,
        out_shape=jax.ShapeDtypeStruct((n_slots // LANES + 1, LANES), I32),
        name="slot_table",
    )(dest.reshape(TOP_K * n_tok // LANES, LANES))
    return out.reshape(-1)[:n_slots]


def kernel(x_prompt, x_sample, cache_k, cache_v, state_conv, state_ssm, page_table, meta_tokens, norm_mix_w,
           w_in, lambda_q1, lambda_k1, lambda_q2, lambda_k2, subln_w, conv_w, conv_b, dt_bias, a_log, d_skip,
           ssm_norm_w, w_out, norm_ffn_w, w_router_group, w_router_expert, w_gate, w_up, w_down, final_norm_w):
    depth = w_in.shape[0]
    assert depth == 1 and x_prompt.shape[0] == 1 and x_sample.shape[1] == 1
    seq = x_prompt.shape[1]
    B = x_sample.shape[0]
    n_pages = page_table.shape[1]
    n_phys = cache_k.shape[1]
    past = n_pages * cache_k.shape[2]
    assert cache_k.shape[2] == PAGE and seq % TM == 0 and B == HALF and FRONT == HALF
    n_t = FRONT + seq
    R = n_t + B
    l = 0
    lam_init = 0.8 - 0.6 * math.exp(-0.3 * l)

    front = jnp.concatenate([jnp.zeros((PAD_ROWS, D_MODEL), F32), meta_tokens.astype(F32)], axis=0)
    xp, xs = x_prompt[0], x_sample[:, 0]
    w = w_in[l]
    wq = w[:, :OFF_K].astype(BF16)
    wk = w[:, OFF_K:OFF_V].astype(BF16)
    wv = w[:, OFF_V:OFF_Z].astype(BF16)
    wz = w[:, OFF_Z:OFF_XBC].astype(BF16)
    wx = w[:, OFF_XBC:OFF_DT].astype(BF16)
    wdt = jnp.pad(w[:, OFF_DT:], ((0, 0), (0, LANES - N_HEADS_S))).astype(BF16)

    pos = jnp.concatenate([jnp.maximum(jnp.arange(n_t, dtype=I32) - PAD_ROWS, 0),
                           jnp.full((B,), past, I32)])
    inv = ROPE_THETA ** (-jnp.arange(0, ROT_DIM, 2, dtype=F32) / ROT_DIM)
    ang = pos.astype(F32)[:, None] * inv[None, :]
    cos, sin = jnp.cos(ang), jnp.sin(ang)
    half = ROT_DIM // 2
    ones = jnp.ones((R, DK - ROT_DIM), F32)
    zeros = jnp.zeros((R, DK - ROT_DIM), F32)
    zh = jnp.zeros((R, half), F32)
    ctab = jnp.tile(jnp.concatenate([cos, cos, ones], axis=1), (1, 2))
    s1tab = jnp.tile(jnp.concatenate([-sin, zh, zeros], axis=1), (1, 2))
    s2tab = jnp.tile(jnp.concatenate([zh, sin, zeros], axis=1), (1, 2))

    row1 = lambda v: v.reshape(1, -1).astype(F32)
    pad_heads = lambda v: jnp.pad(v.astype(F32), (0, LANES - N_HEADS_S)).reshape(1, LANES)
    a_row = pad_heads(-jnp.exp(a_log[l].astype(F32)))
    dtb = pad_heads(dt_bias[l])
    dskip_full = jnp.repeat(d_skip[l].astype(F32), SSM_HEAD_DIM).reshape(1, D_INNER)
    normw = row1(ssm_norm_w[l])
    head_of_col = jnp.arange(D_INNER, dtype=I32) // SSM_HEAD_DIM
    e_mat = (jnp.arange(LANES, dtype=I32)[:, None] == head_of_col[None, :]).astype(BF16)
    tri = (jnp.arange(CHUNK)[:, None] >= jnp.arange(CHUNK)[None, :]).astype(BF16)
    lq1, lk1, lq2, lk2 = (row1(v[l]) for v in (lambda_q1, lambda_k1, lambda_q2, lambda_k2))
    sub_w = row1(subln_w[l])

    q, k, v, kb, vb, z, xbc, dt_raw = _inproj(front, xp, xs, row1(norm_mix_w[l]), wq, wk, wv, wz, wx, wdt,
                                              ctab, s1tab, s2tab)

    k_s, v_s = k[n_t * N_KV_HEADS:], v[n_t * N_KV_HEADS:]
    pad8 = lambda a: jnp.pad(a.reshape(B, N_KV_HEADS, LANES), ((0, 0), (0, LANES - N_KV_HEADS), (0, 0)))
    att_all, att_s = _attention(page_table, q, kb, vb, q[n_t:].reshape(B, N_HEADS_A, LANES), pad8(k_s), pad8(v_s),
                                cache_k[l].reshape(n_phys * PAGE * N_KV_HEADS, LANES),
                                cache_v[l].reshape(n_phys * PAGE * N_KV_HEADS, LANES),
                                lq1, lk1, lq2, lk2, sub_w, lam_init)
    ssm_p, state_p = _prompt_ssd(xbc, z, dt_raw, conv_w[l].astype(F32), row1(conv_b[l]), dtb, a_row,
                                 dskip_full, normw, e_mat, e_mat.T.astype(F32), tri, n_t)

    sc = state_conv[l]
    xbc_s = xbc[n_t:]
    ssm_s, state_s = _decode_ssm(xbc_s, sc[:, 0], sc[:, 1], sc[:, 2], dt_raw[n_t:], z[n_t:],
                                 state_ssm[l].reshape(B, D_INNER, D_STATE), conv_w[l].astype(F32),
                                 row1(conv_b[l]), dtb, a_row, dskip_full, normw, e_mat)

    att = lax.dynamic_update_slice(att_all, att_s.reshape(B, N_HEADS_A * DV), (n_t, 0))

    wo = w_out[l].astype(BF16)
    wr = jnp.pad(jnp.concatenate([w_router_group[l], w_router_expert[l]], axis=1).astype(F32),
                 ((0, 0), (0, LANES - N_EGROUPS - N_EXPERTS)))
    wr_hi = wr.astype(BF16)
    wr_lo = (wr - wr_hi.astype(F32)).astype(BF16)
    h1, u2, route = _outproj(front, xp, xs, att, ssm_p, ssm_s, wo[:N_HEADS_A * DV], wo[N_HEADS_A * DV:],
                             row1(norm_ffn_w[l]), wr_hi, wr_lo)

    wts, dest, buf_tok, block_e, next_e, n_real, n_valid = _route(route)
    yb = _moe(block_e, next_e, n_real, n_valid, buf_tok, u2, w_gate[l], w_up[l], w_down[l])
    y_p, y_s = _final(h1, yb, dest, wts, row1(final_norm_w), seq)

    t_real = N_META + seq
    y_prompt = y_p.reshape(1, seq, D_MODEL)
    y_sample = y_s.reshape(B, 1, D_MODEL)
    k_prompt = k[PAD_ROWS * N_KV_HEADS:n_t * N_KV_HEADS].reshape(1, 1, t_real, N_KV_HEADS, 2 * DK)
    v_prompt = v[PAD_ROWS * N_KV_HEADS:n_t * N_KV_HEADS].reshape(1, 1, t_real, N_KV_HEADS, DV)
    conv_prompt = xbc[n_t - (CONV_W - 1):n_t].reshape(1, 1, CONV_W - 1, XBC_DIM)
    ssm_prompt = state_p.reshape(1, 1, N_HEADS_S, SSM_HEAD_DIM, D_STATE)
    k_sample = k_s.reshape(1, B, 1, N_KV_HEADS, 2 * DK)
    v_sample = v_s.reshape(1, B, 1, N_KV_HEADS, DV)
    conv_sample = jnp.stack([sc[:, 1], sc[:, 2], xbc_s], axis=1)[None]
    ssm_sample = state_s.reshape(1, B, N_HEADS_S, SSM_HEAD_DIM, D_STATE)
    return (y_prompt, y_sample, k_prompt, v_prompt, conv_prompt, ssm_prompt,
            k_sample, v_sample, conv_sample, ssm_sample)
```

```python
import functools
import math

import jax
import jax.numpy as jnp
from jax import lax
from jax.experimental import pallas as pl
from jax.experimental.pallas import tpu as pltpu

F32 = jnp.float32
BF16 = jnp.bfloat16
I32 = jnp.int32

D_MODEL = 2048
N_META = 16
EPS = 1e-6
DV = 128
DK = 64
N_HEADS_A = 8
N_KV_HEADS = 4
ROT_DIM = 16
ROPE_THETA = 500000.0
NEG_INF = -1e30
LOG2E = 1.4426950408889634
D_INNER = 1024
SSM_HEAD_DIM = 64
N_HEADS_S = 16
N_BC_GROUPS = 2
D_STATE = 128
CONV_W = 4
XBC_DIM = D_INNER + 2 * N_BC_GROUPS * D_STATE
CHUNK = 128
Q_DIM = N_HEADS_A * 2 * DK
K_DIM = N_KV_HEADS * 2 * DK
V_DIM = N_KV_HEADS * DV
OFF_K = Q_DIM
OFF_V = OFF_K + K_DIM
OFF_Z = OFF_V + V_DIM
OFF_XBC = OFF_Z + D_INNER
OFF_DT = OFF_XBC + XBC_DIM
IN_DIM = OFF_DT + N_HEADS_S
N_EGROUPS = 4
EXPERTS_PER_GROUP = 8
N_EXPERTS = N_EGROUPS * EXPERTS_PER_GROUP
TOP_K = 2
D_FF = 1024
PAGE = 128

LANES = 128
VMEM_LIMIT = 56 * 1024 * 1024

FRONT = CHUNK
PAD_ROWS = FRONT - N_META
TM = 256
TQ = 256
TKF = 1024
TB = 8
TMOE = 256
TSMALL = 128
TF = 256


def _cparams(sem):
    return pltpu.CompilerParams(dimension_semantics=sem, vmem_limit_bytes=VMEM_LIMIT)


def _resident(shape):
    nd = len(shape)
    return pl.BlockSpec(shape, lambda *_: (0,) * nd, pipeline_mode=pl.Buffered(1))


def _split2(x):
    hi = x.astype(BF16)
    lo = (x - hi.astype(F32)).astype(BF16)
    return hi, lo


def _split3(x):
    hi = x.astype(BF16)
    r = x - hi.astype(F32)
    mid = r.astype(BF16)
    lo = (r - mid.astype(F32)).astype(BF16)
    return hi, mid, lo


def _silu(x):
    return x * (1.0 / (1.0 + jnp.exp(-x)))


def _softplus(x):
    return jnp.maximum(x, 0.0) + jnp.log(1.0 + jnp.exp(-jnp.abs(x)))


HALF = TM // 2


def _token_specs(seq):
    nb = seq // HALF
    return [pl.BlockSpec((HALF, D_MODEL), lambda i: (0, 0)),
            pl.BlockSpec((HALF, D_MODEL), lambda i: (jnp.clip(2 * i - 1, 0, nb - 1), 0)),
            pl.BlockSpec((HALF, D_MODEL), lambda i: (jnp.clip(2 * i, 0, nb - 1), 0)),
            pl.BlockSpec((HALF, D_MODEL), lambda i: (0, 0))]


def _token_rows(front_ref, xa_ref, xb_ref, xs_ref):
    i = pl.program_id(0)
    top = jnp.where(i == 0, front_ref[...], xa_ref[...])
    bot = jnp.where(i == pl.num_programs(0) - 1, xs_ref[...], xb_ref[...])
    return jnp.concatenate([top, bot], axis=0)


def _inproj_kernel(front_ref, xa_ref, xb_ref, xs_ref, nw_ref, wq_ref, wk_ref, wv_ref, wz_ref, wx_ref, wdt_ref,
                   c_ref, s1_ref, s2_ref,
                   q_ref, k_ref, v_ref, kb_ref, vb_ref, z_ref, xbc_ref, dt_ref):
    x = _token_rows(front_ref, xa_ref, xb_ref, xs_ref)
    ms = jnp.mean(x * x, axis=-1, keepdims=True)
    u = (x * lax.rsqrt(ms + EPS) * nw_ref[...]).astype(BF16)
    c, s1, s2 = c_ref[...], s1_ref[...], s2_ref[...]

    def rope(p):
        outs = []
        for h in range(p.shape[1] // LANES):
            xh = p[:, h * LANES:(h + 1) * LANES]
            outs.append(xh * c + pltpu.roll(xh, LANES - ROT_DIM // 2, 1) * s1
                        + pltpu.roll(xh, ROT_DIM // 2, 1) * s2)
        return jnp.concatenate(outs, axis=1)

    q = rope(jnp.dot(u, wq_ref[...], preferred_element_type=F32))
    q_ref[...] = (q * (DK ** -0.5 * LOG2E)).astype(BF16)
    k = rope(jnp.dot(u, wk_ref[...], preferred_element_type=F32))
    kb_ref[...] = k.astype(BF16)
    v = jnp.dot(u, wv_ref[...], preferred_element_type=F32)
    for g in range(N_KV_HEADS):
        k_ref[pl.ds(g, TM, stride=N_KV_HEADS), :] = k[:, g * LANES:(g + 1) * LANES]
        v_ref[pl.ds(g, TM, stride=N_KV_HEADS), :] = v[:, g * LANES:(g + 1) * LANES]
    vb = v.astype(BF16)
    ones = jnp.ones((vb.shape[0], DV), BF16)
    vb_ref[...] = jnp.concatenate(
        [piece for g in range(N_KV_HEADS) for piece in (vb[:, g * DV:(g + 1) * DV], ones)], axis=1)
    z_ref[...] = jnp.dot(u, wz_ref[...], preferred_element_type=F32)
    xbc_ref[...] = jnp.dot(u, wx_ref[...], preferred_element_type=F32)
    dt_ref[...] = jnp.dot(u, wdt_ref[...], preferred_element_type=F32)


def _column_block(width, offset):
    assert offset % width == 0
    return pl.BlockSpec((D_MODEL, width), lambda i: (0, offset // width), pipeline_mode=pl.Buffered(1))


def _inproj(front, xp, xs, norm_w, w_all, wdt, ctab, s1tab, s2tab):
    R = front.shape[0] + xp.shape[0] + xs.shape[0]
    row = lambda n: pl.BlockSpec((TM, n), lambda i: (i, 0))
    head_rows = pl.BlockSpec((TM * N_KV_HEADS, LANES), lambda i: (i, 0))
    return pl.pallas_call(
        _inproj_kernel,
        grid=(R // TM,),
        in_specs=_token_specs(xp.shape[0]) + [_resident((1, D_MODEL)),
                  _column_block(Q_DIM, 0), _column_block(K_DIM, OFF_K), _column_block(V_DIM, OFF_V),
                  _column_block(D_INNER, OFF_Z), _column_block(XBC_DIM, OFF_XBC), _resident(wdt.shape),
                  row(LANES), row(LANES), row(LANES)],
        out_specs=[row(Q_DIM), head_rows, head_rows, row(K_DIM), row(2 * V_DIM),
                   row(D_INNER), row(XBC_DIM), row(LANES)],
        out_shape=[jax.ShapeDtypeStruct((R, Q_DIM), BF16),
                   jax.ShapeDtypeStruct((R * N_KV_HEADS, LANES), F32),
                   jax.ShapeDtypeStruct((R * N_KV_HEADS, LANES), F32),
                   jax.ShapeDtypeStruct((R, K_DIM), BF16),
                   jax.ShapeDtypeStruct((R, 2 * V_DIM), BF16),
                   jax.ShapeDtypeStruct((R, D_INNER), F32),
                   jax.ShapeDtypeStruct((R, XBC_DIM), F32),
                   jax.ShapeDtypeStruct((R, LANES), F32)],
        compiler_params=_cparams(("arbitrary",)),
        name="inproj",
    )(front, xp, xp, xs, norm_w, w_all, w_all, w_all, w_all, w_all, wdt, ctab, s1tab, s2tab)


def _diff_lambda(lq1, lk1, lq2, lk2, lam_init):
    a = jnp.sum(lq1[...] * lk1[...], axis=-1, keepdims=True)
    b = jnp.sum(lq2[...] * lk2[...], axis=-1, keepdims=True)
    return jnp.exp(a) - jnp.exp(b) + lam_init


def _subln(o, sub_w, lam_init):
    ms = jnp.mean(o * o, axis=-1, keepdims=True)
    return (o * lax.rsqrt(ms + EPS) * sub_w) * (1.0 - lam_init)


def _attn_kernel(lq1, lk1, lq2, lk2, sub_ref, q_ref, k_ref, ve_ref, o_ref,
                 qs_scr, m_scr, acc_scr, *, lam_init, n_rows):
    qi = pl.program_id(1)
    lam = _diff_lambda(lq1, lk1, lq2, lk2, lam_init)

    q = q_ref[...]
    lane = lax.broadcasted_iota(I32, (TQ, LANES), 1)
    first = lane < DK
    zero = jnp.zeros((TQ, LANES), BF16)
    for hh in range(2):
        qh = q[:, hh * LANES:(hh + 1) * LANES]
        qs_scr[(2 * hh) * TQ:(2 * hh + 1) * TQ, :] = jnp.where(first, qh, zero)
        qs_scr[(2 * hh + 1) * TQ:(2 * hh + 2) * TQ, :] = jnp.where(first, zero, qh)
    m_scr[...] = jnp.full(m_scr.shape, NEG_INF, F32)
    acc_scr[...] = jnp.zeros(acc_scr.shape, F32)

    nt = (((1,), (1,)), ((), ()))

    def process(start, width, keep):
        for part in range(4):
            rows = slice(part * TQ, (part + 1) * TQ)
            s = lax.dot_general(qs_scr[rows, :], k_ref[pl.ds(start, width), :], nt,
                                preferred_element_type=F32)
            if keep is not None:
                s = jnp.where(keep, s, NEG_INF)
            m_prev = m_scr[rows, :]
            m_new = jnp.maximum(m_prev, jnp.max(s, axis=-1, keepdims=True))
            alpha = jnp.exp2(m_prev - m_new)
            p = jnp.concatenate([jnp.exp2(s[:, t * LANES:(t + 1) * LANES] - m_new)
                                 for t in range(width // LANES)], axis=1).astype(BF16)
            pv = jnp.dot(p, ve_ref[pl.ds(start, width), :], preferred_element_type=F32)
            acc_scr[rows, :LANES] = alpha * acc_scr[rows, :LANES] + pv[:, :LANES]
            acc_scr[rows, LANES:] = alpha * acc_scr[rows, LANES:] + pv[:, LANES:]
            m_scr[rows, :] = m_new

    q_lo = qi * TQ
    n_full = jnp.maximum(q_lo - PAD_ROWS, 0) // TKF
    tail_lo = PAD_ROWS + n_full * TKF
    n_tail = (q_lo + TQ - tail_lo + TQ - 1) // TQ

    def full(c):
        process(pl.multiple_of(PAD_ROWS + c * TKF, 16), TKF, None)

    def full_pair(c2, carry):
        full(2 * c2)
        full(2 * c2 + 1)
        return carry

    lax.fori_loop(0, n_full // 2, full_pair, 0)

    @pl.when(n_full % 2 == 1)
    def _():
        full(n_full - 1)

    def tail(w):
        width = w * TQ

        def run():
            start = pl.multiple_of(q_lo + TQ - width, 16)
            col = lax.broadcasted_iota(I32, (TQ, width), 1)
            qpos = q_lo + lax.broadcasted_iota(I32, (TQ, width), 0)
            process(start, width, (col <= qpos - start) & (col >= tail_lo - start))

        return run

    for w in range(1, TKF // TQ + 2):
        pl.when(n_tail == w)(tail(w))

    acc = acc_scr[...]
    o = acc[:, :LANES] / acc[:, LANES:]
    sub_w = sub_ref[...]
    for hh in range(2):
        o1 = o[(2 * hh) * TQ:(2 * hh + 1) * TQ, :]
        o2 = o[(2 * hh + 1) * TQ:(2 * hh + 2) * TQ, :]
        o_ref[:, hh * LANES:(hh + 1) * LANES] = _subln(o1 - lam * o2, sub_w, lam_init).astype(BF16)


def _expand_heads(x, e_ref):
    hi, lo = _split2(x)
    e = e_ref[...]
    return (jnp.dot(hi, e, preferred_element_type=F32)
            + jnp.dot(lo, e, preferred_element_type=F32))


def _gated_out(y, xs, z, dskip_full, normw):
    y = (y + dskip_full * xs) * _silu(z)
    half = D_INNER // N_BC_GROUPS
    outs = []
    for g in range(N_BC_GROUPS):
        yg = y[:, g * half:(g + 1) * half]
        ms = jnp.mean(yg * yg, axis=-1, keepdims=True)
        outs.append(yg * lax.rsqrt(ms + EPS))
    return jnp.concatenate(outs, axis=1) * normw


def _ssd_kernel(xbc_ref, z_ref, dt_ref, cw_ref, cb_ref, dtb_ref, a_ref, dskip_ref, normw_ref,
                e_ref, et_ref, tri_ref, ssm_ref, state_ref, ext_scr, h_scr):
    c = pl.program_id(0)
    L = CHUNK

    @pl.when(c == 0)
    def _():
        ext_scr[0:8, :] = jnp.zeros((8, XBC_DIM), F32)
        h_scr[...] = jnp.zeros(h_scr.shape, F32)

    x_new = xbc_ref[...]
    ext_scr[8:8 + L, :] = x_new
    conv = cb_ref[...] + cw_ref[CONV_W - 1:CONV_W, :] * x_new
    for j in range(CONV_W - 1):
        sh = CONV_W - 1 - j
        conv = conv + cw_ref[j:j + 1, :] * ext_scr[8 - sh:8 - sh + L, :]
    ext_scr[0:8, :] = x_new[L - 8:L, :]
    xc = _silu(conv)
    xs = xc[:, :D_INNER]

    rowi = lax.broadcasted_iota(I32, (L, LANES), 0)
    coli = lax.broadcasted_iota(I32, (L, LANES), 1)
    valid = (c * L + rowi) >= PAD_ROWS
    dt = jnp.where(valid, _softplus(dt_ref[...] + dtb_ref[...]), 0.0)
    da = dt * a_ref[...]
    tri = tri_ref[...]
    acs = sum(jnp.dot(tri, part, preferred_element_type=F32) for part in _split3(da))
    acs_t = acs.T
    dt_t = dt.T
    causal = coli <= rowi
    last = acs[L - 1:L, :]

    w_state = _expand_heads(dt * jnp.exp(last - acs), e_ref)
    dec_out = _expand_heads(jnp.exp(acs), e_ref)
    dec_rows = jnp.sum(et_ref[...] * jnp.exp(last), axis=-1, keepdims=True)
    xw = xs * w_state
    xs_b = xs.astype(BF16)

    half = D_INNER // N_BC_GROUPS
    hpg = N_HEADS_S // N_BC_GROUPS
    y_parts = []
    for g in range(N_BC_GROUPS):
        bm = xc[:, D_INNER + g * D_STATE:D_INNER + (g + 1) * D_STATE].astype(BF16)
        cm = xc[:, D_INNER + (N_BC_GROUPS + g) * D_STATE:
                D_INNER + (N_BC_GROUPS + g + 1) * D_STATE].astype(BF16)
        cb = lax.dot_general(cm, bm, (((1,), (1,)), ((), ())), preferred_element_type=F32)
        h_prev = h_scr[g * half:(g + 1) * half, :]
        y_off = lax.dot_general(cm, h_prev.astype(BF16), (((1,), (1,)), ((), ())),
                                preferred_element_type=F32)
        y_off = y_off * dec_out[:, g * half:(g + 1) * half]
        yd = []
        for r in range(0, hpg, 2):
            pair = []
            for hh in (r, r + 1):
                h = g * hpg + hh
                seg = jnp.broadcast_to(acs[:, h:h + 1], (L, L)) - acs_t[h:h + 1, :]
                lmat = jnp.where(causal, jnp.exp(jnp.where(causal, seg, 0.0)), 0.0)
                pair.append((cb * lmat * dt_t[h:h + 1, :]).astype(BF16))
            h0 = g * hpg + r
            xp = xs_b[:, h0 * SSM_HEAD_DIM:(h0 + 2) * SSM_HEAD_DIM]
            ya = jnp.dot(pair[0], xp, preferred_element_type=F32)
            yb = jnp.dot(pair[1], xp, preferred_element_type=F32)
            yd.append(jnp.where(coli < SSM_HEAD_DIM, ya, yb))
        y_parts.append(jnp.concatenate(yd, axis=1) + y_off)
        xw_t = xw[:, g * half:(g + 1) * half].T.astype(BF16)
        st = jnp.dot(xw_t, bm, preferred_element_type=F32)
        h_scr[g * half:(g + 1) * half, :] = dec_rows[g * half:(g + 1) * half, :] * h_prev + st
    y = jnp.concatenate(y_parts, axis=1)

    ssm_ref[...] = _gated_out(y, xs, z_ref[...], dskip_ref[...], normw_ref[...]).astype(BF16)

    @pl.when(c == pl.num_programs(0) - 1)
    def _():
        state_ref[...] = h_scr[...]


def _prompt_ssd(xbc, z, dt_raw, conv_w, conv_b, dtb, a_row, dskip_full, normw, e_mat, e_t, tri, n_t):
    row = lambda n: pl.BlockSpec((CHUNK, n), lambda c: (c, 0))
    const = lambda shp: pl.BlockSpec(shp, lambda c: (0, 0))
    return pl.pallas_call(
        _ssd_kernel,
        grid=(n_t // CHUNK,),
        in_specs=[row(XBC_DIM), row(D_INNER), row(LANES),
                  const((CONV_W, XBC_DIM)), const((1, XBC_DIM)), const((1, LANES)), const((1, LANES)),
                  const((1, D_INNER)), const((1, D_INNER)), const((LANES, D_INNER)),
                  const((D_INNER, LANES)), const((CHUNK, CHUNK))],
        out_specs=[row(D_INNER), pl.BlockSpec((D_INNER, D_STATE), lambda c: (0, 0))],
        out_shape=[jax.ShapeDtypeStruct((n_t, D_INNER), BF16),
                   jax.ShapeDtypeStruct((D_INNER, D_STATE), F32)],
        scratch_shapes=[pltpu.VMEM((8 + CHUNK, XBC_DIM), F32),
                        pltpu.VMEM((D_INNER, D_STATE), F32)],
        compiler_params=_cparams(("arbitrary",)),
        name="prompt_ssd",
    )(xbc, z, dt_raw, conv_w, conv_b, dtb, a_row, dskip_full, normw, e_mat, e_t, tri)


def _decode_attn_kernel(pt_ref, lq1, lk1, lq2, lk2, sub_ref, q_ref, kn_ref, vn_ref, *rest,
                        n_pages, lam_init):
    k_pages = rest[:n_pages]
    v_pages = rest[n_pages:2 * n_pages]
    o_ref = rest[2 * n_pages]
    del pt_ref
    lam = _diff_lambda(lq1, lk1, lq2, lk2, lam_init)
    nq = 2 * N_HEADS_A
    q8 = q_ref[...]
    lane = lax.broadcasted_iota(I32, (N_HEADS_A, LANES), 1)
    zero = jnp.zeros((N_HEADS_A, LANES), BF16)
    q16 = jnp.concatenate([jnp.where(lane < DK, q8, zero), jnp.where(lane < DK, zero, q8)], axis=0)

    pw = PAGE * N_KV_HEADS
    row_head = (lax.broadcasted_iota(I32, (nq, pw), 0) & (N_HEADS_A - 1)) >> 1
    col_head = lax.broadcasted_iota(I32, (nq, pw), 1) & (N_KV_HEADS - 1)
    own = row_head == col_head
    nt = (((1,), (1,)), ((), ()))
    s_pages = []
    for j in range(n_pages):
        kp = k_pages[j][...].astype(BF16)
        s = lax.dot_general(q16, kp, nt, preferred_element_type=F32)
        s_pages.append(jnp.where(own, s, NEG_INF))
    kn = kn_ref[...].astype(BF16)
    row_head_n = (lax.broadcasted_iota(I32, (nq, LANES), 0) & (N_HEADS_A - 1)) >> 1
    col_n = lax.broadcasted_iota(I32, (nq, LANES), 1)
    s_new = jnp.where(row_head_n == col_n, lax.dot_general(q16, kn, nt, preferred_element_type=F32), NEG_INF)

    m = jnp.max(s_new, axis=-1, keepdims=True)
    for s in s_pages:
        m = jnp.maximum(m, jnp.max(s, axis=-1, keepdims=True))
    p_new = jnp.exp2(s_new - m)
    l = jnp.sum(p_new, axis=-1, keepdims=True)
    acc = jnp.dot(p_new.astype(BF16), vn_ref[...].astype(BF16), preferred_element_type=F32)
    for j in range(n_pages):
        p = jnp.exp2(s_pages[j] - m)
        l = l + jnp.sum(p, axis=-1, keepdims=True)
        acc = acc + jnp.dot(p.astype(BF16), v_pages[j][...].astype(BF16), preferred_element_type=F32)
    o = acc / l
    res = o[:N_HEADS_A, :] - lam * o[N_HEADS_A:, :]
    o_ref[...] = _subln(res, sub_ref[...], lam_init).astype(BF16)


def _fused_attn_kernel(pt_ref, lq1, lk1, lq2, lk2, sub_ref, q_ref, k_ref, ve_ref, qd_ref, kn_ref, vn_ref, *rest,
                       n_pages, n_dec, lam_init, n_rows):
    pages = rest[:2 * n_pages]
    o_ref, od_ref = rest[2 * n_pages], rest[2 * n_pages + 1]
    scratch = rest[2 * n_pages + 2:]

    @pl.when(pl.program_id(1) < n_rows // TQ)
    def _():
        _attn_kernel(lq1, lk1, lq2, lk2, sub_ref, q_ref, k_ref, ve_ref, o_ref, *scratch,
                     lam_init=lam_init, n_rows=n_rows)

    step = pl.program_id(0) * pl.num_programs(1) + pl.program_id(1)

    @pl.when(step < n_dec)
    def _():
        _decode_attn_kernel(pt_ref, lq1, lk1, lq2, lk2, sub_ref, qd_ref, kn_ref, vn_ref, *pages, od_ref,
                            n_pages=n_pages, lam_init=lam_init)


def _attention(page_table, q, kb, vbe, q_s, k_new, v_new, cache_k2, cache_v2, lq1, lk1, lq2, lk2, sub_w,
               lam_init):
    R = q.shape[0]
    nq = R // TQ
    B, n_pages = page_table.shape
    assert R % TQ == 0 and TKF % TQ == 0
    n_inner = max(nq, pl.cdiv(B, N_KV_HEADS))
    pw = PAGE * N_KV_HEADS
    seq_of = lambda g, i: jnp.minimum(g * n_inner + i, B - 1)
    qblk = lambda i: jnp.minimum(i, nq - 1)
    small = lambda n: pl.BlockSpec((1, n), lambda g, i, pt: (0, 0))
    per_b = pl.BlockSpec((None, 8, LANES), lambda g, i, pt: (seq_of(g, i), 0, 0))
    new_tok = pl.BlockSpec((None, LANES, LANES), lambda g, i, pt: (seq_of(g, i), 0, 0))
    page_spec = lambda j: pl.BlockSpec((pw, LANES), lambda g, i, pt: (pt[seq_of(g, i), j], 0))
    grid_spec = pltpu.PrefetchScalarGridSpec(
        num_scalar_prefetch=1,
        grid=(N_KV_HEADS, n_inner),
        in_specs=[small(DK), small(DK), small(DK), small(DK), small(DV),
                  pl.BlockSpec((TQ, 2 * LANES), lambda g, i, pt: (qblk(i), g)),
                  pl.BlockSpec((R, LANES), lambda g, i, pt: (0, g)),
                  pl.BlockSpec((R, 2 * LANES), lambda g, i, pt: (0, g)),
                  per_b, new_tok, new_tok]
                 + [page_spec(j) for j in range(n_pages)] + [page_spec(j) for j in range(n_pages)],
        out_specs=[pl.BlockSpec((TQ, 2 * LANES), lambda g, i, pt: (qblk(i), g)), per_b],
        scratch_shapes=[pltpu.VMEM((4 * TQ, LANES), BF16),
                        pltpu.VMEM((4 * TQ, LANES), F32),
                        pltpu.VMEM((4 * TQ, 2 * LANES), F32)],
    )
    return pl.pallas_call(
        functools.partial(_fused_attn_kernel, n_pages=n_pages, n_dec=B, lam_init=lam_init, n_rows=R),
        grid_spec=grid_spec,
        out_shape=[jax.ShapeDtypeStruct((R, N_HEADS_A * DV), BF16),
                   jax.ShapeDtypeStruct((B, 8, LANES), BF16)],
        compiler_params=_cparams(("arbitrary", "arbitrary")),
        name="attention",
    )(page_table, lq1, lk1, lq2, lk2, sub_w, q, kb, vbe, q_s, k_new, v_new,
      *([cache_k2] * n_pages), *([cache_v2] * n_pages))


def _dec_pre_kernel(xbc_ref, s0_ref, s1_ref, s2_ref, dt_ref, cw_ref, cb_ref, dtb_ref, a_ref, e_ref,
                    xc_ref, xdt_ref, dect_ref):
    conv = (cb_ref[...] + cw_ref[0:1, :] * s0_ref[...] + cw_ref[1:2, :] * s1_ref[...]
            + cw_ref[2:3, :] * s2_ref[...] + cw_ref[3:4, :] * xbc_ref[...])
    xc = _silu(conv)
    xc_ref[...] = xc
    dt = _softplus(dt_ref[...] + dtb_ref[...])
    dec = jnp.exp(dt * a_ref[...])
    hi, mid, lo = _split3(dt)
    e = e_ref[...]
    dt_full = (jnp.dot(hi, e, preferred_element_type=F32) + jnp.dot(mid, e, preferred_element_type=F32)
               + jnp.dot(lo, e, preferred_element_type=F32))
    hi, mid, lo = _split3(dec)
    dec_full = (jnp.dot(hi, e, preferred_element_type=F32) + jnp.dot(mid, e, preferred_element_type=F32)
                + jnp.dot(lo, e, preferred_element_type=F32))
    xdt_ref[...] = (dt_full * xc[:, :D_INNER]).T
    dect_ref[...] = dec_full.T


def _dec_state_kernel(h_ref, xdt_ref, dect_ref, bm_ref, cm_ref, hout_ref, y_ref):
    i = pl.program_id(0)
    B = xdt_ref.shape[1]
    lane = lax.broadcasted_iota(I32, (D_INNER, B), 1)
    sub = lax.broadcasted_iota(I32, (TB, D_INNER // N_BC_GROUPS), 0)
    half = D_INNER // N_BC_GROUPS
    nt = (((1,), (1,)), ((), ()))
    y_acc = [jnp.zeros((TB, half), F32) for _ in range(N_BC_GROUPS)]
    start = pl.multiple_of(i * TB, TB)
    for j in range(TB):
        b = i * TB + j
        sel = lane == b
        xcol = jnp.sum(jnp.where(sel, xdt_ref[...], 0.0), axis=-1, keepdims=True)
        dcol = jnp.sum(jnp.where(sel, dect_ref[...], 0.0), axis=-1, keepdims=True)
        hnew = dcol * h_ref[j]
        for g in range(N_BC_GROUPS):
            brow = bm_ref[pl.ds(start, TB), g * D_STATE:(g + 1) * D_STATE][j:j + 1, :]
            hg = hnew[g * half:(g + 1) * half] + xcol[g * half:(g + 1) * half] * brow
            hout_ref[j, g * half:(g + 1) * half, :] = hg
            c8 = cm_ref[pl.ds(start, TB), g * D_STATE:(g + 1) * D_STATE].astype(BF16)
            yj = lax.dot_general(c8, hg.astype(BF16), nt, preferred_element_type=F32)
            y_acc[g] = y_acc[g] + jnp.where(sub == j, yj, 0.0)
    y_ref[...] = jnp.concatenate(y_acc, axis=1)


def _dec_post_kernel(y_ref, xc_ref, z_ref, dskip_ref, normw_ref, o_ref):
    o_ref[...] = _gated_out(y_ref[...], xc_ref[:, :D_INNER], z_ref[...], dskip_ref[...],
                            normw_ref[...]).astype(BF16)


def _decode_ssm(xbc_s, sc0, sc1, sc2, dt_s, z_s, state, conv_w, conv_b, dtb, a_row, dskip_full, normw, e_mat):
    B = xbc_s.shape[0]
    xc, xdt_t, dec_t = pl.pallas_call(
        _dec_pre_kernel,
        out_shape=[jax.ShapeDtypeStruct((B, XBC_DIM), F32),
                   jax.ShapeDtypeStruct((D_INNER, B), F32),
                   jax.ShapeDtypeStruct((D_INNER, B), F32)],
        compiler_params=pltpu.CompilerParams(vmem_limit_bytes=VMEM_LIMIT),
        name="decode_ssm_pre",
    )(xbc_s, sc0, sc1, sc2, dt_s, conv_w, conv_b, dtb, a_row, e_mat)
    bm = xc[:, D_INNER:D_INNER + N_BC_GROUPS * D_STATE]
    cm = xc[:, D_INNER + N_BC_GROUPS * D_STATE:]
    const = lambda shp: pl.BlockSpec(shp, lambda i: (0,) * len(shp))
    state_spec = pl.BlockSpec((TB, D_INNER, D_STATE), lambda i: (i, 0, 0))
    h_new, y = pl.pallas_call(
        _dec_state_kernel,
        grid=(B // TB,),
        in_specs=[state_spec, const((D_INNER, B)), const((D_INNER, B)),
                  const((B, N_BC_GROUPS * D_STATE)), const((B, N_BC_GROUPS * D_STATE))],
        out_specs=[state_spec, pl.BlockSpec((TB, D_INNER), lambda i: (i, 0))],
        out_shape=[jax.ShapeDtypeStruct((B, D_INNER, D_STATE), F32),
                   jax.ShapeDtypeStruct((B, D_INNER), F32)],
        compiler_params=_cparams(("arbitrary",)),
        name="decode_ssm_state",
    )(state, xdt_t, dec_t, bm, cm)
    ssm = pl.pallas_call(
        _dec_post_kernel,
        out_shape=jax.ShapeDtypeStruct((B, D_INNER), BF16),
        compiler_params=pltpu.CompilerParams(vmem_limit_bytes=VMEM_LIMIT),
        name="decode_ssm_post",
    )(y, xc, z_s, dskip_full, normw)
    return ssm, h_new


def _outproj_kernel(front_ref, xa_ref, xb_ref, xs_ref, att_ref, sa_ref, sb_ref, ss_ref,
                    wa_ref, ws_ref, nw_ref, wrh_ref, wrl_ref, h_ref, u_ref, lg_ref):
    last = pl.program_id(0) == pl.num_programs(0) - 1
    ssm = jnp.concatenate([sa_ref[...], jnp.where(last, ss_ref[...], sb_ref[...])], axis=0)
    acc = jnp.dot(att_ref[...], wa_ref[...], preferred_element_type=F32)
    acc = acc + jnp.dot(ssm, ws_ref[...], preferred_element_type=F32)
    h = _token_rows(front_ref, xa_ref, xb_ref, xs_ref) + acc
    h_ref[...] = h
    ms = jnp.mean(h * h, axis=-1, keepdims=True)
    u = h * lax.rsqrt(ms + EPS) * nw_ref[...]
    u_ref[...] = u
    uh, ul = _split2(u)
    wh, wl = wrh_ref[...], wrl_ref[...]
    lg = (jnp.dot(uh, wh, preferred_element_type=F32) + jnp.dot(uh, wl, preferred_element_type=F32)
          + jnp.dot(ul, wh, preferred_element_type=F32))
    lg_ref[...] = _route_rows(lg, pl.program_id(0) * TM)


def _first_max(x, lane):
    m = jnp.max(x, axis=-1, keepdims=True)
    return m, jnp.min(jnp.where(x == m, lane, float(LANES)), axis=-1, keepdims=True)


def _route_rows(lg, row0):
    rows = lg.shape[0]
    lane_i = lax.broadcasted_iota(I32, (rows, LANES), 1)
    lane = lane_i.astype(F32)
    gl = jnp.where(lane_i < N_EGROUPS, lg, -jnp.inf)
    gmax, g = _first_max(gl, lane)
    pg = 1.0 / jnp.sum(jnp.exp(gl - gmax), axis=-1, keepdims=True)
    lo = N_EGROUPS + g * EXPERTS_PER_GROUP
    el = jnp.where((lane >= lo) & (lane < lo + EXPERTS_PER_GROUP), lg, -jnp.inf)
    m1, i1 = _first_max(el, lane)
    m2, i2 = _first_max(jnp.where(lane == i1, -jnp.inf, el), lane)
    z = jnp.sum(jnp.exp(el - m1), axis=-1, keepdims=True)
    p1 = 1.0 / z
    p2 = jnp.exp(m2 - m1) / z
    tot = p1 + p2
    valid = (row0 + lax.broadcasted_iota(I32, (rows, 1), 0)) >= PAD_ROWS
    e1 = jnp.where(valid, i1 - N_EGROUPS, -1.0)
    e2 = jnp.where(valid, i2 - N_EGROUPS, -1.0)
    w1 = jnp.where(valid, p1 / tot * pg, 0.0)
    w2 = jnp.where(valid, p2 / tot * pg, 0.0)
    return jnp.where(lane_i == 0, e1, jnp.where(lane_i == 1, e2, jnp.where(lane_i == 2, w1,
                                                                         jnp.where(lane_i == 3, w2, 0.0))))


def _rank_kernel(route_ref, tri_ref, utri_ref, rank_ref, meta_ref, carry, *, nblk):
    i = pl.program_id(0)

    @pl.when(i == 0)
    def _():
        carry[...] = jnp.zeros(carry.shape, F32)

    r = route_ref[...]
    lane = lax.broadcasted_iota(I32, r.shape, 1).astype(F32)
    oh0 = (lane == r[:, 0:1]).astype(F32)
    oh1 = (lane == r[:, 1:2]).astype(F32)
    oh = oh0 + oh1
    before = jnp.dot(tri_ref[...], oh.astype(BF16), preferred_element_type=F32) + carry[0:1, :]
    rank0 = jnp.sum(before * oh0, axis=-1, keepdims=True)
    rank1 = jnp.sum(before * oh1, axis=-1, keepdims=True)
    lane_i = lax.broadcasted_iota(I32, r.shape, 1)
    rank_ref[...] = jnp.where(lane_i == 0, rank0, jnp.where(lane_i == 1, rank1, 0.0))
    carry[0:1, :] = carry[0:1, :] + jnp.sum(oh, axis=0, keepdims=True)

    @pl.when(i == pl.num_programs(0) - 1)
    def _():
        nb_pad = meta_ref.shape[0] - 8
        counts = carry[0:1, :]
        pb = jnp.floor((counts + (TMOE - 1)) / TMOE)
        ends = jnp.dot(jnp.broadcast_to(pb, (8, LANES)).astype(BF16), utri_ref[...],
                       preferred_element_type=F32)[0:1, :]
        starts = ends - pb
        lane1 = lax.broadcasted_iota(I32, (1, LANES), 1)
        n_valid = jnp.sum(jnp.where(lane1 == N_EXPERTS - 1, ends, 0.0), axis=-1, keepdims=True)
        blk = lax.broadcasted_iota(I32, (nb_pad, LANES), 0).astype(F32)
        lane_b = lax.broadcasted_iota(I32, (nb_pad, LANES), 1)
        is_e = lane_b < N_EXPERTS
        block_e = jnp.minimum(jnp.sum(jnp.where(is_e & (ends <= blk), 1.0, 0.0), axis=-1, keepdims=True),
                              N_EXPERTS - 1.0)
        lane_bf = lane_b.astype(F32)
        has = is_e & (counts > 0.0) & (lane_bf > block_e)
        nxt = jnp.min(jnp.where(has, lane_bf, float(LANES)), axis=-1, keepdims=True)
        next_e = jnp.where(nxt >= LANES, -1.0, nxt)
        mine = lane_bf == block_e
        cnt_b = jnp.sum(jnp.where(mine, counts, 0.0), axis=-1, keepdims=True)
        st_b = jnp.sum(jnp.where(mine, starts, 0.0), axis=-1, keepdims=True)
        n_real = jnp.clip(cnt_b - (blk[:, 0:1] - st_b) * TMOE, 0.0, float(TMOE))
        meta_ref[0:8, :] = jnp.where(lax.broadcasted_iota(I32, (8, LANES), 0) == 0, starts,
                                     jnp.broadcast_to(n_valid, (8, LANES)))
        meta_ref[8:, :] = jnp.where(lane_b == 0, block_e, jnp.where(lane_b == 1, next_e,
                                                                    jnp.where(lane_b == 2, n_real, 0.0)))


def _rank(route, nblk):
    R = route.shape[0]
    nb_pad = (nblk + 7) // 8 * 8
    tri = (jnp.arange(TM)[:, None] > jnp.arange(TM)[None, :]).astype(BF16)
    utri = (jnp.arange(LANES)[:, None] <= jnp.arange(LANES)[None, :]).astype(BF16)
    return pl.pallas_call(
        functools.partial(_rank_kernel, nblk=nblk),
        grid=(R // TM,),
        in_specs=[pl.BlockSpec((TM, LANES), lambda i: (i, 0)),
                  pl.BlockSpec((TM, TM), lambda i: (0, 0)),
                  pl.BlockSpec((LANES, LANES), lambda i: (0, 0))],
        out_specs=[pl.BlockSpec((TM, LANES), lambda i: (i, 0)),
                   pl.BlockSpec((8 + nb_pad, LANES), lambda i: (0, 0))],
        out_shape=[jax.ShapeDtypeStruct((R, LANES), F32),
                   jax.ShapeDtypeStruct((8 + nb_pad, LANES), F32)],
        scratch_shapes=[pltpu.VMEM((8, LANES), F32)],
        compiler_params=_cparams(("arbitrary",)),
        name="expert_rank",
    )(route, tri, utri)


def _outproj(front, xp, xs, att, ssm_p, ssm_s, wa, ws, norm_w, wr_hi, wr_lo):
    R = front.shape[0] + xp.shape[0] + xs.shape[0]
    row = lambda n: pl.BlockSpec((TM, n), lambda i: (i, 0))
    nbp = ssm_p.shape[0] // HALF
    half = lambda f: pl.BlockSpec((HALF, D_INNER), f)
    return pl.pallas_call(
        _outproj_kernel,
        grid=(R // TM,),
        in_specs=_token_specs(xp.shape[0]) + [
            row(N_HEADS_A * DV),
            half(lambda i: (2 * i, 0)), half(lambda i: (jnp.minimum(2 * i + 1, nbp - 1), 0)),
            half(lambda i: (0, 0)),
            _resident(wa.shape), _resident(ws.shape), _resident((1, D_MODEL)),
            _resident(wr_hi.shape), _resident(wr_lo.shape)],
        out_specs=[row(D_MODEL), row(D_MODEL), row(LANES)],
        out_shape=[jax.ShapeDtypeStruct((R, D_MODEL), F32),
                   jax.ShapeDtypeStruct((R, D_MODEL), F32),
                   jax.ShapeDtypeStruct((R, LANES), F32)],
        compiler_params=_cparams(("arbitrary",)),
        name="outproj",
    )(front, xp, xp, xs, att, ssm_p, ssm_p, ssm_s, wa, ws, norm_w, wr_hi, wr_lo)


GATHER_UNROLL = 8

def _start_row_gather(idx_ref, src_hbm, dst, sem, n_groups=None):
    def start(g, carry):
        for j in range(GATHER_UNROLL):
            pltpu.make_async_copy(src_hbm.at[pl.ds(idx_ref[0, g * GATHER_UNROLL + j], 1)],
                                  dst.at[g, pl.ds(j, 1)], sem).start(priority=j % 2)
        return carry

    lax.fori_loop(0, dst.shape[0] if n_groups is None else n_groups, start, 0)


def _wait_row_gather(src_hbm, dst, sem, n_groups=None):
    def wait(g, carry):
        pltpu.make_async_copy(src_hbm.at[pl.ds(0, GATHER_UNROLL)], dst.at[0], sem).wait()
        return carry

    lax.fori_loop(0, dst.shape[0] if n_groups is None else n_groups, wait, 0)


def _row_groups(n_real):
    return (n_real + (GATHER_UNROLL - 1)) // GATHER_UNROLL


def _moe_kernel(be_ref, nx_ref, nr_ref, nv_ref, idx_ref, idx_next_ref, x_hbm, wg_hbm, wu_hbm, wd_hbm, o_ref,
                xbuf, stage_g, stage_u, stage_d, wg_b, wu_b, wd_b, sems, row_sems):
    i = pl.program_id(0)
    valid = i < nv_ref[0]
    e = be_ref[i]
    changed = (i == 0) | (be_ref[jnp.maximum(i - 1, 0)] != e)
    slot = i % 2
    n_real = nr_ref[i]

    @pl.when(i == 0)
    def _():
        xbuf[...] = jnp.zeros(xbuf.shape, F32)
        _start_row_gather(idx_ref, x_hbm, xbuf.at[0], row_sems.at[0], _row_groups(n_real))

    @pl.when(i + 1 < nv_ref[0])
    def _():
        _start_row_gather(idx_next_ref, x_hbm, xbuf.at[1 - slot], row_sems.at[1 - slot],
                          _row_groups(nr_ref[jnp.minimum(i + 1, pl.num_programs(0) - 1)]))

    def fetch(expert):
        return (pltpu.make_async_copy(wg_hbm.at[expert], stage_g, sems.at[0]),
                pltpu.make_async_copy(wu_hbm.at[expert], stage_u, sems.at[1]),
                pltpu.make_async_copy(wd_hbm.at[expert], stage_d, sems.at[2]))

    @pl.when(i == 0)
    def _():
        for c in fetch(e):
            c.start()

    @pl.when(valid & changed)
    def _():
        for c in fetch(e):
            c.wait()
        wg_b[...] = stage_g[...].astype(BF16)
        wu_b[...] = stage_u[...].astype(BF16)
        wd_b[...] = stage_d[...].astype(BF16)
        nxt = nx_ref[i]

        @pl.when(nxt >= 0)
        def _():
            for c in fetch(nxt):
                c.start()

    def mlp(rows):
        x = xbuf[slot, :rows // GATHER_UNROLL].reshape(rows, D_MODEL).astype(BF16)
        g = jnp.dot(x, wg_b[...], preferred_element_type=F32)
        u = jnp.dot(x, wu_b[...], preferred_element_type=F32)
        h = (_silu(g) * u).astype(BF16)
        o_ref[:rows, :] = jnp.dot(h, wd_b[...], preferred_element_type=F32)

    @pl.when(valid)
    def _():
        _wait_row_gather(x_hbm, xbuf.at[slot], row_sems.at[slot], _row_groups(n_real))

    @pl.when(valid & (n_real > TSMALL))
    def _():
        mlp(TMOE)

    @pl.when(valid & (n_real <= TSMALL))
    def _():
        mlp(TSMALL)
        o_ref[TSMALL:, :] = jnp.zeros((TMOE - TSMALL, D_MODEL), F32)

    @pl.when(jnp.logical_not(valid))
    def _():
        o_ref[...] = jnp.zeros(o_ref.shape, F32)


def _moe(block_e, next_e, n_real, n_valid, buf_tok, x_tok, w_gate, w_up, w_down):
    rows = buf_tok.shape[0]
    nblk = rows // TMOE

    def blk(i, nv):
        return jnp.minimum(i, nv[0] - 1)

    grid_spec = pltpu.PrefetchScalarGridSpec(
        num_scalar_prefetch=4,
        grid=(nblk,),
        in_specs=[pl.BlockSpec((None, 1, TMOE), lambda i, be, nx, nr, nv: (blk(i, nv), 0, 0),
                               memory_space=pltpu.SMEM),
                  pl.BlockSpec((None, 1, TMOE), lambda i, be, nx, nr, nv: (blk(i + 1, nv), 0, 0),
                               memory_space=pltpu.SMEM),
                  pl.BlockSpec(memory_space=pl.ANY), pl.BlockSpec(memory_space=pl.ANY),
                  pl.BlockSpec(memory_space=pl.ANY), pl.BlockSpec(memory_space=pl.ANY)],
        out_specs=pl.BlockSpec((TMOE, D_MODEL), lambda i, be, nx, nr, nv: (i, 0)),
        scratch_shapes=[pltpu.VMEM((2, TMOE // GATHER_UNROLL, GATHER_UNROLL, D_MODEL), F32),
                        pltpu.VMEM((D_MODEL, D_FF), F32), pltpu.VMEM((D_MODEL, D_FF), F32),
                        pltpu.VMEM((D_FF, D_MODEL), F32),
                        pltpu.VMEM((D_MODEL, D_FF), BF16), pltpu.VMEM((D_MODEL, D_FF), BF16),
                        pltpu.VMEM((D_FF, D_MODEL), BF16),
                        pltpu.SemaphoreType.DMA((3,)), pltpu.SemaphoreType.DMA((2,))],
    )
    slot_rows = buf_tok.reshape(nblk, 1, TMOE)
    return pl.pallas_call(
        _moe_kernel,
        grid_spec=grid_spec,
        out_shape=jax.ShapeDtypeStruct((rows, D_MODEL), F32),
        compiler_params=_cparams(("arbitrary",)),
        name="moe_experts",
    )(block_e, next_e, n_real, n_valid, slot_rows, slot_rows, x_tok, w_gate, w_up, w_down)


def _final_kernel(idx_ref, idx_next_ref, h_ref, w_ref, nw_ref, y_hbm, op_ref, os_ref, ybuf, row_sems):
    i = pl.program_id(0)
    last = pl.num_programs(0) - 1
    slot = i % 2

    @pl.when(i == 0)
    def _():
        _start_row_gather(idx_ref, y_hbm, ybuf.at[0], row_sems.at[0])

    @pl.when(i + 1 < pl.num_programs(0))
    def _():
        _start_row_gather(idx_next_ref, y_hbm, ybuf.at[1 - slot], row_sems.at[1 - slot])

    _wait_row_gather(y_hbm, ybuf.at[slot], row_sems.at[slot])
    w = w_ref[...]
    hg = HALF // GATHER_UNROLL
    moe = (ybuf[slot, :hg].reshape(HALF, D_MODEL) * w[:, 0:1]
           + ybuf[slot, hg:].reshape(HALF, D_MODEL) * w[:, 1:2])
    h = h_ref[...] + moe
    ms = jnp.mean(h * h, axis=-1, keepdims=True)
    out = h * lax.rsqrt(ms + EPS) * nw_ref[...]

    @pl.when(i < last)
    def _():
        op_ref[...] = out

    @pl.when(i == last)
    def _():
        os_ref[...] = out


def _final(h1, yb, dest, wts, norm_w, seq):
    R = h1.shape[0]
    n = R // HALF
    nbp = seq // HALF
    assert R == (1 + nbp + 1) * HALF
    row = lambda w: pl.BlockSpec((HALF, w), lambda i: (i, 0))
    slots = dest.reshape(TOP_K, n, HALF).transpose(1, 0, 2).reshape(n, 1, TOP_K * HALF)
    idx_spec = lambda off: pl.BlockSpec((None, 1, TOP_K * HALF), lambda i: (jnp.minimum(i + off, n - 1), 0, 0),
                                        memory_space=pltpu.SMEM)
    return pl.pallas_call(
        _final_kernel,
        grid=(n,),
        in_specs=[idx_spec(0), idx_spec(1), row(D_MODEL), row(TOP_K),
                  pl.BlockSpec((1, D_MODEL), lambda i: (0, 0)),
                  pl.BlockSpec(memory_space=pl.ANY)],
        out_specs=[pl.BlockSpec((HALF, D_MODEL), lambda i: (jnp.clip(i - 1, 0, nbp - 1), 0)),
                   pl.BlockSpec((HALF, D_MODEL), lambda i: (0, 0))],
        out_shape=[jax.ShapeDtypeStruct((seq, D_MODEL), F32),
                   jax.ShapeDtypeStruct((HALF, D_MODEL), F32)],
        scratch_shapes=[pltpu.VMEM((2, TOP_K * HALF // GATHER_UNROLL, GATHER_UNROLL, D_MODEL), F32),
                        pltpu.SemaphoreType.DMA((2,))],
        compiler_params=_cparams(("arbitrary",)),
        name="final_norm",
    )(slots, slots, h1, wts, norm_w, yb)


def _route(route):
    R = route.shape[0]
    a_max = TOP_K * R
    nblk = (a_max + N_EXPERTS * (TMOE - 1) + TMOE - 1) // TMOE
    rows = nblk * TMOE
    rank, meta = _rank(route, nblk)
    eid = route[:, :TOP_K].T.astype(I32)
    wts = route[:, TOP_K:2 * TOP_K]
    starts = meta[0, :N_EXPERTS].astype(I32) * TMOE
    n_valid = meta[1, :1].astype(I32)
    block_e = meta[8:8 + nblk, 0].astype(I32)
    next_e = meta[8:8 + nblk, 1].astype(I32)
    n_real = meta[8:8 + nblk, 2].astype(I32)
    is_real = eid >= 0
    experts = jnp.arange(N_EXPERTS, dtype=I32)[:, None, None]
    first_slot = jnp.sum(jnp.where(eid[None] == experts, starts[:, None, None], 0), axis=0)
    dest = jnp.where(is_real, first_slot + rank[:, :TOP_K].T.astype(I32), 0)
    buf_tok = _slot_table(jnp.where(is_real, dest, rows), rows)
    return wts, dest, buf_tok, block_e, next_e, n_real, n_valid


def _slot_table_kernel(dest_ref, out_ref, *, rows_per_choice):
    def clear(r, carry):
        for c in range(LANES):
            out_ref[r, c] = 0
        return carry

    lax.fori_loop(0, out_ref.shape[0], clear, 0)

    def place(r, carry):
        tok0 = jnp.where(r >= rows_per_choice, r - rows_per_choice, r) * LANES
        for c in range(LANES):
            d = dest_ref[r, c]
            out_ref[d >> 7, d & (LANES - 1)] = tok0 + c
        return carry

    lax.fori_loop(0, TOP_K * rows_per_choice, place, 0)


def _slot_table(dest, n_slots):
    n_tok = dest.shape[1]
    assert n_tok % LANES == 0 and n_slots % LANES == 0 and dest.shape[0] == TOP_K == 2
    smem = pl.BlockSpec(memory_space=pltpu.SMEM)
    out = pl.pallas_call(
        functools.partial(_slot_table_kernel, rows_per_choice=n_tok // LANES),
        in_specs=[smem],
        out_specs=smem,
        out_shape=jax.ShapeDtypeStruct((n_slots // LANES + 1, LANES), I32),
        name="slot_table",
    )(dest.reshape(TOP_K * n_tok // LANES, LANES))
    return out.reshape(-1)[:n_slots]


def kernel(x_prompt, x_sample, cache_k, cache_v, state_conv, state_ssm, page_table, meta_tokens, norm_mix_w,
           w_in, lambda_q1, lambda_k1, lambda_q2, lambda_k2, subln_w, conv_w, conv_b, dt_bias, a_log, d_skip,
           ssm_norm_w, w_out, norm_ffn_w, w_router_group, w_router_expert, w_gate, w_up, w_down, final_norm_w):
    depth = w_in.shape[0]
    assert depth == 1 and x_prompt.shape[0] == 1 and x_sample.shape[1] == 1
    seq = x_prompt.shape[1]
    B = x_sample.shape[0]
    n_pages = page_table.shape[1]
    n_phys = cache_k.shape[1]
    past = n_pages * cache_k.shape[2]
    assert cache_k.shape[2] == PAGE and seq % TM == 0 and B == HALF and FRONT == HALF
    n_t = FRONT + seq
    R = n_t + B
    l = 0
    lam_init = 0.8 - 0.6 * math.exp(-0.3 * l)

    front = jnp.concatenate([jnp.zeros((PAD_ROWS, D_MODEL), F32), meta_tokens.astype(F32)], axis=0)
    xp, xs = x_prompt[0], x_sample[:, 0]
    w = w_in[l]
    w_all = w.astype(BF16)
    wdt = jnp.pad(w[:, OFF_DT:], ((0, 0), (0, LANES - N_HEADS_S))).astype(BF16)

    pos = jnp.concatenate([jnp.maximum(jnp.arange(n_t, dtype=I32) - PAD_ROWS, 0),
                           jnp.full((B,), past, I32)])
    inv = ROPE_THETA ** (-jnp.arange(0, ROT_DIM, 2, dtype=F32) / ROT_DIM)
    ang = pos.astype(F32)[:, None] * inv[None, :]
    cos, sin = jnp.cos(ang), jnp.sin(ang)
    half = ROT_DIM // 2
    ones = jnp.ones((R, DK - ROT_DIM), F32)
    zeros = jnp.zeros((R, DK - ROT_DIM), F32)
    zh = jnp.zeros((R, half), F32)
    ctab = jnp.tile(jnp.concatenate([cos, cos, ones], axis=1), (1, 2))
    s1tab = jnp.tile(jnp.concatenate([-sin, zh, zeros], axis=1), (1, 2))
    s2tab = jnp.tile(jnp.concatenate([zh, sin, zeros], axis=1), (1, 2))

    row1 = lambda v: v.reshape(1, -1).astype(F32)
    pad_heads = lambda v: jnp.pad(v.astype(F32), (0, LANES - N_HEADS_S)).reshape(1, LANES)
    a_row = pad_heads(-jnp.exp(a_log[l].astype(F32)))
    dtb = pad_heads(dt_bias[l])
    dskip_full = jnp.repeat(d_skip[l].astype(F32), SSM_HEAD_DIM).reshape(1, D_INNER)
    normw = row1(ssm_norm_w[l])
    head_of_col = jnp.arange(D_INNER, dtype=I32) // SSM_HEAD_DIM
    e_mat = (jnp.arange(LANES, dtype=I32)[:, None] == head_of_col[None, :]).astype(BF16)
    tri = (jnp.arange(CHUNK)[:, None] >= jnp.arange(CHUNK)[None, :]).astype(BF16)
    lq1, lk1, lq2, lk2 = (row1(v[l]) for v in (lambda_q1, lambda_k1, lambda_q2, lambda_k2))
    sub_w = row1(subln_w[l])

    q, k, v, kb, vb, z, xbc, dt_raw = _inproj(front, xp, xs, row1(norm_mix_w[l]), w_all, wdt,
                                              ctab, s1tab, s2tab)

    k_s, v_s = k[n_t * N_KV_HEADS:], v[n_t * N_KV_HEADS:]
    pad8 = lambda a: jnp.pad(a.reshape(B, N_KV_HEADS, LANES), ((0, 0), (0, LANES - N_KV_HEADS), (0, 0)))
    att_all, att_s = _attention(page_table, q, kb, vb, q[n_t:].reshape(B, N_HEADS_A, LANES), pad8(k_s), pad8(v_s),
                                cache_k[l].reshape(n_phys * PAGE * N_KV_HEADS, LANES),
                                cache_v[l].reshape(n_phys * PAGE * N_KV_HEADS, LANES),
                                lq1, lk1, lq2, lk2, sub_w, lam_init)
    ssm_p, state_p = _prompt_ssd(xbc, z, dt_raw, conv_w[l].astype(F32), row1(conv_b[l]), dtb, a_row,
                                 dskip_full, normw, e_mat, e_mat.T.astype(F32), tri, n_t)

    sc = state_conv[l]
    xbc_s = xbc[n_t:]
    ssm_s, state_s = _decode_ssm(xbc_s, sc[:, 0], sc[:, 1], sc[:, 2], dt_raw[n_t:], z[n_t:],
                                 state_ssm[l].reshape(B, D_INNER, D_STATE), conv_w[l].astype(F32),
                                 row1(conv_b[l]), dtb, a_row, dskip_full, normw, e_mat)

    att = lax.dynamic_update_slice(att_all, att_s.reshape(B, N_HEADS_A * DV), (n_t, 0))

    wo = w_out[l].astype(BF16)
    wr = jnp.pad(jnp.concatenate([w_router_group[l], w_router_expert[l]], axis=1).astype(F32),
                 ((0, 0), (0, LANES - N_EGROUPS - N_EXPERTS)))
    wr_hi = wr.astype(BF16)
    wr_lo = (wr - wr_hi.astype(F32)).astype(BF16)
    h1, u2, route = _outproj(front, xp, xs, att, ssm_p, ssm_s, wo[:N_HEADS_A * DV], wo[N_HEADS_A * DV:],
                             row1(norm_ffn_w[l]), wr_hi, wr_lo)

    wts, dest, buf_tok, block_e, next_e, n_real, n_valid = _route(route)
    yb = _moe(block_e, next_e, n_real, n_valid, buf_tok, u2, w_gate[l], w_up[l], w_down[l])
    y_p, y_s = _final(h1, yb, dest, wts, row1(final_norm_w), seq)

    t_real = N_META + seq
    y_prompt = y_p.reshape(1, seq, D_MODEL)
    y_sample = y_s.reshape(B, 1, D_MODEL)
    k_prompt = k[PAD_ROWS * N_KV_HEADS:n_t * N_KV_HEADS].reshape(1, 1, t_real, N_KV_HEADS, 2 * DK)
    v_prompt = v[PAD_ROWS * N_KV_HEADS:n_t * N_KV_HEADS].reshape(1, 1, t_real, N_KV_HEADS, DV)
    conv_prompt = xbc[n_t - (CONV_W - 1):n_t].reshape(1, 1, CONV_W - 1, XBC_DIM)
    ssm_prompt = state_p.reshape(1, 1, N_HEADS_S, SSM_HEAD_DIM, D_STATE)
    k_sample = k_s.reshape(1, B, 1, N_KV_HEADS, 2 * DK)
    v_sample = v_s.reshape(1, B, 1, N_KV_HEADS, DV)
    conv_sample = jnp.stack([sc[:, 1], sc[:, 2], xbc_s], axis=1)[None]
    ssm_sample = state_s.reshape(1, B, N_HEADS_S, SSM_HEAD_DIM, D_STATE)
    return (y_prompt, y_sample, k_prompt, v_prompt, conv_prompt, ssm_prompt,
            k_sample, v_sample, conv_sample, ssm_sample)
```
